```python
import math
import jax, jax.numpy as jnp
from jax import lax
import numpy as np

D_MODEL = 2048
BATCH = 2
SEQ = 4096
DEPTH = 2

CHUNK = 64
N_MEM = 256
CONV_W = 4
N_EVEN = (DEPTH + 1) // 2
N_ODD = DEPTH // 2
ALPHA = (2 * DEPTH) ** 0.25
BETA = (8 * DEPTH) ** -0.25
LN_EPS = 1e-5
RMS_EPS = 1e-6
DT_MIN, DT_MAX = 1e-3, 1e-1

W_A = D_MODEL // 2
H_A = 8
BW_A = W_A // H_A
RG_C = 8.0
W_B = D_MODEL
HD_B = 64
H_B = W_B // HD_B
NG_B = 2
N_B = 128
CONV_B = W_B + 2 * NG_B * N_B
IN_AB = 2 * W_A + W_B + CONV_B + H_B
OUT_AB = W_A + W_B
W_C = D_MODEL // 2
GS_C = 16
G_C = W_C // GS_C
P_C = 64
H_D = 8
DK_D = D_MODEL // 16
DV_D = D_MODEL // 16
W_D = H_D * DV_D
QKV_D = 2 * H_D * DK_D + W_D
IN_CD = W_C + QKV_D + W_D + 2 * H_D
OUT_CD = W_C + W_D
H_X = 4
HD_X = D_MODEL // H_X
NG_E = 4
E_PER = 8
N_EXP = NG_E * E_PER
D_E = D_MODEL // 8
TOPK_IN = 2

kernel_name = "hybrid_rglru_ssd_s5_gdn_hmoe_deepnorm"


def layer_norm(x, g, b):
    xf = x.astype(jnp.float32)
    mu = jnp.mean(xf, -1, keepdims=True)
    var = jnp.mean(jnp.square(xf - mu), -1, keepdims=True)
    return ((xf - mu) * lax.rsqrt(var + LN_EPS) * g + b).astype(x.dtype)


def rms_norm(x, g):
    xf = x.astype(jnp.float32)
    return (xf * lax.rsqrt(jnp.mean(xf * xf, -1, keepdims=True) + RMS_EPS) * g).astype(x.dtype)


def l2norm(x):
    return x * lax.rsqrt(jnp.sum(x * x, -1, keepdims=True) + 1e-6)


def causal_conv(x, w, b=None):
    K, C = w.shape
    xp = jnp.pad(x, ((0, 0), (K - 1, 0), (0, 0)))
    y = lax.conv_general_dilated(xp, w[:, None, :].astype(x.dtype), (1,), 'VALID',
                                 dimension_numbers=('NWC', 'WIO', 'NWC'),
                                 feature_group_count=C)
    return y if b is None else y + b


def segsum(a):
    cs = jnp.cumsum(a, -1)
    T = a.shape[-1]
    mask = jnp.tril(jnp.ones((T, T), bool))
    return jnp.where(mask, cs[..., :, None] - cs[..., None, :], -jnp.inf)


def rg_lru(x, wa, ba, wx, bx, lam):
    Bsz, S, _ = x.shape
    f32 = jnp.float32
    xh = x.reshape(Bsz, S, H_A, BW_A)
    r = jax.nn.sigmoid(jnp.einsum('bshi,hij->bshj', xh, wa).reshape(Bsz, S, W_A) + ba)
    i = jax.nn.sigmoid(jnp.einsum('bshi,hij->bshj', xh, wx).reshape(Bsz, S, W_A) + bx)
    log_a = -RG_C * r.astype(f32) * jax.nn.softplus(-lam.astype(f32))
    a = jnp.exp(log_a)
    u = jnp.sqrt(-jnp.expm1(2.0 * log_a)) * (i * x).astype(f32)

    def comb(e1, e2):
        a1, b1 = e1
        a2, b2 = e2
        return a1 * a2, a2 * b1 + b2

    _, h = lax.associative_scan(comb, (a, u), axis=1)
    return h.astype(x.dtype)


def ssd_mixer(xbc, z, dt_raw, conv_w, conv_b, dt_bias, a_log, d, norm_w):
    Bsz, S, _ = xbc.shape
    f32 = jnp.float32
    NC = S // CHUNK
    HG = H_B // NG_B
    xbc = jax.nn.silu(causal_conv(xbc, conv_w, conv_b))
    xs, Bm, Cm = jnp.split(xbc, [W_B, W_B + NG_B * N_B], axis=-1)
    dt = jax.nn.softplus((dt_raw + dt_bias).astype(f32))
    A = -jnp.exp(a_log.astype(f32))
    X = xs.reshape(Bsz, NC, CHUNK, NG_B, HG, HD_B).astype(f32)
    Bc = Bm.reshape(Bsz, NC, CHUNK, NG_B, N_B).astype(f32)
    Cc = Cm.reshape(Bsz, NC, CHUNK, NG_B, N_B).astype(f32)
    dtc = dt.reshape(Bsz, NC, CHUNK, NG_B, HG)
    Xdt = X * dtc[..., None]
    Adt = jnp.moveaxis(dtc * A.reshape(NG_B, HG), 2, -1)
    A_cs = jnp.cumsum(Adt, -1)
    Lmat = jnp.exp(segsum(Adt))
    CB = jnp.einsum('bclgn,bcsgn->bcgls', Cc, Bc)
    y_diag = jnp.einsum('bcgls,bcgjls,bcsgjp->bclgjp', CB, Lmat, Xdt)
    decay_states = jnp.exp(A_cs[..., -1:] - A_cs)
    chunk_states = jnp.einsum('bclgn,bcgjl,bclgjp->bcgjpn', Bc, decay_states, Xdt)
    chunk_decay = jnp.exp(A_cs[..., -1])

    def step(state, inp):
        cst, dec = inp
        return state * dec[..., None, None] + cst, state

    init = jnp.zeros((Bsz, NG_B, HG, HD_B, N_B), f32)
    _, prev = lax.scan(step, init, (jnp.moveaxis(chunk_states, 1, 0), jnp.moveaxis(chunk_decay, 1, 0)))
    prev = jnp.moveaxis(prev, 0, 1)
    y_off = jnp.einsum('bclgn,bcgjpn,bcgjl->bclgjp', Cc, prev, jnp.exp(A_cs))
    y = (y_diag + y_off).reshape(Bsz, S, W_B) + xs.astype(f32) * jnp.repeat(d.astype(f32), HD_B)
    yg = (y * jax.nn.silu(z.astype(f32))).reshape(Bsz, S, NG_B, W_B // NG_B)
    y = rms_norm(yg, norm_w.reshape(NG_B, W_B // NG_B)).reshape(Bsz, S, W_B)
    return y.astype(xbc.dtype)


def s5_mixer(u, a_re, a_im, log_step, b_re, b_im, c_re, c_im, d, glu_w, glu_b):
    Bsz, S, _ = u.shape
    f32 = jnp.float32
    ar, ai = a_re.astype(f32), a_im.astype(f32)
    step = jnp.exp(log_step.astype(f32))[:, None]
    mag = jnp.exp(ar * step)
    lb_re, lb_im = mag * jnp.cos(ai * step), mag * jnp.sin(ai * step)
    den = ar * ar + ai * ai
    f_re = ((lb_re - 1.0) * ar + lb_im * ai) / den
    f_im = (lb_im * ar - (lb_re - 1.0) * ai) / den
    br, bi = b_re.astype(f32), b_im.astype(f32)
    bb_re = f_re[..., None] * br - f_im[..., None] * bi
    bb_im = f_re[..., None] * bi + f_im[..., None] * br
    uh = u.reshape(Bsz, S, G_C, GS_C).astype(f32)
    bu_re = jnp.einsum('bsgk,gpk->bsgp', uh, bb_re)
    bu_im = jnp.einsum('bsgk,gpk->bsgp', uh, bb_im)
    la_re = jnp.broadcast_to(lb_re, bu_re.shape)
    la_im = jnp.broadcast_to(lb_im, bu_im.shape)

    def comb(e1, e2):
        a1r, a1i, b1r, b1i = e1
        a2r, a2i, b2r, b2i = e2
        return (a2r * a1r - a2i * a1i, a2r * a1i + a2i * a1r,
                a2r * b1r - a2i * b1i + b2r, a2r * b1i + a2i * b1r + b2i)

    _, _, h_re, h_im = lax.associative_scan(comb, (la_re, la_im, bu_re, bu_im), axis=1)
    y = (jnp.einsum('gkp,bsgp->bsgk', c_re.astype(f32), h_re)
         - jnp.einsum('gkp,bsgp->bsgk', c_im.astype(f32), h_im)).reshape(Bsz, S, W_C)
    y = (y + d * u.astype(f32)).astype(u.dtype)
    g = jax.nn.gelu(y)
    return g * jax.nn.sigmoid(g @ glu_w + glu_b)


def gated_deltanet(qkv, z, a_raw, b_raw, conv_w, a_log, dt_bias, norm_w):
    Bsz, S, _ = qkv.shape
    f32 = jnp.float32
    NC = S // CHUNK
    qkv = jax.nn.silu(causal_conv(qkv, conv_w))
    q, k, v = jnp.split(qkv, [H_D * DK_D, 2 * H_D * DK_D], -1)
    q = l2norm(q.reshape(Bsz, S, H_D, DK_D).astype(f32)) * (DK_D ** -0.5)
    k = l2norm(k.reshape(Bsz, S, H_D, DK_D).astype(f32))
    v = v.reshape(Bsz, S, H_D, DV_D).astype(f32)
    g = -jnp.exp(a_log.astype(f32)) * jax.nn.softplus((a_raw + dt_bias).astype(f32))
    beta = jax.nn.sigmoid(b_raw.astype(f32))

    def chunks(t):
        return jnp.moveaxis(t.reshape(Bsz, NC, CHUNK, H_D, -1), 3, 1)

    qc, kc, vc = chunks(q), chunks(k), chunks(v)
    gcs = jnp.cumsum(chunks(g[..., None])[..., 0], -1)
    bc = chunks(beta[..., None])[..., 0]
    tri_incl = jnp.tril(jnp.ones((CHUNK, CHUNK), bool))
    tri_strict = jnp.tril(jnp.ones((CHUNK, CHUNK), bool), -1)
    decay = jnp.exp(jnp.where(tri_incl, gcs[..., :, None] - gcs[..., None, :], -jnp.inf))
    kb = kc * bc[..., None]
    kk = jnp.einsum('bhnld,bhnsd->bhnls', kb, kc) * decay
    m = jnp.where(tri_strict, kk, 0.0) + jnp.eye(CHUNK, dtype=f32)
    rhs = jnp.concatenate([vc * bc[..., None], kb * jnp.exp(gcs)[..., None]], -1)
    sol = lax.linalg.triangular_solve(m, rhs, left_side=True, lower=True, unit_diagonal=True)
    u_c, w_c = sol[..., :DV_D], sol[..., DV_D:]
    qk = jnp.where(tri_incl, jnp.einsum('bhnld,bhnsd->bhnls', qc, kc) * decay, 0.0)
    q_dec = qc * jnp.exp(gcs)[..., None]
    k_dec = kc * jnp.exp(gcs[..., -1:] - gcs)[..., None]
    g_last = jnp.exp(gcs[..., -1])

    def step(state, inp):
        w_i, u_i, qk_i, qd_i, kd_i, gl_i = inp
        v_new = u_i - jnp.einsum('bhld,bhdv->bhlv', w_i, state)
        o = jnp.einsum('bhld,bhdv->bhlv', qd_i, state) + jnp.einsum('bhls,bhsv->bhlv', qk_i, v_new)
        state = state * gl_i[..., None, None] + jnp.einsum('bhld,bhlv->bhdv', kd_i, v_new)
        return state, o

    xs = tuple(jnp.moveaxis(t, 2, 0) for t in (w_c, u_c, qk, q_dec, k_dec, g_last))
    _, o = lax.scan(step, jnp.zeros((Bsz, H_D, DK_D, DV_D), f32), xs)
    o = jnp.moveaxis(jnp.moveaxis(o, 0, 2), 1, 3).reshape(Bsz, S, H_D, DV_D)
    o = rms_norm(o, norm_w) * jax.nn.silu(z.reshape(Bsz, S, H_D, DV_D).astype(f32))
    return o.reshape(Bsz, S, W_D).astype(qkv.dtype)


def mixer_ab(x, w_in, rg_conv_w, rg_conv_b, rg_wa, rg_ba, rg_wx, rg_bx, rg_lam,
             ssd_conv_w, ssd_conv_b, ssd_dt_bias, ssd_a_log, ssd_d, ssd_norm_w, w_out):
    proj = x @ w_in
    gate_a, xa, z_b, xbc_b, dt_b = jnp.split(
        proj, [W_A, 2 * W_A, 2 * W_A + W_B, 2 * W_A + W_B + CONV_B], -1)
    ya = jax.nn.gelu(gate_a) * rg_lru(causal_conv(xa, rg_conv_w, rg_conv_b),
                                      rg_wa, rg_ba, rg_wx, rg_bx, rg_lam)
    yb = ssd_mixer(xbc_b, z_b, dt_b, ssd_conv_w, ssd_conv_b, ssd_dt_bias, ssd_a_log, ssd_d, ssd_norm_w)
    return jnp.concatenate([ya, yb], -1) @ w_out


def mixer_cd(x, w_in, s5_a_re, s5_a_im, s5_log_step, s5_b_re, s5_b_im, s5_c_re, s5_c_im, s5_d,
             s5_glu_w, s5_glu_b, dn_conv_w, dn_a_log, dn_dt_bias, dn_norm_w, w_out):
    proj = x @ w_in
    u_c, qkv, z_d, a_d, b_d = jnp.split(
        proj, [W_C, W_C + QKV_D, W_C + QKV_D + W_D, W_C + QKV_D + W_D + H_D], -1)
    yc = s5_mixer(u_c, s5_a_re, s5_a_im, s5_log_step, s5_b_re, s5_b_im, s5_c_re, s5_c_im,
                  s5_d, s5_glu_w, s5_glu_b)
    yd = gated_deltanet(qkv, z_d, a_d, b_d, dn_conv_w, dn_a_log, dn_dt_bias, dn_norm_w)
    return jnp.concatenate([yc, yd], -1) @ w_out


def mem_xattn(x, mem, w_q, w_kv, w_o):
    Bsz, S, _ = x.shape
    M = mem.shape[1]
    q = (x @ w_q).reshape(Bsz, S, H_X, HD_X)
    k, v = jnp.split(mem @ w_kv, 2, -1)
    k = k.reshape(Bsz, M, H_X, HD_X)
    v = v.reshape(Bsz, M, H_X, HD_X)
    s = jnp.einsum('bshd,bmhd->bhsm', q, k).astype(jnp.float32) * (HD_X ** -0.5)
    p = jax.nn.softmax(s, -1).astype(x.dtype)
    o = jnp.einsum('bhsm,bmhd->bshd', p, v).reshape(Bsz, S, D_MODEL)
    return o @ w_o


def hier_moe(x, w_group, b_group, w_expert, b_expert, w_gate, w_up, w_down):
    Bsz, S, D = x.shape
    f32 = jnp.float32
    t = x.reshape(-1, D)
    gp = jax.nn.softmax((t @ w_group + b_group).astype(f32), -1)
    g_idx = jnp.argmax(gp, -1)
    g_prob = jnp.take_along_axis(gp, g_idx[:, None], -1)
    el = (t @ w_expert + b_expert).astype(f32).reshape(-1, NG_E, E_PER)
    el_sel = jnp.take_along_axis(el, g_idx[:, None, None], 1)[:, 0]
    top_v, top_i = lax.top_k(el_sel, TOPK_IN)
    top_w = jax.nn.softmax(top_v, -1) * g_prob
    within = jnp.sum(jax.nn.one_hot(top_i, E_PER, dtype=f32) * top_w[..., None], 1)
    combine = (jax.nn.one_hot(g_idx, NG_E, dtype=f32)[:, :, None] * within[:, None, :]).reshape(-1, N_EXP)
    h = jax.nn.silu(jnp.einsum('td,edf->tef', t, w_gate)) * jnp.einsum('td,edf->tef', t, w_up)
    y = jnp.einsum('tef,te,efd->td', h, combine.astype(x.dtype), w_down)
    return y.reshape(Bsz, S, D)


def setup_inputs(seed: int = 0) -> dict:
    key = jax.random.key(seed)
    ks = iter(jax.random.split(key, 64))
    f32 = jnp.float32

    def nrm(shape, scale=1.0):
        return jax.random.normal(next(ks), shape, f32) * scale

    def unif(shape, lo, hi):
        return jax.random.uniform(next(ks), shape, f32, lo, hi)

    def dt_bias_init(shape):
        dt = jnp.exp(unif(shape, math.log(DT_MIN), math.log(DT_MAX)))
        return dt + jnp.log(-jnp.expm1(-dt))

    NE, NO, L, D = N_EVEN, N_ODD, DEPTH, D_MODEL
    a_rg = unif((NE, W_A), 0.9, 0.999) ** (1.0 / RG_C)
    xa_k = nrm((L, D, D), D ** -0.5)
    xa_v = nrm((L, D, D), D ** -0.5 * BETA)
    return {
        "x": nrm((BATCH, SEQ, D)),
        "mem": nrm((BATCH, N_MEM, D)),
        "ab_w_in": nrm((NE, D, IN_AB), D ** -0.5),
        "rg_conv_w": nrm((NE, CONV_W, W_A), CONV_W ** -0.5),
        "rg_conv_b": nrm((NE, W_A), 0.01),
        "rg_wa": nrm((NE, H_A, BW_A, BW_A), BW_A ** -0.5),
        "rg_ba": nrm((NE, W_A), 0.01),
        "rg_wx": nrm((NE, H_A, BW_A, BW_A), BW_A ** -0.5),
        "rg_bx": nrm((NE, W_A), 0.01),
        "rg_lam": jnp.log(a_rg) - jnp.log1p(-a_rg),
        "ssd_conv_w": nrm((NE, CONV_W, CONV_B), CONV_W ** -0.5),
        "ssd_conv_b": nrm((NE, CONV_B), 0.01),
        "ssd_dt_bias": dt_bias_init((NE, H_B)),
        "ssd_a_log": jnp.log(unif((NE, H_B), 1.0, 16.0)),
        "ssd_d": 1.0 + nrm((NE, H_B), 0.01),
        "ssd_norm_w": 1.0 + nrm((NE, W_B), 0.01),
        "ab_w_out": nrm((NE, OUT_AB, D), OUT_AB ** -0.5 * BETA),
        "cd_w_in": nrm((NO, D, IN_CD), D ** -0.5),
        "s5_a_re": -0.5 + nrm((NO, G_C, P_C), 0.01),
        "s5_a_im": jnp.pi * jnp.arange(P_C, dtype=f32) + nrm((NO, G_C, P_C), 0.01),
        "s5_log_step": unif((NO, G_C), math.log(DT_MIN), math.log(DT_MAX)),
        "s5_b_re": nrm((NO, G_C, P_C, GS_C), (2 * GS_C) ** -0.5),
        "s5_b_im": nrm((NO, G_C, P_C, GS_C), (2 * GS_C) ** -0.5),
        "s5_c_re": nrm((NO, G_C, GS_C, P_C), P_C ** -0.5),
        "s5_c_im": nrm((NO, G_C, GS_C, P_C), P_C ** -0.5),
        "s5_d": nrm((NO, W_C)),
        "s5_glu_w": nrm((NO, W_C, W_C), W_C ** -0.5),
        "s5_glu_b": nrm((NO, W_C), 0.01),
        "dn_conv_w": nrm((NO, CONV_W, QKV_D), CONV_W ** -0.5),
        "dn_a_log": jnp.log(unif((NO, H_D), 1.0, 16.0)),
        "dn_dt_bias": dt_bias_init((NO, H_D)),
        "dn_norm_w": 1.0 + nrm((NO, DV_D), 0.01),
        "cd_w_out": nrm((NO, OUT_CD, D), OUT_CD ** -0.5 * BETA),
        "xa_w_q": nrm((L, D, D), D ** -0.5),
        "xa_w_kv": jnp.concatenate([xa_k, xa_v], -1),
        "xa_w_o": nrm((L, D, D), D ** -0.5 * BETA),
        "moe_w_group": nrm((L, D, NG_E), D ** -0.5),
        "moe_b_group": nrm((L, NG_E), 0.01),
        "moe_w_expert": nrm((L, D, N_EXP), D ** -0.5),
        "moe_b_expert": nrm((L, N_EXP), 0.01),
        "moe_w_gate": nrm((L, N_EXP, D, D_E), D ** -0.5),
        "moe_w_up": nrm((L, N_EXP, D, D_E), D ** -0.5),
        "moe_w_down": nrm((L, N_EXP, D_E, D), D_E ** -0.5 * BETA),
        "ln1_g": 1.0 + nrm((L, D), 0.01),
        "ln1_b": nrm((L, D), 0.01),
        "ln2_g": 1.0 + nrm((L, D), 0.01),
        "ln2_b": nrm((L, D), 0.01),
        "ln3_g": 1.0 + nrm((L, D), 0.01),
        "ln3_b": nrm((L, D), 0.01),
    }


def reference(x, mem,
              ab_w_in, rg_conv_w, rg_conv_b, rg_wa, rg_ba, rg_wx, rg_bx, rg_lam,
              ssd_conv_w, ssd_conv_b, ssd_dt_bias, ssd_a_log, ssd_d, ssd_norm_w, ab_w_out,
              cd_w_in, s5_a_re, s5_a_im, s5_log_step, s5_b_re, s5_b_im, s5_c_re, s5_c_im, s5_d,
              s5_glu_w, s5_glu_b, dn_conv_w, dn_a_log, dn_dt_bias, dn_norm_w, cd_w_out,
              xa_w_q, xa_w_kv, xa_w_o,
              moe_w_group, moe_b_group, moe_w_expert, moe_b_expert, moe_w_gate, moe_w_up, moe_w_down,
              ln1_g, ln1_b, ln2_g, ln2_b, ln3_g, ln3_b):
    for l in range(DEPTH):
        i = l // 2
        if l % 2 == 0:
            mix = mixer_ab(x, ab_w_in[i], rg_conv_w[i], rg_conv_b[i], rg_wa[i], rg_ba[i], rg_wx[i],
                           rg_bx[i], rg_lam[i], ssd_conv_w[i], ssd_conv_b[i], ssd_dt_bias[i],
                           ssd_a_log[i], ssd_d[i], ssd_norm_w[i], ab_w_out[i])
        else:
            mix = mixer_cd(x, cd_w_in[i], s5_a_re[i], s5_a_im[i], s5_log_step[i], s5_b_re[i],
                           s5_b_im[i], s5_c_re[i], s5_c_im[i], s5_d[i], s5_glu_w[i], s5_glu_b[i],
                           dn_conv_w[i], dn_a_log[i], dn_dt_bias[i], dn_norm_w[i], cd_w_out[i])
        x = layer_norm(ALPHA * x + mix, ln1_g[l], ln1_b[l])
        x = layer_norm(ALPHA * x + mem_xattn(x, mem, xa_w_q[l], xa_w_kv[l], xa_w_o[l]), ln2_g[l], ln2_b[l])
        x = layer_norm(ALPHA * x + hier_moe(x, moe_w_group[l], moe_b_group[l], moe_w_expert[l],
                                            moe_b_expert[l], moe_w_gate[l], moe_w_up[l], moe_w_down[l]),
                       ln3_g[l], ln3_b[l])
    return x
```

```python
import functools
import math

import jax
import jax.numpy as jnp
from jax import lax
from jax.experimental import pallas as pl
from jax.experimental.pallas import tpu as pltpu

f32 = jnp.float32
bf16 = jnp.bfloat16

D_MODEL = 2048
DEPTH = 2
CHUNK = 64
CONV_W = 4
ALPHA = (2 * DEPTH) ** 0.25
LN_EPS = 1e-5
RMS_EPS = 1e-6
W_A = D_MODEL // 2
H_A = 8
BW_A = W_A // H_A
RG_C = 8.0
W_B = D_MODEL
HD_B = 64
H_B = W_B // HD_B
NG_B = 2
N_B = 128
HG_B = H_B // NG_B
CONV_B = W_B + 2 * NG_B * N_B
MAIN_AB = 2 * W_A + W_B + CONV_B
W_C = D_MODEL // 2
GS_C = 16
G_C = W_C // GS_C
P_C = 64
L_C = 16
H_D = 8
DK_D = D_MODEL // 16
DV_D = D_MODEL // 16
W_D = H_D * DV_D
QKV_D = 2 * H_D * DK_D + W_D
MAIN_CD = W_C + QKV_D + W_D
H_X = 4
HD_X = D_MODEL // H_X
NG_E = 4
E_PER = 8
N_EXP = NG_E * E_PER
D_E = D_MODEL // 8

LANES = 128
SUBLANES = 8
VMEM_LIMIT = 56 * 1024 * 1024


def _params(*sem):
    return pltpu.CompilerParams(dimension_semantics=sem, vmem_limit_bytes=VMEM_LIMIT)


def _sigmoid(x):
    return 1.0 / (1.0 + jnp.exp(-x))


def _silu(x):
    return x * _sigmoid(x)


def _softplus(x):
    return jnp.maximum(x, 0.0) + jnp.log(1.0 + jnp.exp(-jnp.abs(x)))


def _gelu_tanh(x):
    return 0.5 * x * (1.0 + jnp.tanh(math.sqrt(2.0 / math.pi) * (x + 0.044715 * (x * x * x))))


def _dot(a, b):
    return jnp.dot(a.astype(bf16), b.astype(bf16), preferred_element_type=f32)


def _dot_nt(a, b):
    return lax.dot_general(a.astype(bf16), b.astype(bf16), (((1,), (1,)), ((), ())),
                           preferred_element_type=f32)


def _dot_tn(a, b):
    return lax.dot_general(a.astype(bf16), b.astype(bf16), (((0,), (0,)), ((), ())),
                           preferred_element_type=f32)


def _split3(a):
    hi = a.astype(bf16)
    r = a - hi.astype(f32)
    mid = r.astype(bf16)
    lo = (r - mid.astype(f32)).astype(bf16)
    return hi, mid, lo


def _dot_exact_lhs(sel, b):
    s = sel.astype(bf16)
    b1, b2, b3 = _split3(b)
    d = functools.partial(jnp.dot, preferred_element_type=f32)
    return d(s, b1) + d(s, b2) + d(s, b3)


def _dot_exact_rhs(a, sel):
    s = sel.astype(bf16)
    a1, a2, a3 = _split3(a)
    d = functools.partial(jnp.dot, preferred_element_type=f32)
    return d(a1, s) + d(a2, s) + d(a3, s)


def _transpose_exact(a, eye):
    a1, a2, a3 = _split3(a)
    e = eye.astype(bf16)
    d = lambda x: lax.dot_general(e, x, (((1,), (1,)), ((), ())), preferred_element_type=f32)
    return d(a1) + d(a2) + d(a3)


def _dot3(a, b):
    a1 = a.astype(bf16)
    a2 = (a - a1.astype(f32)).astype(bf16)
    b1 = b.astype(bf16)
    b2 = (b - b1.astype(f32)).astype(bf16)
    d = functools.partial(jnp.dot, preferred_element_type=f32)
    return d(a1, b1) + (d(a1, b2) + d(a2, b1))


def _iota(shape, axis):
    return lax.broadcasted_iota(jnp.int32, shape, axis)


def _tri(n, strict=False):
    r, c = _iota((n, n), 0), _iota((n, n), 1)
    return (r > c) if strict else (r >= c)


def _mm_kernel(x_ref, w_ref, o_ref):
    o_ref[...] = jnp.dot(x_ref[...], w_ref[...].astype(bf16),
                         preferred_element_type=f32).astype(o_ref.dtype)


def _mm(x, w, layer, n_cols, out_dtype, tn=512):
    M, K = x.shape
    tm = min(M, 1024)
    return pl.pallas_call(
        _mm_kernel,
        out_shape=jax.ShapeDtypeStruct((M, n_cols), out_dtype),
        grid=(M // tm, n_cols // tn),
        in_specs=[pl.BlockSpec((tm, K), lambda i, j: (i, 0)),
                  pl.BlockSpec((None, K, tn), lambda i, j: (layer, 0, j))],
        out_specs=pl.BlockSpec((tm, tn), lambda i, j: (i, j)),
        compiler_params=_params("parallel", "parallel"),
    )(x, w)


def _mm_hi_kernel(x_ref, w_ref, o_ref):
    o_ref[...] = _dot3(x_ref[...], w_ref[...])


def _mm_hi(x, w):
    M, K = x.shape
    n = w.shape[1]
    tm = min(M, 512)
    return pl.pallas_call(
        _mm_hi_kernel,
        out_shape=jax.ShapeDtypeStruct((M, n), f32),
        grid=(M // tm,),
        in_specs=[pl.BlockSpec((tm, K), lambda i: (i, 0)),
                  pl.BlockSpec((K, n), lambda i: (0, 0))],
        out_specs=pl.BlockSpec((tm, n), lambda i: (i, 0)),
        compiler_params=_params("parallel"),
    )(x, w)


def _ln_kernel(x_ref, y_ref, g_ref, b_ref, o_ref, ob_ref):
    v = ALPHA * x_ref[...] + y_ref[...]
    mu = jnp.mean(v, -1, keepdims=True)
    d = v - mu
    var = jnp.mean(d * d, -1, keepdims=True)
    o = d * lax.rsqrt(var + LN_EPS) * g_ref[...] + b_ref[...]
    o_ref[...] = o
    ob_ref[...] = o.astype(bf16)


def _ln_res(x, y, g, b, layer):
    T, D = x.shape
    tm = min(T, 512)
    vec = pl.BlockSpec((None, 1, D), lambda i: (layer, 0, 0))
    row = pl.BlockSpec((tm, D), lambda i: (i, 0))
    return pl.pallas_call(
        _ln_kernel,
        out_shape=(jax.ShapeDtypeStruct((T, D), f32), jax.ShapeDtypeStruct((T, D), bf16)),
        grid=(T // tm,),
        in_specs=[row, row, vec, vec],
        out_specs=(row, row),
        compiler_params=_params("parallel"),
    )(x, y, g.reshape(g.shape[0], 1, D), b.reshape(b.shape[0], 1, D))


def _conv_step(x_ref, w, hist_ref, first):
    tt = x_ref.shape[0]

    @pl.when(first)
    def _():
        hist_ref[0:SUBLANES, :] = jnp.zeros((SUBLANES, hist_ref.shape[1]), f32)

    hist_ref[SUBLANES:SUBLANES + tt, :] = x_ref[...]
    acc = None
    for k in range(CONV_W):
        term = w[k:k + 1, :] * hist_ref[pl.ds(SUBLANES - (CONV_W - 1) + k, tt), :]
        acc = term if acc is None else acc + term
    hist_ref[0:SUBLANES, :] = hist_ref[tt:tt + SUBLANES, :]
    return acc


def _scan_affine(a, u):
    n = a.shape[0]
    row = _iota(a.shape, 0)
    d = 1
    while d < n:
        keep = row >= d
        a_s = pltpu.roll(a, d, 0)
        u_s = pltpu.roll(u, d, 0)
        u = u + jnp.where(keep, a * u_s, 0.0)
        a = jnp.where(keep, a * a_s, a)
        d *= 2
    return a, u


def _rglru_kernel(gate_ref, xa_ref, cw_ref, cb_ref, wa_ref, ba_ref, wx_ref, bx_ref, lam_ref,
                  o_ref, hist_ref, h_ref):
    first = pl.program_id(2) == 0

    @pl.when(first)
    def _():
        h_ref[...] = jnp.zeros_like(h_ref)

    xc = _conv_step(xa_ref, cw_ref[...], hist_ref, first) + cb_ref[...]
    r = _sigmoid(_dot(xc, wa_ref[...]) + ba_ref[...])
    i = _sigmoid(_dot(xc, wx_ref[...]) + bx_ref[...])
    log_a = -RG_C * r * _softplus(-lam_ref[...])
    a = jnp.exp(log_a)
    u = jnp.sqrt(1.0 - jnp.exp(2.0 * log_a)) * (i * xc)
    a_cum, h = _scan_affine(a, u)
    h = h + a_cum * h_ref[0:1, :]
    tt = h.shape[0]
    h_ref[...] = jnp.broadcast_to(h[tt - 1:tt, :], h_ref.shape)
    o_ref[...] = (_gelu_tanh(gate_ref[...]) * h).astype(o_ref.dtype)


def _rglru(proj, B, S, layer, conv_w, conv_b, wa, ba, wx, bx, lam):
    T = B * S
    tt = min(S, 256)
    ns = S // tt
    gate_col0, xa_col0 = 0, W_A // BW_A
    row = lambda b, h, s: b * ns + s
    vec = lambda a: a.reshape(a.shape[0], 1, W_A)
    vspec = pl.BlockSpec((None, 1, BW_A), lambda b, h, s: (layer, 0, h))
    wspec = pl.BlockSpec((None, None, BW_A, BW_A), lambda b, h, s: (layer, h, 0, 0))
    return pl.pallas_call(
        _rglru_kernel,
        out_shape=jax.ShapeDtypeStruct((T, W_A), bf16),
        grid=(B, H_A, ns),
        in_specs=[pl.BlockSpec((tt, BW_A), lambda b, h, s: (row(b, h, s), gate_col0 + h)),
                  pl.BlockSpec((tt, BW_A), lambda b, h, s: (row(b, h, s), xa_col0 + h)),
                  pl.BlockSpec((None, CONV_W, BW_A), lambda b, h, s: (layer, 0, h)),
                  vspec, wspec, vspec, wspec, vspec, vspec],
        out_specs=pl.BlockSpec((tt, BW_A), lambda b, h, s: (row(b, h, s), h)),
        scratch_shapes=[pltpu.VMEM((SUBLANES + tt, BW_A), f32), pltpu.VMEM((SUBLANES, BW_A), f32)],
        compiler_params=_params("parallel", "parallel", "arbitrary"),
    )(proj, proj, conv_w, vec(conv_b), wa, vec(ba), wx, vec(bx), vec(lam))


def _ssd_kernel(z_ref, x_ref, b_ref, c_ref, dt_ref, cwx_ref, cwb_ref, cwc_ref, cbx_ref, cbb_ref,
                cbc_ref, dtb_ref, alog_ref, d_ref, nw_ref, o_ref,
                hx_ref, hb_ref, hc_ref, st_ref):
    g = pl.program_id(1)
    first = pl.program_id(2) == 0
    L = CHUNK

    @pl.when(first)
    def _():
        st_ref[...] = jnp.zeros_like(st_ref)

    X = _silu(_conv_step(x_ref, cwx_ref[...], hx_ref, first) + cbx_ref[...])
    Bc = _silu(_conv_step(b_ref, cwb_ref[...], hb_ref, first) + cbb_ref[...])
    Cc = _silu(_conv_step(c_ref, cwc_ref[...], hc_ref, first) + cbc_ref[...])

    pick = (_iota((H_B, HG_B), 0) == _iota((H_B, HG_B), 1) + g * HG_B).astype(f32)
    dt = _dot_exact_rhs(_softplus(dt_ref[...] + dtb_ref[...]), pick)
    a_neg = _dot_exact_rhs(jnp.broadcast_to(-jnp.exp(alog_ref[...]), (SUBLANES, H_B)), pick)[0:1]
    d_head = _dot_exact_rhs(jnp.broadcast_to(d_ref[...], (SUBLANES, H_B)), pick)
    adt = dt * a_neg
    tri = _tri(L)
    cs = _dot_exact_lhs(tri.astype(f32), adt)
    eye_h = (_iota((HG_B, HG_B), 0) == _iota((HG_B, HG_B), 1)).astype(f32)
    cs_t = _transpose_exact(cs, eye_h)
    cs_last = cs[L - 1:L, :]

    expand = (_iota((HG_B, HG_B * HD_B), 1) // HD_B == _iota((HG_B, HG_B * HD_B), 0)).astype(f32)
    dt_x = _dot_exact_rhs(dt, expand)
    ecs_x = _dot_exact_rhs(jnp.exp(cs), expand)
    dec_x = _dot_exact_rhs(jnp.exp(cs_last - cs), expand)
    cdec_x = _dot_exact_rhs(jnp.broadcast_to(jnp.exp(cs_last), (SUBLANES, HG_B)), expand)[0:1]
    d_x = _dot_exact_rhs(d_head, expand)[0:1]

    xdt = X * dt_x
    cb = _dot_nt(Cc, Bc)
    xdt_b = xdt.astype(bf16)
    left = _iota((L, 2 * HD_B), 1) < HD_B
    pieces = []
    for j in range(0, HG_B, 2):
        pair = xdt_b[:, j * HD_B:(j + 2) * HD_B]
        outs = []
        for jj in (j, j + 1):
            seg = jnp.where(tri, cs[:, jj:jj + 1] - cs_t[jj:jj + 1, :], -jnp.inf)
            outs.append(jnp.dot((cb * jnp.exp(seg)).astype(bf16), pair, preferred_element_type=f32))
        pieces.append(jnp.where(left, outs[0], outs[1]))
    y_diag = jnp.concatenate(pieces, axis=1)

    state = st_ref[...]
    y_off = ecs_x * _dot(Cc, state)
    st_ref[...] = state * cdec_x + _dot_tn(Bc, xdt * dec_x)

    y = y_diag + y_off + X * d_x
    yg = y * _silu(z_ref[...])
    out = yg * lax.rsqrt(jnp.mean(yg * yg, -1, keepdims=True) + RMS_EPS) * nw_ref[...]
    o_ref[...] = out.astype(o_ref.dtype)


def _ssd(proj, dt_tail, B, S, layer, conv_w, conv_b, dt_bias, a_log, d, norm_w):
    T = B * S
    L = CHUNK
    nc = S // L
    GW = W_B // NG_B
    row = lambda b, g, c: b * nc + c
    z_c0 = 2 * W_A // GW
    x_c0 = (2 * W_A + W_B) // GW
    b_c0 = (2 * W_A + W_B + W_B) // N_B
    c_c0 = b_c0 + NG_B
    cw_b0 = W_B // N_B
    cw_c0 = cw_b0 + NG_B
    conv_b3 = conv_b.reshape(conv_b.shape[0], 1, CONV_B)
    vec32 = lambda a: a.reshape(a.shape[0], 1, H_B)
    v32 = pl.BlockSpec((None, 1, H_B), lambda b, g, c: (layer, 0, 0))
    return pl.pallas_call(
        _ssd_kernel,
        out_shape=jax.ShapeDtypeStruct((T, W_B), bf16),
        grid=(B, NG_B, nc),
        in_specs=[pl.BlockSpec((L, GW), lambda b, g, c: (row(b, g, c), z_c0 + g)),
                  pl.BlockSpec((L, GW), lambda b, g, c: (row(b, g, c), x_c0 + g)),
                  pl.BlockSpec((L, N_B), lambda b, g, c: (row(b, g, c), b_c0 + g)),
                  pl.BlockSpec((L, N_B), lambda b, g, c: (row(b, g, c), c_c0 + g)),
                  pl.BlockSpec((L, H_B), lambda b, g, c: (row(b, g, c), 0)),
                  pl.BlockSpec((None, CONV_W, GW), lambda b, g, c: (layer, 0, g)),
                  pl.BlockSpec((None, CONV_W, N_B), lambda b, g, c: (layer, 0, cw_b0 + g)),
                  pl.BlockSpec((None, CONV_W, N_B), lambda b, g, c: (layer, 0, cw_c0 + g)),
                  pl.BlockSpec((None, 1, GW), lambda b, g, c: (layer, 0, g)),
                  pl.BlockSpec((None, 1, N_B), lambda b, g, c: (layer, 0, cw_b0 + g)),
                  pl.BlockSpec((None, 1, N_B), lambda b, g, c: (layer, 0, cw_c0 + g)),
                  v32, v32, v32,
                  pl.BlockSpec((None, 1, GW), lambda b, g, c: (layer, 0, g))],
        out_specs=pl.BlockSpec((L, GW), lambda b, g, c: (row(b, g, c), g)),
        scratch_shapes=[pltpu.VMEM((SUBLANES + L, GW), f32), pltpu.VMEM((SUBLANES + L, N_B), f32),
                        pltpu.VMEM((SUBLANES + L, N_B), f32), pltpu.VMEM((N_B, GW), f32)],
        compiler_params=_params("parallel", "parallel", "arbitrary"),
    )(proj, proj, proj, proj, dt_tail, conv_w, conv_w, conv_w, conv_b3, conv_b3, conv_b3,
      vec32(dt_bias), vec32(a_log), vec32(d), norm_w.reshape(norm_w.shape[0], 1, W_B))


def _s5_tables(a_re, a_im, log_step, b_re, b_im, c_re, c_im):
    L = L_C
    ar, ai = a_re.astype(f32), a_im.astype(f32)
    step = jnp.exp(log_step.astype(f32))[:, None]
    mag = jnp.exp(ar * step)
    lb_re, lb_im = mag * jnp.cos(ai * step), mag * jnp.sin(ai * step)
    den = ar * ar + ai * ai
    f_re = ((lb_re - 1.0) * ar + lb_im * ai) / den
    f_im = (lb_im * ar - (lb_re - 1.0) * ai) / den
    br, bi = b_re.astype(f32), b_im.astype(f32)
    bb_re = f_re[..., None] * br - f_im[..., None] * bi
    bb_im = f_re[..., None] * bi + f_im[..., None] * br
    cr, ci = c_re.astype(f32), c_im.astype(f32)

    def power(n):
        n = n.astype(f32)[None, :, None]
        m = jnp.exp(ar[:, None, :] * step[:, None, :] * n)
        ang = ai[:, None, :] * step[:, None, :] * n
        return m * jnp.cos(ang), m * jnp.sin(ang)

    j = jnp.arange(L)
    pr, pi = power(j)
    lbr = pr[..., None] * bb_re[:, None] - pi[..., None] * bb_im[:, None]
    lbi = pr[..., None] * bb_im[:, None] + pi[..., None] * bb_re[:, None]
    kern = (jnp.einsum('gop,gjpk->gjko', cr, lbr) - jnp.einsum('gop,gjpk->gjko', ci, lbi))
    s_idx, t_idx = j[:, None], j[None, :]
    lag = jnp.clip(t_idx - s_idx, 0, L - 1)
    toep = jnp.where((t_idx >= s_idx)[None, :, :, None, None], kern[:, lag], 0.0)
    toep = toep.transpose(0, 1, 3, 2, 4).reshape(G_C, L * GS_C, L * GS_C)
    rev = (L - 1) - j
    bend_re = jnp.take(lbr, rev, axis=1).transpose(0, 1, 3, 2).reshape(G_C, L * GS_C, P_C)
    bend_im = jnp.take(lbi, rev, axis=1).transpose(0, 1, 3, 2).reshape(G_C, L * GS_C, P_C)
    bend = jnp.concatenate([bend_re, bend_im], -1)
    qr, qi = power(j + 1)
    car_re = cr[:, None] * qr[:, :, None, :] - ci[:, None] * qi[:, :, None, :]
    car_im = -(cr[:, None] * qi[:, :, None, :] + ci[:, None] * qr[:, :, None, :])
    ccar = jnp.concatenate([car_re, car_im], -1).reshape(G_C, L * GS_C, 2 * P_C).transpose(0, 2, 1)
    return toep.astype(bf16), bend.astype(bf16), ccar.astype(bf16), power


def _s5_kernel(u_ref, toep_ref, bend_ref, ccar_ref, sc_ref, o_ref):
    U = u_ref[...]
    Y = jnp.dot(U, toep_ref[...], preferred_element_type=f32)
    H = jnp.dot(U, bend_ref[...], preferred_element_type=f32)
    n = H.shape[0]
    row = _iota(H.shape, 0)
    d, k = 1, 0
    while d < n:
        hs = pltpu.roll(H, d, 0)
        sw = pltpu.roll(hs, P_C, 1)
        H = H + jnp.where(row >= d, sc_ref[2 * k:2 * k + 1, :] * hs + sc_ref[2 * k + 1:2 * k + 2, :] * sw, 0.0)
        d *= 2
        k += 1
    h_prev = jnp.where(row >= 1, pltpu.roll(H, 1, 0), 0.0)
    o_ref[...] = Y + jnp.dot(h_prev.astype(bf16), ccar_ref[...], preferred_element_type=f32)


def _s5_post_kernel(y_ref, u_ref, d_ref, w_ref, b_ref, o_ref):
    y = y_ref[...] + d_ref[...] * u_ref[...]
    g = _gelu_tanh(y)
    o_ref[...] = (g * _sigmoid(_dot(g, w_ref[...]) + b_ref[...])).astype(o_ref.dtype)


def _s5(proj, B, S, layer, a_re, a_im, log_step, b_re, b_im, c_re, c_im, d, glu_w, glu_b):
    T = B * S
    L = L_C
    nch = S // L
    LW = L * GS_C
    toep, bend, ccar, power = _s5_tables(a_re[layer], a_im[layer], log_step[layer], b_re[layer],
                                         b_im[layer], c_re[layer], c_im[layer])
    nsteps = max(1, (nch - 1).bit_length())
    sr, si = power(L * (2 ** jnp.arange(nsteps)))
    scan_c = jnp.stack([jnp.concatenate([sr, sr], -1), jnp.concatenate([-si, si], -1)], 2)
    scan_c = scan_c.reshape(G_C, 2 * nsteps, 2 * P_C)
    u = proj[:, :W_C].astype(bf16).reshape(B, nch, L, G_C, GS_C)
    u = u.transpose(3, 0, 1, 2, 4).reshape(G_C, B, nch, LW)
    y = pl.pallas_call(
        _s5_kernel,
        out_shape=jax.ShapeDtypeStruct((G_C, B, nch, LW), f32),
        grid=(G_C, B),
        in_specs=[pl.BlockSpec((None, None, nch, LW), lambda g, b: (g, b, 0, 0)),
                  pl.BlockSpec((None, LW, LW), lambda g, b: (g, 0, 0)),
                  pl.BlockSpec((None, LW, 2 * P_C), lambda g, b: (g, 0, 0)),
                  pl.BlockSpec((None, 2 * P_C, LW), lambda g, b: (g, 0, 0)),
                  pl.BlockSpec((None, 2 * nsteps, 2 * P_C), lambda g, b: (g, 0, 0))],
        out_specs=pl.BlockSpec((None, None, nch, LW), lambda g, b: (g, b, 0, 0)),
        compiler_params=_params("parallel", "parallel"),
    )(u, toep, bend, ccar, scan_c)
    y = y.reshape(G_C, B, nch, L, GS_C).transpose(1, 2, 3, 0, 4).reshape(T, W_C)
    tm = min(T, 1024)
    vec = pl.BlockSpec((None, 1, W_C), lambda i: (layer, 0, 0))
    return pl.pallas_call(
        _s5_post_kernel,
        out_shape=jax.ShapeDtypeStruct((T, W_C), bf16),
        grid=(T // tm,),
        in_specs=[pl.BlockSpec((tm, W_C), lambda i: (i, 0)),
                  pl.BlockSpec((tm, W_C), lambda i: (i, 0)),
                  vec,
                  pl.BlockSpec((None, W_C, W_C), lambda i: (layer, 0, 0)),
                  vec],
        out_specs=pl.BlockSpec((tm, W_C), lambda i: (i, 0)),
        compiler_params=_params("parallel"),
    )(y, proj, d.reshape(d.shape[0], 1, W_C), glu_w, glu_b.reshape(glu_b.shape[0], 1, W_C))


def _gdn_prep_kernel(q_ref, k_ref, v_ref, qp_ref, kp_ref, vp_ref, ab_ref, cwq_ref, cwk_ref, cwv_ref,
                     alog_ref, dtb_ref, w_ref, u_ref, qd_ref, kd_ref, qk_ref, ge_ref):
    L = CHUNK
    has_prev = (pl.program_id(1) > 0).astype(f32)

    def conv(cur_ref, prev_ref, w_ref_):
        cur = cur_ref[...]
        ext = jnp.concatenate([prev_ref[...] * has_prev, cur], axis=0)
        w = w_ref_[...]
        acc = None
        for k in range(CONV_W):
            off = SUBLANES - (CONV_W - 1) + k
            term = w[k:k + 1, :] * ext[off:off + L, :]
            acc = term if acc is None else acc + term
        return _silu(acc)

    q_all = conv(q_ref, qp_ref, cwq_ref)
    k_all = conv(k_ref, kp_ref, cwk_ref)
    v_all = conv(v_ref, vp_ref, cwv_ref)
    ab = ab_ref[...]
    g_all = -jnp.exp(alog_ref[...]) * _softplus(ab[:, 0:H_D] + dtb_ref[...])
    beta_all = _sigmoid(ab[:, H_D:2 * H_D])
    tri = _tri(L)
    tri_s = _tri(L, strict=True)
    gcs_all = _dot_exact_lhs(tri.astype(f32), g_all)
    eye_h = (_iota((H_D, H_D), 0) == _iota((H_D, H_D), 1)).astype(f32)
    gcs_t = _transpose_exact(gcs_all, eye_h)
    eye = (_iota((L, L), 0) == _iota((L, L), 1)).astype(f32)
    ge_ref[...] = jnp.concatenate([jnp.exp(gcs_all), jnp.zeros((L, LANES - H_D), f32)], axis=1)

    for h in range(H_D):
        sl = slice(h * DK_D, (h + 1) * DK_D)
        q = q_all[:, sl]
        k = k_all[:, sl]
        v = v_all[:, sl]
        q = q * lax.rsqrt(jnp.sum(q * q, -1, keepdims=True) + 1e-6) * (DK_D ** -0.5)
        k = k * lax.rsqrt(jnp.sum(k * k, -1, keepdims=True) + 1e-6)
        beta = beta_all[:, h:h + 1]
        gcs = gcs_all[:, h:h + 1]
        decay = jnp.exp(jnp.where(tri, gcs - gcs_t[h:h + 1, :], -jnp.inf))
        kb = k * beta
        kk = _dot_nt(kb, k) * decay
        neg = jnp.where(tri_s, -kk, 0.0)
        inv = eye + neg
        pw = neg
        for _ in range(int(math.log2(L)) - 1):
            pw = _dot(pw, pw)
            inv = inv + _dot(inv, pw)
        eg = jnp.exp(gcs)
        u_ref[:, sl] = _dot(inv, v * beta)
        w_ref[:, sl] = _dot(inv, kb * eg)
        qk_ref[:, h * L:(h + 1) * L] = jnp.where(tri, _dot_nt(q, k) * decay, 0.0)
        qd_ref[:, sl] = q * eg
        kd_ref[:, sl] = k * jnp.exp(gcs[L - 1:L, :] - gcs)


def _gdn_scan_kernel(w_ref, u_ref, qd_ref, kd_ref, qk_ref, ge_ref, z_ref, nw_ref, o_ref, st_ref):
    L = CHUNK

    @pl.when(pl.program_id(1) == 0)
    def _():
        st_ref[...] = jnp.zeros_like(st_ref)

    nw = nw_ref[...]
    for h in range(H_D):
        sl = slice(h * DK_D, (h + 1) * DK_D)
        state = st_ref[h]
        v_new = u_ref[:, sl] - _dot(w_ref[:, sl], state)
        o = _dot(qd_ref[:, sl], state) + _dot(qk_ref[:, h * L:(h + 1) * L], v_new)
        g_last = ge_ref[L - 1:L, h:h + 1]
        st_ref[h] = state * g_last + _dot_tn(kd_ref[:, sl], v_new)
        o = o * lax.rsqrt(jnp.mean(o * o, -1, keepdims=True) + RMS_EPS) * nw
        o_ref[:, sl] = (o * _silu(z_ref[:, sl])).astype(o_ref.dtype)


def _gdn(proj, ab_tail, B, S, layer, conv_w, a_log, dt_bias, norm_w):
    T = B * S
    L = CHUNK
    nc = S // L
    per8 = L // SUBLANES
    q_c0 = W_C // W_D
    row = lambda b, c: (b * nc + c, 0)

    def cur(cb):
        return pl.BlockSpec((L, W_D), lambda b, c: (b * nc + c, cb))

    def prev(cb):
        return pl.BlockSpec((SUBLANES, W_D), lambda b, c: (jnp.maximum((b * nc + c) * per8 - 1, 0), cb))

    def cw(cb):
        return pl.BlockSpec((None, CONV_W, W_D), lambda b, c: (layer, 0, cb))

    v8 = pl.BlockSpec((None, 1, H_D), lambda b, c: (layer, 0, 0))
    wide = pl.BlockSpec((L, W_D), row)
    shp = lambda n: jax.ShapeDtypeStruct((T, n), f32)
    w_c, u_c, q_dec, k_dec, qk, gexp = pl.pallas_call(
        _gdn_prep_kernel,
        out_shape=(shp(W_D), shp(W_D), shp(W_D), shp(W_D), shp(H_D * L), shp(LANES)),
        grid=(B, nc),
        in_specs=[cur(q_c0), cur(q_c0 + 1), cur(q_c0 + 2), prev(q_c0), prev(q_c0 + 1), prev(q_c0 + 2),
                  pl.BlockSpec((L, 2 * H_D), row), cw(0), cw(1), cw(2), v8, v8],
        out_specs=(wide, wide, wide, wide, pl.BlockSpec((L, H_D * L), row),
                   pl.BlockSpec((L, LANES), row)),
        compiler_params=_params("parallel", "parallel"),
    )(proj, proj, proj, proj, proj, proj, ab_tail, conv_w, conv_w, conv_w,
      a_log.reshape(a_log.shape[0], 1, H_D), dt_bias.reshape(dt_bias.shape[0], 1, H_D))
    z_c0 = (W_C + QKV_D) // W_D
    return pl.pallas_call(
        _gdn_scan_kernel,
        out_shape=jax.ShapeDtypeStruct((T, W_D), bf16),
        grid=(B, nc),
        in_specs=[wide, wide, wide, wide, pl.BlockSpec((L, H_D * L), row),
                  pl.BlockSpec((L, LANES), row),
                  pl.BlockSpec((L, W_D), lambda b, c: (b * nc + c, z_c0)),
                  pl.BlockSpec((None, 1, DV_D), lambda b, c: (layer, 0, 0))],
        out_specs=wide,
        scratch_shapes=[pltpu.VMEM((H_D, DK_D, DV_D), f32)],
        compiler_params=_params("parallel", "arbitrary"),
    )(w_c, u_c, q_dec, k_dec, qk, gexp, proj, norm_w.reshape(norm_w.shape[0], 1, DV_D))


def _xattn_kernel(q_ref, k_ref, v_ref, o_ref):
    s = _dot_nt(q_ref[...], k_ref[...]) * (HD_X ** -0.5)
    m = jnp.max(s, -1, keepdims=True)
    p = jnp.exp(s - m)
    p = p / jnp.sum(p, -1, keepdims=True)
    o_ref[...] = jnp.dot(p.astype(bf16), v_ref[...], preferred_element_type=f32).astype(o_ref.dtype)


def _xattn(q, kv, B, S, M):
    T = B * S
    tq = min(S, 1024)
    nq = S // tq
    return pl.pallas_call(
        _xattn_kernel,
        out_shape=jax.ShapeDtypeStruct((T, D_MODEL), bf16),
        grid=(B, nq, H_X),
        in_specs=[pl.BlockSpec((tq, HD_X), lambda b, i, h: (b * nq + i, h)),
                  pl.BlockSpec((M, HD_X), lambda b, i, h: (b, h)),
                  pl.BlockSpec((M, HD_X), lambda b, i, h: (b, H_X + h))],
        out_specs=pl.BlockSpec((tq, HD_X), lambda b, i, h: (b * nq + i, h)),
        compiler_params=_params("parallel", "parallel", "parallel"),
    )(q, kv, kv)


def _route_kernel(x_ref, w_ref, b_ref, o_ref):
    logits = _dot3(x_ref[...], w_ref[...]) + b_ref[...]
    lane = _iota(logits.shape, 1)
    neg = -jnp.inf
    big = jnp.int32(LANES)
    is_g = (lane >= N_EXP) & (lane < N_EXP + NG_E)
    gl = jnp.where(is_g, logits, neg)
    gmax = jnp.max(gl, -1, keepdims=True)
    g_lane = jnp.min(jnp.where(gl == gmax, lane, big), -1, keepdims=True)
    g_prob = 1.0 / jnp.sum(jnp.where(is_g, jnp.exp(gl - gmax), 0.0), -1, keepdims=True)
    e0 = (g_lane - N_EXP) * E_PER
    sel = jnp.where((lane >= e0) & (lane < e0 + E_PER), logits, neg)
    m1 = jnp.max(sel, -1, keepdims=True)
    i1 = jnp.min(jnp.where(sel == m1, lane, big), -1, keepdims=True)
    sel2 = jnp.where(lane == i1, neg, sel)
    m2 = jnp.max(sel2, -1, keepdims=True)
    i2 = jnp.min(jnp.where(sel2 == m2, lane, big), -1, keepdims=True)
    e2 = jnp.exp(m2 - m1)
    w1 = g_prob / (1.0 + e2)
    w2 = g_prob * e2 / (1.0 + e2)
    o_ref[...] = jnp.where(lane == i1, w1, 0.0) + jnp.where(lane == i2, w2, 0.0)


def _route(x, w_group, b_group, w_expert, b_expert, layer):
    T, D = x.shape
    pad = LANES - N_EXP - NG_E
    w = jnp.concatenate([w_expert[layer], w_group[layer], jnp.zeros((D, pad), f32)], 1)
    b = jnp.concatenate([b_expert[layer], b_group[layer], jnp.zeros((pad,), f32)])[None, :]
    tm = min(T, 512)
    return pl.pallas_call(
        _route_kernel,
        out_shape=jax.ShapeDtypeStruct((T, LANES), f32),
        grid=(T // tm,),
        in_specs=[pl.BlockSpec((tm, D), lambda i: (i, 0)),
                  pl.BlockSpec((D, LANES), lambda i: (0, 0)),
                  pl.BlockSpec((1, LANES), lambda i: (0, 0))],
        out_specs=pl.BlockSpec((tm, LANES), lambda i: (i, 0)),
        compiler_params=_params("parallel"),
    )(x, w, b)


def _moe_kernel(x_ref, c_ref, wg_ref, wu_ref, wd_ref, o_ref):
    e = pl.program_id(1)
    x = x_ref[...]
    h = _silu(jnp.dot(x, wg_ref[...].astype(bf16), preferred_element_type=f32)) * \
        jnp.dot(x, wu_ref[...].astype(bf16), preferred_element_type=f32)
    c = c_ref[...]
    ce = jnp.sum(jnp.where(_iota(c.shape, 1) == e, c, 0.0), -1, keepdims=True)
    y = _dot(h * ce, wd_ref[...])

    @pl.when(e == 0)
    def _():
        o_ref[...] = y

    @pl.when(e > 0)
    def _():
        o_ref[...] += y


def _moe(xb, comb, w_gate, w_up, w_down, layer):
    T, D = xb.shape
    tm = min(T, 1024)
    return pl.pallas_call(
        _moe_kernel,
        out_shape=jax.ShapeDtypeStruct((T, D), f32),
        grid=(T // tm, N_EXP),
        in_specs=[pl.BlockSpec((tm, D), lambda i, e: (i, 0)),
                  pl.BlockSpec((tm, LANES), lambda i, e: (i, 0)),
                  pl.BlockSpec((None, None, D, D_E), lambda i, e: (layer, e, 0, 0)),
                  pl.BlockSpec((None, None, D, D_E), lambda i, e: (layer, e, 0, 0)),
                  pl.BlockSpec((None, None, D_E, D), lambda i, e: (layer, e, 0, 0))],
        out_specs=pl.BlockSpec((tm, D), lambda i, e: (i, 0)),
        compiler_params=_params("parallel", "arbitrary"),
    )(xb, comb, w_gate, w_up, w_down)


def kernel(x, mem, ab_w_in, rg_conv_w, rg_conv_b, rg_wa, rg_ba, rg_wx, rg_bx, rg_lam, ssd_conv_w, ssd_conv_b, ssd_dt_bias, ssd_a_log, ssd_d, ssd_norm_w, ab_w_out, cd_w_in, s5_a_re, s5_a_im, s5_log_step, s5_b_re, s5_b_im, s5_c_re, s5_c_im, s5_d, s5_glu_w, s5_glu_b, dn_conv_w, dn_a_log, dn_dt_bias, dn_norm_w, cd_w_out, xa_w_q, xa_w_kv, xa_w_o, moe_w_group, moe_b_group, moe_w_expert, moe_b_expert, moe_w_gate, moe_w_up, moe_w_down, ln1_g, ln1_b, ln2_g, ln2_b, ln3_g, ln3_b):
    B, S, D = x.shape
    M = mem.shape[1]
    T = B * S
    xf = x.reshape(T, D)
    xb = xf.astype(bf16)
    memb = mem.reshape(B * M, D).astype(bf16)
    for l in range(DEPTH):
        i = l // 2
        if l % 2 == 0:
            proj = _mm(xb, ab_w_in, i, MAIN_AB, f32)
            dt_tail = _mm_hi(xf, ab_w_in[i][:, MAIN_AB:])
            ya = _rglru(proj, B, S, i, rg_conv_w, rg_conv_b, rg_wa, rg_ba, rg_wx, rg_bx, rg_lam)
            yb = _ssd(proj, dt_tail, B, S, i, ssd_conv_w, ssd_conv_b, ssd_dt_bias, ssd_a_log, ssd_d,
                      ssd_norm_w)
            mix = _mm(jnp.concatenate([ya, yb], -1), ab_w_out, i, D, f32)
        else:
            proj = _mm(xb, cd_w_in, i, MAIN_CD, f32)
            ab_tail = _mm_hi(xf, cd_w_in[i][:, MAIN_CD:])
            yc = _s5(proj, B, S, i, s5_a_re, s5_a_im, s5_log_step, s5_b_re, s5_b_im, s5_c_re, s5_c_im,
                     s5_d, s5_glu_w, s5_glu_b)
            yd = _gdn(proj, ab_tail, B, S, i, dn_conv_w, dn_a_log, dn_dt_bias, dn_norm_w)
            mix = _mm(jnp.concatenate([yc, yd], -1), cd_w_out, i, D, f32)
        xf, xb = _ln_res(xf, mix, ln1_g, ln1_b, l)
        q = _mm(xb, xa_w_q, l, D, bf16)
        kv = _mm(memb, xa_w_kv, l, 2 * D, bf16)
        att = _xattn(q, kv, B, S, M)
        xf, xb = _ln_res(xf, _mm(att, xa_w_o, l, D, f32), ln2_g, ln2_b, l)
        comb = _route(xf, moe_w_group, moe_b_group, moe_w_expert, moe_b_expert, l)
        xf, xb = _ln_res(xf, _moe(xb, comb, moe_w_gate, moe_w_up, moe_w_down, l), ln3_g, ln3_b, l)
    return xf.reshape(B, S, D)
```

```python
import functools
import math

import jax
import jax.numpy as jnp
from jax import lax
from jax.experimental import pallas as pl
from jax.experimental.pallas import tpu as pltpu

f32 = jnp.float32
bf16 = jnp.bfloat16

D_MODEL = 2048
DEPTH = 2
CHUNK = 64
CONV_W = 4
ALPHA = (2 * DEPTH) ** 0.25
LN_EPS = 1e-5
RMS_EPS = 1e-6
W_A = D_MODEL // 2
H_A = 8
BW_A = W_A // H_A
RG_C = 8.0
W_B = D_MODEL
HD_B = 64
H_B = W_B // HD_B
NG_B = 2
N_B = 128
HG_B = H_B // NG_B
CONV_B = W_B + 2 * NG_B * N_B
MAIN_AB = 2 * W_A + W_B + CONV_B
W_C = D_MODEL // 2
GS_C = 16
G_C = W_C // GS_C
P_C = 64
L_C = 16
H_D = 8
DK_D = D_MODEL // 16
DV_D = D_MODEL // 16
W_D = H_D * DV_D
QKV_D = 2 * H_D * DK_D + W_D
MAIN_CD = W_C + QKV_D + W_D
H_X = 4
HD_X = D_MODEL // H_X
NG_E = 4
E_PER = 8
N_EXP = NG_E * E_PER
TOPK_IN = 2
D_E = D_MODEL // 8

LANES = 128
SUBLANES = 8
VMEM_LIMIT = 56 * 1024 * 1024


def _params(*sem):
    return pltpu.CompilerParams(dimension_semantics=sem, vmem_limit_bytes=VMEM_LIMIT)


def _sigmoid(x):
    return 1.0 / (1.0 + jnp.exp(-x))


def _silu(x):
    return x * _sigmoid(x)


def _softplus(x):
    return jnp.maximum(x, 0.0) + jnp.log(1.0 + jnp.exp(-jnp.abs(x)))


def _gelu_tanh(x):
    return 0.5 * x * (1.0 + jnp.tanh(math.sqrt(2.0 / math.pi) * (x + 0.044715 * (x * x * x))))


def _dot(a, b):
    return jnp.dot(a.astype(bf16), b.astype(bf16), preferred_element_type=f32)


def _dot_nt(a, b):
    return lax.dot_general(a.astype(bf16), b.astype(bf16), (((1,), (1,)), ((), ())),
                           preferred_element_type=f32)


def _dot_tn(a, b):
    return lax.dot_general(a.astype(bf16), b.astype(bf16), (((0,), (0,)), ((), ())),
                           preferred_element_type=f32)


def _split3(a):
    hi = a.astype(bf16)
    r = a - hi.astype(f32)
    mid = r.astype(bf16)
    lo = (r - mid.astype(f32)).astype(bf16)
    return hi, mid, lo


def _dot_exact_lhs(sel, b):
    s = sel.astype(bf16)
    b1, b2, b3 = _split3(b)
    d = functools.partial(jnp.dot, preferred_element_type=f32)
    return d(s, b1) + d(s, b2) + d(s, b3)


def _dot_exact_rhs(a, sel):
    s = sel.astype(bf16)
    a1, a2, a3 = _split3(a)
    d = functools.partial(jnp.dot, preferred_element_type=f32)
    return d(a1, s) + d(a2, s) + d(a3, s)


def _transpose_exact(a, eye):
    a1, a2, a3 = _split3(a)
    e = eye.astype(bf16)
    d = lambda x: lax.dot_general(e, x, (((1,), (1,)), ((), ())), preferred_element_type=f32)
    return d(a1) + d(a2) + d(a3)


def _dot3(a, b):
    a1 = a.astype(bf16)
    a2 = (a - a1.astype(f32)).astype(bf16)
    b1 = b.astype(bf16)
    b2 = (b - b1.astype(f32)).astype(bf16)
    d = functools.partial(jnp.dot, preferred_element_type=f32)
    return d(a1, b1) + (d(a1, b2) + d(a2, b1))


def _iota(shape, axis):
    return lax.broadcasted_iota(jnp.int32, shape, axis)


def _tri(n, strict=False):
    r, c = _iota((n, n), 0), _iota((n, n), 1)
    return (r > c) if strict else (r >= c)


def _mm_kernel(x_ref, w_ref, o_ref):
    o_ref[...] = jnp.dot(x_ref[...], w_ref[...].astype(bf16),
                         preferred_element_type=f32).astype(o_ref.dtype)


def _mm(x, w, layer, n_cols, out_dtype, tn=512):
    M, K = x.shape
    tm = min(M, 1024)
    return pl.pallas_call(
        _mm_kernel,
        out_shape=jax.ShapeDtypeStruct((M, n_cols), out_dtype),
        grid=(M // tm, n_cols // tn),
        in_specs=[pl.BlockSpec((tm, K), lambda i, j: (i, 0)),
                  pl.BlockSpec((None, K, tn), lambda i, j: (layer, 0, j))],
        out_specs=pl.BlockSpec((tm, tn), lambda i, j: (i, j)),
        compiler_params=_params("parallel", "parallel"),
    )(x, w)


def _mm_hi_kernel(x_ref, w_ref, o_ref):
    o_ref[...] = _dot3(x_ref[...], w_ref[...])


def _mm_hi(x, w):
    M, K = x.shape
    n = w.shape[1]
    tm = min(M, 512)
    return pl.pallas_call(
        _mm_hi_kernel,
        out_shape=jax.ShapeDtypeStruct((M, n), f32),
        grid=(M // tm,),
        in_specs=[pl.BlockSpec((tm, K), lambda i: (i, 0)),
                  pl.BlockSpec((K, n), lambda i: (0, 0))],
        out_specs=pl.BlockSpec((tm, n), lambda i: (i, 0)),
        compiler_params=_params("parallel"),
    )(x, w)


def _layer_norm(v, g, b):
    mu = jnp.mean(v, -1, keepdims=True)
    d = v - mu
    var = jnp.mean(d * d, -1, keepdims=True)
    return d * lax.rsqrt(var + LN_EPS) * g + b


K_CHUNK = 1024


def _mm_ln_kernel(*refs, chunks):
    n_parts = max(p for p, _ in chunks) + 1
    part_refs = refs[:n_parts]
    w_ref, x_ref, g_ref, b_ref, o_ref, ob_ref, acc_ref = refs[n_parts:]
    k = pl.program_id(1)
    w = w_ref[...].astype(bf16)
    for idx, (p, off) in enumerate(chunks):
        @pl.when(k == idx)
        def _(p=p, off=off, idx=idx):
            c = jnp.dot(part_refs[p][:, off:off + K_CHUNK], w, preferred_element_type=f32)
            if idx == 0:
                acc_ref[...] = c
            else:
                acc_ref[...] += c

    @pl.when(k == len(chunks) - 1)
    def _():
        o = _layer_norm(ALPHA * x_ref[...] + acc_ref[...], g_ref[...], b_ref[...])
        o_ref[...] = o
        ob_ref[...] = o.astype(bf16)


def _mm_ln(parts, w, w_layer, x, g, b, layer):
    T, D = x.shape
    tm = min(T, 512)
    chunks = tuple((p, off) for p, a in enumerate(parts) for off in range(0, a.shape[1], K_CHUNK))
    vec = pl.BlockSpec((None, 1, D), lambda i, k: (layer, 0, 0))
    row = pl.BlockSpec((tm, D), lambda i, k: (i, 0))
    return pl.pallas_call(
        functools.partial(_mm_ln_kernel, chunks=chunks),
        out_shape=(jax.ShapeDtypeStruct((T, D), f32), jax.ShapeDtypeStruct((T, D), bf16)),
        grid=(T // tm, len(chunks)),
        in_specs=[pl.BlockSpec((tm, a.shape[1]), lambda i, k: (i, 0)) for a in parts]
        + [pl.BlockSpec((None, K_CHUNK, D), lambda i, k: (w_layer, k, 0)), row, vec, vec],
        out_specs=(row, row),
        scratch_shapes=[pltpu.VMEM((tm, D), f32)],
        compiler_params=_params("parallel", "arbitrary"),
    )(*parts, w, x, g.reshape(g.shape[0], 1, D), b.reshape(b.shape[0], 1, D))


def _conv_step(x_ref, w, hist_ref, first):
    tt = x_ref.shape[0]

    @pl.when(first)
    def _():
        hist_ref[0:SUBLANES, :] = jnp.zeros((SUBLANES, hist_ref.shape[1]), f32)

    hist_ref[SUBLANES:SUBLANES + tt, :] = x_ref[...]
    acc = None
    for k in range(CONV_W):
        term = w[k:k + 1, :] * hist_ref[pl.ds(SUBLANES - (CONV_W - 1) + k, tt), :]
        acc = term if acc is None else acc + term
    hist_ref[0:SUBLANES, :] = hist_ref[tt:tt + SUBLANES, :]
    return acc


def _scan_affine(a, u):
    n = a.shape[0]
    row = _iota(a.shape, 0)
    d = 1
    while d < n:
        keep = row >= d
        a_s = pltpu.roll(a, d, 0)
        u_s = pltpu.roll(u, d, 0)
        u = u + jnp.where(keep, a * u_s, 0.0)
        a = jnp.where(keep, a * a_s, a)
        d *= 2
    return a, u


def _rglru_kernel(gate_ref, xa_ref, cw_ref, cb_ref, wa_ref, ba_ref, wx_ref, bx_ref, lam_ref,
                  o_ref, hist_ref, h_ref):
    first = pl.program_id(2) == 0

    @pl.when(first)
    def _():
        h_ref[...] = jnp.zeros_like(h_ref)

    xc = _conv_step(xa_ref, cw_ref[...], hist_ref, first) + cb_ref[...]
    r = _sigmoid(_dot(xc, wa_ref[...]) + ba_ref[...])
    i = _sigmoid(_dot(xc, wx_ref[...]) + bx_ref[...])
    log_a = -RG_C * r * _softplus(-lam_ref[...])
    a = jnp.exp(log_a)
    u = jnp.sqrt(1.0 - jnp.exp(2.0 * log_a)) * (i * xc)
    a_cum, h = _scan_affine(a, u)
    h = h + a_cum * h_ref[0:1, :]
    tt = h.shape[0]
    h_ref[...] = jnp.broadcast_to(h[tt - 1:tt, :], h_ref.shape)
    o_ref[...] = (_gelu_tanh(gate_ref[...]) * h).astype(o_ref.dtype)


def _rglru(proj, B, S, layer, conv_w, conv_b, wa, ba, wx, bx, lam):
    T = B * S
    tt = min(S, 256)
    ns = S // tt
    gate_col0, xa_col0 = 0, W_A // BW_A
    row = lambda b, h, s: b * ns + s
    vec = lambda a: a.reshape(a.shape[0], 1, W_A)
    vspec = pl.BlockSpec((None, 1, BW_A), lambda b, h, s: (layer, 0, h))
    wspec = pl.BlockSpec((None, None, BW_A, BW_A), lambda b, h, s: (layer, h, 0, 0))
    return pl.pallas_call(
        _rglru_kernel,
        out_shape=jax.ShapeDtypeStruct((T, W_A), bf16),
        grid=(B, H_A, ns),
        in_specs=[pl.BlockSpec((tt, BW_A), lambda b, h, s: (row(b, h, s), gate_col0 + h)),
                  pl.BlockSpec((tt, BW_A), lambda b, h, s: (row(b, h, s), xa_col0 + h)),
                  pl.BlockSpec((None, CONV_W, BW_A), lambda b, h, s: (layer, 0, h)),
                  vspec, wspec, vspec, wspec, vspec, vspec],
        out_specs=pl.BlockSpec((tt, BW_A), lambda b, h, s: (row(b, h, s), h)),
        scratch_shapes=[pltpu.VMEM((SUBLANES + tt, BW_A), f32), pltpu.VMEM((SUBLANES, BW_A), f32)],
        compiler_params=_params("parallel", "parallel", "arbitrary"),
    )(proj, proj, conv_w, vec(conv_b), wa, vec(ba), wx, vec(bx), vec(lam))


def _ssd_kernel(z_ref, x_ref, b_ref, c_ref, dt_ref, cwx_ref, cwb_ref, cwc_ref, cbx_ref, cbb_ref,
                cbc_ref, dtb_ref, alog_ref, d_ref, nw_ref, o_ref,
                hx_ref, hb_ref, hc_ref, st_ref):
    g = pl.program_id(1)
    first = pl.program_id(2) == 0
    L = CHUNK

    @pl.when(first)
    def _():
        st_ref[...] = jnp.zeros_like(st_ref)

    X = _silu(_conv_step(x_ref, cwx_ref[...], hx_ref, first) + cbx_ref[...])
    Bc = _silu(_conv_step(b_ref, cwb_ref[...], hb_ref, first) + cbb_ref[...])
    Cc = _silu(_conv_step(c_ref, cwc_ref[...], hc_ref, first) + cbc_ref[...])

    pick = (_iota((H_B, HG_B), 0) == _iota((H_B, HG_B), 1) + g * HG_B).astype(f32)
    dt = _dot_exact_rhs(_softplus(dt_ref[...] + dtb_ref[...]), pick)
    a_neg = _dot_exact_rhs(jnp.broadcast_to(-jnp.exp(alog_ref[...]), (SUBLANES, H_B)), pick)[0:1]
    d_head = _dot_exact_rhs(jnp.broadcast_to(d_ref[...], (SUBLANES, H_B)), pick)
    adt = dt * a_neg
    tri = _tri(L)
    cs = _dot_exact_lhs(tri.astype(f32), adt)
    eye_h = (_iota((HG_B, HG_B), 0) == _iota((HG_B, HG_B), 1)).astype(f32)
    cs_t = _transpose_exact(cs, eye_h)
    cs_last = cs[L - 1:L, :]

    expand = (_iota((HG_B, HG_B * HD_B), 1) // HD_B == _iota((HG_B, HG_B * HD_B), 0)).astype(f32)
    dt_x = _dot_exact_rhs(dt, expand)
    ecs_x = _dot_exact_rhs(jnp.exp(cs), expand)
    dec_x = _dot_exact_rhs(jnp.exp(cs_last - cs), expand)
    cdec_x = _dot_exact_rhs(jnp.broadcast_to(jnp.exp(cs_last), (SUBLANES, HG_B)), expand)[0:1]
    d_x = _dot_exact_rhs(d_head, expand)[0:1]

    xdt = X * dt_x
    cb = _dot_nt(Cc, Bc)
    xdt_b = xdt.astype(bf16)
    left = _iota((L, 2 * HD_B), 1) < HD_B
    pieces = []
    for j in range(0, HG_B, 2):
        pair = xdt_b[:, j * HD_B:(j + 2) * HD_B]
        outs = []
        for jj in (j, j + 1):
            seg = jnp.where(tri, cs[:, jj:jj + 1] - cs_t[jj:jj + 1, :], -jnp.inf)
            outs.append(jnp.dot((cb * jnp.exp(seg)).astype(bf16), pair, preferred_element_type=f32))
        pieces.append(jnp.where(left, outs[0], outs[1]))
    y_diag = jnp.concatenate(pieces, axis=1)

    state = st_ref[...]
    y_off = ecs_x * _dot(Cc, state)
    st_ref[...] = state * cdec_x + _dot_tn(Bc, xdt * dec_x)

    y = y_diag + y_off + X * d_x
    yg = y * _silu(z_ref[...])
    out = yg * lax.rsqrt(jnp.mean(yg * yg, -1, keepdims=True) + RMS_EPS) * nw_ref[...]
    o_ref[...] = out.astype(o_ref.dtype)


def _ssd(proj, dt_tail, B, S, layer, conv_w, conv_b, dt_bias, a_log, d, norm_w):
    T = B * S
    L = CHUNK
    nc = S // L
    GW = W_B // NG_B
    row = lambda b, g, c: b * nc + c
    z_c0 = 2 * W_A // GW
    x_c0 = (2 * W_A + W_B) // GW
    b_c0 = (2 * W_A + W_B + W_B) // N_B
    c_c0 = b_c0 + NG_B
    cw_b0 = W_B // N_B
    cw_c0 = cw_b0 + NG_B
    conv_b3 = conv_b.reshape(conv_b.shape[0], 1, CONV_B)
    vec32 = lambda a: a.reshape(a.shape[0], 1, H_B)
    v32 = pl.BlockSpec((None, 1, H_B), lambda b, g, c: (layer, 0, 0))
    return pl.pallas_call(
        _ssd_kernel,
        out_shape=jax.ShapeDtypeStruct((T, W_B), bf16),
        grid=(B, NG_B, nc),
        in_specs=[pl.BlockSpec((L, GW), lambda b, g, c: (row(b, g, c), z_c0 + g)),
                  pl.BlockSpec((L, GW), lambda b, g, c: (row(b, g, c), x_c0 + g)),
                  pl.BlockSpec((L, N_B), lambda b, g, c: (row(b, g, c), b_c0 + g)),
                  pl.BlockSpec((L, N_B), lambda b, g, c: (row(b, g, c), c_c0 + g)),
                  pl.BlockSpec((L, H_B), lambda b, g, c: (row(b, g, c), 0)),
                  pl.BlockSpec((None, CONV_W, GW), lambda b, g, c: (layer, 0, g)),
                  pl.BlockSpec((None, CONV_W, N_B), lambda b, g, c: (layer, 0, cw_b0 + g)),
                  pl.BlockSpec((None, CONV_W, N_B), lambda b, g, c: (layer, 0, cw_c0 + g)),
                  pl.BlockSpec((None, 1, GW), lambda b, g, c: (layer, 0, g)),
                  pl.BlockSpec((None, 1, N_B), lambda b, g, c: (layer, 0, cw_b0 + g)),
                  pl.BlockSpec((None, 1, N_B), lambda b, g, c: (layer, 0, cw_c0 + g)),
                  v32, v32, v32,
                  pl.BlockSpec((None, 1, GW), lambda b, g, c: (layer, 0, g))],
        out_specs=pl.BlockSpec((L, GW), lambda b, g, c: (row(b, g, c), g)),
        scratch_shapes=[pltpu.VMEM((SUBLANES + L, GW), f32), pltpu.VMEM((SUBLANES + L, N_B), f32),
                        pltpu.VMEM((SUBLANES + L, N_B), f32), pltpu.VMEM((N_B, GW), f32)],
        compiler_params=_params("parallel", "parallel", "arbitrary"),
    )(proj, proj, proj, proj, dt_tail, conv_w, conv_w, conv_w, conv_b3, conv_b3, conv_b3,
      vec32(dt_bias), vec32(a_log), vec32(d), norm_w.reshape(norm_w.shape[0], 1, W_B))


def _s5_tables(a_re, a_im, log_step, b_re, b_im, c_re, c_im):
    L = L_C
    ar, ai = a_re.astype(f32), a_im.astype(f32)
    step = jnp.exp(log_step.astype(f32))[:, None]
    mag = jnp.exp(ar * step)
    lb_re, lb_im = mag * jnp.cos(ai * step), mag * jnp.sin(ai * step)
    den = ar * ar + ai * ai
    f_re = ((lb_re - 1.0) * ar + lb_im * ai) / den
    f_im = (lb_im * ar - (lb_re - 1.0) * ai) / den
    br, bi = b_re.astype(f32), b_im.astype(f32)
    bb_re = f_re[..., None] * br - f_im[..., None] * bi
    bb_im = f_re[..., None] * bi + f_im[..., None] * br
    cr, ci = c_re.astype(f32), c_im.astype(f32)

    def power(n):
        n = n.astype(f32)[None, :, None]
        m = jnp.exp(ar[:, None, :] * step[:, None, :] * n)
        ang = ai[:, None, :] * step[:, None, :] * n
        return m * jnp.cos(ang), m * jnp.sin(ang)

    j = jnp.arange(L)
    pr, pi = power(j)
    lbr = pr[..., None] * bb_re[:, None] - pi[..., None] * bb_im[:, None]
    lbi = pr[..., None] * bb_im[:, None] + pi[..., None] * bb_re[:, None]
    kern = (jnp.einsum('gop,gjpk->gjko', cr, lbr) - jnp.einsum('gop,gjpk->gjko', ci, lbi))
    s_idx, t_idx = j[:, None], j[None, :]
    lag = jnp.clip(t_idx - s_idx, 0, L - 1)
    toep = jnp.where((t_idx >= s_idx)[None, :, :, None, None], kern[:, lag], 0.0)
    toep = toep.transpose(0, 1, 3, 2, 4).reshape(G_C, L * GS_C, L * GS_C)
    rev = (L - 1) - j
    bend_re = jnp.take(lbr, rev, axis=1).transpose(0, 1, 3, 2).reshape(G_C, L * GS_C, P_C)
    bend_im = jnp.take(lbi, rev, axis=1).transpose(0, 1, 3, 2).reshape(G_C, L * GS_C, P_C)
    bend = jnp.concatenate([bend_re, bend_im], -1)
    qr, qi = power(j + 1)
    car_re = cr[:, None] * qr[:, :, None, :] - ci[:, None] * qi[:, :, None, :]
    car_im = -(cr[:, None] * qi[:, :, None, :] + ci[:, None] * qr[:, :, None, :])
    ccar = jnp.concatenate([car_re, car_im], -1).reshape(G_C, L * GS_C, 2 * P_C).transpose(0, 2, 1)
    return toep.astype(bf16), bend.astype(bf16), ccar.astype(bf16), power


def _s5_kernel(u_ref, toep_ref, bend_ref, ccar_ref, sc_ref, o_ref):
    U = jnp.concatenate([u_ref[:, l, :].astype(bf16) for l in range(L_C)], axis=1)
    Y = jnp.dot(U, toep_ref[...], preferred_element_type=f32)
    H_all = jnp.dot(U, bend_ref[...], preferred_element_type=f32)
    n = H_all.shape[0]
    row = _iota((n, 2 * P_C), 0)
    prev = []
    for g in range(LANES // GS_C):
        sl = slice(g * 2 * P_C, (g + 1) * 2 * P_C)
        H = H_all[:, sl]
        d, k = 1, 0
        while d < n:
            hs = pltpu.roll(H, d, 0)
            sw = pltpu.roll(hs, P_C, 1)
            H = H + jnp.where(row >= d, sc_ref[2 * k:2 * k + 1, sl] * hs + sc_ref[2 * k + 1:2 * k + 2, sl] * sw, 0.0)
            d *= 2
            k += 1
        prev.append(jnp.where(row >= 1, pltpu.roll(H, 1, 0), 0.0).astype(bf16))
    Y = Y + jnp.dot(jnp.concatenate(prev, axis=1), ccar_ref[...], preferred_element_type=f32)
    for l in range(L_C):
        o_ref[:, l, :] = Y[:, l * LANES:(l + 1) * LANES]


def _s5_post_kernel(y_ref, u_ref, d_ref, w_ref, b_ref, o_ref):
    y = y_ref[...] + d_ref[...] * u_ref[...]
    g = _gelu_tanh(y)
    o_ref[...] = (g * _sigmoid(_dot(g, w_ref[...]) + b_ref[...])).astype(o_ref.dtype)


def _s5(proj, B, S, layer, a_re, a_im, log_step, b_re, b_im, c_re, c_im, d, glu_w, glu_b):
    T = B * S
    L = L_C
    nch = S // L
    toep, bend, ccar, power = _s5_tables(a_re[layer], a_im[layer], log_step[layer], b_re[layer],
                                         b_im[layer], c_re[layer], c_im[layer])
    nsteps = max(1, (nch - 1).bit_length())
    sr, si = power(L * (2 ** jnp.arange(nsteps)))
    scan_c = jnp.stack([jnp.concatenate([sr, sr], -1), jnp.concatenate([-si, si], -1)], 2)
    scan_c = scan_c.reshape(G_C, 2 * nsteps, 2 * P_C)
    GB = LANES // GS_C
    NB = G_C // GB
    eye = jnp.eye(GB, dtype=bf16)
    toep8 = jnp.einsum('bgsitk,gh->bsgithk', toep.reshape(NB, GB, L, GS_C, L, GS_C), eye)
    toep8 = toep8.reshape(NB, L * LANES, L * LANES)
    bend8 = jnp.einsum('bgsip,gh->bsgihp', bend.reshape(NB, GB, L, GS_C, 2 * P_C), eye)
    bend8 = bend8.reshape(NB, L * LANES, GB * 2 * P_C)
    ccar8 = jnp.einsum('bgptk,gh->bhptgk', ccar.reshape(NB, GB, 2 * P_C, L, GS_C), eye)
    ccar8 = ccar8.reshape(NB, GB * 2 * P_C, L * LANES)
    scan8 = scan_c.reshape(NB, GB, 2 * nsteps, 2 * P_C).transpose(0, 2, 1, 3).reshape(NB, 2 * nsteps, GB * 2 * P_C)
    u_rows = proj.reshape(T // L, L, proj.shape[1])
    y_rows = pl.pallas_call(
        _s5_kernel,
        out_shape=jax.ShapeDtypeStruct((T // L, L, W_C), f32),
        grid=(NB, B),
        in_specs=[pl.BlockSpec((nch, L, LANES), lambda g, b: (b, 0, g)),
                  pl.BlockSpec((None, L * LANES, L * LANES), lambda g, b: (g, 0, 0)),
                  pl.BlockSpec((None, L * LANES, GB * 2 * P_C), lambda g, b: (g, 0, 0)),
                  pl.BlockSpec((None, GB * 2 * P_C, L * LANES), lambda g, b: (g, 0, 0)),
                  pl.BlockSpec((None, 2 * nsteps, GB * 2 * P_C), lambda g, b: (g, 0, 0))],
        out_specs=pl.BlockSpec((nch, L, LANES), lambda g, b: (b, 0, g)),
        compiler_params=_params("parallel", "parallel"),
    )(u_rows, toep8, bend8, ccar8, scan8)
    y = y_rows.reshape(T, W_C)
    tm = min(T, 1024)
    vec = pl.BlockSpec((None, 1, W_C), lambda i: (layer, 0, 0))
    return pl.pallas_call(
        _s5_post_kernel,
        out_shape=jax.ShapeDtypeStruct((T, W_C), bf16),
        grid=(T // tm,),
        in_specs=[pl.BlockSpec((tm, W_C), lambda i: (i, 0)),
                  pl.BlockSpec((tm, W_C), lambda i: (i, 0)),
                  vec,
                  pl.BlockSpec((None, W_C, W_C), lambda i: (layer, 0, 0)),
                  vec],
        out_specs=pl.BlockSpec((tm, W_C), lambda i: (i, 0)),
        compiler_params=_params("parallel"),
    )(y, proj, d.reshape(d.shape[0], 1, W_C), glu_w, glu_b.reshape(glu_b.shape[0], 1, W_C))


def _gdn_prep_kernel(q_ref, k_ref, v_ref, qp_ref, kp_ref, vp_ref, ab_ref, cwq_ref, cwk_ref, cwv_ref,
                     alog_ref, dtb_ref, w_ref, u_ref, qd_ref, kd_ref, qk_ref, ge_ref):
    L = CHUNK
    has_prev = (pl.program_id(1) > 0).astype(f32)

    def conv(cur_ref, prev_ref, w_ref_):
        cur = cur_ref[...]
        ext = jnp.concatenate([prev_ref[...] * has_prev, cur], axis=0)
        w = w_ref_[...]
        acc = None
        for k in range(CONV_W):
            off = SUBLANES - (CONV_W - 1) + k
            term = w[k:k + 1, :] * ext[off:off + L, :]
            acc = term if acc is None else acc + term
        return _silu(acc)

    q_all = conv(q_ref, qp_ref, cwq_ref)
    k_all = conv(k_ref, kp_ref, cwk_ref)
    v_all = conv(v_ref, vp_ref, cwv_ref)
    ab = ab_ref[...]
    g_all = -jnp.exp(alog_ref[...]) * _softplus(ab[:, 0:H_D] + dtb_ref[...])
    beta_all = _sigmoid(ab[:, H_D:2 * H_D])
    tri = _tri(L)
    tri_s = _tri(L, strict=True)
    gcs_all = _dot_exact_lhs(tri.astype(f32), g_all)
    eye_h = (_iota((H_D, H_D), 0) == _iota((H_D, H_D), 1)).astype(f32)
    gcs_t = _transpose_exact(gcs_all, eye_h)
    eye = (_iota((L, L), 0) == _iota((L, L), 1)).astype(f32)
    ge_ref[...] = jnp.concatenate([jnp.exp(gcs_all), jnp.zeros((L, LANES - H_D), f32)], axis=1)

    heads = range(H_D)
    sls = [slice(h * DK_D, (h + 1) * DK_D) for h in heads]
    gcs = [gcs_all[:, h:h + 1] for h in heads]
    eg = [jnp.exp(g) for g in gcs]
    qs = [q_all[:, sl] * lax.rsqrt(jnp.sum(q_all[:, sl] * q_all[:, sl], -1, keepdims=True) + 1e-6)
          * (DK_D ** -0.5) for sl in sls]
    ks = [k_all[:, sl] * lax.rsqrt(jnp.sum(k_all[:, sl] * k_all[:, sl], -1, keepdims=True) + 1e-6)
          for sl in sls]
    kbs = [ks[h] * beta_all[:, h:h + 1] for h in heads]
    decay = [jnp.exp(jnp.where(tri, gcs[h] - gcs_t[h:h + 1, :], -jnp.inf)) for h in heads]
    kk = [_dot_nt(kbs[h], ks[h]) for h in heads]
    qk = [_dot_nt(qs[h], ks[h]) for h in heads]
    pw = [jnp.where(tri_s, -(kk[h] * decay[h]), 0.0) for h in heads]
    inv = [eye + p for p in pw]
    for _ in range(int(math.log2(L)) - 1):
        pw = [_dot(p, p) for p in pw]
        inv = [a + _dot(a, p) for a, p in zip(inv, pw)]
    rhs = [jnp.concatenate([v_all[:, sls[h]] * beta_all[:, h:h + 1], kbs[h] * eg[h]], axis=1) for h in heads]
    sol = [_dot(inv[h], rhs[h]) for h in heads]
    for h in heads:
        u_ref[:, sls[h]] = sol[h][:, :DV_D]
        w_ref[:, sls[h]] = sol[h][:, DV_D:]
        qk_ref[:, h * L:(h + 1) * L] = jnp.where(tri, qk[h] * decay[h], 0.0)
        qd_ref[:, sls[h]] = qs[h] * eg[h]
        kd_ref[:, sls[h]] = ks[h] * jnp.exp(gcs[h][L - 1:L, :] - gcs[h])


def _gdn_scan_kernel(w_ref, u_ref, qd_ref, kd_ref, qk_ref, ge_ref, z_ref, nw_ref, o_ref, st_ref):
    L = CHUNK

    @pl.when(pl.program_id(1) == 0)
    def _():
        st_ref[...] = jnp.zeros_like(st_ref)

    nw = nw_ref[...]
    for h in range(H_D):
        sl = slice(h * DK_D, (h + 1) * DK_D)
        state = st_ref[h]
        v_new = u_ref[:, sl] - _dot(w_ref[:, sl], state)
        o = _dot(qd_ref[:, sl], state) + _dot(qk_ref[:, h * L:(h + 1) * L], v_new)
        g_last = ge_ref[L - 1:L, h:h + 1]
        st_ref[h] = state * g_last + _dot_tn(kd_ref[:, sl], v_new)
        o = o * lax.rsqrt(jnp.mean(o * o, -1, keepdims=True) + RMS_EPS) * nw
        o_ref[:, sl] = (o * _silu(z_ref[:, sl])).astype(o_ref.dtype)


def _gdn(proj, ab_tail, B, S, layer, conv_w, a_log, dt_bias, norm_w):
    T = B * S
    L = CHUNK
    nc = S // L
    per8 = L // SUBLANES
    q_c0 = W_C // W_D
    row = lambda b, c: (b * nc + c, 0)

    def cur(cb):
        return pl.BlockSpec((L, W_D), lambda b, c: (b * nc + c, cb))

    def prev(cb):
        return pl.BlockSpec((SUBLANES, W_D), lambda b, c: (jnp.maximum((b * nc + c) * per8 - 1, 0), cb))

    def cw(cb):
        return pl.BlockSpec((None, CONV_W, W_D), lambda b, c: (layer, 0, cb))

    v8 = pl.BlockSpec((None, 1, H_D), lambda b, c: (layer, 0, 0))
    wide = pl.BlockSpec((L, W_D), row)
    shp = lambda n: jax.ShapeDtypeStruct((T, n), f32)
    w_c, u_c, q_dec, k_dec, qk, gexp = pl.pallas_call(
        _gdn_prep_kernel,
        out_shape=(shp(W_D), shp(W_D), shp(W_D), shp(W_D), shp(H_D * L), shp(LANES)),
        grid=(B, nc),
        in_specs=[cur(q_c0), cur(q_c0 + 1), cur(q_c0 + 2), prev(q_c0), prev(q_c0 + 1), prev(q_c0 + 2),
                  pl.BlockSpec((L, 2 * H_D), row), cw(0), cw(1), cw(2), v8, v8],
        out_specs=(wide, wide, wide, wide, pl.BlockSpec((L, H_D * L), row),
                   pl.BlockSpec((L, LANES), row)),
        compiler_params=_params("parallel", "parallel"),
    )(proj, proj, proj, proj, proj, proj, ab_tail, conv_w, conv_w, conv_w,
      a_log.reshape(a_log.shape[0], 1, H_D), dt_bias.reshape(dt_bias.shape[0], 1, H_D))
    z_c0 = (W_C + QKV_D) // W_D
    return pl.pallas_call(
        _gdn_scan_kernel,
        out_shape=jax.ShapeDtypeStruct((T, W_D), bf16),
        grid=(B, nc),
        in_specs=[wide, wide, wide, wide, pl.BlockSpec((L, H_D * L), row),
                  pl.BlockSpec((L, LANES), row),
                  pl.BlockSpec((L, W_D), lambda b, c: (b * nc + c, z_c0)),
                  pl.BlockSpec((None, 1, DV_D), lambda b, c: (layer, 0, 0))],
        out_specs=wide,
        scratch_shapes=[pltpu.VMEM((H_D, DK_D, DV_D), f32)],
        compiler_params=_params("parallel", "arbitrary"),
    )(w_c, u_c, q_dec, k_dec, qk, gexp, proj, norm_w.reshape(norm_w.shape[0], 1, DV_D))


def _xattn_kernel(q_ref, k_ref, v_ref, o_ref):
    s = _dot_nt(q_ref[...], k_ref[...]) * (HD_X ** -0.5)
    m = jnp.max(s, -1, keepdims=True)
    p = jnp.exp(s - m)
    p = p / jnp.sum(p, -1, keepdims=True)
    o_ref[...] = jnp.dot(p.astype(bf16), v_ref[...], preferred_element_type=f32).astype(o_ref.dtype)


def _xattn(q, kv, B, S, M):
    T = B * S
    tq = min(S, 1024)
    nq = S // tq
    return pl.pallas_call(
        _xattn_kernel,
        out_shape=jax.ShapeDtypeStruct((T, D_MODEL), bf16),
        grid=(B, nq, H_X),
        in_specs=[pl.BlockSpec((tq, HD_X), lambda b, i, h: (b * nq + i, h)),
                  pl.BlockSpec((M, HD_X), lambda b, i, h: (b, h)),
                  pl.BlockSpec((M, HD_X), lambda b, i, h: (b, H_X + h))],
        out_specs=pl.BlockSpec((tq, HD_X), lambda b, i, h: (b * nq + i, h)),
        compiler_params=_params("parallel", "parallel", "parallel"),
    )(q, kv, kv)


def _route_kernel(x_ref, w_ref, b_ref, o_ref):
    logits = _dot3(x_ref[...], w_ref[...]) + b_ref[...]
    lane = _iota(logits.shape, 1)
    neg = -jnp.inf
    big = jnp.int32(LANES)
    is_g = (lane >= N_EXP) & (lane < N_EXP + NG_E)
    gl = jnp.where(is_g, logits, neg)
    gmax = jnp.max(gl, -1, keepdims=True)
    g_lane = jnp.min(jnp.where(gl == gmax, lane, big), -1, keepdims=True)
    g_prob = 1.0 / jnp.sum(jnp.where(is_g, jnp.exp(gl - gmax), 0.0), -1, keepdims=True)
    e0 = (g_lane - N_EXP) * E_PER
    sel = jnp.where((lane >= e0) & (lane < e0 + E_PER), logits, neg)
    m1 = jnp.max(sel, -1, keepdims=True)
    i1 = jnp.min(jnp.where(sel == m1, lane, big), -1, keepdims=True)
    sel2 = jnp.where(lane == i1, neg, sel)
    m2 = jnp.max(sel2, -1, keepdims=True)
    i2 = jnp.min(jnp.where(sel2 == m2, lane, big), -1, keepdims=True)
    e2 = jnp.exp(m2 - m1)
    w1 = g_prob / (1.0 + e2)
    w2 = g_prob * e2 / (1.0 + e2)
    o_ref[...] = (jnp.where(lane == 0, w1, 0.0) + jnp.where(lane == 1, w2, 0.0)
                  + jnp.where(lane == 2, i1.astype(f32), 0.0) + jnp.where(lane == 3, i2.astype(f32), 0.0))


def _route(x, w_group, b_group, w_expert, b_expert, layer):
    T, D = x.shape
    pad = LANES - N_EXP - NG_E
    w = jnp.concatenate([w_expert[layer], w_group[layer], jnp.zeros((D, pad), f32)], 1)
    b = jnp.concatenate([b_expert[layer], b_group[layer], jnp.zeros((pad,), f32)])[None, :]
    tm = min(T, 512)
    return pl.pallas_call(
        _route_kernel,
        out_shape=jax.ShapeDtypeStruct((T, LANES), f32),
        grid=(T // tm,),
        in_specs=[pl.BlockSpec((tm, D), lambda i: (i, 0)),
                  pl.BlockSpec((D, LANES), lambda i: (0, 0)),
                  pl.BlockSpec((1, LANES), lambda i: (0, 0))],
        out_specs=pl.BlockSpec((tm, LANES), lambda i: (i, 0)),
        compiler_params=_params("parallel"),
    )(x, w, b)


MOE_TM = 256
LN_TM = 256


def _moe_plan(route, T):
    n_slots = TOPK_IN * T
    n_tiles = n_slots // MOE_TM + N_EXP
    ids = route[:, 2:2 + TOPK_IN].astype(jnp.int32)
    e_flat = ids.T.reshape(n_slots)
    onehot = (e_flat[:, None] == jnp.arange(N_EXP, dtype=jnp.int32)[None, :]).astype(jnp.int32)
    csum = jnp.cumsum(onehot, axis=0)
    rank = jnp.sum(onehot * csum, axis=1) - 1
    counts = csum[-1]
    tiles_per = (counts + MOE_TM - 1) // MOE_TM
    tile_end = jnp.cumsum(tiles_per)
    tile_start = tile_end - tiles_per
    dest = jnp.sum(onehot * tile_start[None, :], axis=1) * MOE_TM + rank
    tok = jnp.arange(n_slots, dtype=jnp.int32) % T
    src = jnp.zeros((n_tiles * MOE_TM,), jnp.int32).at[dest].set(tok, unique_indices=True)
    j = jnp.arange(n_tiles, dtype=jnp.int32)
    tile_e = jnp.sum((j[:, None] >= tile_end[None, :]).astype(jnp.int32), axis=1)
    last_e = jnp.max(jnp.where(counts > 0, jnp.arange(N_EXP, dtype=jnp.int32), 0))
    tile_e = jnp.minimum(tile_e, last_e)
    n_live = tile_end[-1:].astype(jnp.int32)
    return tile_e, n_live, src, dest.astype(jnp.int32), n_tiles


def _moe_group_kernel(tile_e_ref, nt_ref, src_ref, x_hbm, wg_ref, wu_ref, wd_ref, o_ref, xbuf, sems):
    i = pl.program_id(0)
    nt = nt_ref[0]
    tm = xbuf.shape[1]

    def issue(tile, slot):
        def body(r, c):
            tok = src_ref[tile * tm + r]
            pltpu.make_async_copy(x_hbm.at[pl.ds(tok, 1), :], xbuf.at[slot, pl.ds(r, 1), :],
                                  sems.at[slot]).start()
            return c
        lax.fori_loop(0, tm, body, 0, unroll=8)

    @pl.when(i == 0)
    def _():
        issue(0, 0)

    @pl.when(i + 1 < nt)
    def _():
        issue(i + 1, (i + 1) % 2)

    @pl.when(i < nt)
    def _():
        slot = i % 2
        pltpu.make_async_copy(x_hbm.at[pl.ds(0, tm), :], xbuf.at[slot], sems.at[slot]).wait()
        x = xbuf[slot].astype(bf16)
        h = _silu(jnp.dot(x, wg_ref[...].astype(bf16), preferred_element_type=f32)) * \
            jnp.dot(x, wu_ref[...].astype(bf16), preferred_element_type=f32)
        o_ref[...] = _dot(h, wd_ref[...])

    @pl.when(i >= nt)
    def _():
        o_ref[...] = jnp.zeros_like(o_ref)


def _moe_group(xf, tile_e, n_live, src, n_tiles, w_gate, w_up, w_down, layer):
    T, D = xf.shape
    wspec = lambda shp: pl.BlockSpec((None, None) + shp, lambda i, te, nt, sr: (layer, te[i], 0, 0))
    return pl.pallas_call(
        _moe_group_kernel,
        out_shape=jax.ShapeDtypeStruct((n_tiles * MOE_TM, D), f32),
        grid_spec=pltpu.PrefetchScalarGridSpec(
            num_scalar_prefetch=3,
            grid=(n_tiles,),
            in_specs=[pl.BlockSpec(memory_space=pl.ANY), wspec((D, D_E)), wspec((D, D_E)),
                      wspec((D_E, D))],
            out_specs=pl.BlockSpec((MOE_TM, D), lambda i, te, nt, sr: (i, 0)),
            scratch_shapes=[pltpu.VMEM((2, MOE_TM, D), f32), pltpu.SemaphoreType.DMA((2,))]),
        compiler_params=_params("arbitrary"),
    )(tile_e, n_live, src, xf, w_gate, w_up, w_down)


def _moe_combine_ln_kernel(pos_ref, x_ref, rt_ref, g_ref, b_ref, ys_hbm, o_ref, ob_ref, ybuf, sems):
    i = pl.program_id(0)
    n = pl.num_programs(0)
    tm = x_ref.shape[0]
    T = n * tm

    def issue(tile, slot):
        for k in range(TOPK_IN):
            def body(r, c, k=k):
                p = pos_ref[k * T + tile * tm + r]
                pltpu.make_async_copy(ys_hbm.at[pl.ds(p, 1), :], ybuf.at[slot, k, pl.ds(r, 1), :],
                                      sems.at[slot]).start()
                return c
            lax.fori_loop(0, tm, body, 0, unroll=8)

    @pl.when(i == 0)
    def _():
        issue(0, 0)

    @pl.when(i + 1 < n)
    def _():
        issue(i + 1, (i + 1) % 2)

    slot = i % 2
    for k in range(TOPK_IN):
        pltpu.make_async_copy(ys_hbm.at[pl.ds(0, tm), :], ybuf.at[slot, k], sems.at[slot]).wait()
    rt = rt_ref[...]
    y = rt[:, 0:1] * ybuf[slot, 0] + rt[:, 1:2] * ybuf[slot, 1]
    o = _layer_norm(ALPHA * x_ref[...] + y, g_ref[...], b_ref[...])
    o_ref[...] = o
    ob_ref[...] = o.astype(bf16)


def _moe_combine_ln(x, route, pos, ys, g, b, layer):
    T, D = x.shape
    tm = min(T, LN_TM)
    vec = pl.BlockSpec((None, 1, D), lambda i, p: (layer, 0, 0))
    row = pl.BlockSpec((tm, D), lambda i, p: (i, 0))
    return pl.pallas_call(
        _moe_combine_ln_kernel,
        out_shape=(jax.ShapeDtypeStruct((T, D), f32), jax.ShapeDtypeStruct((T, D), bf16)),
        grid_spec=pltpu.PrefetchScalarGridSpec(
            num_scalar_prefetch=1,
            grid=(T // tm,),
            in_specs=[row, pl.BlockSpec((tm, LANES), lambda i, p: (i, 0)), vec, vec,
                      pl.BlockSpec(memory_space=pl.ANY)],
            out_specs=(row, row),
            scratch_shapes=[pltpu.VMEM((2, TOPK_IN, tm, D), f32), pltpu.SemaphoreType.DMA((2,))]),
        compiler_params=_params("arbitrary"),
    )(pos, x, route, g.reshape(g.shape[0], 1, D), b.reshape(b.shape[0], 1, D), ys)


def _moe_ln(xf, route, w_gate, w_up, w_down, g, b, layer):
    tile_e, n_live, src, pos, n_tiles = _moe_plan(route, xf.shape[0])
    ys = _moe_group(xf, tile_e, n_live, src, n_tiles, w_gate, w_up, w_down, layer)
    return _moe_combine_ln(xf, route, pos, ys, g, b, layer)


def kernel(x, mem, ab_w_in, rg_conv_w, rg_conv_b, rg_wa, rg_ba, rg_wx, rg_bx, rg_lam, ssd_conv_w, ssd_conv_b, ssd_dt_bias, ssd_a_log, ssd_d, ssd_norm_w, ab_w_out, cd_w_in, s5_a_re, s5_a_im, s5_log_step, s5_b_re, s5_b_im, s5_c_re, s5_c_im, s5_d, s5_glu_w, s5_glu_b, dn_conv_w, dn_a_log, dn_dt_bias, dn_norm_w, cd_w_out, xa_w_q, xa_w_kv, xa_w_o, moe_w_group, moe_b_group, moe_w_expert, moe_b_expert, moe_w_gate, moe_w_up, moe_w_down, ln1_g, ln1_b, ln2_g, ln2_b, ln3_g, ln3_b):
    B, S, D = x.shape
    M = mem.shape[1]
    T = B * S
    xf = x.reshape(T, D)
    xb = xf.astype(bf16)
    memb = mem.reshape(B * M, D).astype(bf16)
    for l in range(DEPTH):
        i = l // 2
        if l % 2 == 0:
            proj = _mm(xb, ab_w_in, i, MAIN_AB, f32)
            dt_tail = _mm_hi(xf, ab_w_in[i][:, MAIN_AB:])
            ya = _rglru(proj, B, S, i, rg_conv_w, rg_conv_b, rg_wa, rg_ba, rg_wx, rg_bx, rg_lam)
            yb = _ssd(proj, dt_tail, B, S, i, ssd_conv_w, ssd_conv_b, ssd_dt_bias, ssd_a_log, ssd_d,
                      ssd_norm_w)
            xf, xb = _mm_ln([ya, yb], ab_w_out, i, xf, ln1_g, ln1_b, l)
        else:
            proj = _mm(xb, cd_w_in, i, MAIN_CD, f32)
            ab_tail = _mm_hi(xf, cd_w_in[i][:, MAIN_CD:])
            yc = _s5(proj, B, S, i, s5_a_re, s5_a_im, s5_log_step, s5_b_re, s5_b_im, s5_c_re, s5_c_im,
                     s5_d, s5_glu_w, s5_glu_b)
            yd = _gdn(proj, ab_tail, B, S, i, dn_conv_w, dn_a_log, dn_dt_bias, dn_norm_w)
            xf, xb = _mm_ln([yc, yd], cd_w_out, i, xf, ln1_g, ln1_b, l)
        q = _mm(xb, xa_w_q, l, D, bf16)
        kv = _mm(memb, xa_w_kv, l, 2 * D, bf16)
        att = _xattn(q, kv, B, S, M)
        xf, xb = _mm_ln([att], xa_w_o, l, xf, ln2_g, ln2_b, l)
        route = _route(xf, moe_w_group, moe_b_group, moe_w_expert, moe_b_expert, l)
        xf, xb = _moe_ln(xf, route, moe_w_gate, moe_w_up, moe_w_down, ln3_g, ln3_b, l)
    return xf.reshape(B, S, D)
```

```python
import functools
import math

import jax
import jax.numpy as jnp
from jax import lax
from jax.experimental import pallas as pl
from jax.experimental.pallas import tpu as pltpu

f32 = jnp.float32
bf16 = jnp.bfloat16

D_MODEL = 2048
DEPTH = 2
CHUNK = 64
CONV_W = 4
ALPHA = (2 * DEPTH) ** 0.25
LN_EPS = 1e-5
RMS_EPS = 1e-6
W_A = D_MODEL // 2
H_A = 8
BW_A = W_A // H_A
RG_C = 8.0
W_B = D_MODEL
HD_B = 64
H_B = W_B // HD_B
NG_B = 2
N_B = 128
HG_B = H_B // NG_B
CONV_B = W_B + 2 * NG_B * N_B
MAIN_AB = 2 * W_A + W_B + CONV_B
W_C = D_MODEL // 2
GS_C = 16
G_C = W_C // GS_C
P_C = 64
L_C = 16
H_D = 8
DK_D = D_MODEL // 16
DV_D = D_MODEL // 16
W_D = H_D * DV_D
QKV_D = 2 * H_D * DK_D + W_D
MAIN_CD = W_C + QKV_D + W_D
H_X = 4
HD_X = D_MODEL // H_X
NG_E = 4
E_PER = 8
N_EXP = NG_E * E_PER
TOPK_IN = 2
D_E = D_MODEL // 8

LANES = 128
SUBLANES = 8
S5_GB = LANES // GS_C
S5_NB = G_C // S5_GB
VMEM_LIMIT = 56 * 1024 * 1024


def _params(*sem):
    return pltpu.CompilerParams(dimension_semantics=sem, vmem_limit_bytes=VMEM_LIMIT)


def _sigmoid(x):
    return 1.0 / (1.0 + jnp.exp(-x))


def _silu(x):
    return x * _sigmoid(x)


def _softplus(x):
    return jnp.maximum(x, 0.0) + jnp.log(1.0 + jnp.exp(-jnp.abs(x)))


def _gelu_tanh(x):
    return 0.5 * x * (1.0 + jnp.tanh(math.sqrt(2.0 / math.pi) * (x + 0.044715 * (x * x * x))))


def _dot(a, b):
    return jnp.dot(a.astype(bf16), b.astype(bf16), preferred_element_type=f32)


def _dot_nt(a, b):
    return lax.dot_general(a.astype(bf16), b.astype(bf16), (((1,), (1,)), ((), ())),
                           preferred_element_type=f32)


def _dot_tn(a, b):
    return lax.dot_general(a.astype(bf16), b.astype(bf16), (((0,), (0,)), ((), ())),
                           preferred_element_type=f32)


def _split3(a):
    hi = a.astype(bf16)
    r = a - hi.astype(f32)
    mid = r.astype(bf16)
    lo = (r - mid.astype(f32)).astype(bf16)
    return hi, mid, lo


def _dot_exact_lhs(sel, b):
    s = sel.astype(bf16)
    b1, b2, b3 = _split3(b)
    d = functools.partial(jnp.dot, preferred_element_type=f32)
    return d(s, b1) + d(s, b2) + d(s, b3)


def _dot_exact_rhs(a, sel):
    s = sel.astype(bf16)
    a1, a2, a3 = _split3(a)
    d = functools.partial(jnp.dot, preferred_element_type=f32)
    return d(a1, s) + d(a2, s) + d(a3, s)


def _transpose_exact(a, eye):
    a1, a2, a3 = _split3(a)
    e = eye.astype(bf16)
    d = lambda x: lax.dot_general(e, x, (((1,), (1,)), ((), ())), preferred_element_type=f32)
    return d(a1) + d(a2) + d(a3)


def _dot3(a, b):
    a1 = a.astype(bf16)
    a2 = (a - a1.astype(f32)).astype(bf16)
    b1 = b.astype(bf16)
    b2 = (b - b1.astype(f32)).astype(bf16)
    d = functools.partial(jnp.dot, preferred_element_type=f32)
    return d(a1, b1) + (d(a1, b2) + d(a2, b1))


def _iota(shape, axis):
    return lax.broadcasted_iota(jnp.int32, shape, axis)


def _tri(n, strict=False):
    r, c = _iota((n, n), 0), _iota((n, n), 1)
    return (r > c) if strict else (r >= c)


def _mm_kernel(x_ref, w_ref, o_ref):
    o_ref[...] = jnp.dot(x_ref[...], w_ref[...].astype(bf16),
                         preferred_element_type=f32).astype(o_ref.dtype)


def _mm(x, w, layer, n_cols, out_dtype, tn=512):
    M, K = x.shape
    tm = min(M, 1024)
    return pl.pallas_call(
        _mm_kernel,
        out_shape=jax.ShapeDtypeStruct((M, n_cols), out_dtype),
        grid=(M // tm, n_cols // tn),
        in_specs=[pl.BlockSpec((tm, K), lambda i, j: (i, 0)),
                  pl.BlockSpec((None, K, tn), lambda i, j: (layer, 0, j))],
        out_specs=pl.BlockSpec((tm, tn), lambda i, j: (i, j)),
        compiler_params=_params("parallel", "parallel"),
    )(x, w)


def _mm_hi_kernel(x_ref, w_ref, o_ref):
    o_ref[...] = _dot3(x_ref[...], w_ref[...])


def _mm_hi(x, w):
    M, K = x.shape
    n = w.shape[1]
    tm = min(M, 512)
    return pl.pallas_call(
        _mm_hi_kernel,
        out_shape=jax.ShapeDtypeStruct((M, n), f32),
        grid=(M // tm,),
        in_specs=[pl.BlockSpec((tm, K), lambda i: (i, 0)),
                  pl.BlockSpec((K, n), lambda i: (0, 0))],
        out_specs=pl.BlockSpec((tm, n), lambda i: (i, 0)),
        compiler_params=_params("parallel"),
    )(x, w)


def _layer_norm(v, g, b):
    mu = jnp.mean(v, -1, keepdims=True)
    d = v - mu
    var = jnp.mean(d * d, -1, keepdims=True)
    return d * lax.rsqrt(var + LN_EPS) * g + b


K_CHUNK = 512


def _to_token_tiles(o3_ref, v):
    for s in range(v.shape[1] // LANES):
        o3_ref[:, s, :] = v[:, s * LANES:(s + 1) * LANES]


def _mm_ln_kernel(*refs, chunks, token_tiles):
    n_parts = max(p for p, _ in chunks) + 1
    part_refs = refs[:n_parts]
    w_ref, x_ref, g_ref, b_ref, o_ref, o2_ref, acc_ref = refs[n_parts:]
    k = pl.program_id(1)
    w = w_ref[...].astype(bf16)
    for idx, (p, off) in enumerate(chunks):
        @pl.when(k == idx)
        def _(p=p, off=off, idx=idx):
            c = jnp.dot(part_refs[p][:, off:off + K_CHUNK], w, preferred_element_type=f32)
            if idx == 0:
                acc_ref[...] = c
            else:
                acc_ref[...] += c

    @pl.when(k == len(chunks) - 1)
    def _():
        o = _layer_norm(ALPHA * x_ref[...] + acc_ref[...], g_ref[...], b_ref[...])
        o_ref[...] = o
        if token_tiles:
            _to_token_tiles(o2_ref, o)
        else:
            o2_ref[...] = o.astype(bf16)


def _mm_ln(parts, w, w_layer, x, g, b, layer, token_tiles=False):
    T, D = x.shape
    tm = min(T, 512)
    chunks = tuple((p, off) for p, a in enumerate(parts) for off in range(0, a.shape[1], K_CHUNK))
    vec = pl.BlockSpec((None, 1, D), lambda i, k: (layer, 0, 0))
    row = pl.BlockSpec((tm, D), lambda i, k: (i, 0))
    if token_tiles:
        shape2 = jax.ShapeDtypeStruct((T, D // LANES, LANES), f32)
        spec2 = pl.BlockSpec((tm, D // LANES, LANES), lambda i, k: (i, 0, 0))
    else:
        shape2, spec2 = jax.ShapeDtypeStruct((T, D), bf16), row
    return pl.pallas_call(
        functools.partial(_mm_ln_kernel, chunks=chunks, token_tiles=token_tiles),
        out_shape=(jax.ShapeDtypeStruct((T, D), f32), shape2),
        grid=(T // tm, len(chunks)),
        in_specs=[pl.BlockSpec((tm, a.shape[1]), lambda i, k: (i, 0)) for a in parts]
        + [pl.BlockSpec((None, K_CHUNK, D), lambda i, k: (w_layer, k, 0)), row, vec, vec],
        out_specs=(row, spec2),
        scratch_shapes=[pltpu.VMEM((tm, D), f32)],
        compiler_params=_params("parallel", "arbitrary"),
    )(*parts, w, x, g.reshape(g.shape[0], 1, D), b.reshape(b.shape[0], 1, D))


def _conv_step(x_ref, w, hist_ref, first):
    tt = x_ref.shape[0]

    @pl.when(first)
    def _():
        hist_ref[0:SUBLANES, :] = jnp.zeros((SUBLANES, hist_ref.shape[1]), f32)

    hist_ref[SUBLANES:SUBLANES + tt, :] = x_ref[...]
    acc = None
    for k in range(CONV_W):
        term = w[k:k + 1, :] * hist_ref[pl.ds(SUBLANES - (CONV_W - 1) + k, tt), :]
        acc = term if acc is None else acc + term
    hist_ref[0:SUBLANES, :] = hist_ref[tt:tt + SUBLANES, :]
    return acc


def _scan_affine(a, u):
    n = a.shape[0]
    row = _iota(a.shape, 0)
    d = 1
    while d < n:
        keep = row >= d
        a_s = pltpu.roll(a, d, 0)
        u_s = pltpu.roll(u, d, 0)
        u = u + jnp.where(keep, a * u_s, 0.0)
        a = jnp.where(keep, a * a_s, a)
        d *= 2
    return a, u


def _rglru_kernel(gate_ref, xa_ref, cw_ref, cb_ref, wa_ref, ba_ref, wx_ref, bx_ref, lam_ref,
                  o_ref, hist_ref, h_ref):
    first = pl.program_id(2) == 0

    @pl.when(first)
    def _():
        h_ref[...] = jnp.zeros_like(h_ref)

    xc = _conv_step(xa_ref, cw_ref[...], hist_ref, first) + cb_ref[...]
    r = _sigmoid(_dot(xc, wa_ref[...]) + ba_ref[...])
    i = _sigmoid(_dot(xc, wx_ref[...]) + bx_ref[...])
    log_a = -RG_C * r * _softplus(-lam_ref[...])
    a = jnp.exp(log_a)
    u = jnp.sqrt(1.0 - jnp.exp(2.0 * log_a)) * (i * xc)
    a_cum, h = _scan_affine(a, u)
    h = h + a_cum * h_ref[0:1, :]
    tt = h.shape[0]
    h_ref[...] = jnp.broadcast_to(h[tt - 1:tt, :], h_ref.shape)
    o_ref[...] = (_gelu_tanh(gate_ref[...]) * h).astype(o_ref.dtype)


def _rglru(proj, B, S, layer, conv_w, conv_b, wa, ba, wx, bx, lam):
    T = B * S
    tt = min(S, 256)
    ns = S // tt
    gate_col0, xa_col0 = 0, W_A // BW_A
    row = lambda b, h, s: b * ns + s
    vec = lambda a: a.reshape(a.shape[0], 1, W_A)
    vspec = pl.BlockSpec((None, 1, BW_A), lambda b, h, s: (layer, 0, h))
    wspec = pl.BlockSpec((None, None, BW_A, BW_A), lambda b, h, s: (layer, h, 0, 0))
    return pl.pallas_call(
        _rglru_kernel,
        out_shape=jax.ShapeDtypeStruct((T, W_A), bf16),
        grid=(B, H_A, ns),
        in_specs=[pl.BlockSpec((tt, BW_A), lambda b, h, s: (row(b, h, s), gate_col0 + h)),
                  pl.BlockSpec((tt, BW_A), lambda b, h, s: (row(b, h, s), xa_col0 + h)),
                  pl.BlockSpec((None, CONV_W, BW_A), lambda b, h, s: (layer, 0, h)),
                  vspec, wspec, vspec, wspec, vspec, vspec],
        out_specs=pl.BlockSpec((tt, BW_A), lambda b, h, s: (row(b, h, s), h)),
        scratch_shapes=[pltpu.VMEM((SUBLANES + tt, BW_A), f32), pltpu.VMEM((SUBLANES, BW_A), f32)],
        compiler_params=_params("parallel", "parallel", "arbitrary"),
    )(proj, proj, conv_w, vec(conv_b), wa, vec(ba), wx, vec(bx), vec(lam))


def _ssd_kernel(z_ref, x_ref, b_ref, c_ref, dt_ref, cwx_ref, cwb_ref, cwc_ref, cbx_ref, cbb_ref,
                cbc_ref, dtb_ref, alog_ref, d_ref, nw_ref, o_ref,
                hx_ref, hb_ref, hc_ref, st_ref):
    g = pl.program_id(1)
    first = pl.program_id(2) == 0
    L = CHUNK

    @pl.when(first)
    def _():
        st_ref[...] = jnp.zeros_like(st_ref)

    X = _silu(_conv_step(x_ref, cwx_ref[...], hx_ref, first) + cbx_ref[...])
    Bc = _silu(_conv_step(b_ref, cwb_ref[...], hb_ref, first) + cbb_ref[...])
    Cc = _silu(_conv_step(c_ref, cwc_ref[...], hc_ref, first) + cbc_ref[...])

    pick = (_iota((H_B, HG_B), 0) == _iota((H_B, HG_B), 1) + g * HG_B).astype(f32)
    dt = _dot_exact_rhs(_softplus(dt_ref[...] + dtb_ref[...]), pick)
    a_neg = _dot_exact_rhs(jnp.broadcast_to(-jnp.exp(alog_ref[...]), (SUBLANES, H_B)), pick)[0:1]
    d_head = _dot_exact_rhs(jnp.broadcast_to(d_ref[...], (SUBLANES, H_B)), pick)
    adt = dt * a_neg
    tri = _tri(L)
    cs = _dot_exact_lhs(tri.astype(f32), adt)
    eye_h = (_iota((HG_B, HG_B), 0) == _iota((HG_B, HG_B), 1)).astype(f32)
    cs_t = _transpose_exact(cs, eye_h)
    cs_last = cs[L - 1:L, :]

    expand = (_iota((HG_B, HG_B * HD_B), 1) // HD_B == _iota((HG_B, HG_B * HD_B), 0)).astype(f32)
    dt_x = _dot_exact_rhs(dt, expand)
    ecs_x = _dot_exact_rhs(jnp.exp(cs), expand)
    dec_x = _dot_exact_rhs(jnp.exp(cs_last - cs), expand)
    cdec_x = _dot_exact_rhs(jnp.broadcast_to(jnp.exp(cs_last), (SUBLANES, HG_B)), expand)[0:1]
    d_x = _dot_exact_rhs(d_head, expand)[0:1]

    xdt = X * dt_x
    cb = _dot_nt(Cc, Bc)
    xdt_b = xdt.astype(bf16)
    left = _iota((L, 2 * HD_B), 1) < HD_B
    pieces = []
    for j in range(0, HG_B, 2):
        pair = xdt_b[:, j * HD_B:(j + 2) * HD_B]
        outs = []
        for jj in (j, j + 1):
            seg = jnp.where(tri, cs[:, jj:jj + 1] - cs_t[jj:jj + 1, :], -jnp.inf)
            outs.append(jnp.dot((cb * jnp.exp(seg)).astype(bf16), pair, preferred_element_type=f32))
        pieces.append(jnp.where(left, outs[0], outs[1]))
    y_diag = jnp.concatenate(pieces, axis=1)

    state = st_ref[...]
    y_off = ecs_x * _dot(Cc, state)
    st_ref[...] = state * cdec_x + _dot_tn(Bc, xdt * dec_x)

    y = y_diag + y_off + X * d_x
    yg = y * _silu(z_ref[...])
    out = yg * lax.rsqrt(jnp.mean(yg * yg, -1, keepdims=True) + RMS_EPS) * nw_ref[...]
    o_ref[...] = out.astype(o_ref.dtype)


def _ssd(proj, dt_tail, B, S, layer, conv_w, conv_b, dt_bias, a_log, d, norm_w):
    T = B * S
    L = CHUNK
    nc = S // L
    GW = W_B // NG_B
    row = lambda b, g, c: b * nc + c
    z_c0 = 2 * W_A // GW
    x_c0 = (2 * W_A + W_B) // GW
    b_c0 = (2 * W_A + W_B + W_B) // N_B
    c_c0 = b_c0 + NG_B
    cw_b0 = W_B // N_B
    cw_c0 = cw_b0 + NG_B
    conv_b3 = conv_b.reshape(conv_b.shape[0], 1, CONV_B)
    vec32 = lambda a: a.reshape(a.shape[0], 1, H_B)
    v32 = pl.BlockSpec((None, 1, H_B), lambda b, g, c: (layer, 0, 0))
    return pl.pallas_call(
        _ssd_kernel,
        out_shape=jax.ShapeDtypeStruct((T, W_B), bf16),
        grid=(B, NG_B, nc),
        in_specs=[pl.BlockSpec((L, GW), lambda b, g, c: (row(b, g, c), z_c0 + g)),
                  pl.BlockSpec((L, GW), lambda b, g, c: (row(b, g, c), x_c0 + g)),
                  pl.BlockSpec((L, N_B), lambda b, g, c: (row(b, g, c), b_c0 + g)),
                  pl.BlockSpec((L, N_B), lambda b, g, c: (row(b, g, c), c_c0 + g)),
                  pl.BlockSpec((L, H_B), lambda b, g, c: (row(b, g, c), 0)),
                  pl.BlockSpec((None, CONV_W, GW), lambda b, g, c: (layer, 0, g)),
                  pl.BlockSpec((None, CONV_W, N_B), lambda b, g, c: (layer, 0, cw_b0 + g)),
                  pl.BlockSpec((None, CONV_W, N_B), lambda b, g, c: (layer, 0, cw_c0 + g)),
                  pl.BlockSpec((None, 1, GW), lambda b, g, c: (layer, 0, g)),
                  pl.BlockSpec((None, 1, N_B), lambda b, g, c: (layer, 0, cw_b0 + g)),
                  pl.BlockSpec((None, 1, N_B), lambda b, g, c: (layer, 0, cw_c0 + g)),
                  v32, v32, v32,
                  pl.BlockSpec((None, 1, GW), lambda b, g, c: (layer, 0, g))],
        out_specs=pl.BlockSpec((L, GW), lambda b, g, c: (row(b, g, c), g)),
        scratch_shapes=[pltpu.VMEM((SUBLANES + L, GW), f32), pltpu.VMEM((SUBLANES + L, N_B), f32),
                        pltpu.VMEM((SUBLANES + L, N_B), f32), pltpu.VMEM((N_B, GW), f32)],
        compiler_params=_params("parallel", "parallel", "arbitrary"),
    )(proj, proj, proj, proj, dt_tail, conv_w, conv_w, conv_w, conv_b3, conv_b3, conv_b3,
      vec32(dt_bias), vec32(a_log), vec32(d), norm_w.reshape(norm_w.shape[0], 1, W_B))


def _s5_tables(a_re, a_im, log_step, b_re, b_im, c_re, c_im):
    L = L_C
    ar, ai = a_re.astype(f32), a_im.astype(f32)
    step = jnp.exp(log_step.astype(f32))[:, None]
    mag = jnp.exp(ar * step)
    lb_re, lb_im = mag * jnp.cos(ai * step), mag * jnp.sin(ai * step)
    den = ar * ar + ai * ai
    f_re = ((lb_re - 1.0) * ar + lb_im * ai) / den
    f_im = (lb_im * ar - (lb_re - 1.0) * ai) / den
    br, bi = b_re.astype(f32), b_im.astype(f32)
    bb_re = f_re[..., None] * br - f_im[..., None] * bi
    bb_im = f_re[..., None] * bi + f_im[..., None] * br
    cr, ci = c_re.astype(f32), c_im.astype(f32)

    def power(n):
        n = n.astype(f32)[None, :, None]
        m = jnp.exp(ar[:, None, :] * step[:, None, :] * n)
        ang = ai[:, None, :] * step[:, None, :] * n
        return m * jnp.cos(ang), m * jnp.sin(ang)

    j = jnp.arange(L)
    pr, pi = power(j)
    lbr = pr[..., None] * bb_re[:, None] - pi[..., None] * bb_im[:, None]
    lbi = pr[..., None] * bb_im[:, None] + pi[..., None] * bb_re[:, None]
    kern = (jnp.einsum('gop,gjpk->gjko', cr, lbr) - jnp.einsum('gop,gjpk->gjko', ci, lbi))
    eye = jnp.eye(S5_GB, dtype=f32)
    lagb = jnp.einsum('bgjik,gh->bjgihk', kern.reshape(S5_NB, S5_GB, L, GS_C, GS_C), eye)
    lagb = lagb.reshape(S5_NB, L, LANES, LANES)
    rev = (L - 1) - j
    bend = jnp.concatenate([jnp.take(lbr, rev, axis=1), jnp.take(lbi, rev, axis=1)], 2)
    bendc = bend.reshape(S5_NB, S5_GB, L, 2 * P_C, GS_C).transpose(0, 2, 1, 4, 3)
    bendc = bendc.reshape(S5_NB, L, LANES, 2 * P_C)
    qr, qi = power(j + 1)
    car_re = cr[:, None] * qr[:, :, None, :] - ci[:, None] * qi[:, :, None, :]
    car_im = -(cr[:, None] * qi[:, :, None, :] + ci[:, None] * qr[:, :, None, :])
    car = jnp.concatenate([car_re, car_im], -1)
    ccarc = car.reshape(S5_NB, S5_GB, L, GS_C, 2 * P_C).transpose(0, 2, 4, 1, 3)
    ccarc = ccarc.reshape(S5_NB, L, 2 * P_C, LANES)
    return lagb.astype(bf16), bendc.astype(bf16), ccarc.astype(bf16), power


def _s5_kernel(u_ref, lag_ref, bend_ref, ccar_ref, sc_ref, o_ref, toep_t, bend_t, ccar_t):
    L = L_C
    SW = 2 * P_C

    @pl.when(pl.program_id(1) == 0)
    def _():
        zero = jnp.zeros((LANES, LANES), bf16)
        for s in range(L):
            for t in range(L):
                toep_t[s * LANES:(s + 1) * LANES, t * LANES:(t + 1) * LANES] = lag_ref[t - s] if t >= s else zero
        same_g = (_iota((LANES, S5_GB * SW), 0) // GS_C) == (_iota((LANES, S5_GB * SW), 1) // SW)
        for s in range(L):
            wide = jnp.concatenate([bend_ref[s]] * S5_GB, axis=1)
            bend_t[s * LANES:(s + 1) * LANES, :] = jnp.where(same_g, wide, jnp.zeros_like(wide))
        col_g = _iota((SW, LANES), 1) // GS_C
        for t in range(L):
            blk = ccar_ref[t]
            for g in range(S5_GB):
                ccar_t[g * SW:(g + 1) * SW, t * LANES:(t + 1) * LANES] = jnp.where(col_g == g, blk, jnp.zeros_like(blk))

    n = o_ref.shape[0] // L
    U = jnp.concatenate([u_ref[pl.ds(l, n, stride=L), :].astype(bf16) for l in range(L)], axis=1)
    Y = jnp.dot(U, toep_t[...], preferred_element_type=f32)
    H_all = jnp.dot(U, bend_t[...], preferred_element_type=f32)
    row = _iota((n, SW), 0)
    prev = []
    for g in range(S5_GB):
        sl = slice(g * SW, (g + 1) * SW)
        H = H_all[:, sl]
        d, k = 1, 0
        while d < n:
            hs = pltpu.roll(H, d, 0)
            sw = pltpu.roll(hs, P_C, 1)
            H = H + jnp.where(row >= d, sc_ref[2 * k:2 * k + 1, sl] * hs + sc_ref[2 * k + 1:2 * k + 2, sl] * sw, 0.0)
            d *= 2
            k += 1
        prev.append(jnp.where(row >= 1, pltpu.roll(H, 1, 0), 0.0).astype(bf16))
    Y = Y + jnp.dot(jnp.concatenate(prev, axis=1), ccar_t[...], preferred_element_type=f32)
    for l in range(L):
        o_ref[pl.ds(l, n, stride=L), :] = Y[:, l * LANES:(l + 1) * LANES]


def _s5_post_kernel(y_ref, u_ref, d_ref, w_ref, b_ref, o_ref):
    y = y_ref[...] + d_ref[...] * u_ref[...]
    g = _gelu_tanh(y)
    o_ref[...] = (g * _sigmoid(_dot(g, w_ref[...]) + b_ref[...])).astype(o_ref.dtype)


def _s5(proj, B, S, layer, a_re, a_im, log_step, b_re, b_im, c_re, c_im, d, glu_w, glu_b):
    T = B * S
    L = L_C
    nch = S // L
    lagb, bendc, ccarc, power = _s5_tables(a_re[layer], a_im[layer], log_step[layer], b_re[layer],
                                           b_im[layer], c_re[layer], c_im[layer])
    nsteps = max(1, (nch - 1).bit_length())
    sr, si = power(L * (2 ** jnp.arange(nsteps)))
    scan_c = jnp.stack([jnp.concatenate([sr, sr], -1), jnp.concatenate([-si, si], -1)], 2)
    scan8 = scan_c.reshape(S5_NB, S5_GB, 2 * nsteps, 2 * P_C).transpose(0, 2, 1, 3)
    scan8 = scan8.reshape(S5_NB, 2 * nsteps, S5_GB * 2 * P_C)
    tab = lambda r, c: pl.BlockSpec((None, L, r, c), lambda g, b: (g, 0, 0, 0))
    y = pl.pallas_call(
        _s5_kernel,
        out_shape=jax.ShapeDtypeStruct((T, W_C), f32),
        grid=(S5_NB, B),
        in_specs=[pl.BlockSpec((S, LANES), lambda g, b: (b, g)),
                  tab(LANES, LANES), tab(LANES, 2 * P_C), tab(2 * P_C, LANES),
                  pl.BlockSpec((None, 2 * nsteps, S5_GB * 2 * P_C), lambda g, b: (g, 0, 0))],
        out_specs=pl.BlockSpec((S, LANES), lambda g, b: (b, g)),
        scratch_shapes=[pltpu.VMEM((L * LANES, L * LANES), bf16),
                        pltpu.VMEM((L * LANES, S5_GB * 2 * P_C), bf16),
                        pltpu.VMEM((S5_GB * 2 * P_C, L * LANES), bf16)],
        compiler_params=_params("arbitrary", "arbitrary"),
    )(proj, lagb, bendc, ccarc, scan8)
    tm = min(T, 1024)
    vec = pl.BlockSpec((None, 1, W_C), lambda i: (layer, 0, 0))
    return pl.pallas_call(
        _s5_post_kernel,
        out_shape=jax.ShapeDtypeStruct((T, W_C), bf16),
        grid=(T // tm,),
        in_specs=[pl.BlockSpec((tm, W_C), lambda i: (i, 0)),
                  pl.BlockSpec((tm, W_C), lambda i: (i, 0)),
                  vec,
                  pl.BlockSpec((None, W_C, W_C), lambda i: (layer, 0, 0)),
                  vec],
        out_specs=pl.BlockSpec((tm, W_C), lambda i: (i, 0)),
        compiler_params=_params("parallel"),
    )(y, proj, d.reshape(d.shape[0], 1, W_C), glu_w, glu_b.reshape(glu_b.shape[0], 1, W_C))


def _gdn_prep_kernel(q_ref, k_ref, v_ref, qp_ref, kp_ref, vp_ref, ab_ref, cwq_ref, cwk_ref, cwv_ref,
                     alog_ref, dtb_ref, w_ref, u_ref, qd_ref, kd_ref, qk_ref, ge_ref):
    L = CHUNK
    has_prev = (pl.program_id(1) > 0).astype(f32)

    def conv(cur_ref, prev_ref, w_ref_):
        cur = cur_ref[...]
        ext = jnp.concatenate([prev_ref[...] * has_prev, cur], axis=0)
        w = w_ref_[...]
        acc = None
        for k in range(CONV_W):
            off = SUBLANES - (CONV_W - 1) + k
            term = w[k:k + 1, :] * ext[off:off + L, :]
            acc = term if acc is None else acc + term
        return _silu(acc)

    q_all = conv(q_ref, qp_ref, cwq_ref)
    k_all = conv(k_ref, kp_ref, cwk_ref)
    v_all = conv(v_ref, vp_ref, cwv_ref)
    ab = ab_ref[...]
    g_all = -jnp.exp(alog_ref[...]) * _softplus(ab[:, 0:H_D] + dtb_ref[...])
    beta_all = _sigmoid(ab[:, H_D:2 * H_D])
    tri = _tri(L)
    tri_s = _tri(L, strict=True)
    gcs_all = _dot_exact_lhs(tri.astype(f32), g_all)
    eye_h = (_iota((H_D, H_D), 0) == _iota((H_D, H_D), 1)).astype(f32)
    gcs_t = _transpose_exact(gcs_all, eye_h)
    eye = (_iota((L, L), 0) == _iota((L, L), 1)).astype(f32)
    ge_ref[...] = jnp.concatenate([jnp.exp(gcs_all), jnp.zeros((L, LANES - H_D), f32)], axis=1)

    heads = range(H_D)
    sls = [slice(h * DK_D, (h + 1) * DK_D) for h in heads]
    gcs = [gcs_all[:, h:h + 1] for h in heads]
    eg = [jnp.exp(g) for g in gcs]
    qs = [q_all[:, sl] * lax.rsqrt(jnp.sum(q_all[:, sl] * q_all[:, sl], -1, keepdims=True) + 1e-6)
          * (DK_D ** -0.5) for sl in sls]
    ks = [k_all[:, sl] * lax.rsqrt(jnp.sum(k_all[:, sl] * k_all[:, sl], -1, keepdims=True) + 1e-6)
          for sl in sls]
    kbs = [ks[h] * beta_all[:, h:h + 1] for h in heads]
    decay = [jnp.exp(jnp.where(tri, gcs[h] - gcs_t[h:h + 1, :], -jnp.inf)) for h in heads]
    kk = [_dot_nt(kbs[h], ks[h]) for h in heads]
    qk = [_dot_nt(qs[h], ks[h]) for h in heads]
    pw = [jnp.where(tri_s, -(kk[h] * decay[h]), 0.0) for h in heads]
    inv = [eye + p for p in pw]
    for _ in range(int(math.log2(L)) - 1):
        pw = [_dot(p, p) for p in pw]
        inv = [a + _dot(a, p) for a, p in zip(inv, pw)]
    rhs = [jnp.concatenate([v_all[:, sls[h]] * beta_all[:, h:h + 1], kbs[h] * eg[h]], axis=1) for h in heads]
    sol = [_dot(inv[h], rhs[h]) for h in heads]
    for h in heads:
        u_ref[:, sls[h]] = sol[h][:, :DV_D]
        w_ref[:, sls[h]] = sol[h][:, DV_D:]
        qk_ref[:, h * L:(h + 1) * L] = jnp.where(tri, qk[h] * decay[h], 0.0)
        qd_ref[:, sls[h]] = qs[h] * eg[h]
        kd_ref[:, sls[h]] = ks[h] * jnp.exp(gcs[h][L - 1:L, :] - gcs[h])


def _gdn_scan_kernel(w_ref, u_ref, qd_ref, kd_ref, qk_ref, ge_ref, z_ref, nw_ref, o_ref, st_ref):
    L = CHUNK

    @pl.when(pl.program_id(1) == 0)
    def _():
        st_ref[...] = jnp.zeros_like(st_ref)

    nw = nw_ref[...]
    for h in range(H_D):
        sl = slice(h * DK_D, (h + 1) * DK_D)
        state = st_ref[h]
        v_new = u_ref[:, sl] - _dot(w_ref[:, sl], state)
        o = _dot(qd_ref[:, sl], state) + _dot(qk_ref[:, h * L:(h + 1) * L], v_new)
        g_last = ge_ref[L - 1:L, h:h + 1]
        st_ref[h] = state * g_last + _dot_tn(kd_ref[:, sl], v_new)
        o = o * lax.rsqrt(jnp.mean(o * o, -1, keepdims=True) + RMS_EPS) * nw
        o_ref[:, sl] = (o * _silu(z_ref[:, sl])).astype(o_ref.dtype)


def _gdn(proj, ab_tail, B, S, layer, conv_w, a_log, dt_bias, norm_w):
    T = B * S
    L = CHUNK
    nc = S // L
    per8 = L // SUBLANES
    q_c0 = W_C // W_D
    row = lambda b, c: (b * nc + c, 0)

    def cur(cb):
        return pl.BlockSpec((L, W_D), lambda b, c: (b * nc + c, cb))

    def prev(cb):
        return pl.BlockSpec((SUBLANES, W_D), lambda b, c: (jnp.maximum((b * nc + c) * per8 - 1, 0), cb))

    def cw(cb):
        return pl.BlockSpec((None, CONV_W, W_D), lambda b, c: (layer, 0, cb))

    v8 = pl.BlockSpec((None, 1, H_D), lambda b, c: (layer, 0, 0))
    wide = pl.BlockSpec((L, W_D), row)
    shp = lambda n: jax.ShapeDtypeStruct((T, n), f32)
    w_c, u_c, q_dec, k_dec, qk, gexp = pl.pallas_call(
        _gdn_prep_kernel,
        out_shape=(shp(W_D), shp(W_D), shp(W_D), shp(W_D), shp(H_D * L), shp(LANES)),
        grid=(B, nc),
        in_specs=[cur(q_c0), cur(q_c0 + 1), cur(q_c0 + 2), prev(q_c0), prev(q_c0 + 1), prev(q_c0 + 2),
                  pl.BlockSpec((L, 2 * H_D), row), cw(0), cw(1), cw(2), v8, v8],
        out_specs=(wide, wide, wide, wide, pl.BlockSpec((L, H_D * L), row),
                   pl.BlockSpec((L, LANES), row)),
        compiler_params=_params("parallel", "parallel"),
    )(proj, proj, proj, proj, proj, proj, ab_tail, conv_w, conv_w, conv_w,
      a_log.reshape(a_log.shape[0], 1, H_D), dt_bias.reshape(dt_bias.shape[0], 1, H_D))
    z_c0 = (W_C + QKV_D) // W_D
    return pl.pallas_call(
        _gdn_scan_kernel,
        out_shape=jax.ShapeDtypeStruct((T, W_D), bf16),
        grid=(B, nc),
        in_specs=[wide, wide, wide, wide, pl.BlockSpec((L, H_D * L), row),
                  pl.BlockSpec((L, LANES), row),
                  pl.BlockSpec((L, W_D), lambda b, c: (b * nc + c, z_c0)),
                  pl.BlockSpec((None, 1, DV_D), lambda b, c: (layer, 0, 0))],
        out_specs=wide,
        scratch_shapes=[pltpu.VMEM((H_D, DK_D, DV_D), f32)],
        compiler_params=_params("parallel", "arbitrary"),
    )(w_c, u_c, q_dec, k_dec, qk, gexp, proj, norm_w.reshape(norm_w.shape[0], 1, DV_D))


def _xattn_kernel(q_ref, k_ref, v_ref, o_ref):
    s = _dot_nt(q_ref[...], k_ref[...]) * (HD_X ** -0.5)
    m = jnp.max(s, -1, keepdims=True)
    p = jnp.exp(s - m)
    p = p / jnp.sum(p, -1, keepdims=True)
    o_ref[...] = jnp.dot(p.astype(bf16), v_ref[...], preferred_element_type=f32).astype(o_ref.dtype)


def _xattn(q, kv, B, S, M):
    T = B * S
    tq = min(S, 1024)
    nq = S // tq
    return pl.pallas_call(
        _xattn_kernel,
        out_shape=jax.ShapeDtypeStruct((T, D_MODEL), bf16),
        grid=(B, nq, H_X),
        in_specs=[pl.BlockSpec((tq, HD_X), lambda b, i, h: (b * nq + i, h)),
                  pl.BlockSpec((M, HD_X), lambda b, i, h: (b, h)),
                  pl.BlockSpec((M, HD_X), lambda b, i, h: (b, H_X + h))],
        out_specs=pl.BlockSpec((tq, HD_X), lambda b, i, h: (b * nq + i, h)),
        compiler_params=_params("parallel", "parallel", "parallel"),
    )(q, kv, kv)


def _route_kernel(x_ref, w_ref, b_ref, o_ref):
    logits = _dot3(x_ref[...], w_ref[...]) + b_ref[...]
    lane = _iota(logits.shape, 1)
    neg = -jnp.inf
    big = jnp.int32(LANES)
    is_g = (lane >= N_EXP) & (lane < N_EXP + NG_E)
    gl = jnp.where(is_g, logits, neg)
    gmax = jnp.max(gl, -1, keepdims=True)
    g_lane = jnp.min(jnp.where(gl == gmax, lane, big), -1, keepdims=True)
    g_prob = 1.0 / jnp.sum(jnp.where(is_g, jnp.exp(gl - gmax), 0.0), -1, keepdims=True)
    e0 = (g_lane - N_EXP) * E_PER
    sel = jnp.where((lane >= e0) & (lane < e0 + E_PER), logits, neg)
    m1 = jnp.max(sel, -1, keepdims=True)
    i1 = jnp.min(jnp.where(sel == m1, lane, big), -1, keepdims=True)
    sel2 = jnp.where(lane == i1, neg, sel)
    m2 = jnp.max(sel2, -1, keepdims=True)
    i2 = jnp.min(jnp.where(sel2 == m2, lane, big), -1, keepdims=True)
    e2 = jnp.exp(m2 - m1)
    w1 = g_prob / (1.0 + e2)
    w2 = g_prob * e2 / (1.0 + e2)
    o_ref[...] = (jnp.where(lane == 0, w1, 0.0) + jnp.where(lane == 1, w2, 0.0)
                  + jnp.where(lane == 2, i1.astype(f32), 0.0) + jnp.where(lane == 3, i2.astype(f32), 0.0))


def _route(x, w_group, b_group, w_expert, b_expert, layer):
    T, D = x.shape
    pad = LANES - N_EXP - NG_E
    w = jnp.concatenate([w_expert[layer], w_group[layer], jnp.zeros((D, pad), f32)], 1)
    b = jnp.concatenate([b_expert[layer], b_group[layer], jnp.zeros((pad,), f32)])[None, :]
    tm = min(T, 512)
    return pl.pallas_call(
        _route_kernel,
        out_shape=jax.ShapeDtypeStruct((T, LANES), f32),
        grid=(T // tm,),
        in_specs=[pl.BlockSpec((tm, D), lambda i: (i, 0)),
                  pl.BlockSpec((D, LANES), lambda i: (0, 0)),
                  pl.BlockSpec((1, LANES), lambda i: (0, 0))],
        out_specs=pl.BlockSpec((tm, LANES), lambda i: (i, 0)),
        compiler_params=_params("parallel"),
    )(x, w, b)


MOE_TM = 256
LN_TM = 256


def _moe_plan(route, T):
    n_slots = TOPK_IN * T
    n_tiles = n_slots // MOE_TM + N_EXP
    ids = route[:, 2:2 + TOPK_IN].astype(jnp.int32)
    e_flat = ids.T.reshape(n_slots)
    onehot = (e_flat[:, None] == jnp.arange(N_EXP, dtype=jnp.int32)[None, :]).astype(jnp.int32)
    csum = jnp.cumsum(onehot, axis=0)
    rank = jnp.sum(onehot * csum, axis=1) - 1
    counts = csum[-1]
    tiles_per = (counts + MOE_TM - 1) // MOE_TM
    tile_end = jnp.cumsum(tiles_per)
    tile_start = tile_end - tiles_per
    dest = jnp.sum(onehot * tile_start[None, :], axis=1) * MOE_TM + rank
    tok = jnp.arange(n_slots, dtype=jnp.int32) % T
    src = jnp.zeros((n_tiles * MOE_TM,), jnp.int32).at[dest].set(tok, unique_indices=True)
    j = jnp.arange(n_tiles, dtype=jnp.int32)
    tile_e = jnp.sum((j[:, None] >= tile_end[None, :]).astype(jnp.int32), axis=1)
    last_e = jnp.max(jnp.where(counts > 0, jnp.arange(N_EXP, dtype=jnp.int32), 0))
    tile_e = jnp.minimum(tile_e, last_e)
    n_live = tile_end[-1:].astype(jnp.int32)
    return tile_e, n_live, src, dest.astype(jnp.int32), n_tiles


def _moe_group_kernel(tile_e_ref, nt_ref, src_ref, x_hbm, wg_ref, wu_ref, wd_ref, o_ref, xbuf, sems):
    i = pl.program_id(0)
    nt = nt_ref[0]
    n_col, tm = xbuf.shape[1], xbuf.shape[2]

    def issue(tile, slot):
        def body(r, c):
            tok = src_ref[tile * tm + r]
            pltpu.make_async_copy(x_hbm.at[tok], xbuf.at[slot, pl.ds(0, n_col), r, :], sems.at[slot]).start()
            return c
        lax.fori_loop(0, tm, body, 0, unroll=8)

    @pl.when(i == 0)
    def _():
        issue(0, 0)

    @pl.when(i + 1 < nt)
    def _():
        issue(i + 1, (i + 1) % 2)

    @pl.when(i < nt)
    def _():
        slot = i % 2
        for s in range(n_col):
            pltpu.make_async_copy(x_hbm.at[pl.ds(0, tm), s, :], xbuf.at[slot, s], sems.at[slot]).wait()
        x = jnp.concatenate([xbuf[slot, s].astype(bf16) for s in range(n_col)], axis=1)
        h = _silu(jnp.dot(x, wg_ref[...].astype(bf16), preferred_element_type=f32)) * \
            jnp.dot(x, wu_ref[...].astype(bf16), preferred_element_type=f32)
        o_ref[...] = _dot(h, wd_ref[...])

    @pl.when(i >= nt)
    def _():
        o_ref[...] = jnp.zeros_like(o_ref)


def _moe_group(x3, tile_e, n_live, src, n_tiles, w_gate, w_up, w_down, layer):
    T, S3, _ = x3.shape
    D = S3 * LANES
    wspec = lambda shp: pl.BlockSpec((None, None) + shp, lambda i, te, nt, sr: (layer, te[i], 0, 0))
    return pl.pallas_call(
        _moe_group_kernel,
        out_shape=jax.ShapeDtypeStruct((n_tiles * MOE_TM, D), f32),
        grid_spec=pltpu.PrefetchScalarGridSpec(
            num_scalar_prefetch=3,
            grid=(n_tiles,),
            in_specs=[pl.BlockSpec(memory_space=pl.ANY), wspec((D, D_E)), wspec((D, D_E)),
                      wspec((D_E, D))],
            out_specs=pl.BlockSpec((MOE_TM, D), lambda i, te, nt, sr: (i, 0)),
            scratch_shapes=[pltpu.VMEM((2, S3, MOE_TM, LANES), f32), pltpu.SemaphoreType.DMA((2,))]),
        compiler_params=_params("arbitrary"),
    )(tile_e, n_live, src, x3, w_gate, w_up, w_down)


def _moe_combine_ln_kernel(pos_ref, x_ref, rt_ref, g_ref, b_ref, ys_hbm, o_ref, ob_ref, ybuf, sems):
    i = pl.program_id(0)
    n = pl.num_programs(0)
    tm = x_ref.shape[0]
    T = n * tm

    def issue(tile, slot):
        for k in range(TOPK_IN):
            def body(r, c, k=k):
                p = pos_ref[k * T + tile * tm + r]
                pltpu.make_async_copy(ys_hbm.at[pl.ds(p, 1), :], ybuf.at[slot, k, pl.ds(r, 1), :],
                                      sems.at[slot]).start()
                return c
            lax.fori_loop(0, tm, body, 0, unroll=8)

    @pl.when(i == 0)
    def _():
        issue(0, 0)

    @pl.when(i + 1 < n)
    def _():
        issue(i + 1, (i + 1) % 2)

    slot = i % 2
    for k in range(TOPK_IN):
        pltpu.make_async_copy(ys_hbm.at[pl.ds(0, tm), :], ybuf.at[slot, k], sems.at[slot]).wait()
    rt = rt_ref[...]
    y = rt[:, 0:1] * ybuf[slot, 0] + rt[:, 1:2] * ybuf[slot, 1]
    o = _layer_norm(ALPHA * x_ref[...] + y, g_ref[...], b_ref[...])
    o_ref[...] = o
    ob_ref[...] = o.astype(bf16)


def _moe_combine_ln(x, route, pos, ys, g, b, layer):
    T, D = x.shape
    tm = min(T, LN_TM)
    vec = pl.BlockSpec((None, 1, D), lambda i, p: (layer, 0, 0))
    row = pl.BlockSpec((tm, D), lambda i, p: (i, 0))
    return pl.pallas_call(
        _moe_combine_ln_kernel,
        out_shape=(jax.ShapeDtypeStruct((T, D), f32), jax.ShapeDtypeStruct((T, D), bf16)),
        grid_spec=pltpu.PrefetchScalarGridSpec(
            num_scalar_prefetch=1,
            grid=(T // tm,),
            in_specs=[row, pl.BlockSpec((tm, LANES), lambda i, p: (i, 0)), vec, vec,
                      pl.BlockSpec(memory_space=pl.ANY)],
            out_specs=(row, row),
            scratch_shapes=[pltpu.VMEM((2, TOPK_IN, tm, D), f32), pltpu.SemaphoreType.DMA((2,))]),
        compiler_params=_params("arbitrary"),
    )(pos, x, route, g.reshape(g.shape[0], 1, D), b.reshape(b.shape[0], 1, D), ys)


def _moe_ln(xf, x3, route, w_gate, w_up, w_down, g, b, layer):
    tile_e, n_live, src, pos, n_tiles = _moe_plan(route, xf.shape[0])
    ys = _moe_group(x3, tile_e, n_live, src, n_tiles, w_gate, w_up, w_down, layer)
    return _moe_combine_ln(xf, route, pos, ys, g, b, layer)


def kernel(x, mem, ab_w_in, rg_conv_w, rg_conv_b, rg_wa, rg_ba, rg_wx, rg_bx, rg_lam, ssd_conv_w, ssd_conv_b, ssd_dt_bias, ssd_a_log, ssd_d, ssd_norm_w, ab_w_out, cd_w_in, s5_a_re, s5_a_im, s5_log_step, s5_b_re, s5_b_im, s5_c_re, s5_c_im, s5_d, s5_glu_w, s5_glu_b, dn_conv_w, dn_a_log, dn_dt_bias, dn_norm_w, cd_w_out, xa_w_q, xa_w_kv, xa_w_o, moe_w_group, moe_b_group, moe_w_expert, moe_b_expert, moe_w_gate, moe_w_up, moe_w_down, ln1_g, ln1_b, ln2_g, ln2_b, ln3_g, ln3_b):
    B, S, D = x.shape
    M = mem.shape[1]
    T = B * S
    xf = x.reshape(T, D)
    xb = xf.astype(bf16)
    memb = mem.reshape(B * M, D).astype(bf16)
    for l in range(DEPTH):
        i = l // 2
        if l % 2 == 0:
            proj = _mm(xb, ab_w_in, i, MAIN_AB, f32)
            dt_tail = _mm_hi(xf, ab_w_in[i][:, MAIN_AB:])
            ya = _rglru(proj, B, S, i, rg_conv_w, rg_conv_b, rg_wa, rg_ba, rg_wx, rg_bx, rg_lam)
            yb = _ssd(proj, dt_tail, B, S, i, ssd_conv_w, ssd_conv_b, ssd_dt_bias, ssd_a_log, ssd_d,
                      ssd_norm_w)
            xf, xb = _mm_ln([ya, yb], ab_w_out, i, xf, ln1_g, ln1_b, l)
        else:
            proj = _mm(xb, cd_w_in, i, MAIN_CD, f32)
            ab_tail = _mm_hi(xf, cd_w_in[i][:, MAIN_CD:])
            yc = _s5(proj, B, S, i, s5_a_re, s5_a_im, s5_log_step, s5_b_re, s5_b_im, s5_c_re, s5_c_im,
                     s5_d, s5_glu_w, s5_glu_b)
            yd = _gdn(proj, ab_tail, B, S, i, dn_conv_w, dn_a_log, dn_dt_bias, dn_norm_w)
            xf, xb = _mm_ln([yc, yd], cd_w_out, i, xf, ln1_g, ln1_b, l)
        q = _mm(xb, xa_w_q, l, D, bf16)
        kv = _mm(memb, xa_w_kv, l, 2 * D, bf16)
        att = _xattn(q, kv, B, S, M)
        xf, x3 = _mm_ln([att], xa_w_o, l, xf, ln2_g, ln2_b, l, token_tiles=True)
        route = _route(xf, moe_w_group, moe_b_group, moe_w_expert, moe_b_expert, l)
        xf, xb = _moe_ln(xf, x3, route, moe_w_gate, moe_w_up, moe_w_down, ln3_g, ln3_b, l)
    return xf.reshape(B, S, D)
```

```python
import functools
import math

import jax
import jax.numpy as jnp
from jax import lax
from jax.experimental import pallas as pl
from jax.experimental.pallas import tpu as pltpu

f32 = jnp.float32
bf16 = jnp.bfloat16

D_MODEL = 2048
DEPTH = 2
CHUNK = 64
CONV_W = 4
ALPHA = (2 * DEPTH) ** 0.25
LN_EPS = 1e-5
RMS_EPS = 1e-6
W_A = D_MODEL // 2
H_A = 8
BW_A = W_A // H_A
RG_C = 8.0
W_B = D_MODEL
HD_B = 64
H_B = W_B // HD_B
NG_B = 2
N_B = 128
HG_B = H_B // NG_B
CONV_B = W_B + 2 * NG_B * N_B
MAIN_AB = 2 * W_A + W_B + CONV_B
W_C = D_MODEL // 2
GS_C = 16
G_C = W_C // GS_C
P_C = 64
L_C = 16
H_D = 8
DK_D = D_MODEL // 16
DV_D = D_MODEL // 16
W_D = H_D * DV_D
QKV_D = 2 * H_D * DK_D + W_D
MAIN_CD = W_C + QKV_D + W_D
H_X = 4
HD_X = D_MODEL // H_X
NG_E = 4
E_PER = 8
N_EXP = NG_E * E_PER
TOPK_IN = 2
D_E = D_MODEL // 8

LANES = 128
SUBLANES = 8
S5_GB = LANES // GS_C
S5_NB = G_C // S5_GB
VMEM_LIMIT = 56 * 1024 * 1024


def _params(*sem):
    return pltpu.CompilerParams(dimension_semantics=sem, vmem_limit_bytes=VMEM_LIMIT)


def _sigmoid(x):
    return 1.0 / (1.0 + jnp.exp(-x))


def _silu(x):
    return x * _sigmoid(x)


def _softplus(x):
    return jnp.maximum(x, 0.0) + jnp.log(1.0 + jnp.exp(-jnp.abs(x)))


def _gelu_tanh(x):
    return 0.5 * x * (1.0 + jnp.tanh(math.sqrt(2.0 / math.pi) * (x + 0.044715 * (x * x * x))))


def _dot(a, b):
    return jnp.dot(a.astype(bf16), b.astype(bf16), preferred_element_type=f32)


def _dot_nt(a, b):
    return lax.dot_general(a.astype(bf16), b.astype(bf16), (((1,), (1,)), ((), ())),
                           preferred_element_type=f32)


def _dot_tn(a, b):
    return lax.dot_general(a.astype(bf16), b.astype(bf16), (((0,), (0,)), ((), ())),
                           preferred_element_type=f32)


def _split3(a):
    hi = a.astype(bf16)
    r = a - hi.astype(f32)
    mid = r.astype(bf16)
    lo = (r - mid.astype(f32)).astype(bf16)
    return hi, mid, lo


def _dot_exact_lhs(sel, b):
    s = sel.astype(bf16)
    b1, b2, b3 = _split3(b)
    d = functools.partial(jnp.dot, preferred_element_type=f32)
    return d(s, b1) + d(s, b2) + d(s, b3)


def _dot_exact_rhs(a, sel):
    s = sel.astype(bf16)
    a1, a2, a3 = _split3(a)
    d = functools.partial(jnp.dot, preferred_element_type=f32)
    return d(a1, s) + d(a2, s) + d(a3, s)


def _transpose_exact(a, eye):
    a1, a2, a3 = _split3(a)
    e = eye.astype(bf16)
    d = lambda x: lax.dot_general(e, x, (((1,), (1,)), ((), ())), preferred_element_type=f32)
    return d(a1) + d(a2) + d(a3)


def _dot3(a, b):
    a1 = a.astype(bf16)
    a2 = (a - a1.astype(f32)).astype(bf16)
    b1 = b.astype(bf16)
    b2 = (b - b1.astype(f32)).astype(bf16)
    d = functools.partial(jnp.dot, preferred_element_type=f32)
    return d(a1, b1) + (d(a1, b2) + d(a2, b1))


def _iota(shape, axis):
    return lax.broadcasted_iota(jnp.int32, shape, axis)


def _tri(n, strict=False):
    r, c = _iota((n, n), 0), _iota((n, n), 1)
    return (r > c) if strict else (r >= c)


def _mm_kernel(x_ref, w_ref, o_ref):
    o_ref[...] = jnp.dot(x_ref[...], w_ref[...].astype(bf16),
                         preferred_element_type=f32).astype(o_ref.dtype)


def _mm(x, w, layer, n_cols, out_dtype, tn=512):
    M, K = x.shape
    tm = min(M, 2048)
    return pl.pallas_call(
        _mm_kernel,
        out_shape=jax.ShapeDtypeStruct((M, n_cols), out_dtype),
        grid=(M // tm, n_cols // tn),
        in_specs=[pl.BlockSpec((tm, K), lambda i, j: (i, 0)),
                  pl.BlockSpec((None, K, tn), lambda i, j: (layer, 0, j))],
        out_specs=pl.BlockSpec((tm, tn), lambda i, j: (i, j)),
        compiler_params=_params("parallel", "parallel"),
    )(x, w)


def _mm_hi_kernel(x_ref, w_ref, o_ref):
    o_ref[...] = _dot3(x_ref[...], w_ref[...])


def _mm_hi(x, w):
    M, K = x.shape
    n = w.shape[1]
    tm = min(M, 512)
    return pl.pallas_call(
        _mm_hi_kernel,
        out_shape=jax.ShapeDtypeStruct((M, n), f32),
        grid=(M // tm,),
        in_specs=[pl.BlockSpec((tm, K), lambda i: (i, 0)),
                  pl.BlockSpec((K, n), lambda i: (0, 0))],
        out_specs=pl.BlockSpec((tm, n), lambda i: (i, 0)),
        compiler_params=_params("parallel"),
    )(x, w)


def _layer_norm(v, g, b):
    mu = jnp.mean(v, -1, keepdims=True)
    d = v - mu
    var = jnp.mean(d * d, -1, keepdims=True)
    return d * lax.rsqrt(var + LN_EPS) * g + b


K_CHUNK = 1024
HI_MASK = 0xFFFF0000


def _pack_halves(v):
    h = v.shape[1] // 2
    lo = lax.bitcast_convert_type(v[:, :h].astype(bf16).astype(f32), jnp.uint32)
    hi = lax.bitcast_convert_type(v[:, h:].astype(bf16).astype(f32), jnp.uint32)
    return (hi & jnp.uint32(HI_MASK)) | (lo >> 16)


def _unpack_halves(u):
    lo = lax.bitcast_convert_type(u << 16, f32)
    hi = lax.bitcast_convert_type(u & jnp.uint32(HI_MASK), f32)
    return lo, hi


def _mm_ln_kernel(*refs, chunks, packed):
    n_parts = max(p for p, _ in chunks) + 1
    part_refs = refs[:n_parts]
    w_ref, x_ref, g_ref, b_ref, o_ref, o2_ref, acc_ref = refs[n_parts:]
    k = pl.program_id(1)
    for idx, (p, off) in enumerate(chunks):
        @pl.when(k == idx)
        def _(p=p, off=off, idx=idx):
            c = jnp.dot(part_refs[p][:, off:off + K_CHUNK], w_ref[...], preferred_element_type=f32)
            if idx == 0:
                acc_ref[...] = c
            else:
                acc_ref[...] += c

    @pl.when(k == len(chunks) - 1)
    def _():
        o = _layer_norm(ALPHA * x_ref[...] + acc_ref[...], g_ref[...], b_ref[...])
        o_ref[...] = o
        o2_ref[...] = _pack_halves(o) if packed else o.astype(bf16)


def _mm_ln(parts, w, x, g, b, layer, packed=False):
    T, D = x.shape
    tm = min(T, 512)
    chunks = tuple((p, off) for p, a in enumerate(parts) for off in range(0, a.shape[1], K_CHUNK))
    vec = pl.BlockSpec((None, 1, D), lambda i, k: (layer, 0, 0))
    row = pl.BlockSpec((tm, D), lambda i, k: (i, 0))
    if packed:
        shape2 = jax.ShapeDtypeStruct((T, D // 2), jnp.uint32)
        spec2 = pl.BlockSpec((tm, D // 2), lambda i, k: (i, 0))
    else:
        shape2, spec2 = jax.ShapeDtypeStruct((T, D), bf16), row
    return pl.pallas_call(
        functools.partial(_mm_ln_kernel, chunks=chunks, packed=packed),
        out_shape=(jax.ShapeDtypeStruct((T, D), f32), shape2),
        grid=(T // tm, len(chunks)),
        in_specs=[pl.BlockSpec((tm, a.shape[1]), lambda i, k: (i, 0)) for a in parts]
        + [pl.BlockSpec((K_CHUNK, D), lambda i, k: (k, 0)), row, vec, vec],
        out_specs=(row, spec2),
        scratch_shapes=[pltpu.VMEM((tm, D), f32)],
        compiler_params=_params("parallel", "arbitrary"),
    )(*parts, w, x, g.reshape(g.shape[0], 1, D), b.reshape(b.shape[0], 1, D))


def _conv_step(x_ref, w, hist_ref, first):
    tt = x_ref.shape[0]

    @pl.when(first)
    def _():
        hist_ref[0:SUBLANES, :] = jnp.zeros((SUBLANES, hist_ref.shape[1]), f32)

    hist_ref[SUBLANES:SUBLANES + tt, :] = x_ref[...]
    acc = None
    for k in range(CONV_W):
        term = w[k:k + 1, :] * hist_ref[pl.ds(SUBLANES - (CONV_W - 1) + k, tt), :]
        acc = term if acc is None else acc + term
    hist_ref[0:SUBLANES, :] = hist_ref[tt:tt + SUBLANES, :]
    return acc


def _scan_affine(a, u):
    n = a.shape[0]
    row = _iota(a.shape, 0)
    d = 1
    while d < n:
        keep = row >= d
        a_s = pltpu.roll(a, d, 0)
        u_s = pltpu.roll(u, d, 0)
        u = u + jnp.where(keep, a * u_s, 0.0)
        a = jnp.where(keep, a * a_s, a)
        d *= 2
    return a, u


def _rglru_kernel(gate_ref, xa_ref, cw_ref, cb_ref, wa_ref, ba_ref, wx_ref, bx_ref, lam_ref,
                  o_ref, hist_ref, h_ref):
    first = pl.program_id(2) == 0

    @pl.when(first)
    def _():
        h_ref[...] = jnp.zeros_like(h_ref)

    xc = _conv_step(xa_ref, cw_ref[...], hist_ref, first) + cb_ref[...]
    r = _sigmoid(_dot(xc, wa_ref[...]) + ba_ref[...])
    i = _sigmoid(_dot(xc, wx_ref[...]) + bx_ref[...])
    log_a = -RG_C * r * _softplus(-lam_ref[...])
    a = jnp.exp(log_a)
    u = jnp.sqrt(1.0 - jnp.exp(2.0 * log_a)) * (i * xc)
    a_cum, h = _scan_affine(a, u)
    h = h + a_cum * h_ref[0:1, :]
    tt = h.shape[0]
    h_ref[...] = jnp.broadcast_to(h[tt - 1:tt, :], h_ref.shape)
    o_ref[...] = (_gelu_tanh(gate_ref[...]) * h).astype(o_ref.dtype)


def _rglru(proj, B, S, layer, conv_w, conv_b, wa, ba, wx, bx, lam):
    T = B * S
    tt = min(S, 256)
    ns = S // tt
    gate_col0, xa_col0 = 0, W_A // BW_A
    row = lambda b, h, s: b * ns + s
    vec = lambda a: a.reshape(a.shape[0], 1, W_A)
    vspec = pl.BlockSpec((None, 1, BW_A), lambda b, h, s: (layer, 0, h))
    wspec = pl.BlockSpec((None, None, BW_A, BW_A), lambda b, h, s: (layer, h, 0, 0))
    return pl.pallas_call(
        _rglru_kernel,
        out_shape=jax.ShapeDtypeStruct((T, W_A), bf16),
        grid=(B, H_A, ns),
        in_specs=[pl.BlockSpec((tt, BW_A), lambda b, h, s: (row(b, h, s), gate_col0 + h)),
                  pl.BlockSpec((tt, BW_A), lambda b, h, s: (row(b, h, s), xa_col0 + h)),
                  pl.BlockSpec((None, CONV_W, BW_A), lambda b, h, s: (layer, 0, h)),
                  vspec, wspec, vspec, wspec, vspec, vspec],
        out_specs=pl.BlockSpec((tt, BW_A), lambda b, h, s: (row(b, h, s), h)),
        scratch_shapes=[pltpu.VMEM((SUBLANES + tt, BW_A), f32), pltpu.VMEM((SUBLANES, BW_A), f32)],
        compiler_params=_params("parallel", "parallel", "arbitrary"),
    )(proj, proj, conv_w, vec(conv_b), wa, vec(ba), wx, vec(bx), vec(lam))


def _ssd_kernel(z_ref, x_ref, b_ref, c_ref, dt_ref, cwx_ref, cwb_ref, cwc_ref, cbx_ref, cbb_ref,
                cbc_ref, dtb_ref, alog_ref, d_ref, nw_ref, o_ref,
                hx_ref, hb_ref, hc_ref, st_ref):
    g = pl.program_id(1)
    first = pl.program_id(2) == 0
    L = CHUNK

    @pl.when(first)
    def _():
        st_ref[...] = jnp.zeros_like(st_ref)

    X = _silu(_conv_step(x_ref, cwx_ref[...], hx_ref, first) + cbx_ref[...])
    Bc = _silu(_conv_step(b_ref, cwb_ref[...], hb_ref, first) + cbb_ref[...])
    Cc = _silu(_conv_step(c_ref, cwc_ref[...], hc_ref, first) + cbc_ref[...])

    pick = (_iota((H_B, HG_B), 0) == _iota((H_B, HG_B), 1) + g * HG_B).astype(f32)
    dt = _dot_exact_rhs(_softplus(dt_ref[...] + dtb_ref[...]), pick)
    a_neg = _dot_exact_rhs(jnp.broadcast_to(-jnp.exp(alog_ref[...]), (SUBLANES, H_B)), pick)[0:1]
    d_head = _dot_exact_rhs(jnp.broadcast_to(d_ref[...], (SUBLANES, H_B)), pick)
    adt = dt * a_neg
    tri = _tri(L)
    cs = _dot_exact_lhs(tri.astype(f32), adt)
    eye_h = (_iota((HG_B, HG_B), 0) == _iota((HG_B, HG_B), 1)).astype(f32)
    cs_t = _transpose_exact(cs, eye_h)
    cs_last = cs[L - 1:L, :]

    expand = (_iota((HG_B, HG_B * HD_B), 1) // HD_B == _iota((HG_B, HG_B * HD_B), 0)).astype(f32)
    dt_x = _dot_exact_rhs(dt, expand)
    ecs_x = _dot_exact_rhs(jnp.exp(cs), expand)
    dec_x = _dot_exact_rhs(jnp.exp(cs_last - cs), expand)
    cdec_x = _dot_exact_rhs(jnp.broadcast_to(jnp.exp(cs_last), (SUBLANES, HG_B)), expand)[0:1]
    d_x = _dot_exact_rhs(d_head, expand)[0:1]

    xdt = X * dt_x
    cb = _dot_nt(Cc, Bc)
    xdt_b = xdt.astype(bf16)
    left = _iota((L, 2 * HD_B), 1) < HD_B
    pieces = []
    for j in range(0, HG_B, 2):
        pair = xdt_b[:, j * HD_B:(j + 2) * HD_B]
        outs = []
        for jj in (j, j + 1):
            seg = jnp.where(tri, cs[:, jj:jj + 1] - cs_t[jj:jj + 1, :], -jnp.inf)
            outs.append(jnp.dot((cb * jnp.exp(seg)).astype(bf16), pair, preferred_element_type=f32))
        pieces.append(jnp.where(left, outs[0], outs[1]))
    y_diag = jnp.concatenate(pieces, axis=1)

    state = st_ref[...]
    y_off = ecs_x * _dot(Cc, state)
    st_ref[...] = state * cdec_x + _dot_tn(Bc, xdt * dec_x)

    y = y_diag + y_off + X * d_x
    yg = y * _silu(z_ref[...])
    out = yg * lax.rsqrt(jnp.mean(yg * yg, -1, keepdims=True) + RMS_EPS) * nw_ref[...]
    o_ref[...] = out.astype(o_ref.dtype)


def _ssd(proj, dt_tail, B, S, layer, conv_w, conv_b, dt_bias, a_log, d, norm_w):
    T = B * S
    L = CHUNK
    nc = S // L
    GW = W_B // NG_B
    row = lambda b, g, c: b * nc + c
    z_c0 = 2 * W_A // GW
    x_c0 = (2 * W_A + W_B) // GW
    b_c0 = (2 * W_A + W_B + W_B) // N_B
    c_c0 = b_c0 + NG_B
    cw_b0 = W_B // N_B
    cw_c0 = cw_b0 + NG_B
    conv_b3 = conv_b.reshape(conv_b.shape[0], 1, CONV_B)
    vec32 = lambda a: a.reshape(a.shape[0], 1, H_B)
    v32 = pl.BlockSpec((None, 1, H_B), lambda b, g, c: (layer, 0, 0))
    return pl.pallas_call(
        _ssd_kernel,
        out_shape=jax.ShapeDtypeStruct((T, W_B), bf16),
        grid=(B, NG_B, nc),
        in_specs=[pl.BlockSpec((L, GW), lambda b, g, c: (row(b, g, c), z_c0 + g)),
                  pl.BlockSpec((L, GW), lambda b, g, c: (row(b, g, c), x_c0 + g)),
                  pl.BlockSpec((L, N_B), lambda b, g, c: (row(b, g, c), b_c0 + g)),
                  pl.BlockSpec((L, N_B), lambda b, g, c: (row(b, g, c), c_c0 + g)),
                  pl.BlockSpec((L, H_B), lambda b, g, c: (row(b, g, c), 0)),
                  pl.BlockSpec((None, CONV_W, GW), lambda b, g, c: (layer, 0, g)),
                  pl.BlockSpec((None, CONV_W, N_B), lambda b, g, c: (layer, 0, cw_b0 + g)),
                  pl.BlockSpec((None, CONV_W, N_B), lambda b, g, c: (layer, 0, cw_c0 + g)),
                  pl.BlockSpec((None, 1, GW), lambda b, g, c: (layer, 0, g)),
                  pl.BlockSpec((None, 1, N_B), lambda b, g, c: (layer, 0, cw_b0 + g)),
                  pl.BlockSpec((None, 1, N_B), lambda b, g, c: (layer, 0, cw_c0 + g)),
                  v32, v32, v32,
                  pl.BlockSpec((None, 1, GW), lambda b, g, c: (layer, 0, g))],
        out_specs=pl.BlockSpec((L, GW), lambda b, g, c: (row(b, g, c), g)),
        scratch_shapes=[pltpu.VMEM((SUBLANES + L, GW), f32), pltpu.VMEM((SUBLANES + L, N_B), f32),
                        pltpu.VMEM((SUBLANES + L, N_B), f32), pltpu.VMEM((N_B, GW), f32)],
        compiler_params=_params("parallel", "parallel", "arbitrary"),
    )(proj, proj, proj, proj, dt_tail, conv_w, conv_w, conv_w, conv_b3, conv_b3, conv_b3,
      vec32(dt_bias), vec32(a_log), vec32(d), norm_w.reshape(norm_w.shape[0], 1, W_B))


def _s5_tables(a_re, a_im, log_step, b_re, b_im, c_re, c_im):
    L = L_C
    ar, ai = a_re.astype(f32), a_im.astype(f32)
    step = jnp.exp(log_step.astype(f32))[:, None]
    mag = jnp.exp(ar * step)
    lb_re, lb_im = mag * jnp.cos(ai * step), mag * jnp.sin(ai * step)
    den = ar * ar + ai * ai
    f_re = ((lb_re - 1.0) * ar + lb_im * ai) / den
    f_im = (lb_im * ar - (lb_re - 1.0) * ai) / den
    br, bi = b_re.astype(f32), b_im.astype(f32)
    bb_re = f_re[..., None] * br - f_im[..., None] * bi
    bb_im = f_re[..., None] * bi + f_im[..., None] * br
    cr, ci = c_re.astype(f32), c_im.astype(f32)

    def power(n):
        n = n.astype(f32)[None, :, None]
        m = jnp.exp(ar[:, None, :] * step[:, None, :] * n)
        ang = ai[:, None, :] * step[:, None, :] * n
        return m * jnp.cos(ang), m * jnp.sin(ang)

    j = jnp.arange(L)
    pr, pi = power(j)
    lbr = pr[..., None] * bb_re[:, None] - pi[..., None] * bb_im[:, None]
    lbi = pr[..., None] * bb_im[:, None] + pi[..., None] * bb_re[:, None]
    kern = (jnp.einsum('gop,gjpk->gjko', cr, lbr) - jnp.einsum('gop,gjpk->gjko', ci, lbi))
    eye = jnp.eye(S5_GB, dtype=f32)
    lagb = jnp.einsum('bgjik,gh->bjgihk', kern.reshape(S5_NB, S5_GB, L, GS_C, GS_C), eye)
    lagb = lagb.reshape(S5_NB, L, LANES, LANES)
    rev = (L - 1) - j
    bend = jnp.concatenate([jnp.take(lbr, rev, axis=1), jnp.take(lbi, rev, axis=1)], 2)
    bendc = bend.reshape(S5_NB, S5_GB, L, 2 * P_C, GS_C).transpose(0, 2, 1, 4, 3)
    bendc = bendc.reshape(S5_NB, L, LANES, 2 * P_C)
    qr, qi = power(j + 1)
    car_re = cr[:, None] * qr[:, :, None, :] - ci[:, None] * qi[:, :, None, :]
    car_im = -(cr[:, None] * qi[:, :, None, :] + ci[:, None] * qr[:, :, None, :])
    car = jnp.concatenate([car_re, car_im], -1)
    ccarc = car.reshape(S5_NB, S5_GB, L, GS_C, 2 * P_C).transpose(0, 2, 4, 1, 3)
    ccarc = ccarc.reshape(S5_NB, L, 2 * P_C, LANES)
    return lagb.astype(bf16), bendc.astype(bf16), ccarc.astype(bf16), power


def _s5_kernel(u_ref, lag_ref, bend_ref, ccar_ref, sc_ref, o_ref, toep_t, bend_t, ccar_t):
    L = L_C
    SW = 2 * P_C

    @pl.when(pl.program_id(1) == 0)
    def _():
        zero = jnp.zeros((LANES, LANES), bf16)
        for s in range(L):
            for t in range(L):
                toep_t[s * LANES:(s + 1) * LANES, t * LANES:(t + 1) * LANES] = lag_ref[t - s] if t >= s else zero
        same_g = (_iota((LANES, S5_GB * SW), 0) // GS_C) == (_iota((LANES, S5_GB * SW), 1) // SW)
        for s in range(L):
            wide = jnp.concatenate([bend_ref[s]] * S5_GB, axis=1)
            bend_t[s * LANES:(s + 1) * LANES, :] = jnp.where(same_g, wide, jnp.zeros_like(wide))
        col_g = _iota((SW, LANES), 1) // GS_C
        for t in range(L):
            blk = ccar_ref[t]
            for g in range(S5_GB):
                ccar_t[g * SW:(g + 1) * SW, t * LANES:(t + 1) * LANES] = jnp.where(col_g == g, blk, jnp.zeros_like(blk))

    n = o_ref.shape[0] // L
    U = jnp.concatenate([u_ref[pl.ds(l, n, stride=L), :].astype(bf16) for l in range(L)], axis=1)
    Y = jnp.dot(U, toep_t[...], preferred_element_type=f32)
    H_all = jnp.dot(U, bend_t[...], preferred_element_type=f32)
    row = _iota((n, SW), 0)
    prev = []
    for g in range(S5_GB):
        sl = slice(g * SW, (g + 1) * SW)
        H = H_all[:, sl]
        d, k = 1, 0
        while d < n:
            hs = pltpu.roll(H, d, 0)
            sw = pltpu.roll(hs, P_C, 1)
            H = H + jnp.where(row >= d, sc_ref[2 * k:2 * k + 1, sl] * hs + sc_ref[2 * k + 1:2 * k + 2, sl] * sw, 0.0)
            d *= 2
            k += 1
        prev.append(jnp.where(row >= 1, pltpu.roll(H, 1, 0), 0.0).astype(bf16))
    Y = Y + jnp.dot(jnp.concatenate(prev, axis=1), ccar_t[...], preferred_element_type=f32)
    for l in range(L):
        o_ref[pl.ds(l, n, stride=L), :] = Y[:, l * LANES:(l + 1) * LANES]


def _s5_post_kernel(y_ref, u_ref, d_ref, w_ref, b_ref, o_ref):
    y = y_ref[...] + d_ref[...] * u_ref[...]
    g = _gelu_tanh(y)
    o_ref[...] = (g * _sigmoid(_dot(g, w_ref[...]) + b_ref[...])).astype(o_ref.dtype)


def _s5(proj, B, S, layer, a_re, a_im, log_step, b_re, b_im, c_re, c_im, d, glu_w, glu_b):
    T = B * S
    L = L_C
    nch = S // L
    lagb, bendc, ccarc, power = _s5_tables(a_re[layer], a_im[layer], log_step[layer], b_re[layer],
                                           b_im[layer], c_re[layer], c_im[layer])
    nsteps = max(1, (nch - 1).bit_length())
    sr, si = power(L * (2 ** jnp.arange(nsteps)))
    scan_c = jnp.stack([jnp.concatenate([sr, sr], -1), jnp.concatenate([-si, si], -1)], 2)
    scan8 = scan_c.reshape(S5_NB, S5_GB, 2 * nsteps, 2 * P_C).transpose(0, 2, 1, 3)
    scan8 = scan8.reshape(S5_NB, 2 * nsteps, S5_GB * 2 * P_C)
    tab = lambda r, c: pl.BlockSpec((None, L, r, c), lambda g, b: (g, 0, 0, 0))
    y = pl.pallas_call(
        _s5_kernel,
        out_shape=jax.ShapeDtypeStruct((T, W_C), f32),
        grid=(S5_NB, B),
        in_specs=[pl.BlockSpec((S, LANES), lambda g, b: (b, g)),
                  tab(LANES, LANES), tab(LANES, 2 * P_C), tab(2 * P_C, LANES),
                  pl.BlockSpec((None, 2 * nsteps, S5_GB * 2 * P_C), lambda g, b: (g, 0, 0))],
        out_specs=pl.BlockSpec((S, LANES), lambda g, b: (b, g)),
        scratch_shapes=[pltpu.VMEM((L * LANES, L * LANES), bf16),
                        pltpu.VMEM((L * LANES, S5_GB * 2 * P_C), bf16),
                        pltpu.VMEM((S5_GB * 2 * P_C, L * LANES), bf16)],
        compiler_params=_params("arbitrary", "arbitrary"),
    )(proj, lagb, bendc, ccarc, scan8)
    tm = min(T, 1024)
    vec = pl.BlockSpec((None, 1, W_C), lambda i: (layer, 0, 0))
    return pl.pallas_call(
        _s5_post_kernel,
        out_shape=jax.ShapeDtypeStruct((T, W_C), bf16),
        grid=(T // tm,),
        in_specs=[pl.BlockSpec((tm, W_C), lambda i: (i, 0)),
                  pl.BlockSpec((tm, W_C), lambda i: (i, 0)),
                  vec,
                  pl.BlockSpec((None, W_C, W_C), lambda i: (layer, 0, 0)),
                  vec],
        out_specs=pl.BlockSpec((tm, W_C), lambda i: (i, 0)),
        compiler_params=_params("parallel"),
    )(y, proj, d.reshape(d.shape[0], 1, W_C), glu_w, glu_b.reshape(glu_b.shape[0], 1, W_C))


def _gdn_prep_kernel(q_ref, k_ref, v_ref, qp_ref, kp_ref, vp_ref, ab_ref, cwq_ref, cwk_ref, cwv_ref,
                     alog_ref, dtb_ref, w_ref, u_ref, qd_ref, kd_ref, qk_ref, ge_ref):
    L = CHUNK
    has_prev = (pl.program_id(1) > 0).astype(f32)

    def conv(cur_ref, prev_ref, w_ref_):
        cur = cur_ref[...]
        ext = jnp.concatenate([prev_ref[...] * has_prev, cur], axis=0)
        w = w_ref_[...]
        acc = None
        for k in range(CONV_W):
            off = SUBLANES - (CONV_W - 1) + k
            term = w[k:k + 1, :] * ext[off:off + L, :]
            acc = term if acc is None else acc + term
        return _silu(acc)

    q_all = conv(q_ref, qp_ref, cwq_ref)
    k_all = conv(k_ref, kp_ref, cwk_ref)
    v_all = conv(v_ref, vp_ref, cwv_ref)
    ab = ab_ref[...]
    g_all = -jnp.exp(alog_ref[...]) * _softplus(ab[:, 0:H_D] + dtb_ref[...])
    beta_all = _sigmoid(ab[:, H_D:2 * H_D])
    tri = _tri(L)
    tri_s = _tri(L, strict=True)
    gcs_all = _dot_exact_lhs(tri.astype(f32), g_all)
    eye_h = (_iota((H_D, H_D), 0) == _iota((H_D, H_D), 1)).astype(f32)
    gcs_t = _transpose_exact(gcs_all, eye_h)
    eye = (_iota((L, L), 0) == _iota((L, L), 1)).astype(f32)
    ge_ref[...] = jnp.concatenate([jnp.exp(gcs_all), jnp.zeros((L, LANES - H_D), f32)], axis=1)

    heads = range(H_D)
    sls = [slice(h * DK_D, (h + 1) * DK_D) for h in heads]
    gcs = [gcs_all[:, h:h + 1] for h in heads]
    eg = [jnp.exp(g) for g in gcs]
    qs = [q_all[:, sl] * lax.rsqrt(jnp.sum(q_all[:, sl] * q_all[:, sl], -1, keepdims=True) + 1e-6)
          * (DK_D ** -0.5) for sl in sls]
    ks = [k_all[:, sl] * lax.rsqrt(jnp.sum(k_all[:, sl] * k_all[:, sl], -1, keepdims=True) + 1e-6)
          for sl in sls]
    kbs = [ks[h] * beta_all[:, h:h + 1] for h in heads]
    decay = [jnp.exp(jnp.where(tri, gcs[h] - gcs_t[h:h + 1, :], -jnp.inf)) for h in heads]
    kk = [_dot_nt(kbs[h], ks[h]) for h in heads]
    qk = [_dot_nt(qs[h], ks[h]) for h in heads]
    pw = [jnp.where(tri_s, -(kk[h] * decay[h]), 0.0) for h in heads]
    inv = [eye + p for p in pw]
    for _ in range(int(math.log2(L)) - 1):
        pw = [_dot(p, p) for p in pw]
        inv = [a + _dot(a, p) for a, p in zip(inv, pw)]
    rhs = [jnp.concatenate([v_all[:, sls[h]] * beta_all[:, h:h + 1], kbs[h] * eg[h]], axis=1) for h in heads]
    sol = [_dot(inv[h], rhs[h]) for h in heads]
    for h in heads:
        u_ref[:, sls[h]] = sol[h][:, :DV_D]
        w_ref[:, sls[h]] = sol[h][:, DV_D:]
        qk_ref[:, h * L:(h + 1) * L] = jnp.where(tri, qk[h] * decay[h], 0.0)
        qd_ref[:, sls[h]] = qs[h] * eg[h]
        kd_ref[:, sls[h]] = ks[h] * jnp.exp(gcs[h][L - 1:L, :] - gcs[h])


def _gdn_scan_kernel(w_ref, u_ref, qd_ref, kd_ref, qk_ref, ge_ref, z_ref, nw_ref, o_ref, st_ref):
    L = CHUNK

    @pl.when(pl.program_id(0) == 0)
    def _():
        st_ref[...] = jnp.zeros_like(st_ref)

    nw = nw_ref[...]
    pairs = [(b, h) for b in range(w_ref.shape[0]) for h in range(H_D)]
    cols = lambda h: slice(h * DK_D, (h + 1) * DK_D)
    state = [st_ref[b * H_D + h] for b, h in pairs]
    v_new = [u_ref[b, :, cols(h)] - _dot(w_ref[b, :, cols(h)], s) for (b, h), s in zip(pairs, state)]
    o = [_dot(qd_ref[b, :, cols(h)], s) + _dot(qk_ref[b, :, h * L:(h + 1) * L], v)
         for (b, h), s, v in zip(pairs, state, v_new)]
    for (b, h), s, v in zip(pairs, state, v_new):
        st_ref[b * H_D + h] = s * ge_ref[b, L - 1:L, h:h + 1] + _dot_tn(kd_ref[b, :, cols(h)], v)
    for (b, h), y in zip(pairs, o):
        y = y * lax.rsqrt(jnp.mean(y * y, -1, keepdims=True) + RMS_EPS) * nw
        o_ref[b, :, cols(h)] = (y * _silu(z_ref[b, :, cols(h)])).astype(o_ref.dtype)


def _gdn(proj, ab_tail, B, S, layer, conv_w, a_log, dt_bias, norm_w):
    T = B * S
    L = CHUNK
    nc = S // L
    per8 = L // SUBLANES
    q_c0 = W_C // W_D
    row = lambda b, c: (b * nc + c, 0)

    def cur(cb):
        return pl.BlockSpec((L, W_D), lambda b, c: (b * nc + c, cb))

    def prev(cb):
        return pl.BlockSpec((SUBLANES, W_D), lambda b, c: (jnp.maximum((b * nc + c) * per8 - 1, 0), cb))

    def cw(cb):
        return pl.BlockSpec((None, CONV_W, W_D), lambda b, c: (layer, 0, cb))

    v8 = pl.BlockSpec((None, 1, H_D), lambda b, c: (layer, 0, 0))
    wide = pl.BlockSpec((L, W_D), row)
    shp = lambda n: jax.ShapeDtypeStruct((T, n), f32)
    w_c, u_c, q_dec, k_dec, qk, gexp = pl.pallas_call(
        _gdn_prep_kernel,
        out_shape=(shp(W_D), shp(W_D), shp(W_D), shp(W_D), shp(H_D * L), shp(LANES)),
        grid=(B, nc),
        in_specs=[cur(q_c0), cur(q_c0 + 1), cur(q_c0 + 2), prev(q_c0), prev(q_c0 + 1), prev(q_c0 + 2),
                  pl.BlockSpec((L, 2 * H_D), row), cw(0), cw(1), cw(2), v8, v8],
        out_specs=(wide, wide, wide, wide, pl.BlockSpec((L, H_D * L), row),
                   pl.BlockSpec((L, LANES), row)),
        compiler_params=_params("parallel", "parallel"),
    )(proj, proj, proj, proj, proj, proj, ab_tail, conv_w, conv_w, conv_w,
      a_log.reshape(a_log.shape[0], 1, H_D), dt_bias.reshape(dt_bias.shape[0], 1, H_D))
    z_c0 = (W_C + QKV_D) // W_D
    per_batch = lambda a: a.reshape(B, S, a.shape[1])
    blk = lambda n, cb=0: pl.BlockSpec((B, L, n), lambda c: (0, c, cb))
    out = pl.pallas_call(
        _gdn_scan_kernel,
        out_shape=jax.ShapeDtypeStruct((B, S, W_D), bf16),
        grid=(nc,),
        in_specs=[blk(W_D), blk(W_D), blk(W_D), blk(W_D), blk(H_D * L), blk(LANES), blk(W_D, z_c0),
                  pl.BlockSpec((None, 1, DV_D), lambda c: (layer, 0, 0))],
        out_specs=blk(W_D),
        scratch_shapes=[pltpu.VMEM((B * H_D, DK_D, DV_D), f32)],
        compiler_params=_params("arbitrary"),
    )(per_batch(w_c), per_batch(u_c), per_batch(q_dec), per_batch(k_dec), per_batch(qk), per_batch(gexp),
      per_batch(proj), norm_w.reshape(norm_w.shape[0], 1, DV_D))
    return out.reshape(T, W_D)


def _xattn_kernel(q_ref, k_ref, v_ref, o_ref):
    s = _dot_nt(q_ref[...], k_ref[...]) * (HD_X ** -0.5)
    m = jnp.max(s, -1, keepdims=True)
    p = jnp.exp(s - m)
    p = p / jnp.sum(p, -1, keepdims=True)
    o_ref[...] = jnp.dot(p.astype(bf16), v_ref[...], preferred_element_type=f32).astype(o_ref.dtype)


def _xattn(q, kv, B, S, M):
    T = B * S
    tq = min(S, 1024)
    nq = S // tq
    return pl.pallas_call(
        _xattn_kernel,
        out_shape=jax.ShapeDtypeStruct((T, D_MODEL), bf16),
        grid=(B, nq, H_X),
        in_specs=[pl.BlockSpec((tq, HD_X), lambda b, i, h: (b * nq + i, h)),
                  pl.BlockSpec((M, HD_X), lambda b, i, h: (b, h)),
                  pl.BlockSpec((M, HD_X), lambda b, i, h: (b, H_X + h))],
        out_specs=pl.BlockSpec((tq, HD_X), lambda b, i, h: (b * nq + i, h)),
        compiler_params=_params("parallel", "parallel", "parallel"),
    )(q, kv, kv)


def _route_kernel(x_ref, w_ref, b_ref, o_ref):
    logits = _dot3(x_ref[...], w_ref[...]) + b_ref[...]
    lane = _iota(logits.shape, 1)
    neg = -jnp.inf
    big = jnp.int32(LANES)
    is_g = (lane >= N_EXP) & (lane < N_EXP + NG_E)
    gl = jnp.where(is_g, logits, neg)
    gmax = jnp.max(gl, -1, keepdims=True)
    g_lane = jnp.min(jnp.where(gl == gmax, lane, big), -1, keepdims=True)
    g_prob = 1.0 / jnp.sum(jnp.where(is_g, jnp.exp(gl - gmax), 0.0), -1, keepdims=True)
    e0 = (g_lane - N_EXP) * E_PER
    sel = jnp.where((lane >= e0) & (lane < e0 + E_PER), logits, neg)
    m1 = jnp.max(sel, -1, keepdims=True)
    i1 = jnp.min(jnp.where(sel == m1, lane, big), -1, keepdims=True)
    sel2 = jnp.where(lane == i1, neg, sel)
    m2 = jnp.max(sel2, -1, keepdims=True)
    i2 = jnp.min(jnp.where(sel2 == m2, lane, big), -1, keepdims=True)
    e2 = jnp.exp(m2 - m1)
    w1 = g_prob / (1.0 + e2)
    w2 = g_prob * e2 / (1.0 + e2)
    o_ref[...] = (jnp.where(lane == 0, w1, 0.0) + jnp.where(lane == 1, w2, 0.0)
                  + jnp.where(lane == 2, i1.astype(f32), 0.0) + jnp.where(lane == 3, i2.astype(f32), 0.0))


def _route(x, w_group, b_group, w_expert, b_expert, layer):
    T, D = x.shape
    pad = LANES - N_EXP - NG_E
    w = jnp.concatenate([w_expert[layer], w_group[layer], jnp.zeros((D, pad), f32)], 1)
    b = jnp.concatenate([b_expert[layer], b_group[layer], jnp.zeros((pad,), f32)])[None, :]
    tm = min(T, 512)
    return pl.pallas_call(
        _route_kernel,
        out_shape=jax.ShapeDtypeStruct((T, LANES), f32),
        grid=(T // tm,),
        in_specs=[pl.BlockSpec((tm, D), lambda i: (i, 0)),
                  pl.BlockSpec((D, LANES), lambda i: (0, 0)),
                  pl.BlockSpec((1, LANES), lambda i: (0, 0))],
        out_specs=pl.BlockSpec((tm, LANES), lambda i: (i, 0)),
        compiler_params=_params("parallel"),
    )(x, w, b)


MOE_TM = 256
LN_TM = 256


def _moe_plan(route, T):
    n_slots = TOPK_IN * T
    n_tiles = n_slots // MOE_TM + N_EXP
    ids = route[:, 2:2 + TOPK_IN].astype(jnp.int32)
    e_flat = ids.T.reshape(n_slots)
    onehot = (e_flat[:, None] == jnp.arange(N_EXP, dtype=jnp.int32)[None, :]).astype(jnp.int32)
    csum = jnp.cumsum(onehot, axis=0)
    rank = jnp.sum(onehot * csum, axis=1) - 1
    counts = csum[-1]
    tiles_per = (counts + MOE_TM - 1) // MOE_TM
    tile_end = jnp.cumsum(tiles_per)
    tile_start = tile_end - tiles_per
    dest = jnp.sum(onehot * tile_start[None, :], axis=1) * MOE_TM + rank
    tok = jnp.arange(n_slots, dtype=jnp.int32) % T
    src = jnp.zeros((n_tiles * MOE_TM,), jnp.int32).at[dest].set(tok, unique_indices=True)
    j = jnp.arange(n_tiles, dtype=jnp.int32)
    tile_e = jnp.sum((j[:, None] >= tile_end[None, :]).astype(jnp.int32), axis=1)
    last_e = jnp.max(jnp.where(counts > 0, jnp.arange(N_EXP, dtype=jnp.int32), 0))
    tile_e = jnp.minimum(tile_e, last_e)
    n_live = tile_end[-1:].astype(jnp.int32)
    return tile_e, n_live, src, dest.astype(jnp.int32), n_tiles


def _moe_group_kernel(tile_e_ref, nt_ref, src_ref, x_hbm, wg_ref, wu_ref, wd_ref, o_ref, xbuf, sems):
    i = pl.program_id(0)
    nt = nt_ref[0]
    tm = xbuf.shape[1]

    def issue(tile, slot):
        def body(r, c):
            tok = src_ref[tile * tm + r]
            pltpu.make_async_copy(x_hbm.at[pl.ds(tok, 1), :], xbuf.at[slot, pl.ds(r, 1), :],
                                  sems.at[slot]).start()
            return c
        lax.fori_loop(0, tm, body, 0, unroll=8)

    @pl.when(i == 0)
    def _():
        issue(0, 0)

    @pl.when(i + 1 < nt)
    def _():
        issue(i + 1, (i + 1) % 2)

    @pl.when(i < nt)
    def _():
        slot = i % 2
        pltpu.make_async_copy(x_hbm.at[pl.ds(0, tm), :], xbuf.at[slot], sems.at[slot]).wait()
        lo, hi = _unpack_halves(xbuf[slot])
        x = jnp.concatenate([lo.astype(bf16), hi.astype(bf16)], axis=1)
        h = _silu(jnp.dot(x, wg_ref[...].astype(bf16), preferred_element_type=f32)) * \
            jnp.dot(x, wu_ref[...].astype(bf16), preferred_element_type=f32)
        o_ref[...] = _pack_halves(_dot(h, wd_ref[...]))

    @pl.when(i >= nt)
    def _():
        o_ref[...] = jnp.zeros_like(o_ref)


def _moe_group(xp, tile_e, n_live, src, n_tiles, w_gate, w_up, w_down, layer):
    T, half = xp.shape
    D = 2 * half
    wspec = lambda shp: pl.BlockSpec((None, None) + shp, lambda i, te, nt, sr: (layer, te[i], 0, 0))
    return pl.pallas_call(
        _moe_group_kernel,
        out_shape=jax.ShapeDtypeStruct((n_tiles * MOE_TM, half), jnp.uint32),
        grid_spec=pltpu.PrefetchScalarGridSpec(
            num_scalar_prefetch=3,
            grid=(n_tiles,),
            in_specs=[pl.BlockSpec(memory_space=pl.ANY), wspec((D, D_E)), wspec((D, D_E)),
                      wspec((D_E, D))],
            out_specs=pl.BlockSpec((MOE_TM, half), lambda i, te, nt, sr: (i, 0)),
            scratch_shapes=[pltpu.VMEM((2, MOE_TM, half), jnp.uint32), pltpu.SemaphoreType.DMA((2,))]),
        compiler_params=_params("arbitrary"),
    )(tile_e, n_live, src, xp, w_gate, w_up, w_down)


def _moe_combine_ln_kernel(pos_ref, x_ref, rt_ref, g_ref, b_ref, ys_hbm, o_ref, ob_ref, ybuf, sems):
    i = pl.program_id(0)
    n = pl.num_programs(0)
    tm = x_ref.shape[0]
    T = n * tm

    def issue(tile, slot):
        for k in range(TOPK_IN):
            def body(r, c, k=k):
                p = pos_ref[k * T + tile * tm + r]
                pltpu.make_async_copy(ys_hbm.at[pl.ds(p, 1), :], ybuf.at[slot, k, pl.ds(r, 1), :],
                                      sems.at[slot]).start()
                return c
            lax.fori_loop(0, tm, body, 0, unroll=8)

    @pl.when(i == 0)
    def _():
        issue(0, 0)

    @pl.when(i + 1 < n)
    def _():
        issue(i + 1, (i + 1) % 2)

    slot = i % 2
    for k in range(TOPK_IN):
        pltpu.make_async_copy(ys_hbm.at[pl.ds(0, tm), :], ybuf.at[slot, k], sems.at[slot]).wait()
    rt = rt_ref[...]
    w1, w2 = rt[:, 0:1], rt[:, 1:2]
    lo1, hi1 = _unpack_halves(ybuf[slot, 0])
    lo2, hi2 = _unpack_halves(ybuf[slot, 1])
    y = jnp.concatenate([w1 * lo1 + w2 * lo2, w1 * hi1 + w2 * hi2], axis=1)
    o = _layer_norm(ALPHA * x_ref[...] + y, g_ref[...], b_ref[...])
    o_ref[...] = o
    ob_ref[...] = o.astype(bf16)


def _moe_combine_ln(x, route, pos, ys, g, b, layer):
    T, D = x.shape
    tm = min(T, LN_TM)
    vec = pl.BlockSpec((None, 1, D), lambda i, p: (layer, 0, 0))
    row = pl.BlockSpec((tm, D), lambda i, p: (i, 0))
    return pl.pallas_call(
        _moe_combine_ln_kernel,
        out_shape=(jax.ShapeDtypeStruct((T, D), f32), jax.ShapeDtypeStruct((T, D), bf16)),
        grid_spec=pltpu.PrefetchScalarGridSpec(
            num_scalar_prefetch=1,
            grid=(T // tm,),
            in_specs=[row, pl.BlockSpec((tm, LANES), lambda i, p: (i, 0)), vec, vec,
                      pl.BlockSpec(memory_space=pl.ANY)],
            out_specs=(row, row),
            scratch_shapes=[pltpu.VMEM((2, TOPK_IN, tm, D // 2), jnp.uint32),
                            pltpu.SemaphoreType.DMA((2,))]),
        compiler_params=_params("arbitrary"),
    )(pos, x, route, g.reshape(g.shape[0], 1, D), b.reshape(b.shape[0], 1, D), ys)


def _moe_ln(xf, xp, route, w_gate, w_up, w_down, g, b, layer):
    tile_e, n_live, src, pos, n_tiles = _moe_plan(route, xf.shape[0])
    ys = _moe_group(xp, tile_e, n_live, src, n_tiles, w_gate, w_up, w_down, layer)
    return _moe_combine_ln(xf, route, pos, ys, g, b, layer)


def kernel(x, mem, ab_w_in, rg_conv_w, rg_conv_b, rg_wa, rg_ba, rg_wx, rg_bx, rg_lam, ssd_conv_w, ssd_conv_b, ssd_dt_bias, ssd_a_log, ssd_d, ssd_norm_w, ab_w_out, cd_w_in, s5_a_re, s5_a_im, s5_log_step, s5_b_re, s5_b_im, s5_c_re, s5_c_im, s5_d, s5_glu_w, s5_glu_b, dn_conv_w, dn_a_log, dn_dt_bias, dn_norm_w, cd_w_out, xa_w_q, xa_w_kv, xa_w_o, moe_w_group, moe_b_group, moe_w_expert, moe_b_expert, moe_w_gate, moe_w_up, moe_w_down, ln1_g, ln1_b, ln2_g, ln2_b, ln3_g, ln3_b):
    B, S, D = x.shape
    M = mem.shape[1]
    T = B * S
    xf = x.reshape(T, D)
    xb = xf.astype(bf16)
    memb = mem.reshape(B * M, D).astype(bf16)
    for l in range(DEPTH):
        i = l // 2
        if l % 2 == 0:
            proj = _mm(xb, ab_w_in, i, MAIN_AB, f32)
            dt_tail = _mm_hi(xf, ab_w_in[i][:, MAIN_AB:])
            ya = _rglru(proj, B, S, i, rg_conv_w, rg_conv_b, rg_wa, rg_ba, rg_wx, rg_bx, rg_lam)
            yb = _ssd(proj, dt_tail, B, S, i, ssd_conv_w, ssd_conv_b, ssd_dt_bias, ssd_a_log, ssd_d,
                      ssd_norm_w)
            xf, xb = _mm_ln([ya, yb], ab_w_out[i].astype(bf16), xf, ln1_g, ln1_b, l)
        else:
            proj = _mm(xb, cd_w_in, i, MAIN_CD, f32)
            ab_tail = _mm_hi(xf, cd_w_in[i][:, MAIN_CD:])
            yc = _s5(proj, B, S, i, s5_a_re, s5_a_im, s5_log_step, s5_b_re, s5_b_im, s5_c_re, s5_c_im,
                     s5_d, s5_glu_w, s5_glu_b)
            yd = _gdn(proj, ab_tail, B, S, i, dn_conv_w, dn_a_log, dn_dt_bias, dn_norm_w)
            xf, xb = _mm_ln([yc, yd], cd_w_out[i].astype(bf16), xf, ln1_g, ln1_b, l)
        q = _mm(xb, xa_w_q, l, D, bf16)
        kv = _mm(memb, xa_w_kv, l, 2 * D, bf16)
        att = _xattn(q, kv, B, S, M)
        xf, xp = _mm_ln([att], xa_w_o[l].astype(bf16), xf, ln2_g, ln2_b, l, packed=True)
        route = _route(xf, moe_w_group, moe_b_group, moe_w_expert, moe_b_expert, l)
        xf, xb = _moe_ln(xf, xp, route, moe_w_gate, moe_w_up, moe_w_down, ln3_g, ln3_b, l)
    return xf.reshape(B, S, D)
```

```python
import functools
import math

import jax
import jax.numpy as jnp
from jax import lax
from jax.experimental import pallas as pl
from jax.experimental.pallas import tpu as pltpu

f32 = jnp.float32
bf16 = jnp.bfloat16

D_MODEL = 2048
DEPTH = 2
CHUNK = 64
CONV_W = 4
ALPHA = (2 * DEPTH) ** 0.25
LN_EPS = 1e-5
RMS_EPS = 1e-6
W_A = D_MODEL // 2
H_A = 8
BW_A = W_A // H_A
RG_C = 8.0
W_B = D_MODEL
HD_B = 64
H_B = W_B // HD_B
NG_B = 2
N_B = 128
HG_B = H_B // NG_B
CONV_B = W_B + 2 * NG_B * N_B
MAIN_AB = 2 * W_A + W_B + CONV_B
W_C = D_MODEL // 2
GS_C = 16
G_C = W_C // GS_C
P_C = 64
L_C = 16
H_D = 8
DK_D = D_MODEL // 16
DV_D = D_MODEL // 16
W_D = H_D * DV_D
QKV_D = 2 * H_D * DK_D + W_D
MAIN_CD = W_C + QKV_D + W_D
H_X = 4
HD_X = D_MODEL // H_X
NG_E = 4
E_PER = 8
N_EXP = NG_E * E_PER
TOPK_IN = 2
D_E = D_MODEL // 8

LANES = 128
SUBLANES = 8
S5_GB = LANES // GS_C
S5_NB = G_C // S5_GB
VMEM_LIMIT = 56 * 1024 * 1024


def _params(*sem):
    return pltpu.CompilerParams(dimension_semantics=sem, vmem_limit_bytes=VMEM_LIMIT)


def _sigmoid(x):
    return 1.0 / (1.0 + jnp.exp(-x))


def _silu(x):
    return x * _sigmoid(x)


def _softplus(x):
    return jnp.maximum(x, 0.0) + jnp.log(1.0 + jnp.exp(-jnp.abs(x)))


def _gelu_tanh(x):
    return 0.5 * x * (1.0 + jnp.tanh(math.sqrt(2.0 / math.pi) * (x + 0.044715 * (x * x * x))))


def _dot(a, b):
    return jnp.dot(a.astype(bf16), b.astype(bf16), preferred_element_type=f32)


def _dot_nt(a, b):
    return lax.dot_general(a.astype(bf16), b.astype(bf16), (((1,), (1,)), ((), ())),
                           preferred_element_type=f32)


def _dot_tn(a, b):
    return lax.dot_general(a.astype(bf16), b.astype(bf16), (((0,), (0,)), ((), ())),
                           preferred_element_type=f32)


def _split3(a):
    hi = a.astype(bf16)
    r = a - hi.astype(f32)
    mid = r.astype(bf16)
    lo = (r - mid.astype(f32)).astype(bf16)
    return hi, mid, lo


def _dot_exact_lhs(sel, b):
    s = sel.astype(bf16)
    b1, b2, b3 = _split3(b)
    d = functools.partial(jnp.dot, preferred_element_type=f32)
    return d(s, b1) + d(s, b2) + d(s, b3)


def _dot_exact_rhs(a, sel):
    s = sel.astype(bf16)
    a1, a2, a3 = _split3(a)
    d = functools.partial(jnp.dot, preferred_element_type=f32)
    return d(a1, s) + d(a2, s) + d(a3, s)


def _transpose_exact(a, eye):
    a1, a2, a3 = _split3(a)
    e = eye.astype(bf16)
    d = lambda x: lax.dot_general(e, x, (((1,), (1,)), ((), ())), preferred_element_type=f32)
    return d(a1) + d(a2) + d(a3)


def _dot3(a, b):
    a1 = a.astype(bf16)
    a2 = (a - a1.astype(f32)).astype(bf16)
    b1 = b.astype(bf16)
    b2 = (b - b1.astype(f32)).astype(bf16)
    d = functools.partial(jnp.dot, preferred_element_type=f32)
    return d(a1, b1) + (d(a1, b2) + d(a2, b1))


def _iota(shape, axis):
    return lax.broadcasted_iota(jnp.int32, shape, axis)


def _tri(n, strict=False):
    r, c = _iota((n, n), 0), _iota((n, n), 1)
    return (r > c) if strict else (r >= c)


def _mm_kernel(x_ref, w_ref, o_ref):
    o_ref[...] = jnp.dot(x_ref[...], w_ref[...].astype(bf16),
                         preferred_element_type=f32).astype(o_ref.dtype)


def _mm(x, w, layer, n_cols, out_dtype, tn=512):
    M, K = x.shape
    tm = min(M, 2048)
    return pl.pallas_call(
        _mm_kernel,
        out_shape=jax.ShapeDtypeStruct((M, n_cols), out_dtype),
        grid=(M // tm, n_cols // tn),
        in_specs=[pl.BlockSpec((tm, K), lambda i, j: (i, 0)),
                  pl.BlockSpec((None, K, tn), lambda i, j: (layer, 0, j))],
        out_specs=pl.BlockSpec((tm, tn), lambda i, j: (i, j)),
        compiler_params=_params("parallel", "parallel"),
    )(x, w)


def _mm_hi_kernel(x_ref, w_ref, o_ref):
    o_ref[...] = _dot3(x_ref[...], w_ref[...])


def _mm_hi(x, w):
    M, K = x.shape
    n = w.shape[1]
    tm = min(M, 512)
    return pl.pallas_call(
        _mm_hi_kernel,
        out_shape=jax.ShapeDtypeStruct((M, n), f32),
        grid=(M // tm,),
        in_specs=[pl.BlockSpec((tm, K), lambda i: (i, 0)),
                  pl.BlockSpec((K, n), lambda i: (0, 0))],
        out_specs=pl.BlockSpec((tm, n), lambda i: (i, 0)),
        compiler_params=_params("parallel"),
    )(x, w)


def _layer_norm(v, g, b):
    mu = jnp.mean(v, -1, keepdims=True)
    d = v - mu
    var = jnp.mean(d * d, -1, keepdims=True)
    return d * lax.rsqrt(var + LN_EPS) * g + b


K_CHUNK = 1024
HI_MASK = 0xFFFF0000


def _pack_halves(v):
    h = v.shape[1] // 2
    lo = lax.bitcast_convert_type(v[:, :h].astype(bf16).astype(f32), jnp.uint32)
    hi = lax.bitcast_convert_type(v[:, h:].astype(bf16).astype(f32), jnp.uint32)
    return (hi & jnp.uint32(HI_MASK)) | (lo >> 16)


def _unpack_halves(u):
    lo = lax.bitcast_convert_type(u << 16, f32)
    hi = lax.bitcast_convert_type(u & jnp.uint32(HI_MASK), f32)
    return lo, hi


def _mm_ln_kernel(*refs, chunks, packed):
    n_parts = max(p for p, _ in chunks) + 1
    part_refs = refs[:n_parts]
    w_ref, x_ref, g_ref, b_ref, o_ref, o2_ref, acc_ref = refs[n_parts:]
    k = pl.program_id(1)
    for idx, (p, off) in enumerate(chunks):
        @pl.when(k == idx)
        def _(p=p, off=off, idx=idx):
            c = jnp.dot(part_refs[p][:, off:off + K_CHUNK], w_ref[...], preferred_element_type=f32)
            if idx == 0:
                acc_ref[...] = c
            else:
                acc_ref[...] += c

    @pl.when(k == len(chunks) - 1)
    def _():
        o = _layer_norm(ALPHA * x_ref[...] + acc_ref[...], g_ref[...], b_ref[...])
        o_ref[...] = o
        o2_ref[...] = _pack_halves(o) if packed else o.astype(bf16)


def _mm_ln(parts, w, x, g, b, layer, packed=False):
    T, D = x.shape
    tm = min(T, 512)
    chunks = tuple((p, off) for p, a in enumerate(parts) for off in range(0, a.shape[1], K_CHUNK))
    vec = pl.BlockSpec((None, 1, D), lambda i, k: (layer, 0, 0))
    row = pl.BlockSpec((tm, D), lambda i, k: (i, 0))
    if packed:
        shape2 = jax.ShapeDtypeStruct((T, D // 2), jnp.uint32)
        spec2 = pl.BlockSpec((tm, D // 2), lambda i, k: (i, 0))
    else:
        shape2, spec2 = jax.ShapeDtypeStruct((T, D), bf16), row
    return pl.pallas_call(
        functools.partial(_mm_ln_kernel, chunks=chunks, packed=packed),
        out_shape=(jax.ShapeDtypeStruct((T, D), f32), shape2),
        grid=(T // tm, len(chunks)),
        in_specs=[pl.BlockSpec((tm, a.shape[1]), lambda i, k: (i, 0)) for a in parts]
        + [pl.BlockSpec((K_CHUNK, D), lambda i, k: (k, 0)), row, vec, vec],
        out_specs=(row, spec2),
        scratch_shapes=[pltpu.VMEM((tm, D), f32)],
        compiler_params=_params("parallel", "arbitrary"),
    )(*parts, w, x, g.reshape(g.shape[0], 1, D), b.reshape(b.shape[0], 1, D))


def _conv_step(x_ref, w, hist_ref, first):
    tt = x_ref.shape[0]

    @pl.when(first)
    def _():
        hist_ref[0:SUBLANES, :] = jnp.zeros((SUBLANES, hist_ref.shape[1]), f32)

    hist_ref[SUBLANES:SUBLANES + tt, :] = x_ref[...]
    acc = None
    for k in range(CONV_W):
        term = w[k:k + 1, :] * hist_ref[pl.ds(SUBLANES - (CONV_W - 1) + k, tt), :]
        acc = term if acc is None else acc + term
    hist_ref[0:SUBLANES, :] = hist_ref[tt:tt + SUBLANES, :]
    return acc


def _scan_affine(a, u):
    n = a.shape[0]
    row = _iota(a.shape, 0)
    d = 1
    while d < n:
        keep = row >= d
        a_s = pltpu.roll(a, d, 0)
        u_s = pltpu.roll(u, d, 0)
        u = u + jnp.where(keep, a * u_s, 0.0)
        a = jnp.where(keep, a * a_s, a)
        d *= 2
    return a, u


def _rglru_kernel(gate_ref, xa_ref, cw_ref, cb_ref, wa_ref, ba_ref, wx_ref, bx_ref, lam_ref,
                  o_ref, hist_ref, h_ref):
    first = pl.program_id(2) == 0

    @pl.when(first)
    def _():
        h_ref[...] = jnp.zeros_like(h_ref)

    xc = _conv_step(xa_ref, cw_ref[...], hist_ref, first) + cb_ref[...]
    r = _sigmoid(_dot(xc, wa_ref[...]) + ba_ref[...])
    i = _sigmoid(_dot(xc, wx_ref[...]) + bx_ref[...])
    log_a = -RG_C * r * _softplus(-lam_ref[...])
    a = jnp.exp(log_a)
    u = jnp.sqrt(1.0 - jnp.exp(2.0 * log_a)) * (i * xc)
    a_cum, h = _scan_affine(a, u)
    h = h + a_cum * h_ref[0:1, :]
    tt = h.shape[0]
    h_ref[...] = jnp.broadcast_to(h[tt - 1:tt, :], h_ref.shape)
    o_ref[...] = (_gelu_tanh(gate_ref[...]) * h).astype(o_ref.dtype)


def _rglru(proj, B, S, layer, conv_w, conv_b, wa, ba, wx, bx, lam):
    T = B * S
    tt = min(S, 256)
    ns = S // tt
    gate_col0, xa_col0 = 0, W_A // BW_A
    row = lambda b, h, s: b * ns + s
    vec = lambda a: a.reshape(a.shape[0], 1, W_A)
    vspec = pl.BlockSpec((None, 1, BW_A), lambda b, h, s: (layer, 0, h))
    wspec = pl.BlockSpec((None, None, BW_A, BW_A), lambda b, h, s: (layer, h, 0, 0))
    return pl.pallas_call(
        _rglru_kernel,
        out_shape=jax.ShapeDtypeStruct((T, W_A), bf16),
        grid=(B, H_A, ns),
        in_specs=[pl.BlockSpec((tt, BW_A), lambda b, h, s: (row(b, h, s), gate_col0 + h)),
                  pl.BlockSpec((tt, BW_A), lambda b, h, s: (row(b, h, s), xa_col0 + h)),
                  pl.BlockSpec((None, CONV_W, BW_A), lambda b, h, s: (layer, 0, h)),
                  vspec, wspec, vspec, wspec, vspec, vspec],
        out_specs=pl.BlockSpec((tt, BW_A), lambda b, h, s: (row(b, h, s), h)),
        scratch_shapes=[pltpu.VMEM((SUBLANES + tt, BW_A), f32), pltpu.VMEM((SUBLANES, BW_A), f32)],
        compiler_params=_params("parallel", "parallel", "arbitrary"),
    )(proj, proj, conv_w, vec(conv_b), wa, vec(ba), wx, vec(bx), vec(lam))


def _conv_rows(x_ref, w, hist_ref):
    tt = x_ref.shape[0]
    hist_ref[SUBLANES:SUBLANES + tt, :] = x_ref[...]
    acc = None
    for k in range(CONV_W):
        term = w[k:k + 1, :] * hist_ref[pl.ds(SUBLANES - (CONV_W - 1) + k, tt), :]
        acc = term if acc is None else acc + term
    hist_ref[0:SUBLANES, :] = hist_ref[tt:tt + SUBLANES, :]
    return acc


def _ssd_group(X, Bc, Cc, z, dt, a_neg, d_head, nw, st_ref):
    L = CHUNK
    adt = dt * a_neg
    tri = _tri(L)
    cs = _dot_exact_lhs(tri.astype(f32), adt)
    eye_h = (_iota((HG_B, HG_B), 0) == _iota((HG_B, HG_B), 1)).astype(f32)
    cs_t = _transpose_exact(cs, eye_h)
    cs_last = cs[L - 1:L, :]

    expand = (_iota((HG_B, HG_B * HD_B), 1) // HD_B == _iota((HG_B, HG_B * HD_B), 0)).astype(f32)
    dt_x = _dot_exact_rhs(dt, expand)
    ecs_x = _dot_exact_rhs(jnp.exp(cs), expand)
    dec_x = _dot_exact_rhs(jnp.exp(cs_last - cs), expand)
    cdec_x = _dot_exact_rhs(jnp.broadcast_to(jnp.exp(cs_last), (SUBLANES, HG_B)), expand)[0:1]
    d_x = _dot_exact_rhs(jnp.broadcast_to(d_head, (SUBLANES, HG_B)), expand)[0:1]

    xdt = X * dt_x
    cb = _dot_nt(Cc, Bc)
    xdt_b = xdt.astype(bf16)
    left = _iota((L, 2 * HD_B), 1) < HD_B
    pieces = []
    for j in range(0, HG_B, 2):
        pair = xdt_b[:, j * HD_B:(j + 2) * HD_B]
        outs = []
        for jj in (j, j + 1):
            seg = jnp.where(tri, cs[:, jj:jj + 1] - cs_t[jj:jj + 1, :], -jnp.inf)
            outs.append(jnp.dot((cb * jnp.exp(seg)).astype(bf16), pair, preferred_element_type=f32))
        pieces.append(jnp.where(left, outs[0], outs[1]))
    y_diag = jnp.concatenate(pieces, axis=1)

    state = st_ref[...]
    y_off = ecs_x * _dot(Cc, state)
    st_ref[...] = state * cdec_x + _dot_tn(Bc, xdt * dec_x)

    y = y_diag + y_off + X * d_x
    yg = y * _silu(z)
    return yg * lax.rsqrt(jnp.mean(yg * yg, -1, keepdims=True) + RMS_EPS) * nw


def _ssd_kernel(z_ref, x_ref, b_ref, c_ref, dt_ref, cw_ref, cb_ref, dtb_ref, alog_ref, d_ref, nw_ref,
                o_ref, hx_ref, hb_ref, hc_ref, st_ref):
    GW = W_B // NG_B

    @pl.when(pl.program_id(0) == 0)
    def _():
        st_ref[...] = jnp.zeros_like(st_ref)
        for h in (hx_ref, hb_ref, hc_ref):
            h[:, 0:SUBLANES, :] = jnp.zeros((h.shape[0], SUBLANES, h.shape[2]), f32)

    cw, cbias = cw_ref[...], cb_ref[...]
    c_off = W_B + NG_B * N_B
    a_all = -jnp.exp(alog_ref[...])
    d_all = d_ref[...]
    nw_all = nw_ref[...]
    for b in range(x_ref.shape[0]):
        X = _silu(_conv_rows(x_ref.at[b], cw[:, :W_B], hx_ref.at[b]) + cbias[:, :W_B])
        Bm = _silu(_conv_rows(b_ref.at[b], cw[:, W_B:c_off], hb_ref.at[b]) + cbias[:, W_B:c_off])
        Cm = _silu(_conv_rows(c_ref.at[b], cw[:, c_off:], hc_ref.at[b]) + cbias[:, c_off:])
        dt = _softplus(dt_ref[b] + dtb_ref[...])
        for g in range(NG_B):
            hs = slice(g * HG_B, (g + 1) * HG_B)
            ws = slice(g * GW, (g + 1) * GW)
            ns = slice(g * N_B, (g + 1) * N_B)
            out = _ssd_group(X[:, ws], Bm[:, ns], Cm[:, ns], z_ref[b, :, ws], dt[:, hs], a_all[:, hs],
                             d_all[:, hs], nw_all[:, ws], st_ref.at[b * NG_B + g])
            o_ref[b, :, ws] = out.astype(o_ref.dtype)


def _ssd(proj, dt_tail, B, S, layer, conv_w, conv_b, dt_bias, a_log, d, norm_w):
    L = CHUNK
    nc = S // L
    BC = NG_B * N_B
    per_batch = lambda a: a.reshape(B, S, a.shape[1])
    blk = lambda n, cb: pl.BlockSpec((B, L, n), lambda c: (0, c, cb))
    vec = lambda a: a.reshape(a.shape[0], 1, a.shape[-1])
    par = lambda r, n: pl.BlockSpec((None, r, n), lambda c: (layer, 0, 0))
    proj3 = per_batch(proj)
    out = pl.pallas_call(
        _ssd_kernel,
        out_shape=jax.ShapeDtypeStruct((B, S, W_B), bf16),
        grid=(nc,),
        in_specs=[blk(W_B, 2 * W_A // W_B), blk(W_B, (2 * W_A + W_B) // W_B),
                  blk(BC, (2 * W_A + 2 * W_B) // BC), blk(BC, (2 * W_A + 2 * W_B) // BC + 1),
                  blk(H_B, 0), par(CONV_W, CONV_B), par(1, CONV_B), par(1, H_B), par(1, H_B), par(1, H_B),
                  par(1, W_B)],
        out_specs=blk(W_B, 0),
        scratch_shapes=[pltpu.VMEM((B, SUBLANES + L, W_B), f32), pltpu.VMEM((B, SUBLANES + L, BC), f32),
                        pltpu.VMEM((B, SUBLANES + L, BC), f32), pltpu.VMEM((B * NG_B, N_B, W_B // NG_B), f32)],
        compiler_params=_params("arbitrary"),
    )(proj3, proj3, proj3, proj3, per_batch(dt_tail), conv_w, vec(conv_b), vec(dt_bias), vec(a_log), vec(d),
      vec(norm_w))
    return out.reshape(B * S, W_B)


def _s5_tables(a_re, a_im, log_step, b_re, b_im, c_re, c_im):
    L = L_C
    ar, ai = a_re.astype(f32), a_im.astype(f32)
    step = jnp.exp(log_step.astype(f32))[:, None]
    mag = jnp.exp(ar * step)
    lb_re, lb_im = mag * jnp.cos(ai * step), mag * jnp.sin(ai * step)
    den = ar * ar + ai * ai
    f_re = ((lb_re - 1.0) * ar + lb_im * ai) / den
    f_im = (lb_im * ar - (lb_re - 1.0) * ai) / den
    br, bi = b_re.astype(f32), b_im.astype(f32)
    bb_re = f_re[..., None] * br - f_im[..., None] * bi
    bb_im = f_re[..., None] * bi + f_im[..., None] * br
    cr, ci = c_re.astype(f32), c_im.astype(f32)

    def power(n):
        n = n.astype(f32)[None, :, None]
        m = jnp.exp(ar[:, None, :] * step[:, None, :] * n)
        ang = ai[:, None, :] * step[:, None, :] * n
        return m * jnp.cos(ang), m * jnp.sin(ang)

    j = jnp.arange(L)
    pr, pi = power(j)
    lbr = pr[..., None] * bb_re[:, None] - pi[..., None] * bb_im[:, None]
    lbi = pr[..., None] * bb_im[:, None] + pi[..., None] * bb_re[:, None]
    kern = (jnp.einsum('gop,gjpk->gjko', cr, lbr) - jnp.einsum('gop,gjpk->gjko', ci, lbi))
    eye = jnp.eye(S5_GB, dtype=f32)
    lagb = jnp.einsum('bgjik,gh->bjgihk', kern.reshape(S5_NB, S5_GB, L, GS_C, GS_C), eye)
    lagb = lagb.reshape(S5_NB, L, LANES, LANES)
    rev = (L - 1) - j
    bend = jnp.concatenate([jnp.take(lbr, rev, axis=1), jnp.take(lbi, rev, axis=1)], 2)
    bendc = bend.reshape(S5_NB, S5_GB, L, 2 * P_C, GS_C).transpose(0, 2, 1, 4, 3)
    bendc = bendc.reshape(S5_NB, L, LANES, 2 * P_C)
    qr, qi = power(j + 1)
    car_re = cr[:, None] * qr[:, :, None, :] - ci[:, None] * qi[:, :, None, :]
    car_im = -(cr[:, None] * qi[:, :, None, :] + ci[:, None] * qr[:, :, None, :])
    car = jnp.concatenate([car_re, car_im], -1)
    ccarc = car.reshape(S5_NB, S5_GB, L, GS_C, 2 * P_C).transpose(0, 2, 4, 1, 3)
    ccarc = ccarc.reshape(S5_NB, L, 2 * P_C, LANES)
    return lagb.astype(bf16), bendc.astype(bf16), ccarc.astype(bf16), power


def _s5_kernel(u_ref, lag_ref, bend_ref, ccar_ref, sc_ref, o_ref, toep_t, bend_t, ccar_t):
    L = L_C
    SW = 2 * P_C

    @pl.when(pl.program_id(1) == 0)
    def _():
        zero = jnp.zeros((LANES, LANES), bf16)
        for s in range(L):
            for t in range(L):
                toep_t[s * LANES:(s + 1) * LANES, t * LANES:(t + 1) * LANES] = lag_ref[t - s] if t >= s else zero
        same_g = (_iota((LANES, S5_GB * SW), 0) // GS_C) == (_iota((LANES, S5_GB * SW), 1) // SW)
        for s in range(L):
            wide = jnp.concatenate([bend_ref[s]] * S5_GB, axis=1)
            bend_t[s * LANES:(s + 1) * LANES, :] = jnp.where(same_g, wide, jnp.zeros_like(wide))
        col_g = _iota((SW, LANES), 1) // GS_C
        for t in range(L):
            blk = ccar_ref[t]
            for g in range(S5_GB):
                ccar_t[g * SW:(g + 1) * SW, t * LANES:(t + 1) * LANES] = jnp.where(col_g == g, blk, jnp.zeros_like(blk))

    n = o_ref.shape[0] // L
    U = jnp.concatenate([u_ref[pl.ds(l, n, stride=L), :].astype(bf16) for l in range(L)], axis=1)
    Y = jnp.dot(U, toep_t[...], preferred_element_type=f32)
    H_all = jnp.dot(U, bend_t[...], preferred_element_type=f32)
    row = _iota((n, SW), 0)
    prev = []
    for g in range(S5_GB):
        sl = slice(g * SW, (g + 1) * SW)
        H = H_all[:, sl]
        d, k = 1, 0
        while d < n:
            hs = pltpu.roll(H, d, 0)
            sw = pltpu.roll(hs, P_C, 1)
            H = H + jnp.where(row >= d, sc_ref[2 * k:2 * k + 1, sl] * hs + sc_ref[2 * k + 1:2 * k + 2, sl] * sw, 0.0)
            d *= 2
            k += 1
        prev.append(jnp.where(row >= 1, pltpu.roll(H, 1, 0), 0.0).astype(bf16))
    Y = Y + jnp.dot(jnp.concatenate(prev, axis=1), ccar_t[...], preferred_element_type=f32)
    for l in range(L):
        o_ref[pl.ds(l, n, stride=L), :] = Y[:, l * LANES:(l + 1) * LANES]


def _s5_post_kernel(y_ref, u_ref, d_ref, w_ref, b_ref, o_ref):
    y = y_ref[...] + d_ref[...] * u_ref[...]
    g = _gelu_tanh(y)
    o_ref[...] = (g * _sigmoid(_dot(g, w_ref[...]) + b_ref[...])).astype(o_ref.dtype)


def _s5(proj, B, S, layer, a_re, a_im, log_step, b_re, b_im, c_re, c_im, d, glu_w, glu_b):
    T = B * S
    L = L_C
    nch = S // L
    lagb, bendc, ccarc, power = _s5_tables(a_re[layer], a_im[layer], log_step[layer], b_re[layer],
                                           b_im[layer], c_re[layer], c_im[layer])
    nsteps = max(1, (nch - 1).bit_length())
    sr, si = power(L * (2 ** jnp.arange(nsteps)))
    scan_c = jnp.stack([jnp.concatenate([sr, sr], -1), jnp.concatenate([-si, si], -1)], 2)
    scan8 = scan_c.reshape(S5_NB, S5_GB, 2 * nsteps, 2 * P_C).transpose(0, 2, 1, 3)
    scan8 = scan8.reshape(S5_NB, 2 * nsteps, S5_GB * 2 * P_C)
    tab = lambda r, c: pl.BlockSpec((None, L, r, c), lambda g, b: (g, 0, 0, 0))
    y = pl.pallas_call(
        _s5_kernel,
        out_shape=jax.ShapeDtypeStruct((T, W_C), f32),
        grid=(S5_NB, B),
        in_specs=[pl.BlockSpec((S, LANES), lambda g, b: (b, g)),
                  tab(LANES, LANES), tab(LANES, 2 * P_C), tab(2 * P_C, LANES),
                  pl.BlockSpec((None, 2 * nsteps, S5_GB * 2 * P_C), lambda g, b: (g, 0, 0))],
        out_specs=pl.BlockSpec((S, LANES), lambda g, b: (b, g)),
        scratch_shapes=[pltpu.VMEM((L * LANES, L * LANES), bf16),
                        pltpu.VMEM((L * LANES, S5_GB * 2 * P_C), bf16),
                        pltpu.VMEM((S5_GB * 2 * P_C, L * LANES), bf16)],
        compiler_params=_params("arbitrary", "arbitrary"),
    )(proj, lagb, bendc, ccarc, scan8)
    tm = min(T, 1024)
    vec = pl.BlockSpec((None, 1, W_C), lambda i: (layer, 0, 0))
    return pl.pallas_call(
        _s5_post_kernel,
        out_shape=jax.ShapeDtypeStruct((T, W_C), bf16),
        grid=(T // tm,),
        in_specs=[pl.BlockSpec((tm, W_C), lambda i: (i, 0)),
                  pl.BlockSpec((tm, W_C), lambda i: (i, 0)),
                  vec,
                  pl.BlockSpec((None, W_C, W_C), lambda i: (layer, 0, 0)),
                  vec],
        out_specs=pl.BlockSpec((tm, W_C), lambda i: (i, 0)),
        compiler_params=_params("parallel"),
    )(y, proj, d.reshape(d.shape[0], 1, W_C), glu_w, glu_b.reshape(glu_b.shape[0], 1, W_C))


def _gdn_prep_kernel(q_ref, k_ref, v_ref, qp_ref, kp_ref, vp_ref, ab_ref, cwq_ref, cwk_ref, cwv_ref,
                     alog_ref, dtb_ref, w_ref, u_ref, qd_ref, kd_ref, qk_ref, ge_ref):
    L = CHUNK
    has_prev = (pl.program_id(1) > 0).astype(f32)

    def conv(cur_ref, prev_ref, w_ref_):
        cur = cur_ref[...]
        ext = jnp.concatenate([prev_ref[...] * has_prev, cur], axis=0)
        w = w_ref_[...]
        acc = None
        for k in range(CONV_W):
            off = SUBLANES - (CONV_W - 1) + k
            term = w[k:k + 1, :] * ext[off:off + L, :]
            acc = term if acc is None else acc + term
        return _silu(acc)

    q_all = conv(q_ref, qp_ref, cwq_ref)
    k_all = conv(k_ref, kp_ref, cwk_ref)
    v_all = conv(v_ref, vp_ref, cwv_ref)
    ab = ab_ref[...]
    g_all = -jnp.exp(alog_ref[...]) * _softplus(ab[:, 0:H_D] + dtb_ref[...])
    beta_all = _sigmoid(ab[:, H_D:2 * H_D])
    tri = _tri(L)
    tri_s = _tri(L, strict=True)
    gcs_all = _dot_exact_lhs(tri.astype(f32), g_all)
    eye_h = (_iota((H_D, H_D), 0) == _iota((H_D, H_D), 1)).astype(f32)
    gcs_t = _transpose_exact(gcs_all, eye_h)
    eye = (_iota((L, L), 0) == _iota((L, L), 1)).astype(f32)
    ge_ref[...] = jnp.concatenate([jnp.exp(gcs_all), jnp.zeros((L, LANES - H_D), f32)], axis=1)

    heads = range(H_D)
    sls = [slice(h * DK_D, (h + 1) * DK_D) for h in heads]
    gcs = [gcs_all[:, h:h + 1] for h in heads]
    eg = [jnp.exp(g) for g in gcs]
    qs = [q_all[:, sl] * lax.rsqrt(jnp.sum(q_all[:, sl] * q_all[:, sl], -1, keepdims=True) + 1e-6)
          * (DK_D ** -0.5) for sl in sls]
    ks = [k_all[:, sl] * lax.rsqrt(jnp.sum(k_all[:, sl] * k_all[:, sl], -1, keepdims=True) + 1e-6)
          for sl in sls]
    kbs = [ks[h] * beta_all[:, h:h + 1] for h in heads]
    decay = [jnp.exp(jnp.where(tri, gcs[h] - gcs_t[h:h + 1, :], -jnp.inf)) for h in heads]
    kk = [_dot_nt(kbs[h], ks[h]) for h in heads]
    qk = [_dot_nt(qs[h], ks[h]) for h in heads]
    pw = [jnp.where(tri_s, -(kk[h] * decay[h]), 0.0) for h in heads]
    inv = [eye + p for p in pw]
    for _ in range(int(math.log2(L)) - 1):
        pw = [_dot(p, p) for p in pw]
        inv = [a + _dot(a, p) for a, p in zip(inv, pw)]
    rhs = [jnp.concatenate([v_all[:, sls[h]] * beta_all[:, h:h + 1], kbs[h] * eg[h]], axis=1) for h in heads]
    sol = [_dot(inv[h], rhs[h]) for h in heads]
    for h in heads:
        u_ref[:, sls[h]] = sol[h][:, :DV_D]
        w_ref[:, sls[h]] = sol[h][:, DV_D:]
        qk_ref[:, h * L:(h + 1) * L] = jnp.where(tri, qk[h] * decay[h], 0.0)
        qd_ref[:, sls[h]] = qs[h] * eg[h]
        kd_ref[:, sls[h]] = ks[h] * jnp.exp(gcs[h][L - 1:L, :] - gcs[h])


def _gdn_scan_kernel(w_ref, u_ref, qd_ref, kd_ref, qk_ref, ge_ref, z_ref, nw_ref, o_ref, st_ref):
    L = CHUNK

    @pl.when(pl.program_id(0) == 0)
    def _():
        st_ref[...] = jnp.zeros_like(st_ref)

    nw = nw_ref[...]
    pairs = [(b, h) for b in range(w_ref.shape[0]) for h in range(H_D)]
    cols = lambda h: slice(h * DK_D, (h + 1) * DK_D)
    state = [st_ref[b * H_D + h] for b, h in pairs]
    v_new = [u_ref[b, :, cols(h)] - _dot(w_ref[b, :, cols(h)], s) for (b, h), s in zip(pairs, state)]
    o = [_dot(qd_ref[b, :, cols(h)], s) + _dot(qk_ref[b, :, h * L:(h + 1) * L], v)
         for (b, h), s, v in zip(pairs, state, v_new)]
    for (b, h), s, v in zip(pairs, state, v_new):
        st_ref[b * H_D + h] = s * ge_ref[b, L - 1:L, h:h + 1] + _dot_tn(kd_ref[b, :, cols(h)], v)
    for (b, h), y in zip(pairs, o):
        y = y * lax.rsqrt(jnp.mean(y * y, -1, keepdims=True) + RMS_EPS) * nw
        o_ref[b, :, cols(h)] = (y * _silu(z_ref[b, :, cols(h)])).astype(o_ref.dtype)


def _gdn(proj, ab_tail, B, S, layer, conv_w, a_log, dt_bias, norm_w):
    T = B * S
    L = CHUNK
    nc = S // L
    per8 = L // SUBLANES
    q_c0 = W_C // W_D
    row = lambda b, c: (b * nc + c, 0)

    def cur(cb):
        return pl.BlockSpec((L, W_D), lambda b, c: (b * nc + c, cb))

    def prev(cb):
        return pl.BlockSpec((SUBLANES, W_D), lambda b, c: (jnp.maximum((b * nc + c) * per8 - 1, 0), cb))

    def cw(cb):
        return pl.BlockSpec((None, CONV_W, W_D), lambda b, c: (layer, 0, cb))

    v8 = pl.BlockSpec((None, 1, H_D), lambda b, c: (layer, 0, 0))
    wide = pl.BlockSpec((L, W_D), row)
    shp = lambda n: jax.ShapeDtypeStruct((T, n), f32)
    w_c, u_c, q_dec, k_dec, qk, gexp = pl.pallas_call(
        _gdn_prep_kernel,
        out_shape=(shp(W_D), shp(W_D), shp(W_D), shp(W_D), shp(H_D * L), shp(LANES)),
        grid=(B, nc),
        in_specs=[cur(q_c0), cur(q_c0 + 1), cur(q_c0 + 2), prev(q_c0), prev(q_c0 + 1), prev(q_c0 + 2),
                  pl.BlockSpec((L, 2 * H_D), row), cw(0), cw(1), cw(2), v8, v8],
        out_specs=(wide, wide, wide, wide, pl.BlockSpec((L, H_D * L), row),
                   pl.BlockSpec((L, LANES), row)),
        compiler_params=_params("parallel", "parallel"),
    )(proj, proj, proj, proj, proj, proj, ab_tail, conv_w, conv_w, conv_w,
      a_log.reshape(a_log.shape[0], 1, H_D), dt_bias.reshape(dt_bias.shape[0], 1, H_D))
    z_c0 = (W_C + QKV_D) // W_D
    per_batch = lambda a: a.reshape(B, S, a.shape[1])
    blk = lambda n, cb=0: pl.BlockSpec((B, L, n), lambda c: (0, c, cb))
    out = pl.pallas_call(
        _gdn_scan_kernel,
        out_shape=jax.ShapeDtypeStruct((B, S, W_D), bf16),
        grid=(nc,),
        in_specs=[blk(W_D), blk(W_D), blk(W_D), blk(W_D), blk(H_D * L), blk(LANES), blk(W_D, z_c0),
                  pl.BlockSpec((None, 1, DV_D), lambda c: (layer, 0, 0))],
        out_specs=blk(W_D),
        scratch_shapes=[pltpu.VMEM((B * H_D, DK_D, DV_D), f32)],
        compiler_params=_params("arbitrary"),
    )(per_batch(w_c), per_batch(u_c), per_batch(q_dec), per_batch(k_dec), per_batch(qk), per_batch(gexp),
      per_batch(proj), norm_w.reshape(norm_w.shape[0], 1, DV_D))
    return out.reshape(T, W_D)


def _xattn_kernel(q_ref, k_ref, v_ref, o_ref):
    s = _dot_nt(q_ref[...], k_ref[...]) * (HD_X ** -0.5)
    m = jnp.max(s, -1, keepdims=True)
    p = jnp.exp(s - m)
    p = p / jnp.sum(p, -1, keepdims=True)
    o_ref[...] = jnp.dot(p.astype(bf16), v_ref[...], preferred_element_type=f32).astype(o_ref.dtype)


def _xattn(q, kv, B, S, M):
    T = B * S
    tq = min(S, 1024)
    nq = S // tq
    return pl.pallas_call(
        _xattn_kernel,
        out_shape=jax.ShapeDtypeStruct((T, D_MODEL), bf16),
        grid=(B, nq, H_X),
        in_specs=[pl.BlockSpec((tq, HD_X), lambda b, i, h: (b * nq + i, h)),
                  pl.BlockSpec((M, HD_X), lambda b, i, h: (b, h)),
                  pl.BlockSpec((M, HD_X), lambda b, i, h: (b, H_X + h))],
        out_specs=pl.BlockSpec((tq, HD_X), lambda b, i, h: (b * nq + i, h)),
        compiler_params=_params("parallel", "parallel", "parallel"),
    )(q, kv, kv)


def _route_kernel(x_ref, w_ref, b_ref, o_ref):
    logits = _dot3(x_ref[...], w_ref[...]) + b_ref[...]
    lane = _iota(logits.shape, 1)
    neg = -jnp.inf
    big = jnp.int32(LANES)
    is_g = (lane >= N_EXP) & (lane < N_EXP + NG_E)
    gl = jnp.where(is_g, logits, neg)
    gmax = jnp.max(gl, -1, keepdims=True)
    g_lane = jnp.min(jnp.where(gl == gmax, lane, big), -1, keepdims=True)
    g_prob = 1.0 / jnp.sum(jnp.where(is_g, jnp.exp(gl - gmax), 0.0), -1, keepdims=True)
    e0 = (g_lane - N_EXP) * E_PER
    sel = jnp.where((lane >= e0) & (lane < e0 + E_PER), logits, neg)
    m1 = jnp.max(sel, -1, keepdims=True)
    i1 = jnp.min(jnp.where(sel == m1, lane, big), -1, keepdims=True)
    sel2 = jnp.where(lane == i1, neg, sel)
    m2 = jnp.max(sel2, -1, keepdims=True)
    i2 = jnp.min(jnp.where(sel2 == m2, lane, big), -1, keepdims=True)
    e2 = jnp.exp(m2 - m1)
    w1 = g_prob / (1.0 + e2)
    w2 = g_prob * e2 / (1.0 + e2)
    o_ref[...] = (jnp.where(lane == 0, w1, 0.0) + jnp.where(lane == 1, w2, 0.0)
                  + jnp.where(lane == 2, i1.astype(f32), 0.0) + jnp.where(lane == 3, i2.astype(f32), 0.0))


def _route(x, w_group, b_group, w_expert, b_expert, layer):
    T, D = x.shape
    pad = LANES - N_EXP - NG_E
    w = jnp.concatenate([w_expert[layer], w_group[layer], jnp.zeros((D, pad), f32)], 1)
    b = jnp.concatenate([b_expert[layer], b_group[layer], jnp.zeros((pad,), f32)])[None, :]
    tm = min(T, 512)
    return pl.pallas_call(
        _route_kernel,
        out_shape=jax.ShapeDtypeStruct((T, LANES), f32),
        grid=(T // tm,),
        in_specs=[pl.BlockSpec((tm, D), lambda i: (i, 0)),
                  pl.BlockSpec((D, LANES), lambda i: (0, 0)),
                  pl.BlockSpec((1, LANES), lambda i: (0, 0))],
        out_specs=pl.BlockSpec((tm, LANES), lambda i: (i, 0)),
        compiler_params=_params("parallel"),
    )(x, w, b)


MOE_TM = 256
LN_TM = 256


def _moe_plan(route, T):
    n_slots = TOPK_IN * T
    n_tiles = n_slots // MOE_TM + N_EXP
    ids = route[:, 2:2 + TOPK_IN].astype(jnp.int32)
    e_flat = ids.T.reshape(n_slots)
    onehot = (e_flat[:, None] == jnp.arange(N_EXP, dtype=jnp.int32)[None, :]).astype(jnp.int32)
    csum = jnp.cumsum(onehot, axis=0)
    rank = jnp.sum(onehot * csum, axis=1) - 1
    counts = csum[-1]
    tiles_per = (counts + MOE_TM - 1) // MOE_TM
    tile_end = jnp.cumsum(tiles_per)
    tile_start = tile_end - tiles_per
    dest = jnp.sum(onehot * tile_start[None, :], axis=1) * MOE_TM + rank
    tok = jnp.arange(n_slots, dtype=jnp.int32) % T
    src = jnp.zeros((n_tiles * MOE_TM,), jnp.int32).at[dest].set(tok, unique_indices=True)
    j = jnp.arange(n_tiles, dtype=jnp.int32)
    tile_e = jnp.sum((j[:, None] >= tile_end[None, :]).astype(jnp.int32), axis=1)
    last_e = jnp.max(jnp.where(counts > 0, jnp.arange(N_EXP, dtype=jnp.int32), 0))
    tile_e = jnp.minimum(tile_e, last_e)
    n_live = tile_end[-1:].astype(jnp.int32)
    return tile_e, n_live, src, dest.astype(jnp.int32), n_tiles


GATHER_UNROLL = 8


def _moe_group_kernel(tile_e_ref, nt_ref, src_ref, xp_ref, wg_ref, wu_ref, wd_ref, o_ref, xbuf):
    i = pl.program_id(0)
    nt = nt_ref[0]
    tm = xbuf.shape[0]

    @pl.when(i < nt)
    def _():
        def body(j, c):
            rows = [xp_ref[pl.ds(src_ref[i * tm + j * GATHER_UNROLL + q], 1), :] for q in range(GATHER_UNROLL)]
            xbuf[pl.ds(pl.multiple_of(j * GATHER_UNROLL, GATHER_UNROLL), GATHER_UNROLL), :] = jnp.concatenate(rows, axis=0)
            return c
        lax.fori_loop(0, tm // GATHER_UNROLL, body, 0)
        lo, hi = _unpack_halves(xbuf[...])
        x = jnp.concatenate([lo.astype(bf16), hi.astype(bf16)], axis=1)
        h = _silu(jnp.dot(x, wg_ref[...].astype(bf16), preferred_element_type=f32)) * \
            jnp.dot(x, wu_ref[...].astype(bf16), preferred_element_type=f32)
        o_ref[...] = _pack_halves(_dot(h, wd_ref[...]))

    @pl.when(i >= nt)
    def _():
        o_ref[...] = jnp.zeros_like(o_ref)


def _moe_group(xp, tile_e, n_live, src, n_tiles, w_gate, w_up, w_down, layer):
    T, half = xp.shape
    D = 2 * half
    wspec = lambda shp: pl.BlockSpec((None, None) + shp, lambda i, te, nt, sr: (layer, te[i], 0, 0))
    return pl.pallas_call(
        _moe_group_kernel,
        out_shape=jax.ShapeDtypeStruct((n_tiles * MOE_TM, half), jnp.uint32),
        grid_spec=pltpu.PrefetchScalarGridSpec(
            num_scalar_prefetch=3,
            grid=(n_tiles,),
            in_specs=[pl.BlockSpec(memory_space=pltpu.VMEM), wspec((D, D_E)), wspec((D, D_E)),
                      wspec((D_E, D))],
            out_specs=pl.BlockSpec((MOE_TM, half), lambda i, te, nt, sr: (i, 0)),
            scratch_shapes=[pltpu.VMEM((MOE_TM, half), jnp.uint32)]),
        compiler_params=_params("arbitrary"),
    )(tile_e, n_live, src, xp, w_gate, w_up, w_down)


def _moe_combine_ln_kernel(pos_ref, x_ref, rt_ref, g_ref, b_ref, ys_hbm, o_ref, ob_ref, ybuf, sems):
    i = pl.program_id(0)
    n = pl.num_programs(0)
    tm = x_ref.shape[0]
    T = n * tm

    def issue(tile, slot):
        for k in range(TOPK_IN):
            def body(j, c, k=k):
                for q in range(GATHER_UNROLL):
                    r = j * GATHER_UNROLL + q
                    p = pos_ref[k * T + tile * tm + r]
                    pltpu.make_async_copy(ys_hbm.at[pl.ds(p, 1), :], ybuf.at[slot, k, pl.ds(r, 1), :],
                                          sems.at[slot]).start(priority=q % 2)
                return c
            lax.fori_loop(0, tm // GATHER_UNROLL, body, 0)

    @pl.when(i == 0)
    def _():
        issue(0, 0)

    @pl.when(i + 1 < n)
    def _():
        issue(i + 1, (i + 1) % 2)

    slot = i % 2
    for k in range(TOPK_IN):
        pltpu.make_async_copy(ys_hbm.at[pl.ds(0, tm), :], ybuf.at[slot, k], sems.at[slot]).wait()
    rt = rt_ref[...]
    w1, w2 = rt[:, 0:1], rt[:, 1:2]
    lo1, hi1 = _unpack_halves(ybuf[slot, 0])
    lo2, hi2 = _unpack_halves(ybuf[slot, 1])
    y = jnp.concatenate([w1 * lo1 + w2 * lo2, w1 * hi1 + w2 * hi2], axis=1)
    o = _layer_norm(ALPHA * x_ref[...] + y, g_ref[...], b_ref[...])
    o_ref[...] = o
    ob_ref[...] = o.astype(bf16)


def _moe_combine_ln(x, route, pos, ys, g, b, layer):
    T, D = x.shape
    tm = min(T, LN_TM)
    vec = pl.BlockSpec((None, 1, D), lambda i, p: (layer, 0, 0))
    row = pl.BlockSpec((tm, D), lambda i, p: (i, 0))
    return pl.pallas_call(
        _moe_combine_ln_kernel,
        out_shape=(jax.ShapeDtypeStruct((T, D), f32), jax.ShapeDtypeStruct((T, D), bf16)),
        grid_spec=pltpu.PrefetchScalarGridSpec(
            num_scalar_prefetch=1,
            grid=(T // tm,),
            in_specs=[row, pl.BlockSpec((tm, LANES), lambda i, p: (i, 0)), vec, vec,
                      pl.BlockSpec(memory_space=pl.ANY)],
            out_specs=(row, row),
            scratch_shapes=[pltpu.VMEM((2, TOPK_IN, tm, D // 2), jnp.uint32),
                            pltpu.SemaphoreType.DMA((2,))]),
        compiler_params=_params("arbitrary"),
    )(pos, x, route, g.reshape(g.shape[0], 1, D), b.reshape(b.shape[0], 1, D), ys)


def _moe_ln(xf, xp, route, w_gate, w_up, w_down, g, b, layer):
    tile_e, n_live, src, pos, n_tiles = _moe_plan(route, xf.shape[0])
    ys = _moe_group(xp, tile_e, n_live, src, n_tiles, w_gate, w_up, w_down, layer)
    return _moe_combine_ln(xf, route, pos, ys, g, b, layer)


def kernel(x, mem, ab_w_in, rg_conv_w, rg_conv_b, rg_wa, rg_ba, rg_wx, rg_bx, rg_lam, ssd_conv_w, ssd_conv_b, ssd_dt_bias, ssd_a_log, ssd_d, ssd_norm_w, ab_w_out, cd_w_in, s5_a_re, s5_a_im, s5_log_step, s5_b_re, s5_b_im, s5_c_re, s5_c_im, s5_d, s5_glu_w, s5_glu_b, dn_conv_w, dn_a_log, dn_dt_bias, dn_norm_w, cd_w_out, xa_w_q, xa_w_kv, xa_w_o, moe_w_group, moe_b_group, moe_w_expert, moe_b_expert, moe_w_gate, moe_w_up, moe_w_down, ln1_g, ln1_b, ln2_g, ln2_b, ln3_g, ln3_b):
    B, S, D = x.shape
    M = mem.shape[1]
    T = B * S
    xf = x.reshape(T, D)
    xb = xf.astype(bf16)
    memb = mem.reshape(B * M, D).astype(bf16)
    for l in range(DEPTH):
        i = l // 2
        if l % 2 == 0:
            proj = _mm(xb, ab_w_in, i, MAIN_AB, f32)
            dt_tail = _mm_hi(xf, ab_w_in[i][:, MAIN_AB:])
            ya = _rglru(proj, B, S, i, rg_conv_w, rg_conv_b, rg_wa, rg_ba, rg_wx, rg_bx, rg_lam)
            yb = _ssd(proj, dt_tail, B, S, i, ssd_conv_w, ssd_conv_b, ssd_dt_bias, ssd_a_log, ssd_d,
                      ssd_norm_w)
            xf, xb = _mm_ln([ya, yb], ab_w_out[i].astype(bf16), xf, ln1_g, ln1_b, l)
        else:
            proj = _mm(xb, cd_w_in, i, MAIN_CD, f32)
            ab_tail = _mm_hi(xf, cd_w_in[i][:, MAIN_CD:])
            yc = _s5(proj, B, S, i, s5_a_re, s5_a_im, s5_log_step, s5_b_re, s5_b_im, s5_c_re, s5_c_im,
                     s5_d, s5_glu_w, s5_glu_b)
            yd = _gdn(proj, ab_tail, B, S, i, dn_conv_w, dn_a_log, dn_dt_bias, dn_norm_w)
            xf, xb = _mm_ln([yc, yd], cd_w_out[i].astype(bf16), xf, ln1_g, ln1_b, l)
        q = _mm(xb, xa_w_q, l, D, bf16)
        kv = _mm(memb, xa_w_kv, l, 2 * D, bf16)
        att = _xattn(q, kv, B, S, M)
        xf, xp = _mm_ln([att], xa_w_o[l].astype(bf16), xf, ln2_g, ln2_b, l, packed=True)
        route = _route(xf, moe_w_group, moe_b_group, moe_w_expert, moe_b_expert, l)
        xf, xb = _moe_ln(xf, xp, route, moe_w_gate, moe_w_up, moe_w_down, ln3_g, ln3_b, l)
    return xf.reshape(B, S, D)
```

```python
import functools
import math

import jax
import jax.numpy as jnp
from jax import lax
from jax.experimental import pallas as pl
from jax.experimental.pallas import tpu as pltpu

f32 = jnp.float32
bf16 = jnp.bfloat16

D_MODEL = 2048
DEPTH = 2
CHUNK = 64
CONV_W = 4
ALPHA = (2 * DEPTH) ** 0.25
LN_EPS = 1e-5
RMS_EPS = 1e-6
W_A = D_MODEL // 2
H_A = 8
BW_A = W_A // H_A
RG_C = 8.0
W_B = D_MODEL
HD_B = 64
H_B = W_B // HD_B
NG_B = 2
N_B = 128
HG_B = H_B // NG_B
CONV_B = W_B + 2 * NG_B * N_B
MAIN_AB = 2 * W_A + W_B + CONV_B
W_C = D_MODEL // 2
GS_C = 16
G_C = W_C // GS_C
P_C = 64
L_C = 16
H_D = 8
DK_D = D_MODEL // 16
DV_D = D_MODEL // 16
W_D = H_D * DV_D
QKV_D = 2 * H_D * DK_D + W_D
MAIN_CD = W_C + QKV_D + W_D
H_X = 4
HD_X = D_MODEL // H_X
NG_E = 4
E_PER = 8
N_EXP = NG_E * E_PER
TOPK_IN = 2
D_E = D_MODEL // 8

LANES = 128
SUBLANES = 8
S5_GB = LANES // GS_C
S5_NB = G_C // S5_GB
VMEM_LIMIT = 56 * 1024 * 1024


def _params(*sem):
    return pltpu.CompilerParams(dimension_semantics=sem, vmem_limit_bytes=VMEM_LIMIT)


def _sigmoid(x):
    return 1.0 / (1.0 + jnp.exp(-x))


def _silu(x):
    return x * _sigmoid(x)


def _softplus(x):
    return jnp.maximum(x, 0.0) + jnp.log(1.0 + jnp.exp(-jnp.abs(x)))


def _gelu_tanh(x):
    return 0.5 * x * (1.0 + jnp.tanh(math.sqrt(2.0 / math.pi) * (x + 0.044715 * (x * x * x))))


def _dot(a, b):
    return jnp.dot(a.astype(bf16), b.astype(bf16), preferred_element_type=f32)


def _dot_nt(a, b):
    return lax.dot_general(a.astype(bf16), b.astype(bf16), (((1,), (1,)), ((), ())),
                           preferred_element_type=f32)


def _dot_tn(a, b):
    return lax.dot_general(a.astype(bf16), b.astype(bf16), (((0,), (0,)), ((), ())),
                           preferred_element_type=f32)


def _split3(a):
    hi = a.astype(bf16)
    r = a - hi.astype(f32)
    mid = r.astype(bf16)
    lo = (r - mid.astype(f32)).astype(bf16)
    return hi, mid, lo


def _dot_exact_lhs(sel, b):
    s = sel.astype(bf16)
    b1, b2, b3 = _split3(b)
    d = functools.partial(jnp.dot, preferred_element_type=f32)
    return d(s, b1) + d(s, b2) + d(s, b3)


def _dot_exact_rhs(a, sel):
    s = sel.astype(bf16)
    a1, a2, a3 = _split3(a)
    d = functools.partial(jnp.dot, preferred_element_type=f32)
    return d(a1, s) + d(a2, s) + d(a3, s)


def _transpose_exact(a, eye):
    a1, a2, a3 = _split3(a)
    e = eye.astype(bf16)
    d = lambda x: lax.dot_general(e, x, (((1,), (1,)), ((), ())), preferred_element_type=f32)
    return d(a1) + d(a2) + d(a3)


def _dot3(a, b):
    a1 = a.astype(bf16)
    a2 = (a - a1.astype(f32)).astype(bf16)
    b1 = b.astype(bf16)
    b2 = (b - b1.astype(f32)).astype(bf16)
    d = functools.partial(jnp.dot, preferred_element_type=f32)
    return d(a1, b1) + (d(a1, b2) + d(a2, b1))


def _iota(shape, axis):
    return lax.broadcasted_iota(jnp.int32, shape, axis)


def _tri(n, strict=False):
    r, c = _iota((n, n), 0), _iota((n, n), 1)
    return (r > c) if strict else (r >= c)


def _mm_kernel(x_ref, w_ref, o_ref):
    o_ref[...] = jnp.dot(x_ref[...], w_ref[...].astype(bf16),
                         preferred_element_type=f32).astype(o_ref.dtype)


def _mm(x, w, layer, n_cols, out_dtype, tn=512):
    M, K = x.shape
    tm = min(M, 2048)
    return pl.pallas_call(
        _mm_kernel,
        out_shape=jax.ShapeDtypeStruct((M, n_cols), out_dtype),
        grid=(M // tm, n_cols // tn),
        in_specs=[pl.BlockSpec((tm, K), lambda i, j: (i, 0)),
                  pl.BlockSpec((None, K, tn), lambda i, j: (layer, 0, j))],
        out_specs=pl.BlockSpec((tm, tn), lambda i, j: (i, j)),
        compiler_params=_params("parallel", "parallel"),
    )(x, w)


def _tail_cols(w, layer, start):
    _, K, N = w.shape
    return lax.slice(w, (layer, 0, start), (layer + 1, K, N)).reshape(K, N - start)


def _mm_hi_kernel(x_ref, w_ref, o_ref):
    o_ref[...] = _dot3(x_ref[...], w_ref[...])


def _mm_hi(x, w):
    M, K = x.shape
    n = w.shape[1]
    tm = min(M, 512)
    return pl.pallas_call(
        _mm_hi_kernel,
        out_shape=jax.ShapeDtypeStruct((M, n), f32),
        grid=(M // tm,),
        in_specs=[pl.BlockSpec((tm, K), lambda i: (i, 0)),
                  pl.BlockSpec((K, n), lambda i: (0, 0))],
        out_specs=pl.BlockSpec((tm, n), lambda i: (i, 0)),
        compiler_params=_params("parallel"),
    )(x, w)


def _layer_norm(v, g, b):
    mu = jnp.mean(v, -1, keepdims=True)
    d = v - mu
    var = jnp.mean(d * d, -1, keepdims=True)
    return d * lax.rsqrt(var + LN_EPS) * g + b


K_CHUNK = 1024
HI_MASK = 0xFFFF0000


def _pack_halves(v):
    h = v.shape[1] // 2
    lo = lax.bitcast_convert_type(v[:, :h].astype(bf16).astype(f32), jnp.uint32)
    hi = lax.bitcast_convert_type(v[:, h:].astype(bf16).astype(f32), jnp.uint32)
    return (hi & jnp.uint32(HI_MASK)) | (lo >> 16)


def _unpack_halves(u):
    lo = lax.bitcast_convert_type(u << 16, f32)
    hi = lax.bitcast_convert_type(u & jnp.uint32(HI_MASK), f32)
    return lo, hi


def _mm_ln_kernel(*refs, chunks, packed):
    n_parts = max(p for p, _ in chunks) + 1
    part_refs = refs[:n_parts]
    w_ref, x_ref, g_ref, b_ref, o_ref, o2_ref, acc_ref = refs[n_parts:]
    k = pl.program_id(1)
    for idx, (p, off) in enumerate(chunks):
        @pl.when(k == idx)
        def _(p=p, off=off, idx=idx):
            c = jnp.dot(part_refs[p][:, off:off + K_CHUNK], w_ref[...], preferred_element_type=f32)
            if idx == 0:
                acc_ref[...] = c
            else:
                acc_ref[...] += c

    @pl.when(k == len(chunks) - 1)
    def _():
        o = _layer_norm(ALPHA * x_ref[...] + acc_ref[...], g_ref[...], b_ref[...])
        o_ref[...] = o
        o2_ref[...] = _pack_halves(o) if packed else o.astype(bf16)


def _mm_ln(parts, w, x, g, b, layer, packed=False):
    T, D = x.shape
    tm = min(T, 512)
    chunks = tuple((p, off) for p, a in enumerate(parts) for off in range(0, a.shape[1], K_CHUNK))
    vec = pl.BlockSpec((None, 1, D), lambda i, k: (layer, 0, 0))
    row = pl.BlockSpec((tm, D), lambda i, k: (i, 0))
    if packed:
        shape2 = jax.ShapeDtypeStruct((T, D // 2), jnp.uint32)
        spec2 = pl.BlockSpec((tm, D // 2), lambda i, k: (i, 0))
    else:
        shape2, spec2 = jax.ShapeDtypeStruct((T, D), bf16), row
    return pl.pallas_call(
        functools.partial(_mm_ln_kernel, chunks=chunks, packed=packed),
        out_shape=(jax.ShapeDtypeStruct((T, D), f32), shape2),
        grid=(T // tm, len(chunks)),
        in_specs=[pl.BlockSpec((tm, a.shape[1]), lambda i, k: (i, 0)) for a in parts]
        + [pl.BlockSpec((K_CHUNK, D), lambda i, k: (k, 0)), row, vec, vec],
        out_specs=(row, spec2),
        scratch_shapes=[pltpu.VMEM((tm, D), f32)],
        compiler_params=_params("parallel", "arbitrary"),
    )(*parts, w, x, g.reshape(g.shape[0], 1, D), b.reshape(b.shape[0], 1, D))


def _conv_rows(x_ref, w, hist_ref):
    tt = x_ref.shape[0]
    hist_ref[SUBLANES:SUBLANES + tt, :] = x_ref[...]
    acc = _causal_taps(hist_ref[...], w, tt)
    hist_ref[0:SUBLANES, :] = hist_ref[tt:tt + SUBLANES, :]
    return acc


def _causal_taps(ext, w, tt):
    acc = None
    for k in range(CONV_W):
        back = CONV_W - 1 - k
        rows = (pltpu.roll(ext, back, 0) if back else ext)[SUBLANES:SUBLANES + tt, :]
        term = w[k:k + 1, :] * rows
        acc = term if acc is None else acc + term
    return acc


def _scan_affine(a, u):
    n = a.shape[0]
    row = _iota(a.shape, 0)
    d = 1
    while d < n:
        keep = row >= d
        a_s = pltpu.roll(a, d, 0)
        u_s = pltpu.roll(u, d, 0)
        u = u + jnp.where(keep, a * u_s, 0.0)
        a = jnp.where(keep, a * a_s, a)
        d *= 2
    return a, u


def _rglru_kernel(gate_ref, xa_ref, cw_ref, cb_ref, wa_ref, ba_ref, wx_ref, bx_ref, lam_ref,
                  o_ref, hist_ref, h_ref):
    @pl.when(pl.program_id(1) == 0)
    def _():
        h_ref[...] = jnp.zeros_like(h_ref)
        hist_ref[:, 0:SUBLANES, :] = jnp.zeros((hist_ref.shape[0], SUBLANES, hist_ref.shape[2]), f32)

    decay_rate = -RG_C * _softplus(-lam_ref[...])
    for b in range(xa_ref.shape[0]):
        xc = _conv_rows(xa_ref.at[b], cw_ref[...], hist_ref.at[b]) + cb_ref[...]
        r = _sigmoid(_dot(xc, wa_ref[...]) + ba_ref[...])
        i = _sigmoid(_dot(xc, wx_ref[...]) + bx_ref[...])
        log_a = r * decay_rate
        a = jnp.exp(log_a)
        u = jnp.sqrt(1.0 - jnp.exp(2.0 * log_a)) * (i * xc)
        a_cum, h = _scan_affine(a, u)
        h = h + a_cum * h_ref[b, 0:1, :]
        tt = h.shape[0]
        h_ref[b] = jnp.broadcast_to(h[tt - 1:tt, :], h_ref.shape[1:])
        o_ref[b] = (_gelu_tanh(gate_ref[b]) * h).astype(o_ref.dtype)


def _rglru(proj, B, S, layer, conv_w, conv_b, wa, ba, wx, bx, lam):
    tt = min(S, 256)
    ns = S // tt
    xa_col0 = W_A // BW_A
    vec = lambda a: a.reshape(a.shape[0], 1, W_A)
    vspec = pl.BlockSpec((None, 1, BW_A), lambda h, s: (layer, 0, h))
    wspec = pl.BlockSpec((None, None, BW_A, BW_A), lambda h, s: (layer, h, 0, 0))
    proj3 = proj.reshape(B, S, proj.shape[1])
    out = pl.pallas_call(
        _rglru_kernel,
        out_shape=jax.ShapeDtypeStruct((B, S, W_A), bf16),
        grid=(H_A, ns),
        in_specs=[pl.BlockSpec((B, tt, BW_A), lambda h, s: (0, s, h)),
                  pl.BlockSpec((B, tt, BW_A), lambda h, s: (0, s, xa_col0 + h)),
                  pl.BlockSpec((None, CONV_W, BW_A), lambda h, s: (layer, 0, h)),
                  vspec, wspec, vspec, wspec, vspec, vspec],
        out_specs=pl.BlockSpec((B, tt, BW_A), lambda h, s: (0, s, h)),
        scratch_shapes=[pltpu.VMEM((B, SUBLANES + tt, BW_A), f32), pltpu.VMEM((B, SUBLANES, BW_A), f32)],
        compiler_params=_params("parallel", "arbitrary"),
    )(proj3, proj3, conv_w, vec(conv_b), wa, vec(ba), wx, vec(bx), vec(lam))
    return out.reshape(B * S, W_A)


def _ssd_group(X, Bc, Cc, z, dt, a_neg, d_head, nw, st_ref):
    L = CHUNK
    adt = dt * a_neg
    tri = _tri(L)
    cs = _dot_exact_lhs(tri.astype(f32), adt)
    eye_h = (_iota((HG_B, HG_B), 0) == _iota((HG_B, HG_B), 1)).astype(f32)
    cs_t = _transpose_exact(cs, eye_h)
    cs_last = cs[L - 1:L, :]

    expand = (_iota((HG_B, HG_B * HD_B), 1) // HD_B == _iota((HG_B, HG_B * HD_B), 0)).astype(f32)
    dt_x = _dot_exact_rhs(dt, expand)
    ecs_x = _dot_exact_rhs(jnp.exp(cs), expand)
    dec_x = _dot_exact_rhs(jnp.exp(cs_last - cs), expand)
    cdec_x = _dot_exact_rhs(jnp.broadcast_to(jnp.exp(cs_last), (SUBLANES, HG_B)), expand)[0:1]
    d_x = _dot_exact_rhs(jnp.broadcast_to(d_head, (SUBLANES, HG_B)), expand)[0:1]

    xdt = X * dt_x
    cb = _dot_nt(Cc, Bc)
    xdt_b = xdt.astype(bf16)
    left = _iota((L, 2 * HD_B), 1) < HD_B
    pieces = []
    for j in range(0, HG_B, 2):
        pair = xdt_b[:, j * HD_B:(j + 2) * HD_B]
        outs = []
        for jj in (j, j + 1):
            seg = jnp.where(tri, cs[:, jj:jj + 1] - cs_t[jj:jj + 1, :], -jnp.inf)
            outs.append(jnp.dot((cb * jnp.exp(seg)).astype(bf16), pair, preferred_element_type=f32))
        pieces.append(jnp.where(left, outs[0], outs[1]))
    y_diag = jnp.concatenate(pieces, axis=1)

    state = st_ref[...]
    y_off = ecs_x * _dot(Cc, state)
    st_ref[...] = state * cdec_x + _dot_tn(Bc, xdt * dec_x)

    y = y_diag + y_off + X * d_x
    yg = y * _silu(z)
    return yg * lax.rsqrt(jnp.mean(yg * yg, -1, keepdims=True) + RMS_EPS) * nw


def _ssd_kernel(z_ref, x_ref, b_ref, c_ref, dt_ref, cw_ref, cb_ref, dtb_ref, alog_ref, d_ref, nw_ref,
                o_ref, hx_ref, hb_ref, hc_ref, st_ref):
    GW = W_B // NG_B

    @pl.when(pl.program_id(0) == 0)
    def _():
        st_ref[...] = jnp.zeros_like(st_ref)
        for h in (hx_ref, hb_ref, hc_ref):
            h[:, 0:SUBLANES, :] = jnp.zeros((h.shape[0], SUBLANES, h.shape[2]), f32)

    cw, cbias = cw_ref[...], cb_ref[...]
    c_off = W_B + NG_B * N_B
    a_all = -jnp.exp(alog_ref[...])
    d_all = d_ref[...]
    nw_all = nw_ref[...]
    for b in range(x_ref.shape[0]):
        X = _silu(_conv_rows(x_ref.at[b], cw[:, :W_B], hx_ref.at[b]) + cbias[:, :W_B])
        Bm = _silu(_conv_rows(b_ref.at[b], cw[:, W_B:c_off], hb_ref.at[b]) + cbias[:, W_B:c_off])
        Cm = _silu(_conv_rows(c_ref.at[b], cw[:, c_off:], hc_ref.at[b]) + cbias[:, c_off:])
        dt = _softplus(dt_ref[b] + dtb_ref[...])
        for g in range(NG_B):
            hs = slice(g * HG_B, (g + 1) * HG_B)
            ws = slice(g * GW, (g + 1) * GW)
            ns = slice(g * N_B, (g + 1) * N_B)
            out = _ssd_group(X[:, ws], Bm[:, ns], Cm[:, ns], z_ref[b, :, ws], dt[:, hs], a_all[:, hs],
                             d_all[:, hs], nw_all[:, ws], st_ref.at[b * NG_B + g])
            o_ref[b, :, ws] = out.astype(o_ref.dtype)


def _ssd(proj, dt_tail, B, S, layer, conv_w, conv_b, dt_bias, a_log, d, norm_w):
    L = CHUNK
    nc = S // L
    BC = NG_B * N_B
    per_batch = lambda a: a.reshape(B, S, a.shape[1])
    blk = lambda n, cb: pl.BlockSpec((B, L, n), lambda c: (0, c, cb))
    vec = lambda a: a.reshape(a.shape[0], 1, a.shape[-1])
    par = lambda r, n: pl.BlockSpec((None, r, n), lambda c: (layer, 0, 0))
    proj3 = per_batch(proj)
    out = pl.pallas_call(
        _ssd_kernel,
        out_shape=jax.ShapeDtypeStruct((B, S, W_B), bf16),
        grid=(nc,),
        in_specs=[blk(W_B, 2 * W_A // W_B), blk(W_B, (2 * W_A + W_B) // W_B),
                  blk(BC, (2 * W_A + 2 * W_B) // BC), blk(BC, (2 * W_A + 2 * W_B) // BC + 1),
                  blk(H_B, 0), par(CONV_W, CONV_B), par(1, CONV_B), par(1, H_B), par(1, H_B), par(1, H_B),
                  par(1, W_B)],
        out_specs=blk(W_B, 0),
        scratch_shapes=[pltpu.VMEM((B, SUBLANES + L, W_B), f32), pltpu.VMEM((B, SUBLANES + L, BC), f32),
                        pltpu.VMEM((B, SUBLANES + L, BC), f32), pltpu.VMEM((B * NG_B, N_B, W_B // NG_B), f32)],
        compiler_params=_params("arbitrary"),
    )(proj3, proj3, proj3, proj3, per_batch(dt_tail), conv_w, vec(conv_b), vec(dt_bias), vec(a_log), vec(d),
      vec(norm_w))
    return out.reshape(B * S, W_B)


def _s5_tables(a_re, a_im, log_step, b_re, b_im, c_re, c_im):
    L = L_C
    ar, ai = a_re.astype(f32), a_im.astype(f32)
    step = jnp.exp(log_step.astype(f32))[:, None]
    mag = jnp.exp(ar * step)
    lb_re, lb_im = mag * jnp.cos(ai * step), mag * jnp.sin(ai * step)
    den = ar * ar + ai * ai
    f_re = ((lb_re - 1.0) * ar + lb_im * ai) / den
    f_im = (lb_im * ar - (lb_re - 1.0) * ai) / den
    br, bi = b_re.astype(f32), b_im.astype(f32)
    bb_re = f_re[..., None] * br - f_im[..., None] * bi
    bb_im = f_re[..., None] * bi + f_im[..., None] * br
    cr, ci = c_re.astype(f32), c_im.astype(f32)

    def power(n):
        n = n.astype(f32)[None, :, None]
        m = jnp.exp(ar[:, None, :] * step[:, None, :] * n)
        ang = ai[:, None, :] * step[:, None, :] * n
        return m * jnp.cos(ang), m * jnp.sin(ang)

    j = jnp.arange(L)
    pr, pi = power(j)
    lbr = pr[..., None] * bb_re[:, None] - pi[..., None] * bb_im[:, None]
    lbi = pr[..., None] * bb_im[:, None] + pi[..., None] * bb_re[:, None]
    kern = (jnp.einsum('gop,gjpk->gjko', cr, lbr) - jnp.einsum('gop,gjpk->gjko', ci, lbi))
    eye = jnp.eye(S5_GB, dtype=f32)
    lagb = jnp.einsum('bgjik,gh->bjgihk', kern.reshape(S5_NB, S5_GB, L, GS_C, GS_C), eye)
    lagb = lagb.reshape(S5_NB, L, LANES, LANES)
    rev = (L - 1) - j
    bend = jnp.concatenate([jnp.take(lbr, rev, axis=1), jnp.take(lbi, rev, axis=1)], 2)
    bendc = bend.reshape(S5_NB, S5_GB, L, 2 * P_C, GS_C).transpose(0, 2, 1, 4, 3)
    bendc = bendc.reshape(S5_NB, L, LANES, 2 * P_C)
    qr, qi = power(j + 1)
    car_re = cr[:, None] * qr[:, :, None, :] - ci[:, None] * qi[:, :, None, :]
    car_im = -(cr[:, None] * qi[:, :, None, :] + ci[:, None] * qr[:, :, None, :])
    car = jnp.concatenate([car_re, car_im], -1)
    ccarc = car.reshape(S5_NB, S5_GB, L, GS_C, 2 * P_C).transpose(0, 2, 4, 1, 3)
    ccarc = ccarc.reshape(S5_NB, L, 2 * P_C, LANES)
    return lagb.astype(bf16), bendc.astype(bf16), ccarc.astype(bf16), power


def _s5_kernel(u_ref, lag_ref, bend_ref, ccar_ref, sc_ref, o_ref, toep_t, bend_t, ccar_t):
    L = L_C
    SW = 2 * P_C

    @pl.when(pl.program_id(1) == 0)
    def _():
        zero = jnp.zeros((LANES, LANES), bf16)
        for s in range(L):
            for t in range(L):
                toep_t[s * LANES:(s + 1) * LANES, t * LANES:(t + 1) * LANES] = lag_ref[t - s] if t >= s else zero
        same_g = (_iota((LANES, S5_GB * SW), 0) // GS_C) == (_iota((LANES, S5_GB * SW), 1) // SW)
        for s in range(L):
            wide = jnp.concatenate([bend_ref[s]] * S5_GB, axis=1)
            bend_t[s * LANES:(s + 1) * LANES, :] = jnp.where(same_g, wide, jnp.zeros_like(wide))
        col_g = _iota((SW, LANES), 1) // GS_C
        for t in range(L):
            blk = ccar_ref[t]
            for g in range(S5_GB):
                ccar_t[g * SW:(g + 1) * SW, t * LANES:(t + 1) * LANES] = jnp.where(col_g == g, blk, jnp.zeros_like(blk))

    n = o_ref.shape[0] // L
    U = jnp.concatenate([u_ref[pl.ds(l, n, stride=L), :].astype(bf16) for l in range(L)], axis=1)
    Y = jnp.dot(U, toep_t[...], preferred_element_type=f32)
    H_all = jnp.dot(U, bend_t[...], preferred_element_type=f32)
    row = _iota((n, SW), 0)
    prev = []
    for g in range(S5_GB):
        sl = slice(g * SW, (g + 1) * SW)
        H = H_all[:, sl]
        d, k = 1, 0
        while d < n:
            hs = pltpu.roll(H, d, 0)
            sw = pltpu.roll(hs, P_C, 1)
            H = H + jnp.where(row >= d, sc_ref[2 * k:2 * k + 1, sl] * hs + sc_ref[2 * k + 1:2 * k + 2, sl] * sw, 0.0)
            d *= 2
            k += 1
        prev.append(jnp.where(row >= 1, pltpu.roll(H, 1, 0), 0.0).astype(bf16))
    Y = Y + jnp.dot(jnp.concatenate(prev, axis=1), ccar_t[...], preferred_element_type=f32)
    for l in range(L):
        o_ref[pl.ds(l, n, stride=L), :] = Y[:, l * LANES:(l + 1) * LANES]


def _s5_post_kernel(y_ref, u_ref, d_ref, w_ref, b_ref, o_ref):
    y = y_ref[...] + d_ref[...] * u_ref[...]
    g = _gelu_tanh(y)
    o_ref[...] = (g * _sigmoid(_dot(g, w_ref[...]) + b_ref[...])).astype(o_ref.dtype)


def _s5(proj, B, S, layer, a_re, a_im, log_step, b_re, b_im, c_re, c_im, d, glu_w, glu_b):
    T = B * S
    L = L_C
    nch = S // L
    lagb, bendc, ccarc, power = _s5_tables(a_re[layer], a_im[layer], log_step[layer], b_re[layer],
                                           b_im[layer], c_re[layer], c_im[layer])
    nsteps = max(1, (nch - 1).bit_length())
    sr, si = power(L * (2 ** jnp.arange(nsteps)))
    scan_c = jnp.stack([jnp.concatenate([sr, sr], -1), jnp.concatenate([-si, si], -1)], 2)
    scan8 = scan_c.reshape(S5_NB, S5_GB, 2 * nsteps, 2 * P_C).transpose(0, 2, 1, 3)
    scan8 = scan8.reshape(S5_NB, 2 * nsteps, S5_GB * 2 * P_C)
    tab = lambda r, c: pl.BlockSpec((None, L, r, c), lambda g, b: (g, 0, 0, 0))
    y = pl.pallas_call(
        _s5_kernel,
        out_shape=jax.ShapeDtypeStruct((T, W_C), f32),
        grid=(S5_NB, B),
        in_specs=[pl.BlockSpec((S, LANES), lambda g, b: (b, g)),
                  tab(LANES, LANES), tab(LANES, 2 * P_C), tab(2 * P_C, LANES),
                  pl.BlockSpec((None, 2 * nsteps, S5_GB * 2 * P_C), lambda g, b: (g, 0, 0))],
        out_specs=pl.BlockSpec((S, LANES), lambda g, b: (b, g)),
        scratch_shapes=[pltpu.VMEM((L * LANES, L * LANES), bf16),
                        pltpu.VMEM((L * LANES, S5_GB * 2 * P_C), bf16),
                        pltpu.VMEM((S5_GB * 2 * P_C, L * LANES), bf16)],
        compiler_params=_params("arbitrary", "arbitrary"),
    )(proj, lagb, bendc, ccarc, scan8)
    tm = min(T, 1024)
    vec = pl.BlockSpec((None, 1, W_C), lambda i: (layer, 0, 0))
    return pl.pallas_call(
        _s5_post_kernel,
        out_shape=jax.ShapeDtypeStruct((T, W_C), bf16),
        grid=(T // tm,),
        in_specs=[pl.BlockSpec((tm, W_C), lambda i: (i, 0)),
                  pl.BlockSpec((tm, W_C), lambda i: (i, 0)),
                  vec,
                  pl.BlockSpec((None, W_C, W_C), lambda i: (layer, 0, 0)),
                  vec],
        out_specs=pl.BlockSpec((tm, W_C), lambda i: (i, 0)),
        compiler_params=_params("parallel"),
    )(y, proj, d.reshape(d.shape[0], 1, W_C), glu_w, glu_b.reshape(glu_b.shape[0], 1, W_C))


def _gdn_prep_kernel(q_ref, k_ref, v_ref, qp_ref, kp_ref, vp_ref, ab_ref, cwq_ref, cwk_ref, cwv_ref,
                     alog_ref, dtb_ref, w_ref, u_ref, qd_ref, kd_ref, qk_ref, ge_ref, hist_ref):
    L = CHUNK
    has_prev = (pl.program_id(1) > 0).astype(f32)

    def conv(cur_ref, prev_ref, w_ref_, slot):
        hist = hist_ref.at[slot]
        hist[0:SUBLANES, :] = prev_ref[...] * has_prev
        hist[SUBLANES:SUBLANES + L, :] = cur_ref[...]
        return _silu(_causal_taps(hist[...], w_ref_[...], L))

    q_all = conv(q_ref, qp_ref, cwq_ref, 0)
    k_all = conv(k_ref, kp_ref, cwk_ref, 1)
    v_all = conv(v_ref, vp_ref, cwv_ref, 2)
    ab = ab_ref[...]
    g_all = -jnp.exp(alog_ref[...]) * _softplus(ab[:, 0:H_D] + dtb_ref[...])
    beta_all = _sigmoid(ab[:, H_D:2 * H_D])
    tri = _tri(L)
    tri_s = _tri(L, strict=True)
    gcs_all = _dot_exact_lhs(tri.astype(f32), g_all)
    eye_h = (_iota((H_D, H_D), 0) == _iota((H_D, H_D), 1)).astype(f32)
    gcs_t = _transpose_exact(gcs_all, eye_h)
    eye = (_iota((L, L), 0) == _iota((L, L), 1)).astype(f32)
    ge_ref[...] = jnp.concatenate([jnp.exp(gcs_all), jnp.zeros((L, LANES - H_D), f32)], axis=1)

    heads = range(H_D)
    sls = [slice(h * DK_D, (h + 1) * DK_D) for h in heads]
    gcs = [gcs_all[:, h:h + 1] for h in heads]
    eg = [jnp.exp(g) for g in gcs]
    qs = [q_all[:, sl] * lax.rsqrt(jnp.sum(q_all[:, sl] * q_all[:, sl], -1, keepdims=True) + 1e-6)
          * (DK_D ** -0.5) for sl in sls]
    ks = [k_all[:, sl] * lax.rsqrt(jnp.sum(k_all[:, sl] * k_all[:, sl], -1, keepdims=True) + 1e-6)
          for sl in sls]
    kbs = [ks[h] * beta_all[:, h:h + 1] for h in heads]
    decay = [jnp.exp(jnp.where(tri, gcs[h] - gcs_t[h:h + 1, :], -jnp.inf)) for h in heads]
    kk = [_dot_nt(kbs[h], ks[h]) for h in heads]
    qk = [_dot_nt(qs[h], ks[h]) for h in heads]
    pw = [jnp.where(tri_s, -(kk[h] * decay[h]), 0.0) for h in heads]
    inv = [eye + p for p in pw]
    for _ in range(int(math.log2(L)) - 1):
        pw = [_dot(p, p) for p in pw]
        inv = [a + _dot(a, p) for a, p in zip(inv, pw)]
    rhs = [jnp.concatenate([v_all[:, sls[h]] * beta_all[:, h:h + 1], kbs[h] * eg[h]], axis=1) for h in heads]
    sol = [_dot(inv[h], rhs[h]) for h in heads]
    for h in heads:
        u_ref[:, sls[h]] = sol[h][:, :DV_D]
        w_ref[:, sls[h]] = sol[h][:, DV_D:]
        qk_ref[:, h * L:(h + 1) * L] = jnp.where(tri, qk[h] * decay[h], 0.0)
        qd_ref[:, sls[h]] = qs[h] * eg[h]
        kd_ref[:, sls[h]] = ks[h] * jnp.exp(gcs[h][L - 1:L, :] - gcs[h])


def _gdn_scan_kernel(w_ref, u_ref, qd_ref, kd_ref, qk_ref, ge_ref, z_ref, nw_ref, o_ref, st_ref):
    L = CHUNK

    @pl.when(pl.program_id(0) == 0)
    def _():
        st_ref[...] = jnp.zeros_like(st_ref)

    nw = nw_ref[...]
    pairs = [(b, h) for b in range(w_ref.shape[0]) for h in range(H_D)]
    cols = lambda h: slice(h * DK_D, (h + 1) * DK_D)
    state = [st_ref[b * H_D + h] for b, h in pairs]
    v_new = [u_ref[b, :, cols(h)] - _dot(w_ref[b, :, cols(h)], s) for (b, h), s in zip(pairs, state)]
    o = [_dot(qd_ref[b, :, cols(h)], s) + _dot(qk_ref[b, :, h * L:(h + 1) * L], v)
         for (b, h), s, v in zip(pairs, state, v_new)]
    for (b, h), s, v in zip(pairs, state, v_new):
        st_ref[b * H_D + h] = s * ge_ref[b, L - 1:L, h:h + 1] + _dot_tn(kd_ref[b, :, cols(h)], v)
    for (b, h), y in zip(pairs, o):
        y = y * lax.rsqrt(jnp.mean(y * y, -1, keepdims=True) + RMS_EPS) * nw
        o_ref[b, :, cols(h)] = (y * _silu(z_ref[b, :, cols(h)])).astype(o_ref.dtype)


def _gdn(proj, ab_tail, B, S, layer, conv_w, a_log, dt_bias, norm_w):
    T = B * S
    L = CHUNK
    nc = S // L
    per8 = L // SUBLANES
    q_c0 = W_C // W_D
    row = lambda b, c: (b * nc + c, 0)

    def cur(cb):
        return pl.BlockSpec((L, W_D), lambda b, c: (b * nc + c, cb))

    def prev(cb):
        return pl.BlockSpec((SUBLANES, W_D), lambda b, c: (jnp.maximum((b * nc + c) * per8 - 1, 0), cb))

    def cw(cb):
        return pl.BlockSpec((None, CONV_W, W_D), lambda b, c: (layer, 0, cb))

    v8 = pl.BlockSpec((None, 1, H_D), lambda b, c: (layer, 0, 0))
    wide = pl.BlockSpec((L, W_D), row)
    shp = lambda n: jax.ShapeDtypeStruct((T, n), f32)
    w_c, u_c, q_dec, k_dec, qk, gexp = pl.pallas_call(
        _gdn_prep_kernel,
        out_shape=(shp(W_D), shp(W_D), shp(W_D), shp(W_D), shp(H_D * L), shp(LANES)),
        grid=(B, nc),
        in_specs=[cur(q_c0), cur(q_c0 + 1), cur(q_c0 + 2), prev(q_c0), prev(q_c0 + 1), prev(q_c0 + 2),
                  pl.BlockSpec((L, 2 * H_D), row), cw(0), cw(1), cw(2), v8, v8],
        out_specs=(wide, wide, wide, wide, pl.BlockSpec((L, H_D * L), row),
                   pl.BlockSpec((L, LANES), row)),
        scratch_shapes=[pltpu.VMEM((3, SUBLANES + L, W_D), f32)],
        compiler_params=_params("parallel", "parallel"),
    )(proj, proj, proj, proj, proj, proj, ab_tail, conv_w, conv_w, conv_w,
      a_log.reshape(a_log.shape[0], 1, H_D), dt_bias.reshape(dt_bias.shape[0], 1, H_D))
    z_c0 = (W_C + QKV_D) // W_D
    per_batch = lambda a: a.reshape(B, S, a.shape[1])
    blk = lambda n, cb=0: pl.BlockSpec((B, L, n), lambda c: (0, c, cb))
    out = pl.pallas_call(
        _gdn_scan_kernel,
        out_shape=jax.ShapeDtypeStruct((B, S, W_D), bf16),
        grid=(nc,),
        in_specs=[blk(W_D), blk(W_D), blk(W_D), blk(W_D), blk(H_D * L), blk(LANES), blk(W_D, z_c0),
                  pl.BlockSpec((None, 1, DV_D), lambda c: (layer, 0, 0))],
        out_specs=blk(W_D),
        scratch_shapes=[pltpu.VMEM((B * H_D, DK_D, DV_D), f32)],
        compiler_params=_params("arbitrary"),
    )(per_batch(w_c), per_batch(u_c), per_batch(q_dec), per_batch(k_dec), per_batch(qk), per_batch(gexp),
      per_batch(proj), norm_w.reshape(norm_w.shape[0], 1, DV_D))
    return out.reshape(T, W_D)


def _xattn_kernel(q_ref, k_ref, v_ref, o_ref):
    s = _dot_nt(q_ref[...], k_ref[...]) * (HD_X ** -0.5)
    m = jnp.max(s, -1, keepdims=True)
    p = jnp.exp(s - m)
    p = p / jnp.sum(p, -1, keepdims=True)
    o_ref[...] = jnp.dot(p.astype(bf16), v_ref[...], preferred_element_type=f32).astype(o_ref.dtype)


def _xattn(q, kv, B, S, M):
    T = B * S
    tq = min(S, 1024)
    nq = S // tq
    return pl.pallas_call(
        _xattn_kernel,
        out_shape=jax.ShapeDtypeStruct((T, D_MODEL), bf16),
        grid=(B, nq, H_X),
        in_specs=[pl.BlockSpec((tq, HD_X), lambda b, i, h: (b * nq + i, h)),
                  pl.BlockSpec((M, HD_X), lambda b, i, h: (b, h)),
                  pl.BlockSpec((M, HD_X), lambda b, i, h: (b, H_X + h))],
        out_specs=pl.BlockSpec((tq, HD_X), lambda b, i, h: (b * nq + i, h)),
        compiler_params=_params("parallel", "parallel", "parallel"),
    )(q, kv, kv)


def _route_kernel(x_ref, w_ref, b_ref, o_ref):
    logits = _dot3(x_ref[...], w_ref[...]) + b_ref[...]
    lane = _iota(logits.shape, 1)
    neg = -jnp.inf
    big = jnp.int32(LANES)
    is_g = (lane >= N_EXP) & (lane < N_EXP + NG_E)
    gl = jnp.where(is_g, logits, neg)
    gmax = jnp.max(gl, -1, keepdims=True)
    g_lane = jnp.min(jnp.where(gl == gmax, lane, big), -1, keepdims=True)
    g_prob = 1.0 / jnp.sum(jnp.where(is_g, jnp.exp(gl - gmax), 0.0), -1, keepdims=True)
    e0 = (g_lane - N_EXP) * E_PER
    sel = jnp.where((lane >= e0) & (lane < e0 + E_PER), logits, neg)
    m1 = jnp.max(sel, -1, keepdims=True)
    i1 = jnp.min(jnp.where(sel == m1, lane, big), -1, keepdims=True)
    sel2 = jnp.where(lane == i1, neg, sel)
    m2 = jnp.max(sel2, -1, keepdims=True)
    i2 = jnp.min(jnp.where(sel2 == m2, lane, big), -1, keepdims=True)
    e2 = jnp.exp(m2 - m1)
    w1 = g_prob / (1.0 + e2)
    w2 = g_prob * e2 / (1.0 + e2)
    o_ref[...] = (jnp.where(lane == 0, w1, 0.0) + jnp.where(lane == 1, w2, 0.0)
                  + jnp.where(lane == 2, i1.astype(f32), 0.0) + jnp.where(lane == 3, i2.astype(f32), 0.0))


def _route(x, w_group, b_group, w_expert, b_expert, layer):
    T, D = x.shape
    pad = LANES - N_EXP - NG_E
    w = jnp.concatenate([w_expert[layer], w_group[layer], jnp.zeros((D, pad), f32)], 1)
    b = jnp.concatenate([b_expert[layer], b_group[layer], jnp.zeros((pad,), f32)])[None, :]
    tm = min(T, 512)
    return pl.pallas_call(
        _route_kernel,
        out_shape=jax.ShapeDtypeStruct((T, LANES), f32),
        grid=(T // tm,),
        in_specs=[pl.BlockSpec((tm, D), lambda i: (i, 0)),
                  pl.BlockSpec((D, LANES), lambda i: (0, 0)),
                  pl.BlockSpec((1, LANES), lambda i: (0, 0))],
        out_specs=pl.BlockSpec((tm, LANES), lambda i: (i, 0)),
        compiler_params=_params("parallel"),
    )(x, w, b)


MOE_TM = 256
LN_TM = 256


def _moe_plan(route, T):
    n_slots = TOPK_IN * T
    n_tiles = n_slots // MOE_TM + N_EXP
    ids = route[:, 2:2 + TOPK_IN].astype(jnp.int32)
    e_flat = ids.T.reshape(n_slots)
    onehot = (e_flat[:, None] == jnp.arange(N_EXP, dtype=jnp.int32)[None, :]).astype(jnp.int32)
    csum = jnp.cumsum(onehot, axis=0)
    rank = jnp.sum(onehot * csum, axis=1) - 1
    counts = csum[-1]
    tiles_per = (counts + MOE_TM - 1) // MOE_TM
    tile_end = jnp.cumsum(tiles_per)
    tile_start = tile_end - tiles_per
    dest = jnp.sum(onehot * tile_start[None, :], axis=1) * MOE_TM + rank
    tok = jnp.arange(n_slots, dtype=jnp.int32) % T
    src = jnp.zeros((n_tiles * MOE_TM,), jnp.int32).at[dest].set(tok, unique_indices=True)
    j = jnp.arange(n_tiles, dtype=jnp.int32)
    tile_e = jnp.sum((j[:, None] >= tile_end[None, :]).astype(jnp.int32), axis=1)
    last_e = jnp.max(jnp.where(counts > 0, jnp.arange(N_EXP, dtype=jnp.int32), 0))
    tile_e = jnp.minimum(tile_e, last_e)
    n_live = tile_end[-1:].astype(jnp.int32)
    return tile_e, n_live, src, dest.astype(jnp.int32), n_tiles


GATHER_UNROLL = 8


def _moe_group_kernel(tile_e_ref, nt_ref, src_ref, xp_ref, wg_ref, wu_ref, wd_ref, o_ref, xbuf):
    i = pl.program_id(0)
    nt = nt_ref[0]
    tm = xbuf.shape[0]

    @pl.when(i < nt)
    def _():
        def body(j, c):
            rows = [xp_ref[pl.ds(src_ref[i * tm + j * GATHER_UNROLL + q], 1), :] for q in range(GATHER_UNROLL)]
            xbuf[pl.ds(pl.multiple_of(j * GATHER_UNROLL, GATHER_UNROLL), GATHER_UNROLL), :] = jnp.concatenate(rows, axis=0)
            return c
        lax.fori_loop(0, tm // GATHER_UNROLL, body, 0)
        lo, hi = _unpack_halves(xbuf[...])
        x = jnp.concatenate([lo.astype(bf16), hi.astype(bf16)], axis=1)
        h = _silu(jnp.dot(x, wg_ref[...].astype(bf16), preferred_element_type=f32)) * \
            jnp.dot(x, wu_ref[...].astype(bf16), preferred_element_type=f32)
        o_ref[...] = _pack_halves(_dot(h, wd_ref[...]))

    @pl.when(i >= nt)
    def _():
        o_ref[...] = jnp.zeros_like(o_ref)


def _moe_group(xp, tile_e, n_live, src, n_tiles, w_gate, w_up, w_down, layer):
    T, half = xp.shape
    D = 2 * half
    wspec = lambda shp: pl.BlockSpec((None, None) + shp, lambda i, te, nt, sr: (layer, te[i], 0, 0))
    return pl.pallas_call(
        _moe_group_kernel,
        out_shape=jax.ShapeDtypeStruct((n_tiles * MOE_TM, half), jnp.uint32),
        grid_spec=pltpu.PrefetchScalarGridSpec(
            num_scalar_prefetch=3,
            grid=(n_tiles,),
            in_specs=[pl.BlockSpec(memory_space=pltpu.VMEM), wspec((D, D_E)), wspec((D, D_E)),
                      wspec((D_E, D))],
            out_specs=pl.BlockSpec((MOE_TM, half), lambda i, te, nt, sr: (i, 0)),
            scratch_shapes=[pltpu.VMEM((MOE_TM, half), jnp.uint32)]),
        compiler_params=_params("arbitrary"),
    )(tile_e, n_live, src, xp, w_gate, w_up, w_down)


def _moe_combine_ln_kernel(pos_ref, x_ref, rt_ref, g_ref, b_ref, ys_hbm, o_ref, ob_ref, ybuf, sems):
    i = pl.program_id(0)
    n = pl.num_programs(0)
    tm = x_ref.shape[0]
    T = n * tm

    def issue(tile, slot):
        for k in range(TOPK_IN):
            def body(j, c, k=k):
                for q in range(GATHER_UNROLL):
                    r = j * GATHER_UNROLL + q
                    p = pos_ref[k * T + tile * tm + r]
                    pltpu.make_async_copy(ys_hbm.at[pl.ds(p, 1), :], ybuf.at[slot, k, pl.ds(r, 1), :],
                                          sems.at[slot]).start(priority=q % 2)
                return c
            lax.fori_loop(0, tm // GATHER_UNROLL, body, 0)

    @pl.when(i == 0)
    def _():
        issue(0, 0)

    @pl.when(i + 1 < n)
    def _():
        issue(i + 1, (i + 1) % 2)

    slot = i % 2
    for k in range(TOPK_IN):
        pltpu.make_async_copy(ys_hbm.at[pl.ds(0, tm), :], ybuf.at[slot, k], sems.at[slot]).wait()
    rt = rt_ref[...]
    w1, w2 = rt[:, 0:1], rt[:, 1:2]
    lo1, hi1 = _unpack_halves(ybuf[slot, 0])
    lo2, hi2 = _unpack_halves(ybuf[slot, 1])
    y = jnp.concatenate([w1 * lo1 + w2 * lo2, w1 * hi1 + w2 * hi2], axis=1)
    o = _layer_norm(ALPHA * x_ref[...] + y, g_ref[...], b_ref[...])
    o_ref[...] = o
    ob_ref[...] = o.astype(bf16)


def _moe_combine_ln(x, route, pos, ys, g, b, layer):
    T, D = x.shape
    tm = min(T, LN_TM)
    vec = pl.BlockSpec((None, 1, D), lambda i, p: (layer, 0, 0))
    row = pl.BlockSpec((tm, D), lambda i, p: (i, 0))
    return pl.pallas_call(
        _moe_combine_ln_kernel,
        out_shape=(jax.ShapeDtypeStruct((T, D), f32), jax.ShapeDtypeStruct((T, D), bf16)),
        grid_spec=pltpu.PrefetchScalarGridSpec(
            num_scalar_prefetch=1,
            grid=(T // tm,),
            in_specs=[row, pl.BlockSpec((tm, LANES), lambda i, p: (i, 0)), vec, vec,
                      pl.BlockSpec(memory_space=pl.ANY)],
            out_specs=(row, row),
            scratch_shapes=[pltpu.VMEM((2, TOPK_IN, tm, D // 2), jnp.uint32),
                            pltpu.SemaphoreType.DMA((2,))]),
        compiler_params=_params("arbitrary"),
    )(pos, x, route, g.reshape(g.shape[0], 1, D), b.reshape(b.shape[0], 1, D), ys)


def _moe_ln(xf, xp, route, w_gate, w_up, w_down, g, b, layer):
    tile_e, n_live, src, pos, n_tiles = _moe_plan(route, xf.shape[0])
    ys = _moe_group(xp, tile_e, n_live, src, n_tiles, w_gate, w_up, w_down, layer)
    return _moe_combine_ln(xf, route, pos, ys, g, b, layer)


def kernel(x, mem, ab_w_in, rg_conv_w, rg_conv_b, rg_wa, rg_ba, rg_wx, rg_bx, rg_lam, ssd_conv_w, ssd_conv_b, ssd_dt_bias, ssd_a_log, ssd_d, ssd_norm_w, ab_w_out, cd_w_in, s5_a_re, s5_a_im, s5_log_step, s5_b_re, s5_b_im, s5_c_re, s5_c_im, s5_d, s5_glu_w, s5_glu_b, dn_conv_w, dn_a_log, dn_dt_bias, dn_norm_w, cd_w_out, xa_w_q, xa_w_kv, xa_w_o, moe_w_group, moe_b_group, moe_w_expert, moe_b_expert, moe_w_gate, moe_w_up, moe_w_down, ln1_g, ln1_b, ln2_g, ln2_b, ln3_g, ln3_b):
    B, S, D = x.shape
    M = mem.shape[1]
    T = B * S
    xf = x.reshape(T, D)
    xb = xf.astype(bf16)
    memb = mem.reshape(B * M, D).astype(bf16)
    for l in range(DEPTH):
        i = l // 2
        if l % 2 == 0:
            proj = _mm(xb, ab_w_in, i, MAIN_AB, f32)
            dt_tail = _mm_hi(xf, _tail_cols(ab_w_in, i, MAIN_AB))
            ya = _rglru(proj, B, S, i, rg_conv_w, rg_conv_b, rg_wa, rg_ba, rg_wx, rg_bx, rg_lam)
            yb = _ssd(proj, dt_tail, B, S, i, ssd_conv_w, ssd_conv_b, ssd_dt_bias, ssd_a_log, ssd_d,
                      ssd_norm_w)
            xf, xb = _mm_ln([ya, yb], ab_w_out[i].astype(bf16), xf, ln1_g, ln1_b, l)
        else:
            proj = _mm(xb, cd_w_in, i, MAIN_CD, f32)
            ab_tail = _mm_hi(xf, _tail_cols(cd_w_in, i, MAIN_CD))
            yc = _s5(proj, B, S, i, s5_a_re, s5_a_im, s5_log_step, s5_b_re, s5_b_im, s5_c_re, s5_c_im,
                     s5_d, s5_glu_w, s5_glu_b)
            yd = _gdn(proj, ab_tail, B, S, i, dn_conv_w, dn_a_log, dn_dt_bias, dn_norm_w)
            xf, xb = _mm_ln([yc, yd], cd_w_out[i].astype(bf16), xf, ln1_g, ln1_b, l)
        q = _mm(xb, xa_w_q, l, D, bf16)
        kv = _mm(memb, xa_w_kv, l, 2 * D, bf16)
        att = _xattn(q, kv, B, S, M)
        xf, xp = _mm_ln([att], xa_w_o[l].astype(bf16), xf, ln2_g, ln2_b, l, packed=True)
        route = _route(xf, moe_w_group, moe_b_group, moe_w_expert, moe_b_expert, l)
        xf, xb = _moe_ln(xf, xp, route, moe_w_gate, moe_w_up, moe_w_down, ln3_g, ln3_b, l)
    return xf.reshape(B, S, D)
```

```python
import functools
import math

import jax
import jax.numpy as jnp
from jax import lax
from jax.experimental import pallas as pl
from jax.experimental.pallas import tpu as pltpu

f32 = jnp.float32
bf16 = jnp.bfloat16

D_MODEL = 2048
DEPTH = 2
CHUNK = 64
CONV_W = 4
ALPHA = (2 * DEPTH) ** 0.25
LN_EPS = 1e-5
RMS_EPS = 1e-6
W_A = D_MODEL // 2
H_A = 8
BW_A = W_A // H_A
RG_C = 8.0
W_B = D_MODEL
HD_B = 64
H_B = W_B // HD_B
NG_B = 2
N_B = 128
HG_B = H_B // NG_B
CONV_B = W_B + 2 * NG_B * N_B
MAIN_AB = 2 * W_A + W_B + CONV_B
W_C = D_MODEL // 2
GS_C = 16
G_C = W_C // GS_C
P_C = 64
L_C = 16
H_D = 8
DK_D = D_MODEL // 16
DV_D = D_MODEL // 16
W_D = H_D * DV_D
QKV_D = 2 * H_D * DK_D + W_D
MAIN_CD = W_C + QKV_D + W_D
H_X = 4
HD_X = D_MODEL // H_X
NG_E = 4
E_PER = 8
N_EXP = NG_E * E_PER
TOPK_IN = 2
D_E = D_MODEL // 8

LANES = 128
SUBLANES = 8
S5_GB = LANES // GS_C
S5_NB = G_C // S5_GB
VMEM_LIMIT = 56 * 1024 * 1024


def _params(*sem):
    return pltpu.CompilerParams(dimension_semantics=sem, vmem_limit_bytes=VMEM_LIMIT)


def _sigmoid(x):
    return 1.0 / (1.0 + jnp.exp(-x))


def _silu(x):
    return x * _sigmoid(x)


def _softplus(x):
    return jnp.maximum(x, 0.0) + jnp.log(1.0 + jnp.exp(-jnp.abs(x)))


def _gelu_tanh(x):
    return 0.5 * x * (1.0 + jnp.tanh(math.sqrt(2.0 / math.pi) * (x + 0.044715 * (x * x * x))))


def _dot(a, b):
    return jnp.dot(a.astype(bf16), b.astype(bf16), preferred_element_type=f32)


def _dot_nt(a, b):
    return lax.dot_general(a.astype(bf16), b.astype(bf16), (((1,), (1,)), ((), ())),
                           preferred_element_type=f32)


def _dot_tn(a, b):
    return lax.dot_general(a.astype(bf16), b.astype(bf16), (((0,), (0,)), ((), ())),
                           preferred_element_type=f32)


def _split3(a):
    hi = a.astype(bf16)
    r = a - hi.astype(f32)
    mid = r.astype(bf16)
    lo = (r - mid.astype(f32)).astype(bf16)
    return hi, mid, lo


def _dot_exact_lhs(sel, b):
    s = sel.astype(bf16)
    b1, b2, b3 = _split3(b)
    d = functools.partial(jnp.dot, preferred_element_type=f32)
    return d(s, b1) + d(s, b2) + d(s, b3)


def _dot_exact_rhs(a, sel):
    s = sel.astype(bf16)
    a1, a2, a3 = _split3(a)
    d = functools.partial(jnp.dot, preferred_element_type=f32)
    return d(a1, s) + d(a2, s) + d(a3, s)


def _transpose_exact(a, eye):
    a1, a2, a3 = _split3(a)
    e = eye.astype(bf16)
    d = lambda x: lax.dot_general(e, x, (((1,), (1,)), ((), ())), preferred_element_type=f32)
    return d(a1) + d(a2) + d(a3)


def _dot3(a, b):
    a1 = a.astype(bf16)
    a2 = (a - a1.astype(f32)).astype(bf16)
    b1 = b.astype(bf16)
    b2 = (b - b1.astype(f32)).astype(bf16)
    d = functools.partial(jnp.dot, preferred_element_type=f32)
    return d(a1, b1) + (d(a1, b2) + d(a2, b1))


def _iota(shape, axis):
    return lax.broadcasted_iota(jnp.int32, shape, axis)


def _tri(n, strict=False):
    r, c = _iota((n, n), 0), _iota((n, n), 1)
    return (r > c) if strict else (r >= c)


def _mm_kernel(x_ref, w_ref, o_ref):
    o_ref[...] = jnp.dot(x_ref[...], w_ref[...].astype(bf16),
                         preferred_element_type=f32).astype(o_ref.dtype)


def _mm_nt_kernel(x_ref, wt_ref, o_ref):
    o_ref[...] = lax.dot_general(x_ref[...], wt_ref[...].astype(bf16), (((1,), (1,)), ((), ())),
                                 preferred_element_type=f32).astype(o_ref.dtype)


def _mm(x, w, layer, n_cols, out_dtype, tn=512, transposed=False):
    M, K = x.shape
    tm = min(M, 2048)
    if transposed:
        body, w_spec = _mm_nt_kernel, pl.BlockSpec((None, tn, K), lambda i, j: (layer, j, 0))
    else:
        body, w_spec = _mm_kernel, pl.BlockSpec((None, K, tn), lambda i, j: (layer, 0, j))
    return pl.pallas_call(
        body,
        out_shape=jax.ShapeDtypeStruct((M, n_cols), out_dtype),
        grid=(M // tm, n_cols // tn),
        in_specs=[pl.BlockSpec((tm, K), lambda i, j: (i, 0)), w_spec],
        out_specs=pl.BlockSpec((tm, tn), lambda i, j: (i, j)),
        compiler_params=_params("parallel", "parallel"),
    )(x, w)


def _tail_cols(wt, layer, start):
    _, N, K = wt.shape
    return lax.slice(wt, (layer, start, 0), (layer + 1, N, K)).reshape(N - start, K).T


def _mm_hi_kernel(x_ref, w_ref, o_ref):
    o_ref[...] = _dot3(x_ref[...], w_ref[...])


def _mm_hi(x, w):
    M, K = x.shape
    n = w.shape[1]
    tm = min(M, 512)
    return pl.pallas_call(
        _mm_hi_kernel,
        out_shape=jax.ShapeDtypeStruct((M, n), f32),
        grid=(M // tm,),
        in_specs=[pl.BlockSpec((tm, K), lambda i: (i, 0)),
                  pl.BlockSpec((K, n), lambda i: (0, 0))],
        out_specs=pl.BlockSpec((tm, n), lambda i: (i, 0)),
        compiler_params=_params("parallel"),
    )(x, w)


def _layer_norm(v, g, b):
    mu = jnp.mean(v, -1, keepdims=True)
    d = v - mu
    var = jnp.mean(d * d, -1, keepdims=True)
    return d * lax.rsqrt(var + LN_EPS) * g + b


HI_MASK = 0xFFFF0000


def _pack_halves(v):
    h = v.shape[1] // 2
    lo = lax.bitcast_convert_type(v[:, :h].astype(bf16).astype(f32), jnp.uint32)
    hi = lax.bitcast_convert_type(v[:, h:].astype(bf16).astype(f32), jnp.uint32)
    return (hi & jnp.uint32(HI_MASK)) | (lo >> 16)


def _unpack_halves(u):
    lo = lax.bitcast_convert_type(u << 16, f32)
    hi = lax.bitcast_convert_type(u & jnp.uint32(HI_MASK), f32)
    return lo, hi


def _route_top2(logits):
    lane = _iota(logits.shape, 1)
    neg = -jnp.inf
    big = jnp.int32(LANES)
    is_g = (lane >= N_EXP) & (lane < N_EXP + NG_E)
    gl = jnp.where(is_g, logits, neg)
    gmax = jnp.max(gl, -1, keepdims=True)
    g_lane = jnp.min(jnp.where(gl == gmax, lane, big), -1, keepdims=True)
    g_prob = 1.0 / jnp.sum(jnp.where(is_g, jnp.exp(gl - gmax), 0.0), -1, keepdims=True)
    e0 = (g_lane - N_EXP) * E_PER
    sel = jnp.where((lane >= e0) & (lane < e0 + E_PER), logits, neg)
    m1 = jnp.max(sel, -1, keepdims=True)
    i1 = jnp.min(jnp.where(sel == m1, lane, big), -1, keepdims=True)
    sel2 = jnp.where(lane == i1, neg, sel)
    m2 = jnp.max(sel2, -1, keepdims=True)
    i2 = jnp.min(jnp.where(sel2 == m2, lane, big), -1, keepdims=True)
    e2 = jnp.exp(m2 - m1)
    w1 = g_prob / (1.0 + e2)
    w2 = g_prob * e2 / (1.0 + e2)
    return (jnp.where(lane == 0, w1, 0.0) + jnp.where(lane == 1, w2, 0.0)
            + jnp.where(lane == 2, i1.astype(f32), 0.0) + jnp.where(lane == 3, i2.astype(f32), 0.0))


def _mm_ln_kernel(*refs, n_parts, routed):
    part_refs = refs[:n_parts]
    w_ref, x_ref, g_ref, b_ref = refs[n_parts:n_parts + 4]
    acc, off = None, 0
    for p in part_refs:
        c = jnp.dot(p[...], w_ref[off:off + p.shape[1], :], preferred_element_type=f32)
        acc = c if acc is None else acc + c
        off += p.shape[1]
    o = _layer_norm(ALPHA * x_ref[...] + acc, g_ref[...], b_ref[...])
    if routed:
        rw_ref, rb_ref, o_ref, o2_ref, rt_ref = refs[n_parts + 4:]
        o2_ref[...] = _pack_halves(o)
        rt_ref[...] = _route_top2(_dot3(o, rw_ref[...]) + rb_ref[...])
    else:
        o_ref, o2_ref = refs[n_parts + 4:]
        o2_ref[...] = o.astype(bf16)
    o_ref[...] = o


def _mm_ln(parts, w, x, g, b, layer, route_wb=None):
    T, D = x.shape
    tm = min(T, 512)
    routed = route_wb is not None
    vec = pl.BlockSpec((None, 1, D), lambda i: (layer, 0, 0))
    row = pl.BlockSpec((tm, D), lambda i: (i, 0))
    in_specs = [pl.BlockSpec((tm, a.shape[1]), lambda i: (i, 0)) for a in parts]
    in_specs += [pl.BlockSpec(memory_space=pltpu.VMEM), row, vec, vec]
    args = [*parts, w, x, g.reshape(g.shape[0], 1, D), b.reshape(b.shape[0], 1, D)]
    if routed:
        in_specs += [pl.BlockSpec((D, LANES), lambda i: (0, 0)), pl.BlockSpec((1, LANES), lambda i: (0, 0))]
        args += list(route_wb)
        out_shape = (jax.ShapeDtypeStruct((T, D), f32), jax.ShapeDtypeStruct((T, D // 2), jnp.uint32),
                     jax.ShapeDtypeStruct((T, LANES), f32))
        out_specs = (row, pl.BlockSpec((tm, D // 2), lambda i: (i, 0)), pl.BlockSpec((tm, LANES), lambda i: (i, 0)))
    else:
        out_shape = (jax.ShapeDtypeStruct((T, D), f32), jax.ShapeDtypeStruct((T, D), bf16))
        out_specs = (row, row)
    return pl.pallas_call(
        functools.partial(_mm_ln_kernel, n_parts=len(parts), routed=routed),
        out_shape=out_shape,
        grid=(T // tm,),
        in_specs=in_specs,
        out_specs=out_specs,
        compiler_params=_params("parallel"),
    )(*args)


def _conv_rows(x_ref, w, hist_ref):
    tt = x_ref.shape[0]
    hist_ref[SUBLANES:SUBLANES + tt, :] = x_ref[...]
    acc = _causal_taps(hist_ref[...], w, tt)
    hist_ref[0:SUBLANES, :] = hist_ref[tt:tt + SUBLANES, :]
    return acc


def _causal_taps(ext, w, tt):
    acc = None
    for k in range(CONV_W):
        back = CONV_W - 1 - k
        rows = (pltpu.roll(ext, back, 0) if back else ext)[SUBLANES:SUBLANES + tt, :]
        term = w[k:k + 1, :] * rows
        acc = term if acc is None else acc + term
    return acc


def _scan_affine(a, u):
    n = a.shape[0]
    row = _iota(a.shape, 0)
    d = 1
    while d < n:
        keep = row >= d
        a_s = pltpu.roll(a, d, 0)
        u_s = pltpu.roll(u, d, 0)
        u = u + jnp.where(keep, a * u_s, 0.0)
        a = jnp.where(keep, a * a_s, a)
        d *= 2
    return a, u


def _rglru_kernel(gate_ref, xa_ref, cw_ref, cb_ref, wa_ref, ba_ref, wx_ref, bx_ref, lam_ref,
                  o_ref, hist_ref, h_ref):
    @pl.when(pl.program_id(1) == 0)
    def _():
        h_ref[...] = jnp.zeros_like(h_ref)
        hist_ref[:, 0:SUBLANES, :] = jnp.zeros((hist_ref.shape[0], SUBLANES, hist_ref.shape[2]), f32)

    decay_rate = -RG_C * _softplus(-lam_ref[...])
    for b in range(xa_ref.shape[0]):
        xc = _conv_rows(xa_ref.at[b], cw_ref[...], hist_ref.at[b]) + cb_ref[...]
        r = _sigmoid(_dot(xc, wa_ref[...]) + ba_ref[...])
        i = _sigmoid(_dot(xc, wx_ref[...]) + bx_ref[...])
        log_a = r * decay_rate
        a = jnp.exp(log_a)
        u = jnp.sqrt(1.0 - jnp.exp(2.0 * log_a)) * (i * xc)
        a_cum, h = _scan_affine(a, u)
        h = h + a_cum * h_ref[b, 0:1, :]
        tt = h.shape[0]
        h_ref[b] = jnp.broadcast_to(h[tt - 1:tt, :], h_ref.shape[1:])
        o_ref[b] = (_gelu_tanh(gate_ref[b]) * h).astype(o_ref.dtype)


def _rglru(proj, B, S, layer, conv_w, conv_b, wa, ba, wx, bx, lam):
    tt = min(S, 256)
    ns = S // tt
    xa_col0 = W_A // BW_A
    vec = lambda a: a.reshape(a.shape[0], 1, W_A)
    vspec = pl.BlockSpec((None, 1, BW_A), lambda h, s: (layer, 0, h))
    wspec = pl.BlockSpec((None, None, BW_A, BW_A), lambda h, s: (layer, h, 0, 0))
    proj3 = proj.reshape(B, S, proj.shape[1])
    out = pl.pallas_call(
        _rglru_kernel,
        out_shape=jax.ShapeDtypeStruct((B, S, W_A), bf16),
        grid=(H_A, ns),
        in_specs=[pl.BlockSpec((B, tt, BW_A), lambda h, s: (0, s, h)),
                  pl.BlockSpec((B, tt, BW_A), lambda h, s: (0, s, xa_col0 + h)),
                  pl.BlockSpec((None, CONV_W, BW_A), lambda h, s: (layer, 0, h)),
                  vspec, wspec, vspec, wspec, vspec, vspec],
        out_specs=pl.BlockSpec((B, tt, BW_A), lambda h, s: (0, s, h)),
        scratch_shapes=[pltpu.VMEM((B, SUBLANES + tt, BW_A), f32), pltpu.VMEM((B, SUBLANES, BW_A), f32)],
        compiler_params=_params("parallel", "arbitrary"),
    )(proj3, proj3, conv_w, vec(conv_b), wa, vec(ba), wx, vec(bx), vec(lam))
    return out.reshape(B * S, W_A)


def _ssd_group(X, Bc, Cc, z, dt, a_neg, d_head, nw, st_ref):
    L = CHUNK
    adt = dt * a_neg
    tri = _tri(L)
    cs = _dot_exact_lhs(tri.astype(f32), adt)
    eye_h = (_iota((HG_B, HG_B), 0) == _iota((HG_B, HG_B), 1)).astype(f32)
    cs_t = _transpose_exact(cs, eye_h)
    cs_last = cs[L - 1:L, :]

    expand = (_iota((HG_B, HG_B * HD_B), 1) // HD_B == _iota((HG_B, HG_B * HD_B), 0)).astype(f32)
    dt_x = _dot_exact_rhs(dt, expand)
    ecs_x = _dot_exact_rhs(jnp.exp(cs), expand)
    dec_x = _dot_exact_rhs(jnp.exp(cs_last - cs), expand)
    cdec_x = _dot_exact_rhs(jnp.broadcast_to(jnp.exp(cs_last), (SUBLANES, HG_B)), expand)[0:1]
    d_x = _dot_exact_rhs(jnp.broadcast_to(d_head, (SUBLANES, HG_B)), expand)[0:1]

    xdt = X * dt_x
    cb = _dot_nt(Cc, Bc)
    xdt_b = xdt.astype(bf16)
    left = _iota((L, 2 * HD_B), 1) < HD_B
    pieces = []
    for j in range(0, HG_B, 2):
        pair = xdt_b[:, j * HD_B:(j + 2) * HD_B]
        outs = []
        for jj in (j, j + 1):
            seg = jnp.where(tri, cs[:, jj:jj + 1] - cs_t[jj:jj + 1, :], -jnp.inf)
            outs.append(jnp.dot((cb * jnp.exp(seg)).astype(bf16), pair, preferred_element_type=f32))
        pieces.append(jnp.where(left, outs[0], outs[1]))
    y_diag = jnp.concatenate(pieces, axis=1)

    state = st_ref[...]
    y_off = ecs_x * _dot(Cc, state)
    st_ref[...] = state * cdec_x + _dot_tn(Bc, xdt * dec_x)

    y = y_diag + y_off + X * d_x
    yg = y * _silu(z)
    return yg * lax.rsqrt(jnp.mean(yg * yg, -1, keepdims=True) + RMS_EPS) * nw


def _ssd_kernel(z_ref, x_ref, b_ref, c_ref, dt_ref, cw_ref, cb_ref, dtb_ref, alog_ref, d_ref, nw_ref,
                o_ref, hx_ref, hb_ref, hc_ref, st_ref):
    GW = W_B // NG_B

    @pl.when(pl.program_id(0) == 0)
    def _():
        st_ref[...] = jnp.zeros_like(st_ref)
        for h in (hx_ref, hb_ref, hc_ref):
            h[:, 0:SUBLANES, :] = jnp.zeros((h.shape[0], SUBLANES, h.shape[2]), f32)

    cw, cbias = cw_ref[...], cb_ref[...]
    c_off = W_B + NG_B * N_B
    a_all = -jnp.exp(alog_ref[...])
    d_all = d_ref[...]
    nw_all = nw_ref[...]
    for b in range(x_ref.shape[0]):
        X = _silu(_conv_rows(x_ref.at[b], cw[:, :W_B], hx_ref.at[b]) + cbias[:, :W_B])
        Bm = _silu(_conv_rows(b_ref.at[b], cw[:, W_B:c_off], hb_ref.at[b]) + cbias[:, W_B:c_off])
        Cm = _silu(_conv_rows(c_ref.at[b], cw[:, c_off:], hc_ref.at[b]) + cbias[:, c_off:])
        dt = _softplus(dt_ref[b] + dtb_ref[...])
        for g in range(NG_B):
            hs = slice(g * HG_B, (g + 1) * HG_B)
            ws = slice(g * GW, (g + 1) * GW)
            ns = slice(g * N_B, (g + 1) * N_B)
            out = _ssd_group(X[:, ws], Bm[:, ns], Cm[:, ns], z_ref[b, :, ws], dt[:, hs], a_all[:, hs],
                             d_all[:, hs], nw_all[:, ws], st_ref.at[b * NG_B + g])
            o_ref[b, :, ws] = out.astype(o_ref.dtype)


def _ssd(proj, dt_tail, B, S, layer, conv_w, conv_b, dt_bias, a_log, d, norm_w):
    L = CHUNK
    nc = S // L
    BC = NG_B * N_B
    per_batch = lambda a: a.reshape(B, S, a.shape[1])
    blk = lambda n, cb: pl.BlockSpec((B, L, n), lambda c: (0, c, cb))
    vec = lambda a: a.reshape(a.shape[0], 1, a.shape[-1])
    par = lambda r, n: pl.BlockSpec((None, r, n), lambda c: (layer, 0, 0))
    proj3 = per_batch(proj)
    out = pl.pallas_call(
        _ssd_kernel,
        out_shape=jax.ShapeDtypeStruct((B, S, W_B), bf16),
        grid=(nc,),
        in_specs=[blk(W_B, 2 * W_A // W_B), blk(W_B, (2 * W_A + W_B) // W_B),
                  blk(BC, (2 * W_A + 2 * W_B) // BC), blk(BC, (2 * W_A + 2 * W_B) // BC + 1),
                  blk(H_B, 0), par(CONV_W, CONV_B), par(1, CONV_B), par(1, H_B), par(1, H_B), par(1, H_B),
                  par(1, W_B)],
        out_specs=blk(W_B, 0),
        scratch_shapes=[pltpu.VMEM((B, SUBLANES + L, W_B), f32), pltpu.VMEM((B, SUBLANES + L, BC), f32),
                        pltpu.VMEM((B, SUBLANES + L, BC), f32), pltpu.VMEM((B * NG_B, N_B, W_B // NG_B), f32)],
        compiler_params=_params("arbitrary"),
    )(proj3, proj3, proj3, proj3, per_batch(dt_tail), conv_w, vec(conv_b), vec(dt_bias), vec(a_log), vec(d),
      vec(norm_w))
    return out.reshape(B * S, W_B)


def _s5_tables(a_re, a_im, log_step, b_re, b_im, c_re, c_im):
    L = L_C
    ar, ai = a_re.astype(f32), a_im.astype(f32)
    step = jnp.exp(log_step.astype(f32))[:, None]
    mag = jnp.exp(ar * step)
    lb_re, lb_im = mag * jnp.cos(ai * step), mag * jnp.sin(ai * step)
    den = ar * ar + ai * ai
    f_re = ((lb_re - 1.0) * ar + lb_im * ai) / den
    f_im = (lb_im * ar - (lb_re - 1.0) * ai) / den
    br, bi = b_re.astype(f32), b_im.astype(f32)
    bb_re = f_re[..., None] * br - f_im[..., None] * bi
    bb_im = f_re[..., None] * bi + f_im[..., None] * br
    cr, ci = c_re.astype(f32), c_im.astype(f32)

    def power(n):
        n = n.astype(f32)[None, :, None]
        m = jnp.exp(ar[:, None, :] * step[:, None, :] * n)
        ang = ai[:, None, :] * step[:, None, :] * n
        return m * jnp.cos(ang), m * jnp.sin(ang)

    j = jnp.arange(L)
    pr, pi = power(j)
    lbr = pr[..., None] * bb_re[:, None] - pi[..., None] * bb_im[:, None]
    lbi = pr[..., None] * bb_im[:, None] + pi[..., None] * bb_re[:, None]
    kern = (jnp.einsum('gop,gjpk->gjko', cr, lbr) - jnp.einsum('gop,gjpk->gjko', ci, lbi))
    lagb = kern.reshape(S5_NB, S5_GB, L, GS_C, GS_C).transpose(0, 2, 1, 3, 4)
    lagb = lagb.reshape(S5_NB, L, LANES, GS_C)
    rev = (L - 1) - j
    bend = jnp.concatenate([jnp.take(lbr, rev, axis=1), jnp.take(lbi, rev, axis=1)], 2)
    bendc = bend.reshape(S5_NB, S5_GB, L, 2 * P_C, GS_C).transpose(0, 2, 1, 4, 3)
    bendc = bendc.reshape(S5_NB, L, LANES, 2 * P_C)
    qr, qi = power(j + 1)
    car_re = cr[:, None] * qr[:, :, None, :] - ci[:, None] * qi[:, :, None, :]
    car_im = -(cr[:, None] * qi[:, :, None, :] + ci[:, None] * qr[:, :, None, :])
    car = jnp.concatenate([car_re, car_im], -1)
    ccarc = car.reshape(S5_NB, S5_GB, L, GS_C, 2 * P_C).transpose(0, 2, 4, 1, 3)
    ccarc = ccarc.reshape(S5_NB, L, 2 * P_C, LANES)
    return lagb, bendc.astype(bf16), ccarc.astype(bf16), power


def _s5_kernel(u_ref, lag_ref, bend_ref, ccar_ref, sc_ref, o_ref, toep_t, bend_t, ccar_t):
    L = L_C
    SW = 2 * P_C

    @pl.when(pl.program_id(1) == 0)
    def _():
        zero = jnp.zeros((LANES, LANES), bf16)
        own_g = (_iota((LANES, LANES), 0) // GS_C) == (_iota((LANES, LANES), 1) // GS_C)
        lag = [jnp.where(own_g, jnp.concatenate([lag_ref[j]] * S5_GB, axis=1), 0.0).astype(bf16) for j in range(L)]
        for s in range(L):
            for t in range(L):
                toep_t[s * LANES:(s + 1) * LANES, t * LANES:(t + 1) * LANES] = lag[t - s] if t >= s else zero
        same_g = (_iota((LANES, S5_GB * SW), 0) // GS_C) == (_iota((LANES, S5_GB * SW), 1) // SW)
        for s in range(L):
            wide = jnp.concatenate([bend_ref[s]] * S5_GB, axis=1)
            bend_t[s * LANES:(s + 1) * LANES, :] = jnp.where(same_g, wide, jnp.zeros_like(wide))
        col_g = _iota((SW, LANES), 1) // GS_C
        for t in range(L):
            blk = ccar_ref[t]
            for g in range(S5_GB):
                ccar_t[g * SW:(g + 1) * SW, t * LANES:(t + 1) * LANES] = jnp.where(col_g == g, blk, jnp.zeros_like(blk))

    n = o_ref.shape[0] // L
    U = jnp.concatenate([u_ref[pl.ds(l, n, stride=L), :].astype(bf16) for l in range(L)], axis=1)
    Y = jnp.dot(U, toep_t[...], preferred_element_type=f32)
    H_all = jnp.dot(U, bend_t[...], preferred_element_type=f32)
    row = _iota((n, SW), 0)
    prev = []
    for g in range(S5_GB):
        sl = slice(g * SW, (g + 1) * SW)
        H = H_all[:, sl]
        d, k = 1, 0
        while d < n:
            hs = pltpu.roll(H, d, 0)
            sw = pltpu.roll(hs, P_C, 1)
            H = H + jnp.where(row >= d, sc_ref[2 * k:2 * k + 1, sl] * hs + sc_ref[2 * k + 1:2 * k + 2, sl] * sw, 0.0)
            d *= 2
            k += 1
        prev.append(jnp.where(row >= 1, pltpu.roll(H, 1, 0), 0.0).astype(bf16))
    Y = Y + jnp.dot(jnp.concatenate(prev, axis=1), ccar_t[...], preferred_element_type=f32)
    for l in range(L):
        o_ref[pl.ds(l, n, stride=L), :] = Y[:, l * LANES:(l + 1) * LANES]


def _s5_post_kernel(y_ref, u_ref, d_ref, w_ref, b_ref, o_ref):
    y = y_ref[...] + d_ref[...] * u_ref[...]
    g = _gelu_tanh(y)
    o_ref[...] = (g * _sigmoid(_dot(g, w_ref[...]) + b_ref[...])).astype(o_ref.dtype)


def _s5(proj, B, S, layer, a_re, a_im, log_step, b_re, b_im, c_re, c_im, d, glu_w, glu_b):
    T = B * S
    L = L_C
    nch = S // L
    lagb, bendc, ccarc, power = _s5_tables(a_re[layer], a_im[layer], log_step[layer], b_re[layer],
                                           b_im[layer], c_re[layer], c_im[layer])
    nsteps = max(1, (nch - 1).bit_length())
    sr, si = power(L * (2 ** jnp.arange(nsteps)))
    scan_c = jnp.stack([jnp.concatenate([sr, sr], -1), jnp.concatenate([-si, si], -1)], 2)
    scan8 = scan_c.reshape(S5_NB, S5_GB, 2 * nsteps, 2 * P_C).transpose(0, 2, 1, 3)
    scan8 = scan8.reshape(S5_NB, 2 * nsteps, S5_GB * 2 * P_C)
    tab = lambda r, c: pl.BlockSpec((None, L, r, c), lambda g, b: (g, 0, 0, 0))
    y = pl.pallas_call(
        _s5_kernel,
        out_shape=jax.ShapeDtypeStruct((T, W_C), f32),
        grid=(S5_NB, B),
        in_specs=[pl.BlockSpec((S, LANES), lambda g, b: (b, g)),
                  tab(LANES, GS_C), tab(LANES, 2 * P_C), tab(2 * P_C, LANES),
                  pl.BlockSpec((None, 2 * nsteps, S5_GB * 2 * P_C), lambda g, b: (g, 0, 0))],
        out_specs=pl.BlockSpec((S, LANES), lambda g, b: (b, g)),
        scratch_shapes=[pltpu.VMEM((L * LANES, L * LANES), bf16),
                        pltpu.VMEM((L * LANES, S5_GB * 2 * P_C), bf16),
                        pltpu.VMEM((S5_GB * 2 * P_C, L * LANES), bf16)],
        compiler_params=_params("arbitrary", "arbitrary"),
    )(proj, lagb, bendc, ccarc, scan8)
    tm = min(T, 1024)
    vec = pl.BlockSpec((None, 1, W_C), lambda i: (layer, 0, 0))
    return pl.pallas_call(
        _s5_post_kernel,
        out_shape=jax.ShapeDtypeStruct((T, W_C), bf16),
        grid=(T // tm,),
        in_specs=[pl.BlockSpec((tm, W_C), lambda i: (i, 0)),
                  pl.BlockSpec((tm, W_C), lambda i: (i, 0)),
                  vec,
                  pl.BlockSpec((None, W_C, W_C), lambda i: (layer, 0, 0)),
                  vec],
        out_specs=pl.BlockSpec((tm, W_C), lambda i: (i, 0)),
        compiler_params=_params("parallel"),
    )(y, proj, d.reshape(d.shape[0], 1, W_C), glu_w, glu_b.reshape(glu_b.shape[0], 1, W_C))


def _gdn_prep_kernel(q_ref, k_ref, v_ref, qp_ref, kp_ref, vp_ref, ab_ref, cwq_ref, cwk_ref, cwv_ref,
                     alog_ref, dtb_ref, w_ref, u_ref, qd_ref, kd_ref, qk_ref, ge_ref, hist_ref):
    L = CHUNK
    has_prev = (pl.program_id(1) > 0).astype(f32)

    def conv(cur_ref, prev_ref, w_ref_, slot):
        hist = hist_ref.at[slot]
        hist[0:SUBLANES, :] = prev_ref[...] * has_prev
        hist[SUBLANES:SUBLANES + L, :] = cur_ref[...]
        return _silu(_causal_taps(hist[...], w_ref_[...], L))

    q_all = conv(q_ref, qp_ref, cwq_ref, 0)
    k_all = conv(k_ref, kp_ref, cwk_ref, 1)
    v_all = conv(v_ref, vp_ref, cwv_ref, 2)
    ab = ab_ref[...]
    g_all = -jnp.exp(alog_ref[...]) * _softplus(ab[:, 0:H_D] + dtb_ref[...])
    beta_all = _sigmoid(ab[:, H_D:2 * H_D])
    tri = _tri(L)
    tri_s = _tri(L, strict=True)
    gcs_all = _dot_exact_lhs(tri.astype(f32), g_all)
    eye_h = (_iota((H_D, H_D), 0) == _iota((H_D, H_D), 1)).astype(f32)
    gcs_t = _transpose_exact(gcs_all, eye_h)
    eye = (_iota((L, L), 0) == _iota((L, L), 1)).astype(f32)
    ge_ref[...] = jnp.concatenate([jnp.exp(gcs_all), jnp.zeros((L, LANES - H_D), f32)], axis=1)

    heads = range(H_D)
    sls = [slice(h * DK_D, (h + 1) * DK_D) for h in heads]
    gcs = [gcs_all[:, h:h + 1] for h in heads]
    eg = [jnp.exp(g) for g in gcs]
    qs = [q_all[:, sl] * lax.rsqrt(jnp.sum(q_all[:, sl] * q_all[:, sl], -1, keepdims=True) + 1e-6)
          * (DK_D ** -0.5) for sl in sls]
    ks = [k_all[:, sl] * lax.rsqrt(jnp.sum(k_all[:, sl] * k_all[:, sl], -1, keepdims=True) + 1e-6)
          for sl in sls]
    kbs = [ks[h] * beta_all[:, h:h + 1] for h in heads]
    decay = [jnp.exp(jnp.where(tri, gcs[h] - gcs_t[h:h + 1, :], -jnp.inf)) for h in heads]
    kk = [_dot_nt(kbs[h], ks[h]) for h in heads]
    qk = [_dot_nt(qs[h], ks[h]) for h in heads]
    pw = [jnp.where(tri_s, -(kk[h] * decay[h]), 0.0) for h in heads]
    inv = [eye + p for p in pw]
    for _ in range(int(math.log2(L)) - 1):
        pw = [_dot(p, p) for p in pw]
        inv = [a + _dot(a, p) for a, p in zip(inv, pw)]
    rhs = [jnp.concatenate([v_all[:, sls[h]] * beta_all[:, h:h + 1], kbs[h] * eg[h]], axis=1) for h in heads]
    sol = [_dot(inv[h], rhs[h]) for h in heads]
    for h in heads:
        u_ref[:, sls[h]] = sol[h][:, :DV_D]
        w_ref[:, sls[h]] = sol[h][:, DV_D:]
        qk_ref[:, h * L:(h + 1) * L] = jnp.where(tri, qk[h] * decay[h], 0.0)
        qd_ref[:, sls[h]] = qs[h] * eg[h]
        kd_ref[:, sls[h]] = ks[h] * jnp.exp(gcs[h][L - 1:L, :] - gcs[h])


def _gdn_scan_kernel(w_ref, u_ref, qd_ref, kd_ref, qk_ref, ge_ref, z_ref, nw_ref, o_ref, st_ref):
    L = CHUNK

    @pl.when(pl.program_id(0) == 0)
    def _():
        st_ref[...] = jnp.zeros_like(st_ref)

    nw = nw_ref[...]
    pairs = [(b, h) for b in range(w_ref.shape[0]) for h in range(H_D)]
    cols = lambda h: slice(h * DK_D, (h + 1) * DK_D)
    state = [st_ref[b * H_D + h] for b, h in pairs]
    v_new = [u_ref[b, :, cols(h)] - _dot(w_ref[b, :, cols(h)], s) for (b, h), s in zip(pairs, state)]
    o = [_dot(qd_ref[b, :, cols(h)], s) + _dot(qk_ref[b, :, h * L:(h + 1) * L], v)
         for (b, h), s, v in zip(pairs, state, v_new)]
    for (b, h), s, v in zip(pairs, state, v_new):
        st_ref[b * H_D + h] = s * ge_ref[b, L - 1:L, h:h + 1] + _dot_tn(kd_ref[b, :, cols(h)], v)
    for (b, h), y in zip(pairs, o):
        y = y * lax.rsqrt(jnp.mean(y * y, -1, keepdims=True) + RMS_EPS) * nw
        o_ref[b, :, cols(h)] = (y * _silu(z_ref[b, :, cols(h)])).astype(o_ref.dtype)


def _gdn(proj, ab_tail, B, S, layer, conv_w, a_log, dt_bias, norm_w):
    T = B * S
    L = CHUNK
    nc = S // L
    per8 = L // SUBLANES
    q_c0 = W_C // W_D
    row = lambda b, c: (b * nc + c, 0)

    def cur(cb):
        return pl.BlockSpec((L, W_D), lambda b, c: (b * nc + c, cb))

    def prev(cb):
        return pl.BlockSpec((SUBLANES, W_D), lambda b, c: (jnp.maximum((b * nc + c) * per8 - 1, 0), cb))

    def cw(cb):
        return pl.BlockSpec((None, CONV_W, W_D), lambda b, c: (layer, 0, cb))

    v8 = pl.BlockSpec((None, 1, H_D), lambda b, c: (layer, 0, 0))
    wide = pl.BlockSpec((L, W_D), row)
    shp = lambda n: jax.ShapeDtypeStruct((T, n), f32)
    w_c, u_c, q_dec, k_dec, qk, gexp = pl.pallas_call(
        _gdn_prep_kernel,
        out_shape=(shp(W_D), shp(W_D), shp(W_D), shp(W_D), shp(H_D * L), shp(LANES)),
        grid=(B, nc),
        in_specs=[cur(q_c0), cur(q_c0 + 1), cur(q_c0 + 2), prev(q_c0), prev(q_c0 + 1), prev(q_c0 + 2),
                  pl.BlockSpec((L, 2 * H_D), row), cw(0), cw(1), cw(2), v8, v8],
        out_specs=(wide, wide, wide, wide, pl.BlockSpec((L, H_D * L), row),
                   pl.BlockSpec((L, LANES), row)),
        scratch_shapes=[pltpu.VMEM((3, SUBLANES + L, W_D), f32)],
        compiler_params=_params("parallel", "parallel"),
    )(proj, proj, proj, proj, proj, proj, ab_tail, conv_w, conv_w, conv_w,
      a_log.reshape(a_log.shape[0], 1, H_D), dt_bias.reshape(dt_bias.shape[0], 1, H_D))
    z_c0 = (W_C + QKV_D) // W_D
    per_batch = lambda a: a.reshape(B, S, a.shape[1])
    blk = lambda n, cb=0: pl.BlockSpec((B, L, n), lambda c: (0, c, cb))
    out = pl.pallas_call(
        _gdn_scan_kernel,
        out_shape=jax.ShapeDtypeStruct((B, S, W_D), bf16),
        grid=(nc,),
        in_specs=[blk(W_D), blk(W_D), blk(W_D), blk(W_D), blk(H_D * L), blk(LANES), blk(W_D, z_c0),
                  pl.BlockSpec((None, 1, DV_D), lambda c: (layer, 0, 0))],
        out_specs=blk(W_D),
        scratch_shapes=[pltpu.VMEM((B * H_D, DK_D, DV_D), f32)],
        compiler_params=_params("arbitrary"),
    )(per_batch(w_c), per_batch(u_c), per_batch(q_dec), per_batch(k_dec), per_batch(qk), per_batch(gexp),
      per_batch(proj), norm_w.reshape(norm_w.shape[0], 1, DV_D))
    return out.reshape(T, W_D)


def _xattn_kernel(q_ref, k_ref, v_ref, o_ref):
    s = _dot_nt(q_ref[...], k_ref[...]) * (HD_X ** -0.5)
    m = jnp.max(s, -1, keepdims=True)
    p = jnp.exp(s - m)
    p = p / jnp.sum(p, -1, keepdims=True)
    o_ref[...] = jnp.dot(p.astype(bf16), v_ref[...], preferred_element_type=f32).astype(o_ref.dtype)


def _xattn(q, kv, B, S, M):
    T = B * S
    tq = min(S, 1024)
    nq = S // tq
    return pl.pallas_call(
        _xattn_kernel,
        out_shape=jax.ShapeDtypeStruct((T, D_MODEL), bf16),
        grid=(B, nq, H_X),
        in_specs=[pl.BlockSpec((tq, HD_X), lambda b, i, h: (b * nq + i, h)),
                  pl.BlockSpec((M, HD_X), lambda b, i, h: (b, h)),
                  pl.BlockSpec((M, HD_X), lambda b, i, h: (b, H_X + h))],
        out_specs=pl.BlockSpec((tq, HD_X), lambda b, i, h: (b * nq + i, h)),
        compiler_params=_params("parallel", "parallel", "parallel"),
    )(q, kv, kv)


def _route_weights(w_group, b_group, w_expert, b_expert, layer):
    D = w_group.shape[1]
    pad = LANES - N_EXP - NG_E
    w = jnp.concatenate([w_expert[layer], w_group[layer], jnp.zeros((D, pad), f32)], 1)
    b = jnp.concatenate([b_expert[layer], b_group[layer], jnp.zeros((pad,), f32)])[None, :]
    return w, b


MOE_TM = 256
LN_TM = 256


def _moe_plan(route, T):
    n_slots = TOPK_IN * T
    n_tiles = n_slots // MOE_TM + N_EXP
    ids = route[:, 2:2 + TOPK_IN].astype(jnp.int32)
    e_flat = ids.T.reshape(n_slots)
    onehot = (e_flat[:, None] == jnp.arange(N_EXP, dtype=jnp.int32)[None, :]).astype(jnp.int32)
    csum = jnp.cumsum(onehot, axis=0)
    rank = jnp.sum(onehot * csum, axis=1) - 1
    counts = csum[-1]
    tiles_per = (counts + MOE_TM - 1) // MOE_TM
    tile_end = jnp.cumsum(tiles_per)
    tile_start = tile_end - tiles_per
    dest = jnp.sum(onehot * tile_start[None, :], axis=1) * MOE_TM + rank
    tok = jnp.arange(n_slots, dtype=jnp.int32) % T
    src = jnp.zeros((n_tiles * MOE_TM,), jnp.int32).at[dest].set(tok, unique_indices=True)
    j = jnp.arange(n_tiles, dtype=jnp.int32)
    tile_e = jnp.sum((j[:, None] >= tile_end[None, :]).astype(jnp.int32), axis=1)
    last_e = jnp.max(jnp.where(counts > 0, jnp.arange(N_EXP, dtype=jnp.int32), 0))
    tile_e = jnp.minimum(tile_e, last_e)
    n_live = tile_end[-1:].astype(jnp.int32)
    return tile_e, n_live, src, dest.astype(jnp.int32), n_tiles


GATHER_UNROLL = 8


def _moe_group_kernel(tile_e_ref, nt_ref, src_ref, xp_ref, wg_ref, wu_ref, wd_ref, o_ref, xbuf, wgb, wub, wdb):
    i = pl.program_id(0)
    nt = nt_ref[0]
    tm = xbuf.shape[0]

    @pl.when((i < nt) & ((i == 0) | (tile_e_ref[i] != tile_e_ref[jnp.maximum(i - 1, 0)])))
    def _():
        wgb[...] = wg_ref[...].astype(bf16)
        wub[...] = wu_ref[...].astype(bf16)
        wdb[...] = wd_ref[...].astype(bf16)

    @pl.when(i < nt)
    def _():
        def body(j, c):
            rows = [xp_ref[pl.ds(src_ref[i * tm + j * GATHER_UNROLL + q], 1), :] for q in range(GATHER_UNROLL)]
            xbuf[pl.ds(pl.multiple_of(j * GATHER_UNROLL, GATHER_UNROLL), GATHER_UNROLL), :] = jnp.concatenate(rows, axis=0)
            return c
        lax.fori_loop(0, tm // GATHER_UNROLL, body, 0)
        lo, hi = _unpack_halves(xbuf[...])
        x = jnp.concatenate([lo.astype(bf16), hi.astype(bf16)], axis=1)
        h = _silu(jnp.dot(x, wgb[...], preferred_element_type=f32)) * \
            jnp.dot(x, wub[...], preferred_element_type=f32)
        o_ref[...] = _pack_halves(jnp.dot(h.astype(bf16), wdb[...], preferred_element_type=f32))

    @pl.when(i >= nt)
    def _():
        o_ref[...] = jnp.zeros_like(o_ref)


def _moe_group(xp, tile_e, n_live, src, n_tiles, w_gate, w_up, w_down, layer):
    T, half = xp.shape
    D = 2 * half
    wspec = lambda shp: pl.BlockSpec((None, None) + shp, lambda i, te, nt, sr: (layer, te[i], 0, 0))
    return pl.pallas_call(
        _moe_group_kernel,
        out_shape=jax.ShapeDtypeStruct((n_tiles * MOE_TM, half), jnp.uint32),
        grid_spec=pltpu.PrefetchScalarGridSpec(
            num_scalar_prefetch=3,
            grid=(n_tiles,),
            in_specs=[pl.BlockSpec(memory_space=pltpu.VMEM), wspec((D, D_E)), wspec((D, D_E)),
                      wspec((D_E, D))],
            out_specs=pl.BlockSpec((MOE_TM, half), lambda i, te, nt, sr: (i, 0)),
            scratch_shapes=[pltpu.VMEM((MOE_TM, half), jnp.uint32), pltpu.VMEM((D, D_E), bf16),
                            pltpu.VMEM((D, D_E), bf16), pltpu.VMEM((D_E, D), bf16)]),
        compiler_params=_params("arbitrary"),
    )(tile_e, n_live, src, xp, w_gate, w_up, w_down)


def _moe_combine_ln_kernel(pos_ref, x_ref, rt_ref, g_ref, b_ref, ys_hbm, o_ref, ob_ref, ybuf, sems):
    i = pl.program_id(0)
    n = pl.num_programs(0)
    tm = x_ref.shape[0]
    T = n * tm

    def issue(tile, slot):
        for k in range(TOPK_IN):
            def body(j, c, k=k):
                for q in range(GATHER_UNROLL):
                    r = j * GATHER_UNROLL + q
                    p = pos_ref[k * T + tile * tm + r]
                    pltpu.make_async_copy(ys_hbm.at[pl.ds(p, 1), :], ybuf.at[slot, k, pl.ds(r, 1), :],
                                          sems.at[slot]).start(priority=q % 2)
                return c
            lax.fori_loop(0, tm // GATHER_UNROLL, body, 0)

    @pl.when(i == 0)
    def _():
        issue(0, 0)

    @pl.when(i + 1 < n)
    def _():
        issue(i + 1, (i + 1) % 2)

    slot = i % 2
    for k in range(TOPK_IN):
        pltpu.make_async_copy(ys_hbm.at[pl.ds(0, tm), :], ybuf.at[slot, k], sems.at[slot]).wait()
    rt = rt_ref[...]
    w1, w2 = rt[:, 0:1], rt[:, 1:2]
    lo1, hi1 = _unpack_halves(ybuf[slot, 0])
    lo2, hi2 = _unpack_halves(ybuf[slot, 1])
    y = jnp.concatenate([w1 * lo1 + w2 * lo2, w1 * hi1 + w2 * hi2], axis=1)
    o = _layer_norm(ALPHA * x_ref[...] + y, g_ref[...], b_ref[...])
    o_ref[...] = o
    ob_ref[...] = o.astype(bf16)


def _moe_combine_ln(x, route, pos, ys, g, b, layer):
    T, D = x.shape
    tm = min(T, LN_TM)
    vec = pl.BlockSpec((None, 1, D), lambda i, p: (layer, 0, 0))
    row = pl.BlockSpec((tm, D), lambda i, p: (i, 0))
    return pl.pallas_call(
        _moe_combine_ln_kernel,
        out_shape=(jax.ShapeDtypeStruct((T, D), f32), jax.ShapeDtypeStruct((T, D), bf16)),
        grid_spec=pltpu.PrefetchScalarGridSpec(
            num_scalar_prefetch=1,
            grid=(T // tm,),
            in_specs=[row, pl.BlockSpec((tm, LANES), lambda i, p: (i, 0)), vec, vec,
                      pl.BlockSpec(memory_space=pl.ANY)],
            out_specs=(row, row),
            scratch_shapes=[pltpu.VMEM((2, TOPK_IN, tm, D // 2), jnp.uint32),
                            pltpu.SemaphoreType.DMA((2,))]),
        compiler_params=_params("arbitrary"),
    )(pos, x, route, g.reshape(g.shape[0], 1, D), b.reshape(b.shape[0], 1, D), ys)


def _moe_ln(xf, xp, route, w_gate, w_up, w_down, g, b, layer):
    tile_e, n_live, src, pos, n_tiles = _moe_plan(route, xf.shape[0])
    ys = _moe_group(xp, tile_e, n_live, src, n_tiles, w_gate, w_up, w_down, layer)
    return _moe_combine_ln(xf, route, pos, ys, g, b, layer)


def kernel(x, mem, ab_w_in, rg_conv_w, rg_conv_b, rg_wa, rg_ba, rg_wx, rg_bx, rg_lam, ssd_conv_w, ssd_conv_b, ssd_dt_bias, ssd_a_log, ssd_d, ssd_norm_w, ab_w_out, cd_w_in, s5_a_re, s5_a_im, s5_log_step, s5_b_re, s5_b_im, s5_c_re, s5_c_im, s5_d, s5_glu_w, s5_glu_b, dn_conv_w, dn_a_log, dn_dt_bias, dn_norm_w, cd_w_out, xa_w_q, xa_w_kv, xa_w_o, moe_w_group, moe_b_group, moe_w_expert, moe_b_expert, moe_w_gate, moe_w_up, moe_w_down, ln1_g, ln1_b, ln2_g, ln2_b, ln3_g, ln3_b):
    B, S, D = x.shape
    M = mem.shape[1]
    T = B * S
    xf = x.reshape(T, D)
    xb = xf.astype(bf16)
    memb = mem.reshape(B * M, D).astype(bf16)
    for l in range(DEPTH):
        i = l // 2
        if l % 2 == 0:
            w_in_t = jnp.swapaxes(ab_w_in, 1, 2)
            proj = _mm(xb, w_in_t, i, MAIN_AB, f32, transposed=True)
            dt_tail = _mm_hi(xf, _tail_cols(w_in_t, i, MAIN_AB))
            ya = _rglru(proj, B, S, i, rg_conv_w, rg_conv_b, rg_wa, rg_ba, rg_wx, rg_bx, rg_lam)
            yb = _ssd(proj, dt_tail, B, S, i, ssd_conv_w, ssd_conv_b, ssd_dt_bias, ssd_a_log, ssd_d,
                      ssd_norm_w)
            xf, xb = _mm_ln([ya, yb], ab_w_out[i].astype(bf16), xf, ln1_g, ln1_b, l)
        else:
            w_in_t = jnp.swapaxes(cd_w_in, 1, 2)
            proj = _mm(xb, w_in_t, i, MAIN_CD, f32, transposed=True)
            ab_tail = _mm_hi(xf, _tail_cols(w_in_t, i, MAIN_CD))
            yc = _s5(proj, B, S, i, s5_a_re, s5_a_im, s5_log_step, s5_b_re, s5_b_im, s5_c_re, s5_c_im,
                     s5_d, s5_glu_w, s5_glu_b)
            yd = _gdn(proj, ab_tail, B, S, i, dn_conv_w, dn_a_log, dn_dt_bias, dn_norm_w)
            xf, xb = _mm_ln([yc, yd], cd_w_out[i].astype(bf16), xf, ln1_g, ln1_b, l)
        q = _mm(xb, xa_w_q, l, D, bf16)
        kv = _mm(memb, xa_w_kv, l, 2 * D, bf16)
        att = _xattn(q, kv, B, S, M)
        xf, xp, route = _mm_ln([att], xa_w_o[l].astype(bf16), xf, ln2_g, ln2_b, l,
                               route_wb=_route_weights(moe_w_group, moe_b_group, moe_w_expert, moe_b_expert, l))
        xf, xb = _moe_ln(xf, xp, route, moe_w_gate, moe_w_up, moe_w_down, ln3_g, ln3_b, l)
    return xf.reshape(B, S, D)
```

```python
import functools
import math

import jax
import jax.numpy as jnp
from jax import lax
from jax.experimental import pallas as pl
from jax.experimental.pallas import tpu as pltpu

f32 = jnp.float32
bf16 = jnp.bfloat16

D_MODEL = 2048
DEPTH = 2
CHUNK = 64
CONV_W = 4
ALPHA = (2 * DEPTH) ** 0.25
LN_EPS = 1e-5
RMS_EPS = 1e-6
W_A = D_MODEL // 2
H_A = 8
BW_A = W_A // H_A
RG_C = 8.0
W_B = D_MODEL
HD_B = 64
H_B = W_B // HD_B
NG_B = 2
N_B = 128
HG_B = H_B // NG_B
CONV_B = W_B + 2 * NG_B * N_B
MAIN_AB = 2 * W_A + W_B + CONV_B
W_C = D_MODEL // 2
GS_C = 16
G_C = W_C // GS_C
P_C = 64
L_C = 16
H_D = 8
DK_D = D_MODEL // 16
DV_D = D_MODEL // 16
W_D = H_D * DV_D
QKV_D = 2 * H_D * DK_D + W_D
MAIN_CD = W_C + QKV_D + W_D
H_X = 4
HD_X = D_MODEL // H_X
NG_E = 4
E_PER = 8
N_EXP = NG_E * E_PER
TOPK_IN = 2
D_E = D_MODEL // 8

LANES = 128
SUBLANES = 8
S5_GB = LANES // GS_C
S5_NB = G_C // S5_GB
VMEM_LIMIT = 56 * 1024 * 1024


def _params(*sem):
    return pltpu.CompilerParams(dimension_semantics=sem, vmem_limit_bytes=VMEM_LIMIT)


def _sigmoid(x):
    return 1.0 / (1.0 + jnp.exp(-x))


def _silu(x):
    return x * _sigmoid(x)


def _softplus(x):
    return jnp.maximum(x, 0.0) + jnp.log(1.0 + jnp.exp(-jnp.abs(x)))


def _gelu_tanh(x):
    return 0.5 * x * (1.0 + jnp.tanh(math.sqrt(2.0 / math.pi) * (x + 0.044715 * (x * x * x))))


def _dot(a, b):
    return jnp.dot(a.astype(bf16), b.astype(bf16), preferred_element_type=f32)


def _dot_nt(a, b):
    return lax.dot_general(a.astype(bf16), b.astype(bf16), (((1,), (1,)), ((), ())),
                           preferred_element_type=f32)


def _dot_tn(a, b):
    return lax.dot_general(a.astype(bf16), b.astype(bf16), (((0,), (0,)), ((), ())),
                           preferred_element_type=f32)


def _split3(a):
    hi = a.astype(bf16)
    r = a - hi.astype(f32)
    mid = r.astype(bf16)
    lo = (r - mid.astype(f32)).astype(bf16)
    return hi, mid, lo


def _dot_exact_lhs(sel, b):
    s = sel.astype(bf16)
    b1, b2, b3 = _split3(b)
    d = functools.partial(jnp.dot, preferred_element_type=f32)
    return d(s, b1) + d(s, b2) + d(s, b3)


def _dot_exact_rhs(a, sel):
    s = sel.astype(bf16)
    a1, a2, a3 = _split3(a)
    d = functools.partial(jnp.dot, preferred_element_type=f32)
    return d(a1, s) + d(a2, s) + d(a3, s)


def _transpose_exact(a, eye):
    a1, a2, a3 = _split3(a)
    e = eye.astype(bf16)
    d = lambda x: lax.dot_general(e, x, (((1,), (1,)), ((), ())), preferred_element_type=f32)
    return d(a1) + d(a2) + d(a3)


def _dot3(a, b):
    a1 = a.astype(bf16)
    a2 = (a - a1.astype(f32)).astype(bf16)
    b1 = b.astype(bf16)
    b2 = (b - b1.astype(f32)).astype(bf16)
    d = functools.partial(jnp.dot, preferred_element_type=f32)
    return d(a1, b1) + (d(a1, b2) + d(a2, b1))


def _iota(shape, axis):
    return lax.broadcasted_iota(jnp.int32, shape, axis)


def _tri(n, strict=False):
    r, c = _iota((n, n), 0), _iota((n, n), 1)
    return (r > c) if strict else (r >= c)


def _mm_kernel(x_ref, w_ref, o_ref):
    o_ref[...] = jnp.dot(x_ref[...], w_ref[...].astype(bf16),
                         preferred_element_type=f32).astype(o_ref.dtype)


def _mm(x, w, layer, n_cols, out_dtype, tn=512):
    M, K = x.shape
    tm = min(M, 2048)
    return pl.pallas_call(
        _mm_kernel,
        out_shape=jax.ShapeDtypeStruct((M, n_cols), out_dtype),
        grid=(M // tm, n_cols // tn),
        in_specs=[pl.BlockSpec((tm, K), lambda i, j: (i, 0)),
                  pl.BlockSpec((None, K, tn), lambda i, j: (layer, 0, j))],
        out_specs=pl.BlockSpec((tm, tn), lambda i, j: (i, j)),
        compiler_params=_params("parallel", "parallel"),
    )(x, w)


def _mm_resident_kernel(x_ref, w_ref, o_ref):
    o_ref[...] = jnp.dot(x_ref[...], w_ref[...], preferred_element_type=f32).astype(o_ref.dtype)


def _mm_resident(x, w, out_dtype):
    M, K = x.shape
    tm = min(M, 1024)
    return pl.pallas_call(
        _mm_resident_kernel,
        out_shape=jax.ShapeDtypeStruct((M, w.shape[1]), out_dtype),
        grid=(M // tm,),
        in_specs=[pl.BlockSpec((tm, K), lambda i: (i, 0)), pl.BlockSpec(memory_space=pltpu.VMEM)],
        out_specs=pl.BlockSpec((tm, w.shape[1]), lambda i: (i, 0)),
        compiler_params=_params("parallel"),
    )(x, w)


def _tail_cols(w, layer, start):
    _, K, N = w.shape
    return lax.slice(w, (layer, 0, start), (layer + 1, K, N)).reshape(K, N - start)


def _mm_hi_kernel(x_ref, w_ref, o_ref):
    o_ref[...] = _dot3(x_ref[...], w_ref[...])


def _mm_hi(x, w):
    M, K = x.shape
    n = w.shape[1]
    tm = min(M, 512)
    return pl.pallas_call(
        _mm_hi_kernel,
        out_shape=jax.ShapeDtypeStruct((M, n), f32),
        grid=(M // tm,),
        in_specs=[pl.BlockSpec((tm, K), lambda i: (i, 0)),
                  pl.BlockSpec((K, n), lambda i: (0, 0))],
        out_specs=pl.BlockSpec((tm, n), lambda i: (i, 0)),
        compiler_params=_params("parallel"),
    )(x, w)


def _layer_norm(v, g, b):
    mu = jnp.mean(v, -1, keepdims=True)
    d = v - mu
    var = jnp.mean(d * d, -1, keepdims=True)
    return d * lax.rsqrt(var + LN_EPS) * g + b


HI_MASK = 0xFFFF0000


def _pack_halves(v):
    h = v.shape[1] // 2
    lo = lax.bitcast_convert_type(v[:, :h].astype(bf16).astype(f32), jnp.uint32)
    hi = lax.bitcast_convert_type(v[:, h:].astype(bf16).astype(f32), jnp.uint32)
    return (hi & jnp.uint32(HI_MASK)) | (lo >> 16)


def _unpack_halves(u):
    lo = lax.bitcast_convert_type(u << 16, f32)
    hi = lax.bitcast_convert_type(u & jnp.uint32(HI_MASK), f32)
    return lo, hi


def _route_top2(logits):
    lane = _iota(logits.shape, 1)
    neg = -jnp.inf
    big = jnp.int32(LANES)
    is_g = (lane >= N_EXP) & (lane < N_EXP + NG_E)
    gl = jnp.where(is_g, logits, neg)
    gmax = jnp.max(gl, -1, keepdims=True)
    g_lane = jnp.min(jnp.where(gl == gmax, lane, big), -1, keepdims=True)
    g_prob = 1.0 / jnp.sum(jnp.where(is_g, jnp.exp(gl - gmax), 0.0), -1, keepdims=True)
    e0 = (g_lane - N_EXP) * E_PER
    sel = jnp.where((lane >= e0) & (lane < e0 + E_PER), logits, neg)
    m1 = jnp.max(sel, -1, keepdims=True)
    i1 = jnp.min(jnp.where(sel == m1, lane, big), -1, keepdims=True)
    sel2 = jnp.where(lane == i1, neg, sel)
    m2 = jnp.max(sel2, -1, keepdims=True)
    i2 = jnp.min(jnp.where(sel2 == m2, lane, big), -1, keepdims=True)
    e2 = jnp.exp(m2 - m1)
    w1 = g_prob / (1.0 + e2)
    w2 = g_prob * e2 / (1.0 + e2)
    return (jnp.where(lane == 0, w1, 0.0) + jnp.where(lane == 1, w2, 0.0)
            + jnp.where(lane == 2, i1.astype(f32), 0.0) + jnp.where(lane == 3, i2.astype(f32), 0.0))


def _mm_ln_kernel(*refs, n_parts, routed):
    part_refs = refs[:n_parts]
    w_ref, x_ref, g_ref, b_ref = refs[n_parts:n_parts + 4]
    acc, off = None, 0
    for p in part_refs:
        c = jnp.dot(p[...], w_ref[off:off + p.shape[1], :], preferred_element_type=f32)
        acc = c if acc is None else acc + c
        off += p.shape[1]
    o = _layer_norm(ALPHA * x_ref[...] + acc, g_ref[...], b_ref[...])
    if routed:
        rw_ref, rb_ref, o_ref, o2_ref, rt_ref = refs[n_parts + 4:]
        o2_ref[...] = _pack_halves(o)
        rt_ref[...] = _route_top2(_dot3(o, rw_ref[...]) + rb_ref[...])
    else:
        o_ref, o2_ref = refs[n_parts + 4:]
        o2_ref[...] = o.astype(bf16)
    o_ref[...] = o


def _mm_ln(parts, w, x, g, b, layer, route_wb=None):
    T, D = x.shape
    tm = min(T, 512)
    routed = route_wb is not None
    vec = pl.BlockSpec((None, 1, D), lambda i: (layer, 0, 0))
    row = pl.BlockSpec((tm, D), lambda i: (i, 0))
    in_specs = [pl.BlockSpec((tm, a.shape[1]), lambda i: (i, 0)) for a in parts]
    in_specs += [pl.BlockSpec(memory_space=pltpu.VMEM), row, vec, vec]
    args = [*parts, w, x, g.reshape(g.shape[0], 1, D), b.reshape(b.shape[0], 1, D)]
    if routed:
        in_specs += [pl.BlockSpec((D, LANES), lambda i: (0, 0)), pl.BlockSpec((1, LANES), lambda i: (0, 0))]
        args += list(route_wb)
        out_shape = (jax.ShapeDtypeStruct((T, D), f32), jax.ShapeDtypeStruct((T, D // 2), jnp.uint32),
                     jax.ShapeDtypeStruct((T, LANES), f32))
        out_specs = (row, pl.BlockSpec((tm, D // 2), lambda i: (i, 0)), pl.BlockSpec((tm, LANES), lambda i: (i, 0)))
    else:
        out_shape = (jax.ShapeDtypeStruct((T, D), f32), jax.ShapeDtypeStruct((T, D), bf16))
        out_specs = (row, row)
    return pl.pallas_call(
        functools.partial(_mm_ln_kernel, n_parts=len(parts), routed=routed),
        out_shape=out_shape,
        grid=(T // tm,),
        in_specs=in_specs,
        out_specs=out_specs,
        compiler_params=_params("parallel"),
    )(*args)


def _conv_rows(x_ref, w, hist_ref):
    tt = x_ref.shape[0]
    hist_ref[SUBLANES:SUBLANES + tt, :] = x_ref[...]
    acc = _causal_taps(hist_ref[...], w, tt)
    hist_ref[0:SUBLANES, :] = hist_ref[tt:tt + SUBLANES, :]
    return acc


def _causal_taps(ext, w, tt):
    acc = None
    for k in range(CONV_W):
        back = CONV_W - 1 - k
        rows = (pltpu.roll(ext, back, 0) if back else ext)[SUBLANES:SUBLANES + tt, :]
        term = w[k:k + 1, :] * rows
        acc = term if acc is None else acc + term
    return acc


def _scan_affine(a, u):
    n = a.shape[0]
    row = _iota(a.shape, 0)
    d = 1
    while d < n:
        keep = row >= d
        a_s = pltpu.roll(a, d, 0)
        u_s = pltpu.roll(u, d, 0)
        u = u + jnp.where(keep, a * u_s, 0.0)
        a = jnp.where(keep, a * a_s, a)
        d *= 2
    return a, u


def _rglru_kernel(gate_ref, xa_ref, cw_ref, cb_ref, wa_ref, ba_ref, wx_ref, bx_ref, lam_ref,
                  o_ref, hist_ref, h_ref):
    @pl.when(pl.program_id(1) == 0)
    def _():
        h_ref[...] = jnp.zeros_like(h_ref)
        hist_ref[:, 0:SUBLANES, :] = jnp.zeros((hist_ref.shape[0], SUBLANES, hist_ref.shape[2]), f32)

    decay_rate = -RG_C * _softplus(-lam_ref[...])
    for b in range(xa_ref.shape[0]):
        xc = _conv_rows(xa_ref.at[b], cw_ref[...], hist_ref.at[b]) + cb_ref[...]
        r = _sigmoid(_dot(xc, wa_ref[...]) + ba_ref[...])
        i = _sigmoid(_dot(xc, wx_ref[...]) + bx_ref[...])
        log_a = r * decay_rate
        a = jnp.exp(log_a)
        u = jnp.sqrt(1.0 - jnp.exp(2.0 * log_a)) * (i * xc)
        a_cum, h = _scan_affine(a, u)
        h = h + a_cum * h_ref[b, 0:1, :]
        tt = h.shape[0]
        h_ref[b] = jnp.broadcast_to(h[tt - 1:tt, :], h_ref.shape[1:])
        o_ref[b] = (_gelu_tanh(gate_ref[b]) * h).astype(o_ref.dtype)


def _rglru(proj, B, S, layer, conv_w, conv_b, wa, ba, wx, bx, lam):
    tt = min(S, 256)
    ns = S // tt
    xa_col0 = W_A // BW_A
    vec = lambda a: a.reshape(a.shape[0], 1, W_A)
    vspec = pl.BlockSpec((None, 1, BW_A), lambda h, s: (layer, 0, h))
    wspec = pl.BlockSpec((None, None, BW_A, BW_A), lambda h, s: (layer, h, 0, 0))
    proj3 = proj.reshape(B, S, proj.shape[1])
    out = pl.pallas_call(
        _rglru_kernel,
        out_shape=jax.ShapeDtypeStruct((B, S, W_A), bf16),
        grid=(H_A, ns),
        in_specs=[pl.BlockSpec((B, tt, BW_A), lambda h, s: (0, s, h)),
                  pl.BlockSpec((B, tt, BW_A), lambda h, s: (0, s, xa_col0 + h)),
                  pl.BlockSpec((None, CONV_W, BW_A), lambda h, s: (layer, 0, h)),
                  vspec, wspec, vspec, wspec, vspec, vspec],
        out_specs=pl.BlockSpec((B, tt, BW_A), lambda h, s: (0, s, h)),
        scratch_shapes=[pltpu.VMEM((B, SUBLANES + tt, BW_A), f32), pltpu.VMEM((B, SUBLANES, BW_A), f32)],
        compiler_params=_params("parallel", "arbitrary"),
    )(proj3, proj3, conv_w, vec(conv_b), wa, vec(ba), wx, vec(bx), vec(lam))
    return out.reshape(B * S, W_A)


def _ssd_group(X, Bc, Cc, z, dt, a_neg, d_head, nw, st_ref):
    L = CHUNK
    adt = dt * a_neg
    tri = _tri(L)
    cs = _dot_exact_lhs(tri.astype(f32), adt)
    eye_h = (_iota((HG_B, HG_B), 0) == _iota((HG_B, HG_B), 1)).astype(f32)
    cs_t = _transpose_exact(cs, eye_h)
    cs_last = cs[L - 1:L, :]

    expand = (_iota((HG_B, HG_B * HD_B), 1) // HD_B == _iota((HG_B, HG_B * HD_B), 0)).astype(f32)
    dt_x = _dot_exact_rhs(dt, expand)
    ecs_x = _dot_exact_rhs(jnp.exp(cs), expand)
    dec_x = _dot_exact_rhs(jnp.exp(cs_last - cs), expand)
    cdec_x = _dot_exact_rhs(jnp.broadcast_to(jnp.exp(cs_last), (SUBLANES, HG_B)), expand)[0:1]
    d_x = _dot_exact_rhs(jnp.broadcast_to(d_head, (SUBLANES, HG_B)), expand)[0:1]

    xdt = X * dt_x
    cb = _dot_nt(Cc, Bc)
    xdt_b = xdt.astype(bf16)
    left = _iota((L, 2 * HD_B), 1) < HD_B
    pieces = []
    for j in range(0, HG_B, 2):
        pair = xdt_b[:, j * HD_B:(j + 2) * HD_B]
        outs = []
        for jj in (j, j + 1):
            seg = jnp.where(tri, cs[:, jj:jj + 1] - cs_t[jj:jj + 1, :], -jnp.inf)
            outs.append(jnp.dot((cb * jnp.exp(seg)).astype(bf16), pair, preferred_element_type=f32))
        pieces.append(jnp.where(left, outs[0], outs[1]))
    y_diag = jnp.concatenate(pieces, axis=1)

    state = st_ref[...]
    y_off = ecs_x * _dot(Cc, state)
    st_ref[...] = state * cdec_x + _dot_tn(Bc, xdt * dec_x)

    y = y_diag + y_off + X * d_x
    yg = y * _silu(z)
    return yg * lax.rsqrt(jnp.mean(yg * yg, -1, keepdims=True) + RMS_EPS) * nw


def _ssd_kernel(z_ref, x_ref, b_ref, c_ref, dt_ref, cw_ref, cb_ref, dtb_ref, alog_ref, d_ref, nw_ref,
                o_ref, hx_ref, hb_ref, hc_ref, st_ref):
    GW = W_B // NG_B

    @pl.when(pl.program_id(0) == 0)
    def _():
        st_ref[...] = jnp.zeros_like(st_ref)
        for h in (hx_ref, hb_ref, hc_ref):
            h[:, 0:SUBLANES, :] = jnp.zeros((h.shape[0], SUBLANES, h.shape[2]), f32)

    cw, cbias = cw_ref[...], cb_ref[...]
    c_off = W_B + NG_B * N_B
    a_all = -jnp.exp(alog_ref[...])
    d_all = d_ref[...]
    nw_all = nw_ref[...]
    for b in range(x_ref.shape[0]):
        X = _silu(_conv_rows(x_ref.at[b], cw[:, :W_B], hx_ref.at[b]) + cbias[:, :W_B])
        Bm = _silu(_conv_rows(b_ref.at[b], cw[:, W_B:c_off], hb_ref.at[b]) + cbias[:, W_B:c_off])
        Cm = _silu(_conv_rows(c_ref.at[b], cw[:, c_off:], hc_ref.at[b]) + cbias[:, c_off:])
        dt = _softplus(dt_ref[b] + dtb_ref[...])
        for g in range(NG_B):
            hs = slice(g * HG_B, (g + 1) * HG_B)
            ws = slice(g * GW, (g + 1) * GW)
            ns = slice(g * N_B, (g + 1) * N_B)
            out = _ssd_group(X[:, ws], Bm[:, ns], Cm[:, ns], z_ref[b, :, ws], dt[:, hs], a_all[:, hs],
                             d_all[:, hs], nw_all[:, ws], st_ref.at[b * NG_B + g])
            o_ref[b, :, ws] = out.astype(o_ref.dtype)


def _ssd(proj, dt_tail, B, S, layer, conv_w, conv_b, dt_bias, a_log, d, norm_w):
    L = CHUNK
    nc = S // L
    BC = NG_B * N_B
    per_batch = lambda a: a.reshape(B, S, a.shape[1])
    blk = lambda n, cb: pl.BlockSpec((B, L, n), lambda c: (0, c, cb))
    vec = lambda a: a.reshape(a.shape[0], 1, a.shape[-1])
    par = lambda r, n: pl.BlockSpec((None, r, n), lambda c: (layer, 0, 0))
    proj3 = per_batch(proj)
    out = pl.pallas_call(
        _ssd_kernel,
        out_shape=jax.ShapeDtypeStruct((B, S, W_B), bf16),
        grid=(nc,),
        in_specs=[blk(W_B, 2 * W_A // W_B), blk(W_B, (2 * W_A + W_B) // W_B),
                  blk(BC, (2 * W_A + 2 * W_B) // BC), blk(BC, (2 * W_A + 2 * W_B) // BC + 1),
                  blk(H_B, 0), par(CONV_W, CONV_B), par(1, CONV_B), par(1, H_B), par(1, H_B), par(1, H_B),
                  par(1, W_B)],
        out_specs=blk(W_B, 0),
        scratch_shapes=[pltpu.VMEM((B, SUBLANES + L, W_B), f32), pltpu.VMEM((B, SUBLANES + L, BC), f32),
                        pltpu.VMEM((B, SUBLANES + L, BC), f32), pltpu.VMEM((B * NG_B, N_B, W_B // NG_B), f32)],
        compiler_params=_params("arbitrary"),
    )(proj3, proj3, proj3, proj3, per_batch(dt_tail), conv_w, vec(conv_b), vec(dt_bias), vec(a_log), vec(d),
      vec(norm_w))
    return out.reshape(B * S, W_B)


def _s5_tables(a_re, a_im, log_step, b_re, b_im, c_re, c_im):
    L = L_C
    ar, ai = a_re.astype(f32), a_im.astype(f32)
    step = jnp.exp(log_step.astype(f32))[:, None]
    mag = jnp.exp(ar * step)
    lb_re, lb_im = mag * jnp.cos(ai * step), mag * jnp.sin(ai * step)
    den = ar * ar + ai * ai
    f_re = ((lb_re - 1.0) * ar + lb_im * ai) / den
    f_im = (lb_im * ar - (lb_re - 1.0) * ai) / den
    br, bi = b_re.astype(f32), b_im.astype(f32)
    bb_re = f_re[..., None] * br - f_im[..., None] * bi
    bb_im = f_re[..., None] * bi + f_im[..., None] * br
    cr, ci = c_re.astype(f32), c_im.astype(f32)

    def power(n):
        n = n.astype(f32)[None, :, None]
        m = jnp.exp(ar[:, None, :] * step[:, None, :] * n)
        ang = ai[:, None, :] * step[:, None, :] * n
        return m * jnp.cos(ang), m * jnp.sin(ang)

    j = jnp.arange(L)
    pr, pi = power(j)
    lbr = pr[..., None] * bb_re[:, None] - pi[..., None] * bb_im[:, None]
    lbi = pr[..., None] * bb_im[:, None] + pi[..., None] * bb_re[:, None]
    kern = (jnp.einsum('gop,gjpk->gjko', cr, lbr) - jnp.einsum('gop,gjpk->gjko', ci, lbi))
    lagb = kern.reshape(S5_NB, S5_GB, L, GS_C, GS_C).transpose(0, 2, 1, 3, 4)
    lagb = lagb.reshape(S5_NB, L, LANES, GS_C)
    rev = (L - 1) - j
    bend = jnp.concatenate([jnp.take(lbr, rev, axis=1), jnp.take(lbi, rev, axis=1)], 2)
    bendc = bend.reshape(S5_NB, S5_GB, L, 2 * P_C, GS_C).transpose(0, 2, 1, 4, 3)
    bendc = bendc.reshape(S5_NB, L, LANES, 2 * P_C)
    qr, qi = power(j + 1)
    car_re = cr[:, None] * qr[:, :, None, :] - ci[:, None] * qi[:, :, None, :]
    car_im = -(cr[:, None] * qi[:, :, None, :] + ci[:, None] * qr[:, :, None, :])
    car = jnp.concatenate([car_re, car_im], -1)
    ccarc = car.reshape(S5_NB, S5_GB, L, GS_C, 2 * P_C).transpose(0, 2, 4, 1, 3)
    ccarc = ccarc.reshape(S5_NB, L, 2 * P_C, LANES)
    return lagb, bendc.astype(bf16), ccarc.astype(bf16), power


def _s5_kernel(u_ref, lag_ref, bend_ref, ccar_ref, sc_ref, o_ref, toep_t, bend_t, ccar_t):
    L = L_C
    SW = 2 * P_C

    @pl.when(pl.program_id(1) == 0)
    def _():
        zero = jnp.zeros((LANES, LANES), bf16)
        own_g = (_iota((LANES, LANES), 0) // GS_C) == (_iota((LANES, LANES), 1) // GS_C)
        lag = [jnp.where(own_g, jnp.concatenate([lag_ref[j]] * S5_GB, axis=1), 0.0).astype(bf16) for j in range(L)]
        for s in range(L):
            for t in range(L):
                toep_t[s * LANES:(s + 1) * LANES, t * LANES:(t + 1) * LANES] = lag[t - s] if t >= s else zero
        same_g = (_iota((LANES, S5_GB * SW), 0) // GS_C) == (_iota((LANES, S5_GB * SW), 1) // SW)
        for s in range(L):
            wide = jnp.concatenate([bend_ref[s]] * S5_GB, axis=1)
            bend_t[s * LANES:(s + 1) * LANES, :] = jnp.where(same_g, wide, jnp.zeros_like(wide))
        col_g = _iota((SW, LANES), 1) // GS_C
        for t in range(L):
            blk = ccar_ref[t]
            for g in range(S5_GB):
                ccar_t[g * SW:(g + 1) * SW, t * LANES:(t + 1) * LANES] = jnp.where(col_g == g, blk, jnp.zeros_like(blk))

    n = o_ref.shape[0] // L
    U = jnp.concatenate([u_ref[pl.ds(l, n, stride=L), :].astype(bf16) for l in range(L)], axis=1)
    Y = jnp.dot(U, toep_t[...], preferred_element_type=f32)
    H_all = jnp.dot(U, bend_t[...], preferred_element_type=f32)
    row = _iota((n, SW), 0)
    prev = []
    for g in range(S5_GB):
        sl = slice(g * SW, (g + 1) * SW)
        H = H_all[:, sl]
        d, k = 1, 0
        while d < n:
            hs = pltpu.roll(H, d, 0)
            sw = pltpu.roll(hs, P_C, 1)
            H = H + jnp.where(row >= d, sc_ref[2 * k:2 * k + 1, sl] * hs + sc_ref[2 * k + 1:2 * k + 2, sl] * sw, 0.0)
            d *= 2
            k += 1
        prev.append(jnp.where(row >= 1, pltpu.roll(H, 1, 0), 0.0).astype(bf16))
    Y = Y + jnp.dot(jnp.concatenate(prev, axis=1), ccar_t[...], preferred_element_type=f32)
    for l in range(L):
        o_ref[pl.ds(l, n, stride=L), :] = Y[:, l * LANES:(l + 1) * LANES]


def _s5_post_kernel(y_ref, u_ref, d_ref, w_ref, b_ref, o_ref):
    y = y_ref[...] + d_ref[...] * u_ref[...]
    g = _gelu_tanh(y)
    o_ref[...] = (g * _sigmoid(_dot(g, w_ref[...]) + b_ref[...])).astype(o_ref.dtype)


def _s5(proj, B, S, layer, a_re, a_im, log_step, b_re, b_im, c_re, c_im, d, glu_w, glu_b):
    T = B * S
    L = L_C
    nch = S // L
    lagb, bendc, ccarc, power = _s5_tables(a_re[layer], a_im[layer], log_step[layer], b_re[layer],
                                           b_im[layer], c_re[layer], c_im[layer])
    nsteps = max(1, (nch - 1).bit_length())
    sr, si = power(L * (2 ** jnp.arange(nsteps)))
    scan_c = jnp.stack([jnp.concatenate([sr, sr], -1), jnp.concatenate([-si, si], -1)], 2)
    scan8 = scan_c.reshape(S5_NB, S5_GB, 2 * nsteps, 2 * P_C).transpose(0, 2, 1, 3)
    scan8 = scan8.reshape(S5_NB, 2 * nsteps, S5_GB * 2 * P_C)
    tab = lambda r, c: pl.BlockSpec((None, L, r, c), lambda g, b: (g, 0, 0, 0))
    y = pl.pallas_call(
        _s5_kernel,
        out_shape=jax.ShapeDtypeStruct((T, W_C), f32),
        grid=(S5_NB, B),
        in_specs=[pl.BlockSpec((S, LANES), lambda g, b: (b, g)),
                  tab(LANES, GS_C), tab(LANES, 2 * P_C), tab(2 * P_C, LANES),
                  pl.BlockSpec((None, 2 * nsteps, S5_GB * 2 * P_C), lambda g, b: (g, 0, 0))],
        out_specs=pl.BlockSpec((S, LANES), lambda g, b: (b, g)),
        scratch_shapes=[pltpu.VMEM((L * LANES, L * LANES), bf16),
                        pltpu.VMEM((L * LANES, S5_GB * 2 * P_C), bf16),
                        pltpu.VMEM((S5_GB * 2 * P_C, L * LANES), bf16)],
        compiler_params=_params("arbitrary", "arbitrary"),
    )(proj, lagb, bendc, ccarc, scan8)
    tm = min(T, 1024)
    vec = pl.BlockSpec((None, 1, W_C), lambda i: (layer, 0, 0))
    return pl.pallas_call(
        _s5_post_kernel,
        out_shape=jax.ShapeDtypeStruct((T, W_C), bf16),
        grid=(T // tm,),
        in_specs=[pl.BlockSpec((tm, W_C), lambda i: (i, 0)),
                  pl.BlockSpec((tm, W_C), lambda i: (i, 0)),
                  vec,
                  pl.BlockSpec((None, W_C, W_C), lambda i: (layer, 0, 0)),
                  vec],
        out_specs=pl.BlockSpec((tm, W_C), lambda i: (i, 0)),
        compiler_params=_params("parallel"),
    )(y, proj, d.reshape(d.shape[0], 1, W_C), glu_w, glu_b.reshape(glu_b.shape[0], 1, W_C))


def _gdn_prep_kernel(q_ref, k_ref, v_ref, qp_ref, kp_ref, vp_ref, ab_ref, cwq_ref, cwk_ref, cwv_ref,
                     alog_ref, dtb_ref, w_ref, u_ref, qd_ref, kd_ref, qk_ref, ge_ref, hist_ref):
    L = CHUNK
    has_prev = (pl.program_id(1) > 0).astype(f32)

    def conv(cur_ref, prev_ref, w_ref_, slot):
        hist = hist_ref.at[slot]
        hist[0:SUBLANES, :] = prev_ref[...] * has_prev
        hist[SUBLANES:SUBLANES + L, :] = cur_ref[...]
        return _silu(_causal_taps(hist[...], w_ref_[...], L))

    q_all = conv(q_ref, qp_ref, cwq_ref, 0)
    k_all = conv(k_ref, kp_ref, cwk_ref, 1)
    v_all = conv(v_ref, vp_ref, cwv_ref, 2)
    ab = ab_ref[...]
    g_all = -jnp.exp(alog_ref[...]) * _softplus(ab[:, 0:H_D] + dtb_ref[...])
    beta_all = _sigmoid(ab[:, H_D:2 * H_D])
    tri = _tri(L)
    tri_s = _tri(L, strict=True)
    gcs_all = _dot_exact_lhs(tri.astype(f32), g_all)
    eye_h = (_iota((H_D, H_D), 0) == _iota((H_D, H_D), 1)).astype(f32)
    gcs_t = _transpose_exact(gcs_all, eye_h)
    eye = (_iota((L, L), 0) == _iota((L, L), 1)).astype(f32)
    ge_ref[...] = jnp.concatenate([jnp.exp(gcs_all), jnp.zeros((L, LANES - H_D), f32)], axis=1)

    heads = range(H_D)
    sls = [slice(h * DK_D, (h + 1) * DK_D) for h in heads]
    gcs = [gcs_all[:, h:h + 1] for h in heads]
    eg = [jnp.exp(g) for g in gcs]
    qs = [q_all[:, sl] * lax.rsqrt(jnp.sum(q_all[:, sl] * q_all[:, sl], -1, keepdims=True) + 1e-6)
          * (DK_D ** -0.5) for sl in sls]
    ks = [k_all[:, sl] * lax.rsqrt(jnp.sum(k_all[:, sl] * k_all[:, sl], -1, keepdims=True) + 1e-6)
          for sl in sls]
    kbs = [ks[h] * beta_all[:, h:h + 1] for h in heads]
    decay = [jnp.exp(jnp.where(tri, gcs[h] - gcs_t[h:h + 1, :], -jnp.inf)) for h in heads]
    kk = [_dot_nt(kbs[h], ks[h]) for h in heads]
    qk = [_dot_nt(qs[h], ks[h]) for h in heads]
    pw = [jnp.where(tri_s, -(kk[h] * decay[h]), 0.0) for h in heads]
    inv = [eye + p for p in pw]
    for _ in range(int(math.log2(L)) - 1):
        pw = [_dot(p, p) for p in pw]
        inv = [a + _dot(a, p) for a, p in zip(inv, pw)]
    rhs = [jnp.concatenate([v_all[:, sls[h]] * beta_all[:, h:h + 1], kbs[h] * eg[h]], axis=1) for h in heads]
    sol = [_dot(inv[h], rhs[h]) for h in heads]
    for h in heads:
        u_ref[:, sls[h]] = sol[h][:, :DV_D]
        w_ref[:, sls[h]] = sol[h][:, DV_D:]
        qk_ref[:, h * L:(h + 1) * L] = jnp.where(tri, qk[h] * decay[h], 0.0)
        qd_ref[:, sls[h]] = qs[h] * eg[h]
        kd_ref[:, sls[h]] = ks[h] * jnp.exp(gcs[h][L - 1:L, :] - gcs[h])


def _gdn_scan_kernel(w_ref, u_ref, qd_ref, kd_ref, qk_ref, ge_ref, z_ref, nw_ref, o_ref, st_ref):
    L = CHUNK

    @pl.when(pl.program_id(0) == 0)
    def _():
        st_ref[...] = jnp.zeros_like(st_ref)

    nw = nw_ref[...]
    pairs = [(b, h) for b in range(w_ref.shape[0]) for h in range(H_D)]
    cols = lambda h: slice(h * DK_D, (h + 1) * DK_D)
    state = [st_ref[b * H_D + h] for b, h in pairs]
    v_new = [u_ref[b, :, cols(h)] - _dot(w_ref[b, :, cols(h)], s) for (b, h), s in zip(pairs, state)]
    o = [_dot(qd_ref[b, :, cols(h)], s) + _dot(qk_ref[b, :, h * L:(h + 1) * L], v)
         for (b, h), s, v in zip(pairs, state, v_new)]
    for (b, h), s, v in zip(pairs, state, v_new):
        st_ref[b * H_D + h] = s * ge_ref[b, L - 1:L, h:h + 1] + _dot_tn(kd_ref[b, :, cols(h)], v)
    for (b, h), y in zip(pairs, o):
        y = y * lax.rsqrt(jnp.mean(y * y, -1, keepdims=True) + RMS_EPS) * nw
        o_ref[b, :, cols(h)] = (y * _silu(z_ref[b, :, cols(h)])).astype(o_ref.dtype)


def _gdn(proj, ab_tail, B, S, layer, conv_w, a_log, dt_bias, norm_w):
    T = B * S
    L = CHUNK
    nc = S // L
    per8 = L // SUBLANES
    q_c0 = W_C // W_D
    row = lambda b, c: (b * nc + c, 0)

    def cur(cb):
        return pl.BlockSpec((L, W_D), lambda b, c: (b * nc + c, cb))

    def prev(cb):
        return pl.BlockSpec((SUBLANES, W_D), lambda b, c: (jnp.maximum((b * nc + c) * per8 - 1, 0), cb))

    def cw(cb):
        return pl.BlockSpec((None, CONV_W, W_D), lambda b, c: (layer, 0, cb))

    v8 = pl.BlockSpec((None, 1, H_D), lambda b, c: (layer, 0, 0))
    wide = pl.BlockSpec((L, W_D), row)
    shp = lambda n: jax.ShapeDtypeStruct((T, n), f32)
    w_c, u_c, q_dec, k_dec, qk, gexp = pl.pallas_call(
        _gdn_prep_kernel,
        out_shape=(shp(W_D), shp(W_D), shp(W_D), shp(W_D), shp(H_D * L), shp(LANES)),
        grid=(B, nc),
        in_specs=[cur(q_c0), cur(q_c0 + 1), cur(q_c0 + 2), prev(q_c0), prev(q_c0 + 1), prev(q_c0 + 2),
                  pl.BlockSpec((L, 2 * H_D), row), cw(0), cw(1), cw(2), v8, v8],
        out_specs=(wide, wide, wide, wide, pl.BlockSpec((L, H_D * L), row),
                   pl.BlockSpec((L, LANES), row)),
        scratch_shapes=[pltpu.VMEM((3, SUBLANES + L, W_D), f32)],
        compiler_params=_params("parallel", "parallel"),
    )(proj, proj, proj, proj, proj, proj, ab_tail, conv_w, conv_w, conv_w,
      a_log.reshape(a_log.shape[0], 1, H_D), dt_bias.reshape(dt_bias.shape[0], 1, H_D))
    z_c0 = (W_C + QKV_D) // W_D
    per_batch = lambda a: a.reshape(B, S, a.shape[1])
    blk = lambda n, cb=0: pl.BlockSpec((B, L, n), lambda c: (0, c, cb))
    out = pl.pallas_call(
        _gdn_scan_kernel,
        out_shape=jax.ShapeDtypeStruct((B, S, W_D), bf16),
        grid=(nc,),
        in_specs=[blk(W_D), blk(W_D), blk(W_D), blk(W_D), blk(H_D * L), blk(LANES), blk(W_D, z_c0),
                  pl.BlockSpec((None, 1, DV_D), lambda c: (layer, 0, 0))],
        out_specs=blk(W_D),
        scratch_shapes=[pltpu.VMEM((B * H_D, DK_D, DV_D), f32)],
        compiler_params=_params("arbitrary"),
    )(per_batch(w_c), per_batch(u_c), per_batch(q_dec), per_batch(k_dec), per_batch(qk), per_batch(gexp),
      per_batch(proj), norm_w.reshape(norm_w.shape[0], 1, DV_D))
    return out.reshape(T, W_D)


def _xattn_kernel(q_ref, k_ref, v_ref, o_ref):
    s = _dot_nt(q_ref[...], k_ref[...]) * (HD_X ** -0.5)
    m = jnp.max(s, -1, keepdims=True)
    p = jnp.exp(s - m)
    p = p / jnp.sum(p, -1, keepdims=True)
    o_ref[...] = jnp.dot(p.astype(bf16), v_ref[...], preferred_element_type=f32).astype(o_ref.dtype)


def _xattn(q, kv, B, S, M):
    T = B * S
    tq = min(S, 1024)
    nq = S // tq
    return pl.pallas_call(
        _xattn_kernel,
        out_shape=jax.ShapeDtypeStruct((T, D_MODEL), bf16),
        grid=(B, nq, H_X),
        in_specs=[pl.BlockSpec((tq, HD_X), lambda b, i, h: (b * nq + i, h)),
                  pl.BlockSpec((M, HD_X), lambda b, i, h: (b, h)),
                  pl.BlockSpec((M, HD_X), lambda b, i, h: (b, H_X + h))],
        out_specs=pl.BlockSpec((tq, HD_X), lambda b, i, h: (b * nq + i, h)),
        compiler_params=_params("parallel", "parallel", "parallel"),
    )(q, kv, kv)


def _route_weights(w_group, b_group, w_expert, b_expert, layer):
    D = w_group.shape[1]
    pad = LANES - N_EXP - NG_E
    w = jnp.concatenate([w_expert[layer], w_group[layer], jnp.zeros((D, pad), f32)], 1)
    b = jnp.concatenate([b_expert[layer], b_group[layer], jnp.zeros((pad,), f32)])[None, :]
    return w, b


MOE_TM = 256
LN_TM = 256


def _moe_plan(route, T):
    n_slots = TOPK_IN * T
    n_tiles = n_slots // MOE_TM + N_EXP
    ids = route[:, 2:2 + TOPK_IN].astype(jnp.int32)
    e_flat = ids.T.reshape(n_slots)
    onehot = (e_flat[:, None] == jnp.arange(N_EXP, dtype=jnp.int32)[None, :]).astype(jnp.int32)
    csum = jnp.cumsum(onehot, axis=0)
    rank = jnp.sum(onehot * csum, axis=1) - 1
    counts = csum[-1]
    tiles_per = (counts + MOE_TM - 1) // MOE_TM
    tile_end = jnp.cumsum(tiles_per)
    tile_start = tile_end - tiles_per
    dest = jnp.sum(onehot * tile_start[None, :], axis=1) * MOE_TM + rank
    tok = jnp.arange(n_slots, dtype=jnp.int32) % T
    src = jnp.zeros((n_tiles * MOE_TM,), jnp.int32).at[dest].set(tok, unique_indices=True)
    j = jnp.arange(n_tiles, dtype=jnp.int32)
    tile_e = jnp.sum((j[:, None] >= tile_end[None, :]).astype(jnp.int32), axis=1)
    last_e = jnp.max(jnp.where(counts > 0, jnp.arange(N_EXP, dtype=jnp.int32), 0))
    tile_e = jnp.minimum(tile_e, last_e)
    n_live = tile_end[-1:].astype(jnp.int32)
    return tile_e, n_live, src, dest.astype(jnp.int32), n_tiles


GATHER_UNROLL = 8


def _moe_group_kernel(tile_e_ref, nt_ref, src_ref, xp_ref, wg_ref, wu_ref, wd_ref, o_ref, xbuf, wgb, wub, wdb):
    i = pl.program_id(0)
    nt = nt_ref[0]
    tm = xbuf.shape[0]

    @pl.when((i < nt) & ((i == 0) | (tile_e_ref[i] != tile_e_ref[jnp.maximum(i - 1, 0)])))
    def _():
        wgb[...] = wg_ref[...].astype(bf16)
        wub[...] = wu_ref[...].astype(bf16)
        wdb[...] = wd_ref[...].astype(bf16)

    @pl.when(i < nt)
    def _():
        def body(j, c):
            rows = [xp_ref[pl.ds(src_ref[i * tm + j * GATHER_UNROLL + q], 1), :] for q in range(GATHER_UNROLL)]
            xbuf[pl.ds(pl.multiple_of(j * GATHER_UNROLL, GATHER_UNROLL), GATHER_UNROLL), :] = jnp.concatenate(rows, axis=0)
            return c
        lax.fori_loop(0, tm // GATHER_UNROLL, body, 0)
        lo, hi = _unpack_halves(xbuf[...])
        x = jnp.concatenate([lo.astype(bf16), hi.astype(bf16)], axis=1)
        h = _silu(jnp.dot(x, wgb[...], preferred_element_type=f32)) * \
            jnp.dot(x, wub[...], preferred_element_type=f32)
        o_ref[...] = _pack_halves(jnp.dot(h.astype(bf16), wdb[...], preferred_element_type=f32))

    @pl.when(i >= nt)
    def _():
        o_ref[...] = jnp.zeros_like(o_ref)


def _moe_group(xp, tile_e, n_live, src, n_tiles, w_gate, w_up, w_down, layer):
    T, half = xp.shape
    D = 2 * half
    wspec = lambda shp: pl.BlockSpec((None, None) + shp, lambda i, te, nt, sr: (layer, te[i], 0, 0))
    return pl.pallas_call(
        _moe_group_kernel,
        out_shape=jax.ShapeDtypeStruct((n_tiles * MOE_TM, half), jnp.uint32),
        grid_spec=pltpu.PrefetchScalarGridSpec(
            num_scalar_prefetch=3,
            grid=(n_tiles,),
            in_specs=[pl.BlockSpec(memory_space=pltpu.VMEM), wspec((D, D_E)), wspec((D, D_E)),
                      wspec((D_E, D))],
            out_specs=pl.BlockSpec((MOE_TM, half), lambda i, te, nt, sr: (i, 0)),
            scratch_shapes=[pltpu.VMEM((MOE_TM, half), jnp.uint32), pltpu.VMEM((D, D_E), bf16),
                            pltpu.VMEM((D, D_E), bf16), pltpu.VMEM((D_E, D), bf16)]),
        compiler_params=_params("arbitrary"),
    )(tile_e, n_live, src, xp, w_gate, w_up, w_down)


def _moe_combine_ln_kernel(pos_ref, x_ref, rt_ref, g_ref, b_ref, ys_hbm, o_ref, ob_ref, ybuf, sems):
    i = pl.program_id(0)
    n = pl.num_programs(0)
    tm = x_ref.shape[0]
    T = n * tm
    groups = tm // GATHER_UNROLL

    def issue(tile, slot):
        for k in range(TOPK_IN):
            base = (slot * TOPK_IN + k) * groups

            def body(j, c, k=k, base=base):
                for q in range(GATHER_UNROLL):
                    p = pos_ref[k * T + tile * tm + j * GATHER_UNROLL + q]
                    pltpu.make_async_copy(ys_hbm.at[pl.ds(p, 1), :], ybuf.at[base + j, pl.ds(q, 1), :],
                                          sems.at[slot]).start()
                return c
            lax.fori_loop(0, groups, body, 0)

    @pl.when(i == 0)
    def _():
        issue(0, 0)

    @pl.when(i + 1 < n)
    def _():
        issue(i + 1, (i + 1) % 2)

    slot = i % 2

    base = slot * TOPK_IN * groups

    def wait_group(j, c):
        pltpu.make_async_copy(ys_hbm.at[pl.ds(0, GATHER_UNROLL), :], ybuf.at[base + j], sems.at[slot]).wait()
        return c
    lax.fori_loop(0, TOPK_IN * groups, wait_group, 0)

    def gathered(k):
        return _unpack_halves(ybuf[pl.ds(base + k * groups, groups)].reshape(tm, ybuf.shape[2]))

    lo1, hi1 = gathered(0)
    lo2, hi2 = gathered(1)
    rt = rt_ref[...]
    w1, w2 = rt[:, 0:1], rt[:, 1:2]
    y = jnp.concatenate([w1 * lo1 + w2 * lo2, w1 * hi1 + w2 * hi2], axis=1)
    o = _layer_norm(ALPHA * x_ref[...] + y, g_ref[...], b_ref[...])
    o_ref[...] = o
    ob_ref[...] = o.astype(bf16)


def _moe_combine_ln(x, route, pos, ys, g, b, layer):
    T, D = x.shape
    tm = min(T, LN_TM)
    vec = pl.BlockSpec((None, 1, D), lambda i, p: (layer, 0, 0))
    row = pl.BlockSpec((tm, D), lambda i, p: (i, 0))
    return pl.pallas_call(
        _moe_combine_ln_kernel,
        out_shape=(jax.ShapeDtypeStruct((T, D), f32), jax.ShapeDtypeStruct((T, D), bf16)),
        grid_spec=pltpu.PrefetchScalarGridSpec(
            num_scalar_prefetch=1,
            grid=(T // tm,),
            in_specs=[row, pl.BlockSpec((tm, LANES), lambda i, p: (i, 0)), vec, vec,
                      pl.BlockSpec(memory_space=pl.ANY)],
            out_specs=(row, row),
            scratch_shapes=[pltpu.VMEM((2 * TOPK_IN * tm // GATHER_UNROLL, GATHER_UNROLL, D // 2), jnp.uint32),
                            pltpu.SemaphoreType.DMA((2,))]),
        compiler_params=_params("arbitrary"),
    )(pos, x, route, g.reshape(g.shape[0], 1, D), b.reshape(b.shape[0], 1, D), ys)


def _moe_ln(xf, xp, route, w_gate, w_up, w_down, g, b, layer):
    tile_e, n_live, src, pos, n_tiles = _moe_plan(route, xf.shape[0])
    ys = _moe_group(xp, tile_e, n_live, src, n_tiles, w_gate, w_up, w_down, layer)
    return _moe_combine_ln(xf, route, pos, ys, g, b, layer)


def kernel(x, mem, ab_w_in, rg_conv_w, rg_conv_b, rg_wa, rg_ba, rg_wx, rg_bx, rg_lam, ssd_conv_w, ssd_conv_b, ssd_dt_bias, ssd_a_log, ssd_d, ssd_norm_w, ab_w_out, cd_w_in, s5_a_re, s5_a_im, s5_log_step, s5_b_re, s5_b_im, s5_c_re, s5_c_im, s5_d, s5_glu_w, s5_glu_b, dn_conv_w, dn_a_log, dn_dt_bias, dn_norm_w, cd_w_out, xa_w_q, xa_w_kv, xa_w_o, moe_w_group, moe_b_group, moe_w_expert, moe_b_expert, moe_w_gate, moe_w_up, moe_w_down, ln1_g, ln1_b, ln2_g, ln2_b, ln3_g, ln3_b):
    B, S, D = x.shape
    M = mem.shape[1]
    T = B * S
    xf = x.reshape(T, D)
    xb = xf.astype(bf16)
    memb = mem.reshape(B * M, D).astype(bf16)
    for l in range(DEPTH):
        i = l // 2
        if l % 2 == 0:
            proj = _mm(xb, ab_w_in, i, MAIN_AB, f32)
            dt_tail = _mm_hi(xf, _tail_cols(ab_w_in, i, MAIN_AB))
            ya = _rglru(proj, B, S, i, rg_conv_w, rg_conv_b, rg_wa, rg_ba, rg_wx, rg_bx, rg_lam)
            yb = _ssd(proj, dt_tail, B, S, i, ssd_conv_w, ssd_conv_b, ssd_dt_bias, ssd_a_log, ssd_d,
                      ssd_norm_w)
            xf, xb = _mm_ln([ya, yb], ab_w_out[i].astype(bf16), xf, ln1_g, ln1_b, l)
        else:
            proj = _mm(xb, cd_w_in, i, MAIN_CD, f32)
            ab_tail = _mm_hi(xf, _tail_cols(cd_w_in, i, MAIN_CD))
            yc = _s5(proj, B, S, i, s5_a_re, s5_a_im, s5_log_step, s5_b_re, s5_b_im, s5_c_re, s5_c_im,
                     s5_d, s5_glu_w, s5_glu_b)
            yd = _gdn(proj, ab_tail, B, S, i, dn_conv_w, dn_a_log, dn_dt_bias, dn_norm_w)
            xf, xb = _mm_ln([yc, yd], cd_w_out[i].astype(bf16), xf, ln1_g, ln1_b, l)
        q = _mm_resident(xb, xa_w_q[l].astype(bf16), bf16)
        kv = _mm(memb, xa_w_kv, l, 2 * D, bf16)
        att = _xattn(q, kv, B, S, M)
        xf, xp, route = _mm_ln([att], xa_w_o[l].astype(bf16), xf, ln2_g, ln2_b, l,
                               route_wb=_route_weights(moe_w_group, moe_b_group, moe_w_expert, moe_b_expert, l))
        xf, xb = _moe_ln(xf, xp, route, moe_w_gate, moe_w_up, moe_w_down, ln3_g, ln3_b, l)
    return xf.reshape(B, S, D)
```

```python
import functools
import math

import jax
import jax.numpy as jnp
from jax import lax
from jax.experimental import pallas as pl
from jax.experimental.pallas import tpu as pltpu

f32 = jnp.float32
bf16 = jnp.bfloat16

D_MODEL = 2048
DEPTH = 2
CHUNK = 64
CONV_W = 4
ALPHA = (2 * DEPTH) ** 0.25
LN_EPS = 1e-5
RMS_EPS = 1e-6
W_A = D_MODEL // 2
H_A = 8
BW_A = W_A // H_A
RG_C = 8.0
W_B = D_MODEL
HD_B = 64
H_B = W_B // HD_B
NG_B = 2
N_B = 128
HG_B = H_B // NG_B
CONV_B = W_B + 2 * NG_B * N_B
MAIN_AB = 2 * W_A + W_B + CONV_B
W_C = D_MODEL // 2
GS_C = 16
G_C = W_C // GS_C
P_C = 64
L_C = 16
H_D = 8
DK_D = D_MODEL // 16
DV_D = D_MODEL // 16
W_D = H_D * DV_D
QKV_D = 2 * H_D * DK_D + W_D
MAIN_CD = W_C + QKV_D + W_D
H_X = 4
HD_X = D_MODEL // H_X
NG_E = 4
E_PER = 8
N_EXP = NG_E * E_PER
TOPK_IN = 2
D_E = D_MODEL // 8

LANES = 128
SUBLANES = 8
S5_GB = LANES // GS_C
S5_NB = G_C // S5_GB
VMEM_LIMIT = 56 * 1024 * 1024


def _params(*sem):
    return pltpu.CompilerParams(dimension_semantics=sem, vmem_limit_bytes=VMEM_LIMIT)


def _sigmoid(x):
    return 1.0 / (1.0 + jnp.exp(-x))


def _silu(x):
    return x * _sigmoid(x)


def _softplus(x):
    return jnp.maximum(x, 0.0) + jnp.log(1.0 + jnp.exp(-jnp.abs(x)))


def _gelu_tanh(x):
    return 0.5 * x * (1.0 + jnp.tanh(math.sqrt(2.0 / math.pi) * (x + 0.044715 * (x * x * x))))


def _dot(a, b):
    return jnp.dot(a.astype(bf16), b.astype(bf16), preferred_element_type=f32)


def _dot_nt(a, b):
    return lax.dot_general(a.astype(bf16), b.astype(bf16), (((1,), (1,)), ((), ())),
                           preferred_element_type=f32)


def _dot_tn(a, b):
    return lax.dot_general(a.astype(bf16), b.astype(bf16), (((0,), (0,)), ((), ())),
                           preferred_element_type=f32)


def _split3(a):
    hi = a.astype(bf16)
    r = a - hi.astype(f32)
    mid = r.astype(bf16)
    lo = (r - mid.astype(f32)).astype(bf16)
    return hi, mid, lo


def _dot_exact_lhs(sel, b):
    s = sel.astype(bf16)
    b1, b2, b3 = _split3(b)
    d = functools.partial(jnp.dot, preferred_element_type=f32)
    return d(s, b1) + d(s, b2) + d(s, b3)


def _dot_exact_rhs(a, sel):
    s = sel.astype(bf16)
    a1, a2, a3 = _split3(a)
    d = functools.partial(jnp.dot, preferred_element_type=f32)
    return d(a1, s) + d(a2, s) + d(a3, s)


def _transpose_exact(a, eye):
    a1, a2, a3 = _split3(a)
    e = eye.astype(bf16)
    d = lambda x: lax.dot_general(e, x, (((1,), (1,)), ((), ())), preferred_element_type=f32)
    return d(a1) + d(a2) + d(a3)


def _dot3(a, b):
    a1 = a.astype(bf16)
    a2 = (a - a1.astype(f32)).astype(bf16)
    b1 = b.astype(bf16)
    b2 = (b - b1.astype(f32)).astype(bf16)
    d = functools.partial(jnp.dot, preferred_element_type=f32)
    return d(a1, b1) + (d(a1, b2) + d(a2, b1))


def _iota(shape, axis):
    return lax.broadcasted_iota(jnp.int32, shape, axis)


def _tri(n, strict=False):
    r, c = _iota((n, n), 0), _iota((n, n), 1)
    return (r > c) if strict else (r >= c)


def _mm_kernel(x_ref, w_ref, o_ref):
    o_ref[...] = jnp.dot(x_ref[...], w_ref[...].astype(bf16),
                         preferred_element_type=f32).astype(o_ref.dtype)


def _mm(x, w, layer, n_cols, out_dtype, tn=512):
    M, K = x.shape
    tm = min(M, 2048)
    return pl.pallas_call(
        _mm_kernel,
        out_shape=jax.ShapeDtypeStruct((M, n_cols), out_dtype),
        grid=(M // tm, n_cols // tn),
        in_specs=[pl.BlockSpec((tm, K), lambda i, j: (i, 0)),
                  pl.BlockSpec((None, K, tn), lambda i, j: (layer, 0, j))],
        out_specs=pl.BlockSpec((tm, tn), lambda i, j: (i, j)),
        compiler_params=_params("parallel", "parallel"),
    )(x, w)


def _mm_resident_kernel(x_ref, w_ref, o_ref):
    o_ref[...] = jnp.dot(x_ref[...], w_ref[...], preferred_element_type=f32).astype(o_ref.dtype)


def _mm_resident(x, w, out_dtype):
    M, K = x.shape
    tm = min(M, 1024)
    return pl.pallas_call(
        _mm_resident_kernel,
        out_shape=jax.ShapeDtypeStruct((M, w.shape[1]), out_dtype),
        grid=(M // tm,),
        in_specs=[pl.BlockSpec((tm, K), lambda i: (i, 0)), pl.BlockSpec(memory_space=pltpu.VMEM)],
        out_specs=pl.BlockSpec((tm, w.shape[1]), lambda i: (i, 0)),
        compiler_params=_params("parallel"),
    )(x, w)


def _tail_cols(w, layer, start):
    _, K, N = w.shape
    return lax.slice(w, (layer, 0, start), (layer + 1, K, N)).reshape(K, N - start)


def _mm_hi_kernel(x_ref, w_ref, o_ref, *xb_ref):
    x = x_ref[...]
    o_ref[...] = _dot3(x, w_ref[...])
    for r in xb_ref:
        r[...] = x.astype(bf16)


def _mm_hi(x, w, with_bf16=False):
    M, K = x.shape
    n = w.shape[1]
    tm = min(M, 512)
    row = pl.BlockSpec((tm, K), lambda i: (i, 0))
    out_shape = [jax.ShapeDtypeStruct((M, n), f32)] + ([jax.ShapeDtypeStruct((M, K), bf16)] if with_bf16 else [])
    out = pl.pallas_call(
        _mm_hi_kernel,
        out_shape=out_shape,
        grid=(M // tm,),
        in_specs=[row, pl.BlockSpec((K, n), lambda i: (0, 0))],
        out_specs=[pl.BlockSpec((tm, n), lambda i: (i, 0))] + ([row] if with_bf16 else []),
        compiler_params=_params("parallel"),
    )(x, w)
    return out if with_bf16 else out[0]


def _layer_norm(v, g, b):
    mu = jnp.mean(v, -1, keepdims=True)
    d = v - mu
    var = jnp.mean(d * d, -1, keepdims=True)
    return d * lax.rsqrt(var + LN_EPS) * g + b


HI_MASK = 0xFFFF0000


def _pack_halves(v):
    h = v.shape[1] // 2
    lo = lax.bitcast_convert_type(v[:, :h].astype(bf16).astype(f32), jnp.uint32)
    hi = lax.bitcast_convert_type(v[:, h:].astype(bf16).astype(f32), jnp.uint32)
    return (hi & jnp.uint32(HI_MASK)) | (lo >> 16)


def _unpack_halves(u):
    lo = lax.bitcast_convert_type(u << 16, f32)
    hi = lax.bitcast_convert_type(u & jnp.uint32(HI_MASK), f32)
    return lo, hi


def _route_top2(logits):
    lane = _iota(logits.shape, 1)
    neg = -jnp.inf
    big = jnp.int32(LANES)
    is_g = (lane >= N_EXP) & (lane < N_EXP + NG_E)
    gl = jnp.where(is_g, logits, neg)
    gmax = jnp.max(gl, -1, keepdims=True)
    g_lane = jnp.min(jnp.where(gl == gmax, lane, big), -1, keepdims=True)
    g_prob = 1.0 / jnp.sum(jnp.where(is_g, jnp.exp(gl - gmax), 0.0), -1, keepdims=True)
    e0 = (g_lane - N_EXP) * E_PER
    sel = jnp.where((lane >= e0) & (lane < e0 + E_PER), logits, neg)
    m1 = jnp.max(sel, -1, keepdims=True)
    i1 = jnp.min(jnp.where(sel == m1, lane, big), -1, keepdims=True)
    sel2 = jnp.where(lane == i1, neg, sel)
    m2 = jnp.max(sel2, -1, keepdims=True)
    i2 = jnp.min(jnp.where(sel2 == m2, lane, big), -1, keepdims=True)
    e2 = jnp.exp(m2 - m1)
    w1 = g_prob / (1.0 + e2)
    w2 = g_prob * e2 / (1.0 + e2)
    return (jnp.where(lane == 0, w1, 0.0) + jnp.where(lane == 1, w2, 0.0)
            + jnp.where(lane == 2, i1.astype(f32), 0.0) + jnp.where(lane == 3, i2.astype(f32), 0.0))


def _mm_ln_kernel(*refs, n_parts, routed):
    part_refs = refs[:n_parts]
    w_ref, x_ref, g_ref, b_ref = refs[n_parts:n_parts + 4]
    acc, off = None, 0
    for p in part_refs:
        c = jnp.dot(p[...], w_ref[off:off + p.shape[1], :], preferred_element_type=f32)
        acc = c if acc is None else acc + c
        off += p.shape[1]
    o = _layer_norm(ALPHA * x_ref[...] + acc, g_ref[...], b_ref[...])
    if routed:
        rw_ref, rb_ref, o_ref, o2_ref, rt_ref = refs[n_parts + 4:]
        o2_ref[...] = _pack_halves(o)
        rt_ref[...] = _route_top2(_dot3(o, rw_ref[...]) + rb_ref[...])
    else:
        o_ref, o2_ref = refs[n_parts + 4:]
        o2_ref[...] = o.astype(bf16)
    o_ref[...] = o


def _mm_ln(parts, w, x, g, b, layer, route_wb=None):
    T, D = x.shape
    tm = min(T, 512)
    routed = route_wb is not None
    vec = pl.BlockSpec((None, 1, D), lambda i: (layer, 0, 0))
    row = pl.BlockSpec((tm, D), lambda i: (i, 0))
    in_specs = [pl.BlockSpec((tm, a.shape[1]), lambda i: (i, 0)) for a in parts]
    in_specs += [pl.BlockSpec(memory_space=pltpu.VMEM), row, vec, vec]
    args = [*parts, w, x, g.reshape(g.shape[0], 1, D), b.reshape(b.shape[0], 1, D)]
    if routed:
        in_specs += [pl.BlockSpec((D, LANES), lambda i: (0, 0)), pl.BlockSpec((1, LANES), lambda i: (0, 0))]
        args += list(route_wb)
        out_shape = (jax.ShapeDtypeStruct((T, D), f32), jax.ShapeDtypeStruct((T, D // 2), jnp.uint32),
                     jax.ShapeDtypeStruct((T, LANES), f32))
        out_specs = (row, pl.BlockSpec((tm, D // 2), lambda i: (i, 0)), pl.BlockSpec((tm, LANES), lambda i: (i, 0)))
    else:
        out_shape = (jax.ShapeDtypeStruct((T, D), f32), jax.ShapeDtypeStruct((T, D), bf16))
        out_specs = (row, row)
    return pl.pallas_call(
        functools.partial(_mm_ln_kernel, n_parts=len(parts), routed=routed),
        out_shape=out_shape,
        grid=(T // tm,),
        in_specs=in_specs,
        out_specs=out_specs,
        compiler_params=_params("parallel"),
    )(*args)


def _conv_rows(x_ref, w, hist_ref):
    tt = x_ref.shape[0]
    hist_ref[SUBLANES:SUBLANES + tt, :] = x_ref[...]
    acc = _causal_taps(hist_ref[...], w, tt)
    hist_ref[0:SUBLANES, :] = hist_ref[tt:tt + SUBLANES, :]
    return acc


def _causal_taps(ext, w, tt):
    acc = None
    for k in range(CONV_W):
        back = CONV_W - 1 - k
        rows = (pltpu.roll(ext, back, 0) if back else ext)[SUBLANES:SUBLANES + tt, :]
        term = w[k:k + 1, :] * rows
        acc = term if acc is None else acc + term
    return acc


def _scan_affine(a, u):
    n = a.shape[0]
    row = _iota(a.shape, 0)
    d = 1
    while d < n:
        keep = row >= d
        a_s = pltpu.roll(a, d, 0)
        u_s = pltpu.roll(u, d, 0)
        u = u + jnp.where(keep, a * u_s, 0.0)
        a = jnp.where(keep, a * a_s, a)
        d *= 2
    return a, u


def _rglru_kernel(gate_ref, xa_ref, cw_ref, cb_ref, wa_ref, ba_ref, wx_ref, bx_ref, lam_ref,
                  o_ref, hist_ref, h_ref):
    @pl.when(pl.program_id(1) == 0)
    def _():
        h_ref[...] = jnp.zeros_like(h_ref)
        hist_ref[:, 0:SUBLANES, :] = jnp.zeros((hist_ref.shape[0], SUBLANES, hist_ref.shape[2]), f32)

    decay_rate = -RG_C * _softplus(-lam_ref[...])
    for b in range(xa_ref.shape[0]):
        xc = _conv_rows(xa_ref.at[b], cw_ref[...], hist_ref.at[b]) + cb_ref[...]
        r = _sigmoid(_dot(xc, wa_ref[...]) + ba_ref[...])
        i = _sigmoid(_dot(xc, wx_ref[...]) + bx_ref[...])
        log_a = r * decay_rate
        a = jnp.exp(log_a)
        u = jnp.sqrt(1.0 - jnp.exp(2.0 * log_a)) * (i * xc)
        a_cum, h = _scan_affine(a, u)
        h = h + a_cum * h_ref[b, 0:1, :]
        tt = h.shape[0]
        h_ref[b] = jnp.broadcast_to(h[tt - 1:tt, :], h_ref.shape[1:])
        o_ref[b] = (_gelu_tanh(gate_ref[b]) * h).astype(o_ref.dtype)


def _rglru(proj, B, S, layer, conv_w, conv_b, wa, ba, wx, bx, lam):
    tt = min(S, 256)
    ns = S // tt
    xa_col0 = W_A // BW_A
    vec = lambda a: a.reshape(a.shape[0], 1, W_A)
    vspec = pl.BlockSpec((None, 1, BW_A), lambda h, s: (layer, 0, h))
    wspec = pl.BlockSpec((None, None, BW_A, BW_A), lambda h, s: (layer, h, 0, 0))
    proj3 = proj.reshape(B, S, proj.shape[1])
    out = pl.pallas_call(
        _rglru_kernel,
        out_shape=jax.ShapeDtypeStruct((B, S, W_A), bf16),
        grid=(H_A, ns),
        in_specs=[pl.BlockSpec((B, tt, BW_A), lambda h, s: (0, s, h)),
                  pl.BlockSpec((B, tt, BW_A), lambda h, s: (0, s, xa_col0 + h)),
                  pl.BlockSpec((None, CONV_W, BW_A), lambda h, s: (layer, 0, h)),
                  vspec, wspec, vspec, wspec, vspec, vspec],
        out_specs=pl.BlockSpec((B, tt, BW_A), lambda h, s: (0, s, h)),
        scratch_shapes=[pltpu.VMEM((B, SUBLANES + tt, BW_A), f32), pltpu.VMEM((B, SUBLANES, BW_A), f32)],
        compiler_params=_params("parallel", "arbitrary"),
    )(proj3, proj3, conv_w, vec(conv_b), wa, vec(ba), wx, vec(bx), vec(lam))
    return out.reshape(B * S, W_A)


def _ssd_group(X, Bc, Cc, z, dt, a_neg, d_head, nw, st_ref):
    L = CHUNK
    adt = dt * a_neg
    tri = _tri(L)
    cs = _dot_exact_lhs(tri.astype(f32), adt)
    eye_h = (_iota((HG_B, HG_B), 0) == _iota((HG_B, HG_B), 1)).astype(f32)
    cs_t = _transpose_exact(cs, eye_h)
    cs_last = cs[L - 1:L, :]

    expand = (_iota((HG_B, HG_B * HD_B), 1) // HD_B == _iota((HG_B, HG_B * HD_B), 0)).astype(f32)
    dt_x = _dot_exact_rhs(dt, expand)
    ecs_x = _dot_exact_rhs(jnp.exp(cs), expand)
    dec_x = _dot_exact_rhs(jnp.exp(cs_last - cs), expand)
    cdec_x = _dot_exact_rhs(jnp.broadcast_to(jnp.exp(cs_last), (SUBLANES, HG_B)), expand)[0:1]
    d_x = _dot_exact_rhs(jnp.broadcast_to(d_head, (SUBLANES, HG_B)), expand)[0:1]

    xdt = X * dt_x
    cb = _dot_nt(Cc, Bc)
    xdt_b = xdt.astype(bf16)
    left = _iota((L, 2 * HD_B), 1) < HD_B
    pieces = []
    for j in range(0, HG_B, 2):
        pair = xdt_b[:, j * HD_B:(j + 2) * HD_B]
        outs = []
        for jj in (j, j + 1):
            seg = jnp.where(tri, cs[:, jj:jj + 1] - cs_t[jj:jj + 1, :], -jnp.inf)
            outs.append(jnp.dot((cb * jnp.exp(seg)).astype(bf16), pair, preferred_element_type=f32))
        pieces.append(jnp.where(left, outs[0], outs[1]))
    y_diag = jnp.concatenate(pieces, axis=1)

    state = st_ref[...]
    y_off = ecs_x * _dot(Cc, state)
    st_ref[...] = state * cdec_x + _dot_tn(Bc, xdt * dec_x)

    y = y_diag + y_off + X * d_x
    yg = y * _silu(z)
    return yg * lax.rsqrt(jnp.mean(yg * yg, -1, keepdims=True) + RMS_EPS) * nw


def _ssd_kernel(z_ref, x_ref, b_ref, c_ref, dt_ref, cw_ref, cb_ref, dtb_ref, alog_ref, d_ref, nw_ref,
                o_ref, hx_ref, hb_ref, hc_ref, st_ref):
    GW = W_B // NG_B

    @pl.when(pl.program_id(0) == 0)
    def _():
        st_ref[...] = jnp.zeros_like(st_ref)
        for h in (hx_ref, hb_ref, hc_ref):
            h[:, 0:SUBLANES, :] = jnp.zeros((h.shape[0], SUBLANES, h.shape[2]), f32)

    cw, cbias = cw_ref[...], cb_ref[...]
    c_off = W_B + NG_B * N_B
    a_all = -jnp.exp(alog_ref[...])
    d_all = d_ref[...]
    nw_all = nw_ref[...]
    for b in range(x_ref.shape[0]):
        X = _silu(_conv_rows(x_ref.at[b], cw[:, :W_B], hx_ref.at[b]) + cbias[:, :W_B])
        Bm = _silu(_conv_rows(b_ref.at[b], cw[:, W_B:c_off], hb_ref.at[b]) + cbias[:, W_B:c_off])
        Cm = _silu(_conv_rows(c_ref.at[b], cw[:, c_off:], hc_ref.at[b]) + cbias[:, c_off:])
        dt = _softplus(dt_ref[b] + dtb_ref[...])
        for g in range(NG_B):
            hs = slice(g * HG_B, (g + 1) * HG_B)
            ws = slice(g * GW, (g + 1) * GW)
            ns = slice(g * N_B, (g + 1) * N_B)
            out = _ssd_group(X[:, ws], Bm[:, ns], Cm[:, ns], z_ref[b, :, ws], dt[:, hs], a_all[:, hs],
                             d_all[:, hs], nw_all[:, ws], st_ref.at[b * NG_B + g])
            o_ref[b, :, ws] = out.astype(o_ref.dtype)


def _ssd(proj, dt_tail, B, S, layer, conv_w, conv_b, dt_bias, a_log, d, norm_w):
    L = CHUNK
    nc = S // L
    BC = NG_B * N_B
    per_batch = lambda a: a.reshape(B, S, a.shape[1])
    blk = lambda n, cb: pl.BlockSpec((B, L, n), lambda c: (0, c, cb))
    vec = lambda a: a.reshape(a.shape[0], 1, a.shape[-1])
    par = lambda r, n: pl.BlockSpec((None, r, n), lambda c: (layer, 0, 0))
    proj3 = per_batch(proj)
    out = pl.pallas_call(
        _ssd_kernel,
        out_shape=jax.ShapeDtypeStruct((B, S, W_B), bf16),
        grid=(nc,),
        in_specs=[blk(W_B, 2 * W_A // W_B), blk(W_B, (2 * W_A + W_B) // W_B),
                  blk(BC, (2 * W_A + 2 * W_B) // BC), blk(BC, (2 * W_A + 2 * W_B) // BC + 1),
                  blk(H_B, 0), par(CONV_W, CONV_B), par(1, CONV_B), par(1, H_B), par(1, H_B), par(1, H_B),
                  par(1, W_B)],
        out_specs=blk(W_B, 0),
        scratch_shapes=[pltpu.VMEM((B, SUBLANES + L, W_B), f32), pltpu.VMEM((B, SUBLANES + L, BC), f32),
                        pltpu.VMEM((B, SUBLANES + L, BC), f32), pltpu.VMEM((B * NG_B, N_B, W_B // NG_B), f32)],
        compiler_params=_params("arbitrary"),
    )(proj3, proj3, proj3, proj3, per_batch(dt_tail), conv_w, vec(conv_b), vec(dt_bias), vec(a_log), vec(d),
      vec(norm_w))
    return out.reshape(B * S, W_B)


def _s5_tables(a_re, a_im, log_step, b_re, b_im, c_re, c_im):
    L = L_C
    ar, ai = a_re.astype(f32), a_im.astype(f32)
    step = jnp.exp(log_step.astype(f32))[:, None]
    mag = jnp.exp(ar * step)
    lb_re, lb_im = mag * jnp.cos(ai * step), mag * jnp.sin(ai * step)
    den = ar * ar + ai * ai
    f_re = ((lb_re - 1.0) * ar + lb_im * ai) / den
    f_im = (lb_im * ar - (lb_re - 1.0) * ai) / den
    br, bi = b_re.astype(f32), b_im.astype(f32)
    bb_re = f_re[..., None] * br - f_im[..., None] * bi
    bb_im = f_re[..., None] * bi + f_im[..., None] * br
    cr, ci = c_re.astype(f32), c_im.astype(f32)

    def power(n):
        n = n.astype(f32)[None, :, None]
        m = jnp.exp(ar[:, None, :] * step[:, None, :] * n)
        ang = ai[:, None, :] * step[:, None, :] * n
        return m * jnp.cos(ang), m * jnp.sin(ang)

    j = jnp.arange(L)
    pr, pi = power(j)
    lbr = pr[..., None] * bb_re[:, None] - pi[..., None] * bb_im[:, None]
    lbi = pr[..., None] * bb_im[:, None] + pi[..., None] * bb_re[:, None]
    kern = (jnp.einsum('gop,gjpk->gjko', cr, lbr) - jnp.einsum('gop,gjpk->gjko', ci, lbi))
    lagb = kern.reshape(S5_NB, S5_GB, L, GS_C, GS_C).transpose(0, 2, 1, 3, 4)
    lagb = lagb.reshape(S5_NB, L, LANES, GS_C)
    rev = (L - 1) - j
    bend = jnp.concatenate([jnp.take(lbr, rev, axis=1), jnp.take(lbi, rev, axis=1)], 2)
    bendc = bend.reshape(S5_NB, S5_GB, L, 2 * P_C, GS_C).transpose(0, 2, 1, 4, 3)
    bendc = bendc.reshape(S5_NB, L, LANES, 2 * P_C)
    qr, qi = power(j + 1)
    car_re = cr[:, None] * qr[:, :, None, :] - ci[:, None] * qi[:, :, None, :]
    car_im = -(cr[:, None] * qi[:, :, None, :] + ci[:, None] * qr[:, :, None, :])
    car = jnp.concatenate([car_re, car_im], -1)
    ccarc = car.reshape(S5_NB, S5_GB, L, GS_C, 2 * P_C).transpose(0, 2, 4, 1, 3)
    ccarc = ccarc.reshape(S5_NB, L, 2 * P_C, LANES)
    return lagb, bendc.astype(bf16), ccarc.astype(bf16), power


def _s5_kernel(u_ref, lag_ref, bend_ref, ccar_ref, sc_ref, o_ref, toep_t, bend_t, ccar_t):
    L = L_C
    SW = 2 * P_C

    @pl.when(pl.program_id(1) == 0)
    def _():
        zero = jnp.zeros((LANES, LANES), bf16)
        own_g = (_iota((LANES, LANES), 0) // GS_C) == (_iota((LANES, LANES), 1) // GS_C)
        lag = [jnp.where(own_g, jnp.concatenate([lag_ref[j]] * S5_GB, axis=1), 0.0).astype(bf16) for j in range(L)]
        for s in range(L):
            for t in range(L):
                toep_t[s * LANES:(s + 1) * LANES, t * LANES:(t + 1) * LANES] = lag[t - s] if t >= s else zero
        same_g = (_iota((LANES, S5_GB * SW), 0) // GS_C) == (_iota((LANES, S5_GB * SW), 1) // SW)
        for s in range(L):
            wide = jnp.concatenate([bend_ref[s]] * S5_GB, axis=1)
            bend_t[s * LANES:(s + 1) * LANES, :] = jnp.where(same_g, wide, jnp.zeros_like(wide))
        col_g = _iota((SW, LANES), 1) // GS_C
        for t in range(L):
            blk = ccar_ref[t]
            for g in range(S5_GB):
                ccar_t[g * SW:(g + 1) * SW, t * LANES:(t + 1) * LANES] = jnp.where(col_g == g, blk, jnp.zeros_like(blk))

    n = o_ref.shape[0] // L
    U = jnp.concatenate([u_ref[pl.ds(l, n, stride=L), :].astype(bf16) for l in range(L)], axis=1)
    Y = jnp.dot(U, toep_t[...], preferred_element_type=f32)
    H_all = jnp.dot(U, bend_t[...], preferred_element_type=f32)
    row = _iota((n, SW), 0)
    prev = []
    for g in range(S5_GB):
        sl = slice(g * SW, (g + 1) * SW)
        H = H_all[:, sl]
        d, k = 1, 0
        while d < n:
            hs = pltpu.roll(H, d, 0)
            sw = pltpu.roll(hs, P_C, 1)
            H = H + jnp.where(row >= d, sc_ref[2 * k:2 * k + 1, sl] * hs + sc_ref[2 * k + 1:2 * k + 2, sl] * sw, 0.0)
            d *= 2
            k += 1
        prev.append(jnp.where(row >= 1, pltpu.roll(H, 1, 0), 0.0).astype(bf16))
    Y = Y + jnp.dot(jnp.concatenate(prev, axis=1), ccar_t[...], preferred_element_type=f32)
    for l in range(L):
        o_ref[pl.ds(l, n, stride=L), :] = Y[:, l * LANES:(l + 1) * LANES]


def _s5_post_kernel(y_ref, u_ref, d_ref, w_ref, b_ref, o_ref):
    y = y_ref[...] + d_ref[...] * u_ref[...]
    g = _gelu_tanh(y)
    o_ref[...] = (g * _sigmoid(_dot(g, w_ref[...]) + b_ref[...])).astype(o_ref.dtype)


def _s5(proj, B, S, layer, a_re, a_im, log_step, b_re, b_im, c_re, c_im, d, glu_w, glu_b):
    T = B * S
    L = L_C
    nch = S // L
    lagb, bendc, ccarc, power = _s5_tables(a_re[layer], a_im[layer], log_step[layer], b_re[layer],
                                           b_im[layer], c_re[layer], c_im[layer])
    nsteps = max(1, (nch - 1).bit_length())
    sr, si = power(L * (2 ** jnp.arange(nsteps)))
    scan_c = jnp.stack([jnp.concatenate([sr, sr], -1), jnp.concatenate([-si, si], -1)], 2)
    scan8 = scan_c.reshape(S5_NB, S5_GB, 2 * nsteps, 2 * P_C).transpose(0, 2, 1, 3)
    scan8 = scan8.reshape(S5_NB, 2 * nsteps, S5_GB * 2 * P_C)
    tab = lambda r, c: pl.BlockSpec((None, L, r, c), lambda g, b: (g, 0, 0, 0))
    y = pl.pallas_call(
        _s5_kernel,
        out_shape=jax.ShapeDtypeStruct((T, W_C), f32),
        grid=(S5_NB, B),
        in_specs=[pl.BlockSpec((S, LANES), lambda g, b: (b, g)),
                  tab(LANES, GS_C), tab(LANES, 2 * P_C), tab(2 * P_C, LANES),
                  pl.BlockSpec((None, 2 * nsteps, S5_GB * 2 * P_C), lambda g, b: (g, 0, 0))],
        out_specs=pl.BlockSpec((S, LANES), lambda g, b: (b, g)),
        scratch_shapes=[pltpu.VMEM((L * LANES, L * LANES), bf16),
                        pltpu.VMEM((L * LANES, S5_GB * 2 * P_C), bf16),
                        pltpu.VMEM((S5_GB * 2 * P_C, L * LANES), bf16)],
        compiler_params=_params("arbitrary", "arbitrary"),
    )(proj, lagb, bendc, ccarc, scan8)
    tm = min(T, 1024)
    vec = pl.BlockSpec((None, 1, W_C), lambda i: (layer, 0, 0))
    return pl.pallas_call(
        _s5_post_kernel,
        out_shape=jax.ShapeDtypeStruct((T, W_C), bf16),
        grid=(T // tm,),
        in_specs=[pl.BlockSpec((tm, W_C), lambda i: (i, 0)),
                  pl.BlockSpec((tm, W_C), lambda i: (i, 0)),
                  vec,
                  pl.BlockSpec((None, W_C, W_C), lambda i: (layer, 0, 0)),
                  vec],
        out_specs=pl.BlockSpec((tm, W_C), lambda i: (i, 0)),
        compiler_params=_params("parallel"),
    )(y, proj, d.reshape(d.shape[0], 1, W_C), glu_w, glu_b.reshape(glu_b.shape[0], 1, W_C))


def _gdn_prep_kernel(q_ref, k_ref, v_ref, qp_ref, kp_ref, vp_ref, ab_ref, cwq_ref, cwk_ref, cwv_ref,
                     alog_ref, dtb_ref, w_ref, u_ref, qd_ref, kd_ref, qk_ref, ge_ref, hist_ref):
    L = CHUNK
    nb = q_ref.shape[0]
    has_prev = (pl.program_id(0) > 0).astype(f32)

    def conv(cur_ref, prev_ref, w_ref_, b, slot):
        hist = hist_ref.at[b * 3 + slot]
        hist[0:SUBLANES, :] = prev_ref[b] * has_prev
        hist[SUBLANES:SUBLANES + L, :] = cur_ref[b]
        return _silu(_causal_taps(hist[...], w_ref_[...], L))

    tri = _tri(L)
    tri_s = _tri(L, strict=True)
    eye_h = (_iota((H_D, H_D), 0) == _iota((H_D, H_D), 1)).astype(f32)
    eye = (_iota((L, L), 0) == _iota((L, L), 1)).astype(f32)
    q_all, k_all, v_all, beta_all, gcs_all, gcs_t = [], [], [], [], [], []
    for b in range(nb):
        q_all.append(conv(q_ref, qp_ref, cwq_ref, b, 0))
        k_all.append(conv(k_ref, kp_ref, cwk_ref, b, 1))
        v_all.append(conv(v_ref, vp_ref, cwv_ref, b, 2))
        ab = ab_ref[b]
        g = -jnp.exp(alog_ref[...]) * _softplus(ab[:, 0:H_D] + dtb_ref[...])
        beta_all.append(_sigmoid(ab[:, H_D:2 * H_D]))
        gcs_all.append(_dot_exact_lhs(tri.astype(f32), g))
        gcs_t.append(_transpose_exact(gcs_all[b], eye_h))
        ge_ref[b] = jnp.concatenate([jnp.exp(gcs_all[b]), jnp.zeros((L, LANES - H_D), f32)], axis=1)

    pairs = [(b, h) for b in range(nb) for h in range(H_D)]
    cols = lambda h: slice(h * DK_D, (h + 1) * DK_D)
    unit = lambda x: x * lax.rsqrt(jnp.sum(x * x, -1, keepdims=True) + 1e-6)
    gcs = [gcs_all[b][:, h:h + 1] for b, h in pairs]
    eg = [jnp.exp(g) for g in gcs]
    beta = [beta_all[b][:, h:h + 1] for b, h in pairs]
    qs = [unit(q_all[b][:, cols(h)]) * (DK_D ** -0.5) for b, h in pairs]
    ks = [unit(k_all[b][:, cols(h)]) for b, h in pairs]
    kbs = [k * bt for k, bt in zip(ks, beta)]
    decay = [jnp.exp(jnp.where(tri, g - gcs_t[b][h:h + 1, :], -jnp.inf)) for (b, h), g in zip(pairs, gcs)]
    kk = [_dot_nt(kb, k) for kb, k in zip(kbs, ks)]
    qk = [_dot_nt(q, k) for q, k in zip(qs, ks)]
    pw = [jnp.where(tri_s, -(a * d), 0.0) for a, d in zip(kk, decay)]
    inv = [eye + p for p in pw]
    for _ in range(int(math.log2(L)) - 1):
        pw = [_dot(p, p) for p in pw]
        inv = [a + _dot(a, p) for a, p in zip(inv, pw)]
    rhs = [jnp.concatenate([v_all[b][:, cols(h)] * bt, kb * e], axis=1)
           for (b, h), bt, kb, e in zip(pairs, beta, kbs, eg)]
    sol = [_dot(a, r) for a, r in zip(inv, rhs)]
    for i, (b, h) in enumerate(pairs):
        u_ref[b, :, cols(h)] = sol[i][:, :DV_D]
        w_ref[b, :, cols(h)] = sol[i][:, DV_D:]
        qk_ref[b, :, h * L:(h + 1) * L] = jnp.where(tri, qk[i] * decay[i], 0.0)
        qd_ref[b, :, cols(h)] = qs[i] * eg[i]
        kd_ref[b, :, cols(h)] = ks[i] * jnp.exp(gcs[i][L - 1:L, :] - gcs[i])


def _gdn_scan_kernel(w_ref, u_ref, qd_ref, kd_ref, qk_ref, ge_ref, z_ref, nw_ref, o_ref, st_ref):
    L = CHUNK

    @pl.when(pl.program_id(0) == 0)
    def _():
        st_ref[...] = jnp.zeros_like(st_ref)

    nw = nw_ref[...]
    pairs = [(b, h) for b in range(w_ref.shape[0]) for h in range(H_D)]
    cols = lambda h: slice(h * DK_D, (h + 1) * DK_D)
    state = [st_ref[b * H_D + h] for b, h in pairs]
    v_new = [u_ref[b, :, cols(h)] - _dot(w_ref[b, :, cols(h)], s) for (b, h), s in zip(pairs, state)]
    o = [_dot(qd_ref[b, :, cols(h)], s) + _dot(qk_ref[b, :, h * L:(h + 1) * L], v)
         for (b, h), s, v in zip(pairs, state, v_new)]
    for (b, h), s, v in zip(pairs, state, v_new):
        st_ref[b * H_D + h] = s * ge_ref[b, L - 1:L, h:h + 1] + _dot_tn(kd_ref[b, :, cols(h)], v)
    for (b, h), y in zip(pairs, o):
        y = y * lax.rsqrt(jnp.mean(y * y, -1, keepdims=True) + RMS_EPS) * nw
        o_ref[b, :, cols(h)] = (y * _silu(z_ref[b, :, cols(h)])).astype(o_ref.dtype)


def _gdn(proj, ab_tail, B, S, layer, conv_w, a_log, dt_bias, norm_w):
    L = CHUNK
    nc = S // L
    per8 = L // SUBLANES
    q_c0 = W_C // W_D
    z_c0 = (W_C + QKV_D) // W_D
    proj3 = proj.reshape(B, S, proj.shape[1])
    blk = lambda n, cb=0: pl.BlockSpec((B, L, n), lambda c: (0, c, cb))
    prev = lambda cb: pl.BlockSpec((B, SUBLANES, W_D), lambda c: (0, jnp.maximum(c * per8 - 1, 0), cb))
    cw = lambda cb: pl.BlockSpec((None, CONV_W, W_D), lambda c: (layer, 0, cb))
    v8 = pl.BlockSpec((None, 1, H_D), lambda c: (layer, 0, 0))
    shp = lambda n: jax.ShapeDtypeStruct((B, S, n), f32)
    w_c, u_c, q_dec, k_dec, qk, gexp = pl.pallas_call(
        _gdn_prep_kernel,
        out_shape=(shp(W_D), shp(W_D), shp(W_D), shp(W_D), shp(H_D * L), shp(LANES)),
        grid=(nc,),
        in_specs=[blk(W_D, q_c0), blk(W_D, q_c0 + 1), blk(W_D, q_c0 + 2), prev(q_c0), prev(q_c0 + 1),
                  prev(q_c0 + 2), blk(2 * H_D), cw(0), cw(1), cw(2), v8, v8],
        out_specs=(blk(W_D), blk(W_D), blk(W_D), blk(W_D), blk(H_D * L), blk(LANES)),
        scratch_shapes=[pltpu.VMEM((3 * B, SUBLANES + L, W_D), f32)],
        compiler_params=_params("parallel"),
    )(proj3, proj3, proj3, proj3, proj3, proj3, ab_tail.reshape(B, S, 2 * H_D), conv_w, conv_w, conv_w,
      a_log.reshape(a_log.shape[0], 1, H_D), dt_bias.reshape(dt_bias.shape[0], 1, H_D))
    out = pl.pallas_call(
        _gdn_scan_kernel,
        out_shape=jax.ShapeDtypeStruct((B, S, W_D), bf16),
        grid=(nc,),
        in_specs=[blk(W_D), blk(W_D), blk(W_D), blk(W_D), blk(H_D * L), blk(LANES), blk(W_D, z_c0),
                  pl.BlockSpec((None, 1, DV_D), lambda c: (layer, 0, 0))],
        out_specs=blk(W_D),
        scratch_shapes=[pltpu.VMEM((B * H_D, DK_D, DV_D), f32)],
        compiler_params=_params("arbitrary"),
    )(w_c, u_c, q_dec, k_dec, qk, gexp, proj3, norm_w.reshape(norm_w.shape[0], 1, DV_D))
    return out.reshape(B * S, W_D)


def _xattn_kernel(q_ref, k_ref, v_ref, o_ref):
    s = _dot_nt(q_ref[...], k_ref[...]) * (HD_X ** -0.5)
    m = jnp.max(s, -1, keepdims=True)
    p = jnp.exp(s - m)
    p = p / jnp.sum(p, -1, keepdims=True)
    o_ref[...] = jnp.dot(p.astype(bf16), v_ref[...], preferred_element_type=f32).astype(o_ref.dtype)


def _xattn(q, kv, B, S, M):
    T = B * S
    tq = min(S, 1024)
    nq = S // tq
    return pl.pallas_call(
        _xattn_kernel,
        out_shape=jax.ShapeDtypeStruct((T, D_MODEL), bf16),
        grid=(B, nq, H_X),
        in_specs=[pl.BlockSpec((tq, HD_X), lambda b, i, h: (b * nq + i, h)),
                  pl.BlockSpec((M, HD_X), lambda b, i, h: (b, h)),
                  pl.BlockSpec((M, HD_X), lambda b, i, h: (b, H_X + h))],
        out_specs=pl.BlockSpec((tq, HD_X), lambda b, i, h: (b * nq + i, h)),
        compiler_params=_params("parallel", "parallel", "parallel"),
    )(q, kv, kv)


def _route_weights(w_group, b_group, w_expert, b_expert, layer):
    D = w_group.shape[1]
    pad = LANES - N_EXP - NG_E
    w = jnp.concatenate([w_expert[layer], w_group[layer], jnp.zeros((D, pad), f32)], 1)
    b = jnp.concatenate([b_expert[layer], b_group[layer], jnp.zeros((pad,), f32)])[None, :]
    return w, b


MOE_TM = 256
LN_TM = 256


def _moe_plan(route, T):
    n_slots = TOPK_IN * T
    n_tiles = n_slots // MOE_TM + N_EXP
    ids = route[:, 2:2 + TOPK_IN].astype(jnp.int32)
    e_flat = ids.T.reshape(n_slots)
    onehot = (e_flat[:, None] == jnp.arange(N_EXP, dtype=jnp.int32)[None, :]).astype(jnp.int32)
    csum = jnp.cumsum(onehot, axis=0)
    rank = jnp.sum(onehot * csum, axis=1) - 1
    counts = csum[-1]
    tiles_per = (counts + MOE_TM - 1) // MOE_TM
    tile_end = jnp.cumsum(tiles_per)
    tile_start = tile_end - tiles_per
    dest = jnp.sum(onehot * tile_start[None, :], axis=1) * MOE_TM + rank
    tok = jnp.arange(n_slots, dtype=jnp.int32) % T
    src = jnp.zeros((n_tiles * MOE_TM,), jnp.int32).at[dest].set(tok, unique_indices=True)
    j = jnp.arange(n_tiles, dtype=jnp.int32)
    tile_e = jnp.sum((j[:, None] >= tile_end[None, :]).astype(jnp.int32), axis=1)
    last_e = jnp.max(jnp.where(counts > 0, jnp.arange(N_EXP, dtype=jnp.int32), 0))
    tile_e = jnp.minimum(tile_e, last_e)
    n_live = tile_end[-1:].astype(jnp.int32)
    return tile_e, n_live, src, dest.astype(jnp.int32), n_tiles


GATHER_UNROLL = 8


def _moe_group_kernel(tile_e_ref, nt_ref, src_ref, xp_ref, wg_ref, wu_ref, wd_ref, o_ref, xbuf, wgb, wub, wdb):
    i = pl.program_id(0)
    nt = nt_ref[0]
    tm = xbuf.shape[0]

    @pl.when((i < nt) & ((i == 0) | (tile_e_ref[i] != tile_e_ref[jnp.maximum(i - 1, 0)])))
    def _():
        wgb[...] = wg_ref[...].astype(bf16)
        wub[...] = wu_ref[...].astype(bf16)
        wdb[...] = wd_ref[...].astype(bf16)

    @pl.when(i < nt)
    def _():
        def body(j, c):
            rows = [xp_ref[pl.ds(src_ref[i * tm + j * GATHER_UNROLL + q], 1), :] for q in range(GATHER_UNROLL)]
            xbuf[pl.ds(pl.multiple_of(j * GATHER_UNROLL, GATHER_UNROLL), GATHER_UNROLL), :] = jnp.concatenate(rows, axis=0)
            return c
        lax.fori_loop(0, tm // GATHER_UNROLL, body, 0)
        lo, hi = _unpack_halves(xbuf[...])
        x = jnp.concatenate([lo.astype(bf16), hi.astype(bf16)], axis=1)
        h = _silu(jnp.dot(x, wgb[...], preferred_element_type=f32)) * \
            jnp.dot(x, wub[...], preferred_element_type=f32)
        o_ref[...] = _pack_halves(jnp.dot(h.astype(bf16), wdb[...], preferred_element_type=f32))

    @pl.when(i >= nt)
    def _():
        o_ref[...] = jnp.zeros_like(o_ref)


def _moe_group(xp, tile_e, n_live, src, n_tiles, w_gate, w_up, w_down, layer):
    T, half = xp.shape
    D = 2 * half
    wspec = lambda shp: pl.BlockSpec((None, None) + shp, lambda i, te, nt, sr: (layer, te[i], 0, 0))
    return pl.pallas_call(
        _moe_group_kernel,
        out_shape=jax.ShapeDtypeStruct((n_tiles * MOE_TM, half), jnp.uint32),
        grid_spec=pltpu.PrefetchScalarGridSpec(
            num_scalar_prefetch=3,
            grid=(n_tiles,),
            in_specs=[pl.BlockSpec(memory_space=pltpu.VMEM), wspec((D, D_E)), wspec((D, D_E)),
                      wspec((D_E, D))],
            out_specs=pl.BlockSpec((MOE_TM, half), lambda i, te, nt, sr: (i, 0)),
            scratch_shapes=[pltpu.VMEM((MOE_TM, half), jnp.uint32), pltpu.VMEM((D, D_E), bf16),
                            pltpu.VMEM((D, D_E), bf16), pltpu.VMEM((D_E, D), bf16)]),
        compiler_params=_params("arbitrary"),
    )(tile_e, n_live, src, xp, w_gate, w_up, w_down)


def _moe_combine_ln_kernel(pos_ref, x_ref, rt_ref, g_ref, b_ref, ys_hbm, o_ref, ob_ref, ybuf, sems):
    i = pl.program_id(0)
    n = pl.num_programs(0)
    tm = x_ref.shape[0]
    T = n * tm
    groups = tm // GATHER_UNROLL

    def issue(tile, slot):
        for k in range(TOPK_IN):
            base = (slot * TOPK_IN + k) * groups

            def body(j, c, k=k, base=base):
                for q in range(GATHER_UNROLL):
                    p = pos_ref[k * T + tile * tm + j * GATHER_UNROLL + q]
                    pltpu.make_async_copy(ys_hbm.at[pl.ds(p, 1), :], ybuf.at[base + j, pl.ds(q, 1), :],
                                          sems.at[slot]).start()
                return c
            lax.fori_loop(0, groups, body, 0)

    @pl.when(i == 0)
    def _():
        issue(0, 0)

    @pl.when(i + 1 < n)
    def _():
        issue(i + 1, (i + 1) % 2)

    slot = i % 2

    base = slot * TOPK_IN * groups

    def wait_group(j, c):
        pltpu.make_async_copy(ys_hbm.at[pl.ds(0, GATHER_UNROLL), :], ybuf.at[base + j], sems.at[slot]).wait()
        return c
    lax.fori_loop(0, TOPK_IN * groups, wait_group, 0)

    def gathered(k):
        return _unpack_halves(ybuf[pl.ds(base + k * groups, groups)].reshape(tm, ybuf.shape[2]))

    lo1, hi1 = gathered(0)
    lo2, hi2 = gathered(1)
    rt = rt_ref[...]
    w1, w2 = rt[:, 0:1], rt[:, 1:2]
    y = jnp.concatenate([w1 * lo1 + w2 * lo2, w1 * hi1 + w2 * hi2], axis=1)
    o = _layer_norm(ALPHA * x_ref[...] + y, g_ref[...], b_ref[...])
    o_ref[...] = o
    ob_ref[...] = o.astype(bf16)


def _moe_combine_ln(x, route, pos, ys, g, b, layer):
    T, D = x.shape
    tm = min(T, LN_TM)
    vec = pl.BlockSpec((None, 1, D), lambda i, p: (layer, 0, 0))
    row = pl.BlockSpec((tm, D), lambda i, p: (i, 0))
    return pl.pallas_call(
        _moe_combine_ln_kernel,
        out_shape=(jax.ShapeDtypeStruct((T, D), f32), jax.ShapeDtypeStruct((T, D), bf16)),
        grid_spec=pltpu.PrefetchScalarGridSpec(
            num_scalar_prefetch=1,
            grid=(T // tm,),
            in_specs=[row, pl.BlockSpec((tm, LANES), lambda i, p: (i, 0)), vec, vec,
                      pl.BlockSpec(memory_space=pl.ANY)],
            out_specs=(row, row),
            scratch_shapes=[pltpu.VMEM((2 * TOPK_IN * tm // GATHER_UNROLL, GATHER_UNROLL, D // 2), jnp.uint32),
                            pltpu.SemaphoreType.DMA((2,))]),
        compiler_params=_params("arbitrary"),
    )(pos, x, route, g.reshape(g.shape[0], 1, D), b.reshape(b.shape[0], 1, D), ys)


def _moe_ln(xf, xp, route, w_gate, w_up, w_down, g, b, layer):
    tile_e, n_live, src, pos, n_tiles = _moe_plan(route, xf.shape[0])
    ys = _moe_group(xp, tile_e, n_live, src, n_tiles, w_gate, w_up, w_down, layer)
    return _moe_combine_ln(xf, route, pos, ys, g, b, layer)


def kernel(x, mem, ab_w_in, rg_conv_w, rg_conv_b, rg_wa, rg_ba, rg_wx, rg_bx, rg_lam, ssd_conv_w, ssd_conv_b, ssd_dt_bias, ssd_a_log, ssd_d, ssd_norm_w, ab_w_out, cd_w_in, s5_a_re, s5_a_im, s5_log_step, s5_b_re, s5_b_im, s5_c_re, s5_c_im, s5_d, s5_glu_w, s5_glu_b, dn_conv_w, dn_a_log, dn_dt_bias, dn_norm_w, cd_w_out, xa_w_q, xa_w_kv, xa_w_o, moe_w_group, moe_b_group, moe_w_expert, moe_b_expert, moe_w_gate, moe_w_up, moe_w_down, ln1_g, ln1_b, ln2_g, ln2_b, ln3_g, ln3_b):
    B, S, D = x.shape
    M = mem.shape[1]
    T = B * S
    xf = x.reshape(T, D)
    xb = None
    memb = mem.reshape(B * M, D).astype(bf16)

    def gate_proj(tail_w):
        if xb is None:
            return _mm_hi(xf, tail_w, with_bf16=True)
        return _mm_hi(xf, tail_w), xb

    for l in range(DEPTH):
        i = l // 2
        if l % 2 == 0:
            dt_tail, xb = gate_proj(_tail_cols(ab_w_in, i, MAIN_AB))
            proj = _mm(xb, ab_w_in, i, MAIN_AB, f32)
            ya = _rglru(proj, B, S, i, rg_conv_w, rg_conv_b, rg_wa, rg_ba, rg_wx, rg_bx, rg_lam)
            yb = _ssd(proj, dt_tail, B, S, i, ssd_conv_w, ssd_conv_b, ssd_dt_bias, ssd_a_log, ssd_d,
                      ssd_norm_w)
            xf, xb = _mm_ln([ya, yb], ab_w_out[i].astype(bf16), xf, ln1_g, ln1_b, l)
        else:
            ab_tail, xb = gate_proj(_tail_cols(cd_w_in, i, MAIN_CD))
            proj = _mm(xb, cd_w_in, i, MAIN_CD, f32)
            yc = _s5(proj, B, S, i, s5_a_re, s5_a_im, s5_log_step, s5_b_re, s5_b_im, s5_c_re, s5_c_im,
                     s5_d, s5_glu_w, s5_glu_b)
            yd = _gdn(proj, ab_tail, B, S, i, dn_conv_w, dn_a_log, dn_dt_bias, dn_norm_w)
            xf, xb = _mm_ln([yc, yd], cd_w_out[i].astype(bf16), xf, ln1_g, ln1_b, l)
        q = _mm_resident(xb, xa_w_q[l].astype(bf16), bf16)
        kv = _mm(memb, xa_w_kv, l, 2 * D, bf16)
        att = _xattn(q, kv, B, S, M)
        xf, xp, route = _mm_ln([att], xa_w_o[l].astype(bf16), xf, ln2_g, ln2_b, l,
                               route_wb=_route_weights(moe_w_group, moe_b_group, moe_w_expert, moe_b_expert, l))
        xf, xb = _moe_ln(xf, xp, route, moe_w_gate, moe_w_up, moe_w_down, ln3_g, ln3_b, l)
    return xf.reshape(B, S, D)
```

```python
import functools
import math

import jax
import jax.numpy as jnp
from jax import lax
from jax.experimental import pallas as pl
from jax.experimental.pallas import tpu as pltpu

f32 = jnp.float32
bf16 = jnp.bfloat16

D_MODEL = 2048
DEPTH = 2
CHUNK = 64
CONV_W = 4
ALPHA = (2 * DEPTH) ** 0.25
LN_EPS = 1e-5
RMS_EPS = 1e-6
W_A = D_MODEL // 2
H_A = 8
BW_A = W_A // H_A
RG_C = 8.0
W_B = D_MODEL
HD_B = 64
H_B = W_B // HD_B
NG_B = 2
N_B = 128
HG_B = H_B // NG_B
CONV_B = W_B + 2 * NG_B * N_B
MAIN_AB = 2 * W_A + W_B + CONV_B
W_C = D_MODEL // 2
GS_C = 16
G_C = W_C // GS_C
P_C = 64
L_C = 16
H_D = 8
DK_D = D_MODEL // 16
DV_D = D_MODEL // 16
W_D = H_D * DV_D
QKV_D = 2 * H_D * DK_D + W_D
MAIN_CD = W_C + QKV_D + W_D
H_X = 4
HD_X = D_MODEL // H_X
NG_E = 4
E_PER = 8
N_EXP = NG_E * E_PER
TOPK_IN = 2
D_E = D_MODEL // 8

LANES = 128
SUBLANES = 8
S5_GB = LANES // GS_C
S5_NB = G_C // S5_GB
VMEM_LIMIT = 56 * 1024 * 1024


def _params(*sem):
    return pltpu.CompilerParams(dimension_semantics=sem, vmem_limit_bytes=VMEM_LIMIT)


def _sigmoid(x):
    return 1.0 / (1.0 + jnp.exp(-x))


def _silu(x):
    return x * _sigmoid(x)


def _softplus(x):
    return jnp.maximum(x, 0.0) + jnp.log(1.0 + jnp.exp(-jnp.abs(x)))


def _gelu_tanh(x):
    return 0.5 * x * (1.0 + jnp.tanh(math.sqrt(2.0 / math.pi) * (x + 0.044715 * (x * x * x))))


def _dot(a, b):
    return jnp.dot(a.astype(bf16), b.astype(bf16), preferred_element_type=f32)


def _dot_nt(a, b):
    return lax.dot_general(a.astype(bf16), b.astype(bf16), (((1,), (1,)), ((), ())),
                           preferred_element_type=f32)


def _dot_tn(a, b):
    return lax.dot_general(a.astype(bf16), b.astype(bf16), (((0,), (0,)), ((), ())),
                           preferred_element_type=f32)


def _split3(a):
    hi = a.astype(bf16)
    r = a - hi.astype(f32)
    mid = r.astype(bf16)
    lo = (r - mid.astype(f32)).astype(bf16)
    return hi, mid, lo


def _dot_exact_lhs(sel, b):
    s = sel.astype(bf16)
    b1, b2, b3 = _split3(b)
    d = functools.partial(jnp.dot, preferred_element_type=f32)
    return d(s, b1) + d(s, b2) + d(s, b3)


def _dot_exact_rhs(a, sel):
    s = sel.astype(bf16)
    a1, a2, a3 = _split3(a)
    d = functools.partial(jnp.dot, preferred_element_type=f32)
    return d(a1, s) + d(a2, s) + d(a3, s)


def _transpose_exact(a, eye):
    a1, a2, a3 = _split3(a)
    e = eye.astype(bf16)
    d = lambda x: lax.dot_general(e, x, (((1,), (1,)), ((), ())), preferred_element_type=f32)
    return d(a1) + d(a2) + d(a3)


def _dot3(a, b):
    a1 = a.astype(bf16)
    a2 = (a - a1.astype(f32)).astype(bf16)
    b1 = b.astype(bf16)
    b2 = (b - b1.astype(f32)).astype(bf16)
    d = functools.partial(jnp.dot, preferred_element_type=f32)
    return d(a1, b1) + (d(a1, b2) + d(a2, b1))


def _iota(shape, axis):
    return lax.broadcasted_iota(jnp.int32, shape, axis)


def _tri(n, strict=False):
    r, c = _iota((n, n), 0), _iota((n, n), 1)
    return (r > c) if strict else (r >= c)


def _mm_kernel(x_ref, w_ref, o_ref):
    o_ref[...] = jnp.dot(x_ref[...], w_ref[...].astype(bf16),
                         preferred_element_type=f32).astype(o_ref.dtype)


def _mm(x, w, layer, n_cols, out_dtype, tn=512):
    M, K = x.shape
    tm = min(M, 2048)
    return pl.pallas_call(
        _mm_kernel,
        out_shape=jax.ShapeDtypeStruct((M, n_cols), out_dtype),
        grid=(M // tm, n_cols // tn),
        in_specs=[pl.BlockSpec((tm, K), lambda i, j: (i, 0)),
                  pl.BlockSpec((None, K, tn), lambda i, j: (layer, 0, j))],
        out_specs=pl.BlockSpec((tm, tn), lambda i, j: (i, j)),
        compiler_params=_params("parallel", "parallel"),
    )(x, w)


def _mm_resident_kernel(x_ref, w_ref, o_ref):
    o_ref[...] = jnp.dot(x_ref[...], w_ref[...], preferred_element_type=f32).astype(o_ref.dtype)


def _mm_resident(x, w, out_dtype):
    M, K = x.shape
    tm = min(M, 1024)
    return pl.pallas_call(
        _mm_resident_kernel,
        out_shape=jax.ShapeDtypeStruct((M, w.shape[1]), out_dtype),
        grid=(M // tm,),
        in_specs=[pl.BlockSpec((tm, K), lambda i: (i, 0)), pl.BlockSpec(memory_space=pltpu.VMEM)],
        out_specs=pl.BlockSpec((tm, w.shape[1]), lambda i: (i, 0)),
        compiler_params=_params("parallel"),
    )(x, w)


def _tail_cols(w, layer, start):
    _, K, N = w.shape
    return lax.slice(w, (layer, 0, start), (layer + 1, K, N)).reshape(K, N - start)


def _mm_hi_kernel(x_ref, w_ref, o_ref, *xb_ref):
    x = x_ref[...]
    o_ref[...] = _dot3(x, w_ref[...])
    for r in xb_ref:
        r[...] = x.astype(bf16)


def _mm_hi(x, w, with_bf16=False):
    M, K = x.shape
    n = w.shape[1]
    tm = min(M, 512)
    row = pl.BlockSpec((tm, K), lambda i: (i, 0))
    out_shape = [jax.ShapeDtypeStruct((M, n), f32)] + ([jax.ShapeDtypeStruct((M, K), bf16)] if with_bf16 else [])
    out = pl.pallas_call(
        _mm_hi_kernel,
        out_shape=out_shape,
        grid=(M // tm,),
        in_specs=[row, pl.BlockSpec((K, n), lambda i: (0, 0))],
        out_specs=[pl.BlockSpec((tm, n), lambda i: (i, 0))] + ([row] if with_bf16 else []),
        compiler_params=_params("parallel"),
    )(x, w)
    return out if with_bf16 else out[0]


def _layer_norm(v, g, b):
    mu = jnp.mean(v, -1, keepdims=True)
    d = v - mu
    var = jnp.mean(d * d, -1, keepdims=True)
    return d * lax.rsqrt(var + LN_EPS) * g + b


HI_MASK = 0xFFFF0000


def _pack_halves(v):
    h = v.shape[1] // 2
    lo = lax.bitcast_convert_type(v[:, :h].astype(bf16).astype(f32), jnp.uint32)
    hi = lax.bitcast_convert_type(v[:, h:].astype(bf16).astype(f32), jnp.uint32)
    return (hi & jnp.uint32(HI_MASK)) | (lo >> 16)


def _unpack_halves(u):
    lo = lax.bitcast_convert_type(u << 16, f32)
    hi = lax.bitcast_convert_type(u & jnp.uint32(HI_MASK), f32)
    return lo, hi


def _route_top2(logits):
    lane = _iota(logits.shape, 1)
    neg = -jnp.inf
    big = jnp.int32(LANES)
    is_g = (lane >= N_EXP) & (lane < N_EXP + NG_E)
    gl = jnp.where(is_g, logits, neg)
    gmax = jnp.max(gl, -1, keepdims=True)
    g_lane = jnp.min(jnp.where(gl == gmax, lane, big), -1, keepdims=True)
    g_prob = 1.0 / jnp.sum(jnp.where(is_g, jnp.exp(gl - gmax), 0.0), -1, keepdims=True)
    e0 = (g_lane - N_EXP) * E_PER
    sel = jnp.where((lane >= e0) & (lane < e0 + E_PER), logits, neg)
    m1 = jnp.max(sel, -1, keepdims=True)
    i1 = jnp.min(jnp.where(sel == m1, lane, big), -1, keepdims=True)
    sel2 = jnp.where(lane == i1, neg, sel)
    m2 = jnp.max(sel2, -1, keepdims=True)
    i2 = jnp.min(jnp.where(sel2 == m2, lane, big), -1, keepdims=True)
    e2 = jnp.exp(m2 - m1)
    w1 = g_prob / (1.0 + e2)
    w2 = g_prob * e2 / (1.0 + e2)
    return (jnp.where(lane == 0, w1, 0.0) + jnp.where(lane == 1, w2, 0.0)
            + jnp.where(lane == 2, i1.astype(f32), 0.0) + jnp.where(lane == 3, i2.astype(f32), 0.0))


def _mm_ln_kernel(*refs, n_parts, routed):
    part_refs = refs[:n_parts]
    w_ref, x_ref, g_ref, b_ref = refs[n_parts:n_parts + 4]
    acc, off = None, 0
    for p in part_refs:
        c = jnp.dot(p[...], w_ref[off:off + p.shape[1], :], preferred_element_type=f32)
        acc = c if acc is None else acc + c
        off += p.shape[1]
    o = _layer_norm(ALPHA * x_ref[...] + acc, g_ref[...], b_ref[...])
    if routed:
        rw_ref, rb_ref, o_ref, o2_ref, rt_ref = refs[n_parts + 4:]
        o2_ref[...] = _pack_halves(o)
        rt_ref[...] = _route_top2(_dot3(o, rw_ref[...]) + rb_ref[...])
    else:
        o_ref, o2_ref = refs[n_parts + 4:]
        o2_ref[...] = o.astype(bf16)
    o_ref[...] = o


def _mm_ln(parts, w, x, g, b, layer, route_wb=None):
    T, D = x.shape
    tm = min(T, 512)
    routed = route_wb is not None
    vec = pl.BlockSpec((None, 1, D), lambda i: (layer, 0, 0))
    row = pl.BlockSpec((tm, D), lambda i: (i, 0))
    in_specs = [pl.BlockSpec((tm, a.shape[1]), lambda i: (i, 0)) for a in parts]
    in_specs += [pl.BlockSpec(memory_space=pltpu.VMEM), row, vec, vec]
    args = [*parts, w, x, g.reshape(g.shape[0], 1, D), b.reshape(b.shape[0], 1, D)]
    if routed:
        in_specs += [pl.BlockSpec((D, LANES), lambda i: (0, 0)), pl.BlockSpec((1, LANES), lambda i: (0, 0))]
        args += list(route_wb)
        out_shape = (jax.ShapeDtypeStruct((T, D), f32), jax.ShapeDtypeStruct((T, D // 2), jnp.uint32),
                     jax.ShapeDtypeStruct((T, LANES), f32))
        out_specs = (row, pl.BlockSpec((tm, D // 2), lambda i: (i, 0)), pl.BlockSpec((tm, LANES), lambda i: (i, 0)))
    else:
        out_shape = (jax.ShapeDtypeStruct((T, D), f32), jax.ShapeDtypeStruct((T, D), bf16))
        out_specs = (row, row)
    return pl.pallas_call(
        functools.partial(_mm_ln_kernel, n_parts=len(parts), routed=routed),
        out_shape=out_shape,
        grid=(T // tm,),
        in_specs=in_specs,
        out_specs=out_specs,
        compiler_params=_params("parallel"),
    )(*args)


def _conv_rows(x_ref, w, hist_ref):
    tt = x_ref.shape[0]
    hist_ref[SUBLANES:SUBLANES + tt, :] = x_ref[...]
    acc = _causal_taps(hist_ref[...], w, tt)
    hist_ref[0:SUBLANES, :] = hist_ref[tt:tt + SUBLANES, :]
    return acc


def _causal_taps(ext, w, tt):
    acc = None
    for k in range(CONV_W):
        back = CONV_W - 1 - k
        rows = (pltpu.roll(ext, back, 0) if back else ext)[SUBLANES:SUBLANES + tt, :]
        term = w[k:k + 1, :] * rows
        acc = term if acc is None else acc + term
    return acc


def _scan_affine(a, u):
    n = a.shape[0]
    row = _iota(a.shape, 0)
    d = 1
    while d < n:
        keep = row >= d
        a_s = pltpu.roll(a, d, 0)
        u_s = pltpu.roll(u, d, 0)
        u = u + jnp.where(keep, a * u_s, 0.0)
        a = jnp.where(keep, a * a_s, a)
        d *= 2
    return a, u


def _rglru_kernel(gate_ref, xa_ref, cw_ref, cb_ref, wa_ref, ba_ref, wx_ref, bx_ref, lam_ref,
                  o_ref, hist_ref, h_ref):
    @pl.when(pl.program_id(1) == 0)
    def _():
        h_ref[...] = jnp.zeros_like(h_ref)
        hist_ref[:, 0:SUBLANES, :] = jnp.zeros((hist_ref.shape[0], SUBLANES, hist_ref.shape[2]), f32)

    decay_rate = -RG_C * _softplus(-lam_ref[...])
    for b in range(xa_ref.shape[0]):
        xc = _conv_rows(xa_ref.at[b], cw_ref[...], hist_ref.at[b]) + cb_ref[...]
        r = _sigmoid(_dot(xc, wa_ref[...]) + ba_ref[...])
        i = _sigmoid(_dot(xc, wx_ref[...]) + bx_ref[...])
        log_a = r * decay_rate
        a = jnp.exp(log_a)
        u = jnp.sqrt(1.0 - jnp.exp(2.0 * log_a)) * (i * xc)
        a_cum, h = _scan_affine(a, u)
        h = h + a_cum * h_ref[b, 0:1, :]
        tt = h.shape[0]
        h_ref[b] = jnp.broadcast_to(h[tt - 1:tt, :], h_ref.shape[1:])
        o_ref[b] = (_gelu_tanh(gate_ref[b]) * h).astype(o_ref.dtype)


def _rglru(proj, B, S, layer, conv_w, conv_b, wa, ba, wx, bx, lam):
    tt = min(S, 256)
    ns = S // tt
    xa_col0 = W_A // BW_A
    vec = lambda a: a.reshape(a.shape[0], 1, W_A)
    vspec = pl.BlockSpec((None, 1, BW_A), lambda h, s: (layer, 0, h))
    wspec = pl.BlockSpec((None, None, BW_A, BW_A), lambda h, s: (layer, h, 0, 0))
    proj3 = proj.reshape(B, S, proj.shape[1])
    out = pl.pallas_call(
        _rglru_kernel,
        out_shape=jax.ShapeDtypeStruct((B, S, W_A), bf16),
        grid=(H_A, ns),
        in_specs=[pl.BlockSpec((B, tt, BW_A), lambda h, s: (0, s, h)),
                  pl.BlockSpec((B, tt, BW_A), lambda h, s: (0, s, xa_col0 + h)),
                  pl.BlockSpec((None, CONV_W, BW_A), lambda h, s: (layer, 0, h)),
                  vspec, wspec, vspec, wspec, vspec, vspec],
        out_specs=pl.BlockSpec((B, tt, BW_A), lambda h, s: (0, s, h)),
        scratch_shapes=[pltpu.VMEM((B, SUBLANES + tt, BW_A), f32), pltpu.VMEM((B, SUBLANES, BW_A), f32)],
        compiler_params=_params("parallel", "arbitrary"),
    )(proj3, proj3, conv_w, vec(conv_b), wa, vec(ba), wx, vec(bx), vec(lam))
    return out.reshape(B * S, W_A)


def _ssd_group(X, Bc, Cc, z, dt, a_neg, d_head, nw, st_ref):
    L = CHUNK
    adt = dt * a_neg
    tri = _tri(L)
    cs = _dot_exact_lhs(tri.astype(f32), adt)
    eye_h = (_iota((HG_B, HG_B), 0) == _iota((HG_B, HG_B), 1)).astype(f32)
    cs_t = _transpose_exact(cs, eye_h)
    cs_last = cs[L - 1:L, :]

    expand = (_iota((HG_B, HG_B * HD_B), 1) // HD_B == _iota((HG_B, HG_B * HD_B), 0)).astype(f32)
    dt_x = _dot_exact_rhs(dt, expand)
    ecs_x = _dot_exact_rhs(jnp.exp(cs), expand)
    dec_x = _dot_exact_rhs(jnp.exp(cs_last - cs), expand)
    cdec_x = _dot_exact_rhs(jnp.broadcast_to(jnp.exp(cs_last), (SUBLANES, HG_B)), expand)[0:1]
    d_x = _dot_exact_rhs(jnp.broadcast_to(d_head, (SUBLANES, HG_B)), expand)[0:1]

    xdt = X * dt_x
    cb = _dot_nt(Cc, Bc)
    xdt_b = xdt.astype(bf16)
    left = _iota((L, 2 * HD_B), 1) < HD_B
    pieces = []
    for j in range(0, HG_B, 2):
        pair = xdt_b[:, j * HD_B:(j + 2) * HD_B]
        outs = []
        for jj in (j, j + 1):
            seg = jnp.where(tri, cs[:, jj:jj + 1] - cs_t[jj:jj + 1, :], -jnp.inf)
            outs.append(jnp.dot((cb * jnp.exp(seg)).astype(bf16), pair, preferred_element_type=f32))
        pieces.append(jnp.where(left, outs[0], outs[1]))
    y_diag = jnp.concatenate(pieces, axis=1)

    state = st_ref[...]
    y_off = ecs_x * _dot(Cc, state)
    st_ref[...] = state * cdec_x + _dot_tn(Bc, xdt * dec_x)

    y = y_diag + y_off + X * d_x
    yg = y * _silu(z)
    return yg * lax.rsqrt(jnp.mean(yg * yg, -1, keepdims=True) + RMS_EPS) * nw


def _ssd_kernel(z_ref, x_ref, b_ref, c_ref, dt_ref, cw_ref, cb_ref, dtb_ref, alog_ref, d_ref, nw_ref,
                o_ref, hx_ref, hb_ref, hc_ref, st_ref):
    GW = W_B // NG_B

    @pl.when(pl.program_id(0) == 0)
    def _():
        st_ref[...] = jnp.zeros_like(st_ref)
        for h in (hx_ref, hb_ref, hc_ref):
            h[:, 0:SUBLANES, :] = jnp.zeros((h.shape[0], SUBLANES, h.shape[2]), f32)

    cw, cbias = cw_ref[...], cb_ref[...]
    c_off = W_B + NG_B * N_B
    a_all = -jnp.exp(alog_ref[...])
    d_all = d_ref[...]
    nw_all = nw_ref[...]
    for b in range(x_ref.shape[0]):
        X = _silu(_conv_rows(x_ref.at[b], cw[:, :W_B], hx_ref.at[b]) + cbias[:, :W_B])
        Bm = _silu(_conv_rows(b_ref.at[b], cw[:, W_B:c_off], hb_ref.at[b]) + cbias[:, W_B:c_off])
        Cm = _silu(_conv_rows(c_ref.at[b], cw[:, c_off:], hc_ref.at[b]) + cbias[:, c_off:])
        dt = _softplus(dt_ref[b] + dtb_ref[...])
        for g in range(NG_B):
            hs = slice(g * HG_B, (g + 1) * HG_B)
            ws = slice(g * GW, (g + 1) * GW)
            ns = slice(g * N_B, (g + 1) * N_B)
            out = _ssd_group(X[:, ws], Bm[:, ns], Cm[:, ns], z_ref[b, :, ws], dt[:, hs], a_all[:, hs],
                             d_all[:, hs], nw_all[:, ws], st_ref.at[b * NG_B + g])
            o_ref[b, :, ws] = out.astype(o_ref.dtype)


def _ssd(proj, dt_tail, B, S, layer, conv_w, conv_b, dt_bias, a_log, d, norm_w):
    L = CHUNK
    nc = S // L
    BC = NG_B * N_B
    per_batch = lambda a: a.reshape(B, S, a.shape[1])
    blk = lambda n, cb: pl.BlockSpec((B, L, n), lambda c: (0, c, cb))
    vec = lambda a: a.reshape(a.shape[0], 1, a.shape[-1])
    par = lambda r, n: pl.BlockSpec((None, r, n), lambda c: (layer, 0, 0))
    proj3 = per_batch(proj)
    out = pl.pallas_call(
        _ssd_kernel,
        out_shape=jax.ShapeDtypeStruct((B, S, W_B), bf16),
        grid=(nc,),
        in_specs=[blk(W_B, 2 * W_A // W_B), blk(W_B, (2 * W_A + W_B) // W_B),
                  blk(BC, (2 * W_A + 2 * W_B) // BC), blk(BC, (2 * W_A + 2 * W_B) // BC + 1),
                  blk(H_B, 0), par(CONV_W, CONV_B), par(1, CONV_B), par(1, H_B), par(1, H_B), par(1, H_B),
                  par(1, W_B)],
        out_specs=blk(W_B, 0),
        scratch_shapes=[pltpu.VMEM((B, SUBLANES + L, W_B), f32), pltpu.VMEM((B, SUBLANES + L, BC), f32),
                        pltpu.VMEM((B, SUBLANES + L, BC), f32), pltpu.VMEM((B * NG_B, N_B, W_B // NG_B), f32)],
        compiler_params=_params("arbitrary"),
    )(proj3, proj3, proj3, proj3, per_batch(dt_tail), conv_w, vec(conv_b), vec(dt_bias), vec(a_log), vec(d),
      vec(norm_w))
    return out.reshape(B * S, W_B)


def _s5_tables(a_re, a_im, log_step, b_re, b_im, c_re, c_im):
    L = L_C
    ar, ai = a_re.astype(f32), a_im.astype(f32)
    step = jnp.exp(log_step.astype(f32))[:, None]
    mag = jnp.exp(ar * step)
    lb_re, lb_im = mag * jnp.cos(ai * step), mag * jnp.sin(ai * step)
    den = ar * ar + ai * ai
    f_re = ((lb_re - 1.0) * ar + lb_im * ai) / den
    f_im = (lb_im * ar - (lb_re - 1.0) * ai) / den
    br, bi = b_re.astype(f32), b_im.astype(f32)
    bb_re = f_re[..., None] * br - f_im[..., None] * bi
    bb_im = f_re[..., None] * bi + f_im[..., None] * br
    cr, ci = c_re.astype(f32), c_im.astype(f32)

    def power(n):
        n = n.astype(f32)[None, :, None]
        m = jnp.exp(ar[:, None, :] * step[:, None, :] * n)
        ang = ai[:, None, :] * step[:, None, :] * n
        return m * jnp.cos(ang), m * jnp.sin(ang)

    j = jnp.arange(L)
    pr, pi = power(j)
    lbr = pr[..., None] * bb_re[:, None] - pi[..., None] * bb_im[:, None]
    lbi = pr[..., None] * bb_im[:, None] + pi[..., None] * bb_re[:, None]
    kern = (jnp.einsum('gop,gjpk->gjko', cr, lbr) - jnp.einsum('gop,gjpk->gjko', ci, lbi))
    lagb = kern.reshape(S5_NB, S5_GB, L, GS_C, GS_C).transpose(0, 2, 1, 3, 4)
    lagb = lagb.reshape(S5_NB, L, LANES, GS_C)
    rev = (L - 1) - j
    bend = jnp.concatenate([jnp.take(lbr, rev, axis=1), jnp.take(lbi, rev, axis=1)], 2)
    bendc = bend.reshape(S5_NB, S5_GB, L, 2 * P_C, GS_C).transpose(0, 2, 1, 4, 3)
    bendc = bendc.reshape(S5_NB, L, LANES, 2 * P_C)
    qr, qi = power(j + 1)
    car_re = cr[:, None] * qr[:, :, None, :] - ci[:, None] * qi[:, :, None, :]
    car_im = -(cr[:, None] * qi[:, :, None, :] + ci[:, None] * qr[:, :, None, :])
    car = jnp.concatenate([car_re, car_im], -1)
    ccarc = car.reshape(S5_NB, S5_GB, L, GS_C, 2 * P_C).transpose(0, 2, 4, 1, 3)
    ccarc = ccarc.reshape(S5_NB, L, 2 * P_C, LANES)
    return lagb, bendc.astype(bf16), ccarc.astype(bf16), power


def _s5_kernel(u_ref, lag_ref, bend_ref, ccar_ref, sc_ref, o_ref, toep_t, bend_t, ccar_t, *, seq_chunks):
    L = L_C
    SW = 2 * P_C

    zero = jnp.zeros((LANES, LANES), bf16)
    own_g = (_iota((LANES, LANES), 0) // GS_C) == (_iota((LANES, LANES), 1) // GS_C)
    lag = [jnp.where(own_g, jnp.concatenate([lag_ref[j]] * S5_GB, axis=1), 0.0).astype(bf16) for j in range(L)]
    for s in range(L):
        for t in range(L):
            toep_t[s * LANES:(s + 1) * LANES, t * LANES:(t + 1) * LANES] = lag[t - s] if t >= s else zero
    same_g = (_iota((LANES, S5_GB * SW), 0) // GS_C) == (_iota((LANES, S5_GB * SW), 1) // SW)
    for s in range(L):
        wide = jnp.concatenate([bend_ref[s]] * S5_GB, axis=1)
        bend_t[s * LANES:(s + 1) * LANES, :] = jnp.where(same_g, wide, jnp.zeros_like(wide))
    col_g = _iota((SW, LANES), 1) // GS_C
    for t in range(L):
        blk = ccar_ref[t]
        for g in range(S5_GB):
            ccar_t[g * SW:(g + 1) * SW, t * LANES:(t + 1) * LANES] = jnp.where(col_g == g, blk, jnp.zeros_like(blk))

    n = o_ref.shape[0] // L
    U = jnp.concatenate([u_ref[pl.ds(l, n, stride=L), :].astype(bf16) for l in range(L)], axis=1)
    Y = jnp.dot(U, toep_t[...], preferred_element_type=f32)
    H_all = jnp.dot(U, bend_t[...], preferred_element_type=f32)
    row = _iota((n, SW), 0) % seq_chunks
    prev = []
    for g in range(S5_GB):
        sl = slice(g * SW, (g + 1) * SW)
        H = H_all[:, sl]
        d, k = 1, 0
        while d < seq_chunks:
            hs = pltpu.roll(H, d, 0)
            sw = pltpu.roll(hs, P_C, 1)
            H = H + jnp.where(row >= d, sc_ref[2 * k:2 * k + 1, sl] * hs + sc_ref[2 * k + 1:2 * k + 2, sl] * sw, 0.0)
            d *= 2
            k += 1
        prev.append(jnp.where(row >= 1, pltpu.roll(H, 1, 0), 0.0).astype(bf16))
    Y = Y + jnp.dot(jnp.concatenate(prev, axis=1), ccar_t[...], preferred_element_type=f32)
    for l in range(L):
        o_ref[pl.ds(l, n, stride=L), :] = Y[:, l * LANES:(l + 1) * LANES]


def _s5_post_kernel(y_ref, u_ref, d_ref, w_ref, b_ref, o_ref):
    y = y_ref[...] + d_ref[...] * u_ref[...]
    g = _gelu_tanh(y)
    o_ref[...] = (g * _sigmoid(_dot(g, w_ref[...]) + b_ref[...])).astype(o_ref.dtype)


def _s5(proj, B, S, layer, a_re, a_im, log_step, b_re, b_im, c_re, c_im, d, glu_w, glu_b):
    T = B * S
    L = L_C
    nch = S // L
    lagb, bendc, ccarc, power = _s5_tables(a_re[layer], a_im[layer], log_step[layer], b_re[layer],
                                           b_im[layer], c_re[layer], c_im[layer])
    nsteps = max(1, (nch - 1).bit_length())
    sr, si = power(L * (2 ** jnp.arange(nsteps)))
    scan_c = jnp.stack([jnp.concatenate([sr, sr], -1), jnp.concatenate([-si, si], -1)], 2)
    scan8 = scan_c.reshape(S5_NB, S5_GB, 2 * nsteps, 2 * P_C).transpose(0, 2, 1, 3)
    scan8 = scan8.reshape(S5_NB, 2 * nsteps, S5_GB * 2 * P_C)
    tab = lambda r, c: pl.BlockSpec((None, L, r, c), lambda g: (g, 0, 0, 0))
    y = pl.pallas_call(
        functools.partial(_s5_kernel, seq_chunks=nch),
        out_shape=jax.ShapeDtypeStruct((T, W_C), f32),
        grid=(S5_NB,),
        in_specs=[pl.BlockSpec((T, LANES), lambda g: (0, g)),
                  tab(LANES, GS_C), tab(LANES, 2 * P_C), tab(2 * P_C, LANES),
                  pl.BlockSpec((None, 2 * nsteps, S5_GB * 2 * P_C), lambda g: (g, 0, 0))],
        out_specs=pl.BlockSpec((T, LANES), lambda g: (0, g)),
        scratch_shapes=[pltpu.VMEM((L * LANES, L * LANES), bf16),
                        pltpu.VMEM((L * LANES, S5_GB * 2 * P_C), bf16),
                        pltpu.VMEM((S5_GB * 2 * P_C, L * LANES), bf16)],
        compiler_params=_params("parallel"),
    )(proj, lagb, bendc, ccarc, scan8)
    tm = min(T, 1024)
    vec = pl.BlockSpec((None, 1, W_C), lambda i: (layer, 0, 0))
    return pl.pallas_call(
        _s5_post_kernel,
        out_shape=jax.ShapeDtypeStruct((T, W_C), bf16),
        grid=(T // tm,),
        in_specs=[pl.BlockSpec((tm, W_C), lambda i: (i, 0)),
                  pl.BlockSpec((tm, W_C), lambda i: (i, 0)),
                  vec,
                  pl.BlockSpec((None, W_C, W_C), lambda i: (layer, 0, 0)),
                  vec],
        out_specs=pl.BlockSpec((tm, W_C), lambda i: (i, 0)),
        compiler_params=_params("parallel"),
    )(y, proj, d.reshape(d.shape[0], 1, W_C), glu_w, glu_b.reshape(glu_b.shape[0], 1, W_C))


def _gdn_prep_kernel(q_ref, k_ref, v_ref, qp_ref, kp_ref, vp_ref, ab_ref, cwq_ref, cwk_ref, cwv_ref,
                     alog_ref, dtb_ref, w_ref, u_ref, qd_ref, kd_ref, qk_ref, ge_ref, hist_ref):
    L = CHUNK
    nb = q_ref.shape[0]
    has_prev = (pl.program_id(0) > 0).astype(f32)

    def conv(cur_ref, prev_ref, w_ref_, b, slot):
        hist = hist_ref.at[b * 3 + slot]
        hist[0:SUBLANES, :] = prev_ref[b] * has_prev
        hist[SUBLANES:SUBLANES + L, :] = cur_ref[b]
        return _silu(_causal_taps(hist[...], w_ref_[...], L))

    tri = _tri(L)
    tri_s = _tri(L, strict=True)
    eye_h = (_iota((H_D, H_D), 0) == _iota((H_D, H_D), 1)).astype(f32)
    eye = (_iota((L, L), 0) == _iota((L, L), 1)).astype(f32)
    q_all, k_all, v_all, beta_all, gcs_all, gcs_t = [], [], [], [], [], []
    for b in range(nb):
        q_all.append(conv(q_ref, qp_ref, cwq_ref, b, 0))
        k_all.append(conv(k_ref, kp_ref, cwk_ref, b, 1))
        v_all.append(conv(v_ref, vp_ref, cwv_ref, b, 2))
        ab = ab_ref[b]
        g = -jnp.exp(alog_ref[...]) * _softplus(ab[:, 0:H_D] + dtb_ref[...])
        beta_all.append(_sigmoid(ab[:, H_D:2 * H_D]))
        gcs_all.append(_dot_exact_lhs(tri.astype(f32), g))
        gcs_t.append(_transpose_exact(gcs_all[b], eye_h))
        ge_ref[b] = jnp.concatenate([jnp.exp(gcs_all[b]), jnp.zeros((L, LANES - H_D), f32)], axis=1)

    pairs = [(b, h) for b in range(nb) for h in range(H_D)]
    cols = lambda h: slice(h * DK_D, (h + 1) * DK_D)
    unit = lambda x: x * lax.rsqrt(jnp.sum(x * x, -1, keepdims=True) + 1e-6)
    gcs = [gcs_all[b][:, h:h + 1] for b, h in pairs]
    eg = [jnp.exp(g) for g in gcs]
    beta = [beta_all[b][:, h:h + 1] for b, h in pairs]
    qs = [unit(q_all[b][:, cols(h)]) * (DK_D ** -0.5) for b, h in pairs]
    ks = [unit(k_all[b][:, cols(h)]) for b, h in pairs]
    kbs = [k * bt for k, bt in zip(ks, beta)]
    decay = [jnp.exp(jnp.where(tri, g - gcs_t[b][h:h + 1, :], -jnp.inf)) for (b, h), g in zip(pairs, gcs)]
    kk = [_dot_nt(kb, k) for kb, k in zip(kbs, ks)]
    qk = [_dot_nt(q, k) for q, k in zip(qs, ks)]
    pw = [jnp.where(tri_s, -(a * d), 0.0) for a, d in zip(kk, decay)]
    inv = [eye + p for p in pw]
    for _ in range(int(math.log2(L)) - 1):
        pw = [_dot(p, p) for p in pw]
        inv = [a + _dot(a, p) for a, p in zip(inv, pw)]
    rhs = [jnp.concatenate([v_all[b][:, cols(h)] * bt, kb * e], axis=1)
           for (b, h), bt, kb, e in zip(pairs, beta, kbs, eg)]
    sol = [_dot(a, r) for a, r in zip(inv, rhs)]
    for i, (b, h) in enumerate(pairs):
        u_ref[b, :, cols(h)] = sol[i][:, :DV_D]
        w_ref[b, :, cols(h)] = sol[i][:, DV_D:]
        qk_ref[b, :, h * L:(h + 1) * L] = jnp.where(tri, qk[i] * decay[i], 0.0)
        qd_ref[b, :, cols(h)] = qs[i] * eg[i]
        kd_ref[b, :, cols(h)] = ks[i] * jnp.exp(gcs[i][L - 1:L, :] - gcs[i])


def _gdn_scan_kernel(w_ref, u_ref, qd_ref, kd_ref, qk_ref, ge_ref, z_ref, nw_ref, o_ref, st_ref):
    L = CHUNK

    @pl.when(pl.program_id(0) == 0)
    def _():
        st_ref[...] = jnp.zeros_like(st_ref)

    nw = nw_ref[...]
    pairs = [(b, h) for b in range(w_ref.shape[0]) for h in range(H_D)]
    cols = lambda h: slice(h * DK_D, (h + 1) * DK_D)
    state = [st_ref[b * H_D + h] for b, h in pairs]
    v_new = [u_ref[b, :, cols(h)] - _dot(w_ref[b, :, cols(h)], s) for (b, h), s in zip(pairs, state)]
    o = [_dot(qd_ref[b, :, cols(h)], s) + _dot(qk_ref[b, :, h * L:(h + 1) * L], v)
         for (b, h), s, v in zip(pairs, state, v_new)]
    for (b, h), s, v in zip(pairs, state, v_new):
        st_ref[b * H_D + h] = s * ge_ref[b, L - 1:L, h:h + 1] + _dot_tn(kd_ref[b, :, cols(h)], v)
    for (b, h), y in zip(pairs, o):
        y = y * lax.rsqrt(jnp.mean(y * y, -1, keepdims=True) + RMS_EPS) * nw
        o_ref[b, :, cols(h)] = (y * _silu(z_ref[b, :, cols(h)])).astype(o_ref.dtype)


def _gdn(proj, ab_tail, B, S, layer, conv_w, a_log, dt_bias, norm_w):
    L = CHUNK
    nc = S // L
    per8 = L // SUBLANES
    q_c0 = W_C // W_D
    z_c0 = (W_C + QKV_D) // W_D
    proj3 = proj.reshape(B, S, proj.shape[1])
    blk = lambda n, cb=0: pl.BlockSpec((B, L, n), lambda c: (0, c, cb))
    prev = lambda cb: pl.BlockSpec((B, SUBLANES, W_D), lambda c: (0, jnp.maximum(c * per8 - 1, 0), cb))
    cw = lambda cb: pl.BlockSpec((None, CONV_W, W_D), lambda c: (layer, 0, cb))
    v8 = pl.BlockSpec((None, 1, H_D), lambda c: (layer, 0, 0))
    shp = lambda n: jax.ShapeDtypeStruct((B, S, n), f32)
    w_c, u_c, q_dec, k_dec, qk, gexp = pl.pallas_call(
        _gdn_prep_kernel,
        out_shape=(shp(W_D), shp(W_D), shp(W_D), shp(W_D), shp(H_D * L), shp(LANES)),
        grid=(nc,),
        in_specs=[blk(W_D, q_c0), blk(W_D, q_c0 + 1), blk(W_D, q_c0 + 2), prev(q_c0), prev(q_c0 + 1),
                  prev(q_c0 + 2), blk(2 * H_D), cw(0), cw(1), cw(2), v8, v8],
        out_specs=(blk(W_D), blk(W_D), blk(W_D), blk(W_D), blk(H_D * L), blk(LANES)),
        scratch_shapes=[pltpu.VMEM((3 * B, SUBLANES + L, W_D), f32)],
        compiler_params=_params("parallel"),
    )(proj3, proj3, proj3, proj3, proj3, proj3, ab_tail.reshape(B, S, 2 * H_D), conv_w, conv_w, conv_w,
      a_log.reshape(a_log.shape[0], 1, H_D), dt_bias.reshape(dt_bias.shape[0], 1, H_D))
    out = pl.pallas_call(
        _gdn_scan_kernel,
        out_shape=jax.ShapeDtypeStruct((B, S, W_D), bf16),
        grid=(nc,),
        in_specs=[blk(W_D), blk(W_D), blk(W_D), blk(W_D), blk(H_D * L), blk(LANES), blk(W_D, z_c0),
                  pl.BlockSpec((None, 1, DV_D), lambda c: (layer, 0, 0))],
        out_specs=blk(W_D),
        scratch_shapes=[pltpu.VMEM((B * H_D, DK_D, DV_D), f32)],
        compiler_params=_params("arbitrary"),
    )(w_c, u_c, q_dec, k_dec, qk, gexp, proj3, norm_w.reshape(norm_w.shape[0], 1, DV_D))
    return out.reshape(B * S, W_D)


def _xattn_kernel(q_ref, k_ref, v_ref, o_ref):
    s = _dot_nt(q_ref[...], k_ref[...]) * (HD_X ** -0.5)
    m = jnp.max(s, -1, keepdims=True)
    p = jnp.exp(s - m)
    p = p / jnp.sum(p, -1, keepdims=True)
    o_ref[...] = jnp.dot(p.astype(bf16), v_ref[...], preferred_element_type=f32).astype(o_ref.dtype)


def _xattn(q, kv, B, S, M):
    T = B * S
    tq = min(S, 1024)
    nq = S // tq
    return pl.pallas_call(
        _xattn_kernel,
        out_shape=jax.ShapeDtypeStruct((T, D_MODEL), bf16),
        grid=(B, nq, H_X),
        in_specs=[pl.BlockSpec((tq, HD_X), lambda b, i, h: (b * nq + i, h)),
                  pl.BlockSpec((M, HD_X), lambda b, i, h: (b, h)),
                  pl.BlockSpec((M, HD_X), lambda b, i, h: (b, H_X + h))],
        out_specs=pl.BlockSpec((tq, HD_X), lambda b, i, h: (b * nq + i, h)),
        compiler_params=_params("parallel", "parallel", "parallel"),
    )(q, kv, kv)


def _route_weights(w_group, b_group, w_expert, b_expert, layer):
    D = w_group.shape[1]
    pad = LANES - N_EXP - NG_E
    w = jnp.concatenate([w_expert[layer], w_group[layer], jnp.zeros((D, pad), f32)], 1)
    b = jnp.concatenate([b_expert[layer], b_group[layer], jnp.zeros((pad,), f32)])[None, :]
    return w, b


MOE_TM = 256
LN_TM = 256


def _moe_plan(route, T):
    n_slots = TOPK_IN * T
    n_tiles = n_slots // MOE_TM + N_EXP
    ids = route[:, 2:2 + TOPK_IN].astype(jnp.int32)
    e_flat = ids.T.reshape(n_slots)
    onehot = (e_flat[:, None] == jnp.arange(N_EXP, dtype=jnp.int32)[None, :]).astype(jnp.int32)
    csum = jnp.cumsum(onehot, axis=0)
    rank = jnp.sum(onehot * csum, axis=1) - 1
    counts = csum[-1]
    tiles_per = (counts + MOE_TM - 1) // MOE_TM
    tile_end = jnp.cumsum(tiles_per)
    tile_start = tile_end - tiles_per
    dest = jnp.sum(onehot * tile_start[None, :], axis=1) * MOE_TM + rank
    tok = jnp.arange(n_slots, dtype=jnp.int32) % T
    src = jnp.zeros((n_tiles * MOE_TM,), jnp.int32).at[dest].set(tok, unique_indices=True)
    j = jnp.arange(n_tiles, dtype=jnp.int32)
    tile_e = jnp.sum((j[:, None] >= tile_end[None, :]).astype(jnp.int32), axis=1)
    last_e = jnp.max(jnp.where(counts > 0, jnp.arange(N_EXP, dtype=jnp.int32), 0))
    tile_e = jnp.minimum(tile_e, last_e)
    n_live = tile_end[-1:].astype(jnp.int32)
    return tile_e, n_live, src, dest.astype(jnp.int32), n_tiles


GATHER_UNROLL = 8


def _moe_group_kernel(tile_e_ref, nt_ref, src_ref, xp_ref, wg_ref, wu_ref, wd_ref, o_ref, xbuf, wgb, wub, wdb):
    i = pl.program_id(0)
    nt = nt_ref[0]
    tm = xbuf.shape[0]

    @pl.when((i < nt) & ((i == 0) | (tile_e_ref[i] != tile_e_ref[jnp.maximum(i - 1, 0)])))
    def _():
        wgb[...] = wg_ref[...].astype(bf16)
        wub[...] = wu_ref[...].astype(bf16)
        wdb[...] = wd_ref[...].astype(bf16)

    @pl.when(i < nt)
    def _():
        def body(j, c):
            rows = [xp_ref[pl.ds(src_ref[i * tm + j * GATHER_UNROLL + q], 1), :] for q in range(GATHER_UNROLL)]
            xbuf[pl.ds(pl.multiple_of(j * GATHER_UNROLL, GATHER_UNROLL), GATHER_UNROLL), :] = jnp.concatenate(rows, axis=0)
            return c
        lax.fori_loop(0, tm // GATHER_UNROLL, body, 0)
        lo, hi = _unpack_halves(xbuf[...])
        x = jnp.concatenate([lo.astype(bf16), hi.astype(bf16)], axis=1)
        h = _silu(jnp.dot(x, wgb[...], preferred_element_type=f32)) * \
            jnp.dot(x, wub[...], preferred_element_type=f32)
        o_ref[...] = _pack_halves(jnp.dot(h.astype(bf16), wdb[...], preferred_element_type=f32))

    @pl.when(i >= nt)
    def _():
        o_ref[...] = jnp.zeros_like(o_ref)


def _moe_group(xp, tile_e, n_live, src, n_tiles, w_gate, w_up, w_down, layer):
    T, half = xp.shape
    D = 2 * half
    wspec = lambda shp: pl.BlockSpec((None, None) + shp, lambda i, te, nt, sr: (layer, te[i], 0, 0))
    return pl.pallas_call(
        _moe_group_kernel,
        out_shape=jax.ShapeDtypeStruct((n_tiles * MOE_TM, half), jnp.uint32),
        grid_spec=pltpu.PrefetchScalarGridSpec(
            num_scalar_prefetch=3,
            grid=(n_tiles,),
            in_specs=[pl.BlockSpec(memory_space=pltpu.VMEM), wspec((D, D_E)), wspec((D, D_E)),
                      wspec((D_E, D))],
            out_specs=pl.BlockSpec((MOE_TM, half), lambda i, te, nt, sr: (i, 0)),
            scratch_shapes=[pltpu.VMEM((MOE_TM, half), jnp.uint32), pltpu.VMEM((D, D_E), bf16),
                            pltpu.VMEM((D, D_E), bf16), pltpu.VMEM((D_E, D), bf16)]),
        compiler_params=_params("arbitrary"),
    )(tile_e, n_live, src, xp, w_gate, w_up, w_down)


def _moe_combine_ln_kernel(pos_ref, x_ref, rt_ref, g_ref, b_ref, ys_hbm, o_ref, ob_ref, ybuf, sems):
    i = pl.program_id(0)
    n = pl.num_programs(0)
    tm = x_ref.shape[0]
    T = n * tm
    groups = tm // GATHER_UNROLL

    def issue(tile, slot):
        for k in range(TOPK_IN):
            base = (slot * TOPK_IN + k) * groups

            def body(j, c, k=k, base=base):
                for q in range(GATHER_UNROLL):
                    p = pos_ref[k * T + tile * tm + j * GATHER_UNROLL + q]
                    pltpu.make_async_copy(ys_hbm.at[pl.ds(p, 1), :], ybuf.at[base + j, pl.ds(q, 1), :],
                                          sems.at[slot]).start()
                return c
            lax.fori_loop(0, groups, body, 0)

    @pl.when(i == 0)
    def _():
        issue(0, 0)

    @pl.when(i + 1 < n)
    def _():
        issue(i + 1, (i + 1) % 2)

    slot = i % 2

    base = slot * TOPK_IN * groups

    def wait_group(j, c):
        pltpu.make_async_copy(ys_hbm.at[pl.ds(0, GATHER_UNROLL), :], ybuf.at[base + j], sems.at[slot]).wait()
        return c
    lax.fori_loop(0, TOPK_IN * groups, wait_group, 0)

    def gathered(k):
        return _unpack_halves(ybuf[pl.ds(base + k * groups, groups)].reshape(tm, ybuf.shape[2]))

    lo1, hi1 = gathered(0)
    lo2, hi2 = gathered(1)
    rt = rt_ref[...]
    w1, w2 = rt[:, 0:1], rt[:, 1:2]
    y = jnp.concatenate([w1 * lo1 + w2 * lo2, w1 * hi1 + w2 * hi2], axis=1)
    o = _layer_norm(ALPHA * x_ref[...] + y, g_ref[...], b_ref[...])
    o_ref[...] = o
    ob_ref[...] = o.astype(bf16)


def _moe_combine_ln(x, route, pos, ys, g, b, layer):
    T, D = x.shape
    tm = min(T, LN_TM)
    vec = pl.BlockSpec((None, 1, D), lambda i, p: (layer, 0, 0))
    row = pl.BlockSpec((tm, D), lambda i, p: (i, 0))
    return pl.pallas_call(
        _moe_combine_ln_kernel,
        out_shape=(jax.ShapeDtypeStruct((T, D), f32), jax.ShapeDtypeStruct((T, D), bf16)),
        grid_spec=pltpu.PrefetchScalarGridSpec(
            num_scalar_prefetch=1,
            grid=(T // tm,),
            in_specs=[row, pl.BlockSpec((tm, LANES), lambda i, p: (i, 0)), vec, vec,
                      pl.BlockSpec(memory_space=pl.ANY)],
            out_specs=(row, row),
            scratch_shapes=[pltpu.VMEM((2 * TOPK_IN * tm // GATHER_UNROLL, GATHER_UNROLL, D // 2), jnp.uint32),
                            pltpu.SemaphoreType.DMA((2,))]),
        compiler_params=_params("arbitrary"),
    )(pos, x, route, g.reshape(g.shape[0], 1, D), b.reshape(b.shape[0], 1, D), ys)


def _moe_ln(xf, xp, route, w_gate, w_up, w_down, g, b, layer):
    tile_e, n_live, src, pos, n_tiles = _moe_plan(route, xf.shape[0])
    ys = _moe_group(xp, tile_e, n_live, src, n_tiles, w_gate, w_up, w_down, layer)
    return _moe_combine_ln(xf, route, pos, ys, g, b, layer)


def kernel(x, mem, ab_w_in, rg_conv_w, rg_conv_b, rg_wa, rg_ba, rg_wx, rg_bx, rg_lam, ssd_conv_w, ssd_conv_b, ssd_dt_bias, ssd_a_log, ssd_d, ssd_norm_w, ab_w_out, cd_w_in, s5_a_re, s5_a_im, s5_log_step, s5_b_re, s5_b_im, s5_c_re, s5_c_im, s5_d, s5_glu_w, s5_glu_b, dn_conv_w, dn_a_log, dn_dt_bias, dn_norm_w, cd_w_out, xa_w_q, xa_w_kv, xa_w_o, moe_w_group, moe_b_group, moe_w_expert, moe_b_expert, moe_w_gate, moe_w_up, moe_w_down, ln1_g, ln1_b, ln2_g, ln2_b, ln3_g, ln3_b):
    B, S, D = x.shape
    M = mem.shape[1]
    T = B * S
    xf = x.reshape(T, D)
    xb = None
    memb = mem.reshape(B * M, D).astype(bf16)

    def gate_proj(tail_w):
        if xb is None:
            return _mm_hi(xf, tail_w, with_bf16=True)
        return _mm_hi(xf, tail_w), xb

    for l in range(DEPTH):
        i = l // 2
        if l % 2 == 0:
            dt_tail, xb = gate_proj(_tail_cols(ab_w_in, i, MAIN_AB))
            proj = _mm(xb, ab_w_in, i, MAIN_AB, f32)
            ya = _rglru(proj, B, S, i, rg_conv_w, rg_conv_b, rg_wa, rg_ba, rg_wx, rg_bx, rg_lam)
            yb = _ssd(proj, dt_tail, B, S, i, ssd_conv_w, ssd_conv_b, ssd_dt_bias, ssd_a_log, ssd_d,
                      ssd_norm_w)
            xf, xb = _mm_ln([ya, yb], ab_w_out[i].astype(bf16), xf, ln1_g, ln1_b, l)
        else:
            ab_tail, xb = gate_proj(_tail_cols(cd_w_in, i, MAIN_CD))
            proj = _mm(xb, cd_w_in, i, MAIN_CD, f32)
            yc = _s5(proj, B, S, i, s5_a_re, s5_a_im, s5_log_step, s5_b_re, s5_b_im, s5_c_re, s5_c_im,
                     s5_d, s5_glu_w, s5_glu_b)
            yd = _gdn(proj, ab_tail, B, S, i, dn_conv_w, dn_a_log, dn_dt_bias, dn_norm_w)
            xf, xb = _mm_ln([yc, yd], cd_w_out[i].astype(bf16), xf, ln1_g, ln1_b, l)
        q = _mm_resident(xb, xa_w_q[l].astype(bf16), bf16)
        kv = _mm(memb, xa_w_kv, l, 2 * D, bf16)
        att = _xattn(q, kv, B, S, M)
        xf, xp, route = _mm_ln([att], xa_w_o[l].astype(bf16), xf, ln2_g, ln2_b, l,
                               route_wb=_route_weights(moe_w_group, moe_b_group, moe_w_expert, moe_b_expert, l))
        xf, xb = _moe_ln(xf, xp, route, moe_w_gate, moe_w_up, moe_w_down, ln3_g, ln3_b, l)
    return xf.reshape(B, S, D)
```

```python
import functools
import math

import jax
import jax.numpy as jnp
from jax import lax
from jax.experimental import pallas as pl
from jax.experimental.pallas import tpu as pltpu

f32 = jnp.float32
bf16 = jnp.bfloat16

D_MODEL = 2048
DEPTH = 2
CHUNK = 64
CONV_W = 4
ALPHA = (2 * DEPTH) ** 0.25
LN_EPS = 1e-5
RMS_EPS = 1e-6
W_A = D_MODEL // 2
H_A = 8
BW_A = W_A // H_A
RG_C = 8.0
W_B = D_MODEL
HD_B = 64
H_B = W_B // HD_B
NG_B = 2
N_B = 128
HG_B = H_B // NG_B
CONV_B = W_B + 2 * NG_B * N_B
MAIN_AB = 2 * W_A + W_B + CONV_B
W_C = D_MODEL // 2
GS_C = 16
G_C = W_C // GS_C
P_C = 64
L_C = 16
H_D = 8
DK_D = D_MODEL // 16
DV_D = D_MODEL // 16
W_D = H_D * DV_D
QKV_D = 2 * H_D * DK_D + W_D
MAIN_CD = W_C + QKV_D + W_D
H_X = 4
HD_X = D_MODEL // H_X
NG_E = 4
E_PER = 8
N_EXP = NG_E * E_PER
TOPK_IN = 2
D_E = D_MODEL // 8

LANES = 128
SUBLANES = 8
S5_GB = LANES // GS_C
S5_NB = G_C // S5_GB
VMEM_LIMIT = 56 * 1024 * 1024


def _params(*sem):
    return pltpu.CompilerParams(dimension_semantics=sem, vmem_limit_bytes=VMEM_LIMIT)


def _sigmoid(x):
    return 1.0 / (1.0 + jnp.exp(-x))


def _silu(x):
    return x * _sigmoid(x)


def _softplus(x):
    return jnp.maximum(x, 0.0) + jnp.log(1.0 + jnp.exp(-jnp.abs(x)))


def _gelu_tanh(x):
    return 0.5 * x * (1.0 + jnp.tanh(math.sqrt(2.0 / math.pi) * (x + 0.044715 * (x * x * x))))


def _dot(a, b):
    return jnp.dot(a.astype(bf16), b.astype(bf16), preferred_element_type=f32)


def _dot_nt(a, b):
    return lax.dot_general(a.astype(bf16), b.astype(bf16), (((1,), (1,)), ((), ())),
                           preferred_element_type=f32)


def _dot_tn(a, b):
    return lax.dot_general(a.astype(bf16), b.astype(bf16), (((0,), (0,)), ((), ())),
                           preferred_element_type=f32)


def _split3(a):
    hi = a.astype(bf16)
    r = a - hi.astype(f32)
    mid = r.astype(bf16)
    lo = (r - mid.astype(f32)).astype(bf16)
    return hi, mid, lo


def _dot_exact_lhs(sel, b):
    s = sel.astype(bf16)
    b1, b2, b3 = _split3(b)
    d = functools.partial(jnp.dot, preferred_element_type=f32)
    return d(s, b1) + d(s, b2) + d(s, b3)


def _dot_exact_rhs(a, sel):
    s = sel.astype(bf16)
    a1, a2, a3 = _split3(a)
    d = functools.partial(jnp.dot, preferred_element_type=f32)
    return d(a1, s) + d(a2, s) + d(a3, s)


def _transpose_exact(a, eye):
    a1, a2, a3 = _split3(a)
    e = eye.astype(bf16)
    d = lambda x: lax.dot_general(e, x, (((1,), (1,)), ((), ())), preferred_element_type=f32)
    return d(a1) + d(a2) + d(a3)


def _dot3(a, b):
    a1 = a.astype(bf16)
    a2 = (a - a1.astype(f32)).astype(bf16)
    b1 = b.astype(bf16)
    b2 = (b - b1.astype(f32)).astype(bf16)
    d = functools.partial(jnp.dot, preferred_element_type=f32)
    return d(a1, b1) + (d(a1, b2) + d(a2, b1))


def _iota(shape, axis):
    return lax.broadcasted_iota(jnp.int32, shape, axis)


def _tri(n, strict=False):
    r, c = _iota((n, n), 0), _iota((n, n), 1)
    return (r > c) if strict else (r >= c)


def _mm_kernel(x_ref, w_ref, o_ref):
    o_ref[...] = jnp.dot(x_ref[...], w_ref[...].astype(bf16),
                         preferred_element_type=f32).astype(o_ref.dtype)


def _mm(x, w, layer, n_cols, out_dtype, tn=512):
    M, K = x.shape
    tm = min(M, 2048)
    return pl.pallas_call(
        _mm_kernel,
        out_shape=jax.ShapeDtypeStruct((M, n_cols), out_dtype),
        grid=(M // tm, n_cols // tn),
        in_specs=[pl.BlockSpec((tm, K), lambda i, j: (i, 0)),
                  pl.BlockSpec((None, K, tn), lambda i, j: (layer, 0, j))],
        out_specs=pl.BlockSpec((tm, tn), lambda i, j: (i, j)),
        compiler_params=_params("parallel", "parallel"),
    )(x, w)


def _mm_resident_kernel(x_ref, w_ref, o_ref):
    o_ref[...] = jnp.dot(x_ref[...], w_ref[...], preferred_element_type=f32).astype(o_ref.dtype)


def _mm_resident(x, w, out_dtype):
    M, K = x.shape
    tm = min(M, 1024)
    return pl.pallas_call(
        _mm_resident_kernel,
        out_shape=jax.ShapeDtypeStruct((M, w.shape[1]), out_dtype),
        grid=(M // tm,),
        in_specs=[pl.BlockSpec((tm, K), lambda i: (i, 0)), pl.BlockSpec(memory_space=pltpu.VMEM)],
        out_specs=pl.BlockSpec((tm, w.shape[1]), lambda i: (i, 0)),
        compiler_params=_params("parallel"),
    )(x, w)


def _tail_cols(w, layer, start):
    _, K, N = w.shape
    return lax.slice(w, (layer, 0, start), (layer + 1, K, N)).reshape(K, N - start)


def _mm_hi_kernel(x_ref, w_ref, o_ref, *xb_ref):
    x = x_ref[...]
    o_ref[...] = _dot3(x, w_ref[...])
    for r in xb_ref:
        r[...] = x.astype(bf16)


def _mm_hi(x, w, with_bf16=False):
    M, K = x.shape
    n = w.shape[1]
    tm = min(M, 512)
    row = pl.BlockSpec((tm, K), lambda i: (i, 0))
    out_shape = [jax.ShapeDtypeStruct((M, n), f32)] + ([jax.ShapeDtypeStruct((M, K), bf16)] if with_bf16 else [])
    out = pl.pallas_call(
        _mm_hi_kernel,
        out_shape=out_shape,
        grid=(M // tm,),
        in_specs=[row, pl.BlockSpec((K, n), lambda i: (0, 0))],
        out_specs=[pl.BlockSpec((tm, n), lambda i: (i, 0))] + ([row] if with_bf16 else []),
        compiler_params=_params("parallel"),
    )(x, w)
    return out if with_bf16 else out[0]


def _layer_norm(v, g, b):
    mu = jnp.mean(v, -1, keepdims=True)
    d = v - mu
    var = jnp.mean(d * d, -1, keepdims=True)
    return d * lax.rsqrt(var + LN_EPS) * g + b


HI_MASK = 0xFFFF0000


def _pack_halves(v):
    h = v.shape[1] // 2
    lo = lax.bitcast_convert_type(v[:, :h].astype(bf16).astype(f32), jnp.uint32)
    hi = lax.bitcast_convert_type(v[:, h:].astype(bf16).astype(f32), jnp.uint32)
    return (hi & jnp.uint32(HI_MASK)) | (lo >> 16)


def _unpack_halves(u):
    lo = lax.bitcast_convert_type(u << 16, f32)
    hi = lax.bitcast_convert_type(u & jnp.uint32(HI_MASK), f32)
    return lo, hi


def _route_top2(logits):
    lane = _iota(logits.shape, 1)
    neg = -jnp.inf
    big = jnp.int32(LANES)
    is_g = (lane >= N_EXP) & (lane < N_EXP + NG_E)
    gl = jnp.where(is_g, logits, neg)
    gmax = jnp.max(gl, -1, keepdims=True)
    g_lane = jnp.min(jnp.where(gl == gmax, lane, big), -1, keepdims=True)
    g_prob = 1.0 / jnp.sum(jnp.where(is_g, jnp.exp(gl - gmax), 0.0), -1, keepdims=True)
    e0 = (g_lane - N_EXP) * E_PER
    sel = jnp.where((lane >= e0) & (lane < e0 + E_PER), logits, neg)
    m1 = jnp.max(sel, -1, keepdims=True)
    i1 = jnp.min(jnp.where(sel == m1, lane, big), -1, keepdims=True)
    sel2 = jnp.where(lane == i1, neg, sel)
    m2 = jnp.max(sel2, -1, keepdims=True)
    i2 = jnp.min(jnp.where(sel2 == m2, lane, big), -1, keepdims=True)
    e2 = jnp.exp(m2 - m1)
    w1 = g_prob / (1.0 + e2)
    w2 = g_prob * e2 / (1.0 + e2)
    return (jnp.where(lane == 0, w1, 0.0) + jnp.where(lane == 1, w2, 0.0)
            + jnp.where(lane == 2, i1.astype(f32), 0.0) + jnp.where(lane == 3, i2.astype(f32), 0.0))


def _mm_ln_kernel(*refs, n_parts, routed):
    part_refs = refs[:n_parts]
    w_ref, x_ref, g_ref, b_ref = refs[n_parts:n_parts + 4]
    acc, off = None, 0
    for p in part_refs:
        c = jnp.dot(p[...], w_ref[off:off + p.shape[1], :], preferred_element_type=f32)
        acc = c if acc is None else acc + c
        off += p.shape[1]
    o = _layer_norm(ALPHA * x_ref[...] + acc, g_ref[...], b_ref[...])
    if routed:
        rw_ref, rb_ref, o_ref, o2_ref, rt_ref = refs[n_parts + 4:]
        o2_ref[...] = _pack_halves(o)
        rt_ref[...] = _route_top2(_dot3(o, rw_ref[...]) + rb_ref[...])
    else:
        o_ref, o2_ref = refs[n_parts + 4:]
        o2_ref[...] = o.astype(bf16)
    o_ref[...] = o


def _mm_ln(parts, w, x, g, b, layer, route_wb=None):
    T, D = x.shape
    tm = min(T, 512)
    routed = route_wb is not None
    vec = pl.BlockSpec((None, 1, D), lambda i: (layer, 0, 0))
    row = pl.BlockSpec((tm, D), lambda i: (i, 0))
    in_specs = [pl.BlockSpec((tm, a.shape[1]), lambda i: (i, 0)) for a in parts]
    in_specs += [pl.BlockSpec(memory_space=pltpu.VMEM), row, vec, vec]
    args = [*parts, w, x, g.reshape(g.shape[0], 1, D), b.reshape(b.shape[0], 1, D)]
    if routed:
        in_specs += [pl.BlockSpec((D, LANES), lambda i: (0, 0)), pl.BlockSpec((1, LANES), lambda i: (0, 0))]
        args += list(route_wb)
        out_shape = (jax.ShapeDtypeStruct((T, D), f32), jax.ShapeDtypeStruct((T, D // 2), jnp.uint32),
                     jax.ShapeDtypeStruct((T, LANES), f32))
        out_specs = (row, pl.BlockSpec((tm, D // 2), lambda i: (i, 0)), pl.BlockSpec((tm, LANES), lambda i: (i, 0)))
    else:
        out_shape = (jax.ShapeDtypeStruct((T, D), f32), jax.ShapeDtypeStruct((T, D), bf16))
        out_specs = (row, row)
    return pl.pallas_call(
        functools.partial(_mm_ln_kernel, n_parts=len(parts), routed=routed),
        out_shape=out_shape,
        grid=(T // tm,),
        in_specs=in_specs,
        out_specs=out_specs,
        compiler_params=_params("parallel"),
    )(*args)


def _conv_rows(x_ref, w, hist_ref):
    tt = x_ref.shape[0]
    hist_ref[SUBLANES:SUBLANES + tt, :] = x_ref[...]
    acc = _causal_taps(hist_ref[...], w, tt)
    hist_ref[0:SUBLANES, :] = hist_ref[tt:tt + SUBLANES, :]
    return acc


def _causal_taps(ext, w, tt):
    acc = None
    for k in range(CONV_W):
        back = CONV_W - 1 - k
        rows = (pltpu.roll(ext, back, 0) if back else ext)[SUBLANES:SUBLANES + tt, :]
        term = w[k:k + 1, :] * rows
        acc = term if acc is None else acc + term
    return acc


def _scan_affine(a, u):
    n = a.shape[0]
    row = _iota(a.shape, 0)
    d = 1
    while d < n:
        keep = row >= d
        a_s = pltpu.roll(a, d, 0)
        u_s = pltpu.roll(u, d, 0)
        u = u + jnp.where(keep, a * u_s, 0.0)
        a = jnp.where(keep, a * a_s, a)
        d *= 2
    return a, u


def _rglru_kernel(gate_ref, xa_ref, cw_ref, cb_ref, wa_ref, ba_ref, wx_ref, bx_ref, lam_ref,
                  o_ref, hist_ref, h_ref):
    @pl.when(pl.program_id(1) == 0)
    def _():
        h_ref[...] = jnp.zeros_like(h_ref)
        hist_ref[:, 0:SUBLANES, :] = jnp.zeros((hist_ref.shape[0], SUBLANES, hist_ref.shape[2]), f32)

    decay_rate = -RG_C * _softplus(-lam_ref[...])
    for b in range(xa_ref.shape[0]):
        xc = _conv_rows(xa_ref.at[b], cw_ref[...], hist_ref.at[b]) + cb_ref[...]
        r = _sigmoid(_dot(xc, wa_ref[...]) + ba_ref[...])
        i = _sigmoid(_dot(xc, wx_ref[...]) + bx_ref[...])
        log_a = r * decay_rate
        a = jnp.exp(log_a)
        u = jnp.sqrt(1.0 - jnp.exp(2.0 * log_a)) * (i * xc)
        a_cum, h = _scan_affine(a, u)
        h = h + a_cum * h_ref[b, 0:1, :]
        tt = h.shape[0]
        h_ref[b] = jnp.broadcast_to(h[tt - 1:tt, :], h_ref.shape[1:])
        o_ref[b] = (_gelu_tanh(gate_ref[b]) * h).astype(o_ref.dtype)


def _rglru(proj, B, S, layer, conv_w, conv_b, wa, ba, wx, bx, lam):
    tt = min(S, 256)
    ns = S // tt
    xa_col0 = W_A // BW_A
    vec = lambda a: a.reshape(a.shape[0], 1, W_A)
    vspec = pl.BlockSpec((None, 1, BW_A), lambda h, s: (layer, 0, h))
    wspec = pl.BlockSpec((None, None, BW_A, BW_A), lambda h, s: (layer, h, 0, 0))
    proj3 = proj.reshape(B, S, proj.shape[1])
    out = pl.pallas_call(
        _rglru_kernel,
        out_shape=jax.ShapeDtypeStruct((B, S, W_A), bf16),
        grid=(H_A, ns),
        in_specs=[pl.BlockSpec((B, tt, BW_A), lambda h, s: (0, s, h)),
                  pl.BlockSpec((B, tt, BW_A), lambda h, s: (0, s, xa_col0 + h)),
                  pl.BlockSpec((None, CONV_W, BW_A), lambda h, s: (layer, 0, h)),
                  vspec, wspec, vspec, wspec, vspec, vspec],
        out_specs=pl.BlockSpec((B, tt, BW_A), lambda h, s: (0, s, h)),
        scratch_shapes=[pltpu.VMEM((B, SUBLANES + tt, BW_A), f32), pltpu.VMEM((B, SUBLANES, BW_A), f32)],
        compiler_params=_params("parallel", "arbitrary"),
    )(proj3, proj3, conv_w, vec(conv_b), wa, vec(ba), wx, vec(bx), vec(lam))
    return out.reshape(B * S, W_A)


def _ssd_group(X, Bc, Cc, z, dt, a_neg, d_head, nw, st_ref):
    L = CHUNK
    adt = dt * a_neg
    tri = _tri(L)
    cs = _dot_exact_lhs(tri.astype(f32), adt)
    eye_h = (_iota((HG_B, HG_B), 0) == _iota((HG_B, HG_B), 1)).astype(f32)
    cs_t = _transpose_exact(cs, eye_h)
    cs_last = cs[L - 1:L, :]

    expand = (_iota((HG_B, HG_B * HD_B), 1) // HD_B == _iota((HG_B, HG_B * HD_B), 0)).astype(f32)
    dt_x = _dot_exact_rhs(dt, expand)
    ecs_x = _dot_exact_rhs(jnp.exp(cs), expand)
    dec_x = _dot_exact_rhs(jnp.exp(cs_last - cs), expand)
    cdec_x = _dot_exact_rhs(jnp.broadcast_to(jnp.exp(cs_last), (SUBLANES, HG_B)), expand)[0:1]
    d_x = _dot_exact_rhs(jnp.broadcast_to(d_head, (SUBLANES, HG_B)), expand)[0:1]

    xdt = X * dt_x
    cb = _dot_nt(Cc, Bc)
    xdt_b = xdt.astype(bf16)
    left = _iota((L, 2 * HD_B), 1) < HD_B
    pieces = []
    for j in range(0, HG_B, 2):
        pair = xdt_b[:, j * HD_B:(j + 2) * HD_B]
        outs = []
        for jj in (j, j + 1):
            seg = jnp.where(tri, cs[:, jj:jj + 1] - cs_t[jj:jj + 1, :], -jnp.inf)
            outs.append(jnp.dot((cb * jnp.exp(seg)).astype(bf16), pair, preferred_element_type=f32))
        pieces.append(jnp.where(left, outs[0], outs[1]))
    y_diag = jnp.concatenate(pieces, axis=1)

    state = st_ref[...]
    y_off = ecs_x * _dot(Cc, state)
    st_ref[...] = state * cdec_x + _dot_tn(Bc, xdt * dec_x)

    y = y_diag + y_off + X * d_x
    yg = y * _silu(z)
    return yg * lax.rsqrt(jnp.mean(yg * yg, -1, keepdims=True) + RMS_EPS) * nw


def _ssd_kernel(z_ref, x_ref, b_ref, c_ref, dt_ref, cw_ref, cb_ref, dtb_ref, alog_ref, d_ref, nw_ref,
                o_ref, hx_ref, hb_ref, hc_ref, st_ref):
    GW = W_B // NG_B

    @pl.when(pl.program_id(0) == 0)
    def _():
        st_ref[...] = jnp.zeros_like(st_ref)
        for h in (hx_ref, hb_ref, hc_ref):
            h[:, 0:SUBLANES, :] = jnp.zeros((h.shape[0], SUBLANES, h.shape[2]), f32)

    cw, cbias = cw_ref[...], cb_ref[...]
    c_off = W_B + NG_B * N_B
    a_all = -jnp.exp(alog_ref[...])
    d_all = d_ref[...]
    nw_all = nw_ref[...]
    for b in range(x_ref.shape[0]):
        X = _silu(_conv_rows(x_ref.at[b], cw[:, :W_B], hx_ref.at[b]) + cbias[:, :W_B])
        Bm = _silu(_conv_rows(b_ref.at[b], cw[:, W_B:c_off], hb_ref.at[b]) + cbias[:, W_B:c_off])
        Cm = _silu(_conv_rows(c_ref.at[b], cw[:, c_off:], hc_ref.at[b]) + cbias[:, c_off:])
        dt = _softplus(dt_ref[b] + dtb_ref[...])
        for g in range(NG_B):
            hs = slice(g * HG_B, (g + 1) * HG_B)
            ws = slice(g * GW, (g + 1) * GW)
            ns = slice(g * N_B, (g + 1) * N_B)
            out = _ssd_group(X[:, ws], Bm[:, ns], Cm[:, ns], z_ref[b, :, ws], dt[:, hs], a_all[:, hs],
                             d_all[:, hs], nw_all[:, ws], st_ref.at[b * NG_B + g])
            o_ref[b, :, ws] = out.astype(o_ref.dtype)


def _ssd(proj, dt_tail, B, S, layer, conv_w, conv_b, dt_bias, a_log, d, norm_w):
    L = CHUNK
    nc = S // L
    BC = NG_B * N_B
    per_batch = lambda a: a.reshape(B, S, a.shape[1])
    blk = lambda n, cb: pl.BlockSpec((B, L, n), lambda c: (0, c, cb))
    vec = lambda a: a.reshape(a.shape[0], 1, a.shape[-1])
    par = lambda r, n: pl.BlockSpec((None, r, n), lambda c: (layer, 0, 0))
    proj3 = per_batch(proj)
    out = pl.pallas_call(
        _ssd_kernel,
        out_shape=jax.ShapeDtypeStruct((B, S, W_B), bf16),
        grid=(nc,),
        in_specs=[blk(W_B, 2 * W_A // W_B), blk(W_B, (2 * W_A + W_B) // W_B),
                  blk(BC, (2 * W_A + 2 * W_B) // BC), blk(BC, (2 * W_A + 2 * W_B) // BC + 1),
                  blk(H_B, 0), par(CONV_W, CONV_B), par(1, CONV_B), par(1, H_B), par(1, H_B), par(1, H_B),
                  par(1, W_B)],
        out_specs=blk(W_B, 0),
        scratch_shapes=[pltpu.VMEM((B, SUBLANES + L, W_B), f32), pltpu.VMEM((B, SUBLANES + L, BC), f32),
                        pltpu.VMEM((B, SUBLANES + L, BC), f32), pltpu.VMEM((B * NG_B, N_B, W_B // NG_B), f32)],
        compiler_params=_params("arbitrary"),
    )(proj3, proj3, proj3, proj3, per_batch(dt_tail), conv_w, vec(conv_b), vec(dt_bias), vec(a_log), vec(d),
      vec(norm_w))
    return out.reshape(B * S, W_B)


def _s5_tables(a_re, a_im, log_step, b_re, b_im, c_re, c_im):
    L = L_C
    ar, ai = a_re.astype(f32), a_im.astype(f32)
    step = jnp.exp(log_step.astype(f32))[:, None]
    mag = jnp.exp(ar * step)
    lb_re, lb_im = mag * jnp.cos(ai * step), mag * jnp.sin(ai * step)
    den = ar * ar + ai * ai
    f_re = ((lb_re - 1.0) * ar + lb_im * ai) / den
    f_im = (lb_im * ar - (lb_re - 1.0) * ai) / den
    br, bi = b_re.astype(f32), b_im.astype(f32)
    bb_re = f_re[..., None] * br - f_im[..., None] * bi
    bb_im = f_re[..., None] * bi + f_im[..., None] * br
    cr, ci = c_re.astype(f32), c_im.astype(f32)

    def power(n):
        n = n.astype(f32)[None, :, None]
        m = jnp.exp(ar[:, None, :] * step[:, None, :] * n)
        ang = ai[:, None, :] * step[:, None, :] * n
        return m * jnp.cos(ang), m * jnp.sin(ang)

    j = jnp.arange(L)
    pr, pi = power(j)
    lbr = pr[..., None] * bb_re[:, None] - pi[..., None] * bb_im[:, None]
    lbi = pr[..., None] * bb_im[:, None] + pi[..., None] * bb_re[:, None]
    kern = (jnp.einsum('gop,gjpk->gjko', cr, lbr) - jnp.einsum('gop,gjpk->gjko', ci, lbi))
    lagb = kern.reshape(S5_NB, S5_GB, L, GS_C, GS_C).transpose(0, 2, 1, 3, 4)
    lagb = lagb.reshape(S5_NB, L, LANES, GS_C)
    rev = (L - 1) - j
    bend = jnp.concatenate([jnp.take(lbr, rev, axis=1), jnp.take(lbi, rev, axis=1)], 2)
    bendc = bend.reshape(S5_NB, S5_GB, L, 2 * P_C, GS_C).transpose(0, 2, 1, 4, 3)
    bendc = bendc.reshape(S5_NB, L, LANES, 2 * P_C)
    qr, qi = power(j + 1)
    car_re = cr[:, None] * qr[:, :, None, :] - ci[:, None] * qi[:, :, None, :]
    car_im = -(cr[:, None] * qi[:, :, None, :] + ci[:, None] * qr[:, :, None, :])
    car = jnp.concatenate([car_re, car_im], -1)
    ccarc = car.reshape(S5_NB, S5_GB, L, GS_C, 2 * P_C).transpose(0, 2, 4, 1, 3)
    ccarc = ccarc.reshape(S5_NB, L, 2 * P_C, LANES)
    return lagb, bendc.astype(bf16), ccarc.astype(bf16), power


def _s5_kernel(u_ref, lag_ref, bend_ref, ccar_ref, sc_ref, o_ref, toep_t, bend_t, ccar_t, *, seq_chunks):
    L = L_C
    SW = 2 * P_C

    zero = jnp.zeros((LANES, LANES), bf16)
    own_g = (_iota((LANES, LANES), 0) // GS_C) == (_iota((LANES, LANES), 1) // GS_C)
    lag = [jnp.where(own_g, jnp.concatenate([lag_ref[j]] * S5_GB, axis=1), 0.0).astype(bf16) for j in range(L)]
    for s in range(L):
        for t in range(L):
            toep_t[s * LANES:(s + 1) * LANES, t * LANES:(t + 1) * LANES] = lag[t - s] if t >= s else zero
    same_g = (_iota((LANES, S5_GB * SW), 0) // GS_C) == (_iota((LANES, S5_GB * SW), 1) // SW)
    for s in range(L):
        wide = jnp.concatenate([bend_ref[s]] * S5_GB, axis=1)
        bend_t[s * LANES:(s + 1) * LANES, :] = jnp.where(same_g, wide, jnp.zeros_like(wide))
    col_g = _iota((SW, LANES), 1) // GS_C
    for t in range(L):
        blk = ccar_ref[t]
        for g in range(S5_GB):
            ccar_t[g * SW:(g + 1) * SW, t * LANES:(t + 1) * LANES] = jnp.where(col_g == g, blk, jnp.zeros_like(blk))

    n = o_ref.shape[0] // L
    U = jnp.concatenate([u_ref[pl.ds(l, n, stride=L), :].astype(bf16) for l in range(L)], axis=1)
    Y = jnp.dot(U, toep_t[...], preferred_element_type=f32)
    H_all = jnp.dot(U, bend_t[...], preferred_element_type=f32)
    row = _iota((n, SW), 0) % seq_chunks
    prev = []
    for g in range(S5_GB):
        sl = slice(g * SW, (g + 1) * SW)
        H = H_all[:, sl]
        d, k = 1, 0
        while d < seq_chunks:
            hs = pltpu.roll(H, d, 0)
            sw = pltpu.roll(hs, P_C, 1)
            H = H + jnp.where(row >= d, sc_ref[2 * k:2 * k + 1, sl] * hs + sc_ref[2 * k + 1:2 * k + 2, sl] * sw, 0.0)
            d *= 2
            k += 1
        prev.append(jnp.where(row >= 1, pltpu.roll(H, 1, 0), 0.0).astype(bf16))
    Y = Y + jnp.dot(jnp.concatenate(prev, axis=1), ccar_t[...], preferred_element_type=f32)
    for l in range(L):
        o_ref[pl.ds(l, n, stride=L), :] = Y[:, l * LANES:(l + 1) * LANES]


def _s5_post_kernel(y_ref, u_ref, d_ref, w_ref, b_ref, o_ref):
    y = y_ref[...] + d_ref[...] * u_ref[...]
    g = _gelu_tanh(y)
    o_ref[...] = (g * _sigmoid(_dot(g, w_ref[...]) + b_ref[...])).astype(o_ref.dtype)


def _s5(proj, B, S, layer, a_re, a_im, log_step, b_re, b_im, c_re, c_im, d, glu_w, glu_b):
    T = B * S
    L = L_C
    nch = S // L
    lagb, bendc, ccarc, power = _s5_tables(a_re[layer], a_im[layer], log_step[layer], b_re[layer],
                                           b_im[layer], c_re[layer], c_im[layer])
    nsteps = max(1, (nch - 1).bit_length())
    sr, si = power(L * (2 ** jnp.arange(nsteps)))
    scan_c = jnp.stack([jnp.concatenate([sr, sr], -1), jnp.concatenate([-si, si], -1)], 2)
    scan8 = scan_c.reshape(S5_NB, S5_GB, 2 * nsteps, 2 * P_C).transpose(0, 2, 1, 3)
    scan8 = scan8.reshape(S5_NB, 2 * nsteps, S5_GB * 2 * P_C)
    tab = lambda r, c: pl.BlockSpec((None, L, r, c), lambda g: (g, 0, 0, 0))
    y = pl.pallas_call(
        functools.partial(_s5_kernel, seq_chunks=nch),
        out_shape=jax.ShapeDtypeStruct((T, W_C), f32),
        grid=(S5_NB,),
        in_specs=[pl.BlockSpec((T, LANES), lambda g: (0, g)),
                  tab(LANES, GS_C), tab(LANES, 2 * P_C), tab(2 * P_C, LANES),
                  pl.BlockSpec((None, 2 * nsteps, S5_GB * 2 * P_C), lambda g: (g, 0, 0))],
        out_specs=pl.BlockSpec((T, LANES), lambda g: (0, g)),
        scratch_shapes=[pltpu.VMEM((L * LANES, L * LANES), bf16),
                        pltpu.VMEM((L * LANES, S5_GB * 2 * P_C), bf16),
                        pltpu.VMEM((S5_GB * 2 * P_C, L * LANES), bf16)],
        compiler_params=_params("parallel"),
    )(proj, lagb, bendc, ccarc, scan8)
    tm = min(T, 1024)
    vec = pl.BlockSpec((None, 1, W_C), lambda i: (layer, 0, 0))
    return pl.pallas_call(
        _s5_post_kernel,
        out_shape=jax.ShapeDtypeStruct((T, W_C), bf16),
        grid=(T // tm,),
        in_specs=[pl.BlockSpec((tm, W_C), lambda i: (i, 0)),
                  pl.BlockSpec((tm, W_C), lambda i: (i, 0)),
                  vec,
                  pl.BlockSpec((None, W_C, W_C), lambda i: (layer, 0, 0)),
                  vec],
        out_specs=pl.BlockSpec((tm, W_C), lambda i: (i, 0)),
        compiler_params=_params("parallel"),
    )(y, proj, d.reshape(d.shape[0], 1, W_C), glu_w, glu_b.reshape(glu_b.shape[0], 1, W_C))


def _gdn_prep_kernel(q_ref, k_ref, v_ref, qp_ref, kp_ref, vp_ref, ab_ref, cwq_ref, cwk_ref, cwv_ref,
                     alog_ref, dtb_ref, w_ref, u_ref, qd_ref, kd_ref, qk_ref, ge_ref, hist_ref):
    L = CHUNK
    nb = q_ref.shape[0]
    has_prev = (pl.program_id(0) > 0).astype(f32)

    def conv(cur_ref, prev_ref, w_ref_, b, slot):
        hist = hist_ref.at[b * 3 + slot]
        hist[0:SUBLANES, :] = prev_ref[b] * has_prev
        hist[SUBLANES:SUBLANES + L, :] = cur_ref[b]
        return _silu(_causal_taps(hist[...], w_ref_[...], L))

    tri = _tri(L)
    tri_s = _tri(L, strict=True)
    eye_h = (_iota((H_D, H_D), 0) == _iota((H_D, H_D), 1)).astype(f32)
    eye = (_iota((L, L), 0) == _iota((L, L), 1)).astype(f32)
    q_all, k_all, v_all, beta_all, gcs_all, gcs_t = [], [], [], [], [], []
    for b in range(nb):
        q_all.append(conv(q_ref, qp_ref, cwq_ref, b, 0))
        k_all.append(conv(k_ref, kp_ref, cwk_ref, b, 1))
        v_all.append(conv(v_ref, vp_ref, cwv_ref, b, 2))
        ab = ab_ref[b]
        g = -jnp.exp(alog_ref[...]) * _softplus(ab[:, 0:H_D] + dtb_ref[...])
        beta_all.append(_sigmoid(ab[:, H_D:2 * H_D]))
        gcs_all.append(_dot_exact_lhs(tri.astype(f32), g))
        gcs_t.append(_transpose_exact(gcs_all[b], eye_h))
        ge_ref[b] = jnp.concatenate([jnp.exp(gcs_all[b]), jnp.zeros((L, LANES - H_D), f32)], axis=1)

    pairs = [(b, h) for b in range(nb) for h in range(H_D)]
    cols = lambda h: slice(h * DK_D, (h + 1) * DK_D)
    unit = lambda x: x * lax.rsqrt(jnp.sum(x * x, -1, keepdims=True) + 1e-6)
    gcs = [gcs_all[b][:, h:h + 1] for b, h in pairs]
    eg = [jnp.exp(g) for g in gcs]
    beta = [beta_all[b][:, h:h + 1] for b, h in pairs]
    qs = [unit(q_all[b][:, cols(h)]) * (DK_D ** -0.5) for b, h in pairs]
    ks = [unit(k_all[b][:, cols(h)]) for b, h in pairs]
    kbs = [k * bt for k, bt in zip(ks, beta)]
    decay = [jnp.exp(jnp.where(tri, g - gcs_t[b][h:h + 1, :], -jnp.inf)) for (b, h), g in zip(pairs, gcs)]
    kk = [_dot_nt(kb, k) for kb, k in zip(kbs, ks)]
    qk = [_dot_nt(q, k) for q, k in zip(qs, ks)]
    pw = [jnp.where(tri_s, -(a * d), 0.0) for a, d in zip(kk, decay)]
    inv = [eye + p for p in pw]
    for _ in range(int(math.log2(L)) - 1):
        pw = [_dot(p, p) for p in pw]
        inv = [a + _dot(a, p) for a, p in zip(inv, pw)]
    rhs = [jnp.concatenate([v_all[b][:, cols(h)] * bt, kb * e], axis=1)
           for (b, h), bt, kb, e in zip(pairs, beta, kbs, eg)]
    sol = [_dot(a, r) for a, r in zip(inv, rhs)]
    for i, (b, h) in enumerate(pairs):
        u_ref[b, :, cols(h)] = sol[i][:, :DV_D]
        w_ref[b, :, cols(h)] = sol[i][:, DV_D:]
        qk_ref[b, :, h * L:(h + 1) * L] = jnp.where(tri, qk[i] * decay[i], 0.0)
        qd_ref[b, :, cols(h)] = qs[i] * eg[i]
        kd_ref[b, :, cols(h)] = ks[i] * jnp.exp(gcs[i][L - 1:L, :] - gcs[i])


def _gdn_scan_kernel(w_ref, u_ref, qd_ref, kd_ref, qk_ref, ge_ref, z_ref, nw_ref, o_ref, st_ref):
    L = CHUNK

    @pl.when(pl.program_id(0) == 0)
    def _():
        st_ref[...] = jnp.zeros_like(st_ref)

    nw = nw_ref[...]
    pairs = [(b, h) for b in range(w_ref.shape[0]) for h in range(H_D)]
    cols = lambda h: slice(h * DK_D, (h + 1) * DK_D)
    state = [st_ref[b * H_D + h] for b, h in pairs]
    v_new = [u_ref[b, :, cols(h)] - _dot(w_ref[b, :, cols(h)], s) for (b, h), s in zip(pairs, state)]
    o = [_dot(qd_ref[b, :, cols(h)], s) + _dot(qk_ref[b, :, h * L:(h + 1) * L], v)
         for (b, h), s, v in zip(pairs, state, v_new)]
    for (b, h), s, v in zip(pairs, state, v_new):
        st_ref[b * H_D + h] = s * ge_ref[b, L - 1:L, h:h + 1] + _dot_tn(kd_ref[b, :, cols(h)], v)
    for (b, h), y in zip(pairs, o):
        y = y * lax.rsqrt(jnp.mean(y * y, -1, keepdims=True) + RMS_EPS) * nw
        o_ref[b, :, cols(h)] = (y * _silu(z_ref[b, :, cols(h)])).astype(o_ref.dtype)


def _gdn(proj, ab_tail, B, S, layer, conv_w, a_log, dt_bias, norm_w):
    L = CHUNK
    nc = S // L
    per8 = L // SUBLANES
    q_c0 = W_C // W_D
    z_c0 = (W_C + QKV_D) // W_D
    proj3 = proj.reshape(B, S, proj.shape[1])
    blk = lambda n, cb=0: pl.BlockSpec((B, L, n), lambda c: (0, c, cb))
    prev = lambda cb: pl.BlockSpec((B, SUBLANES, W_D), lambda c: (0, jnp.maximum(c * per8 - 1, 0), cb))
    cw = lambda cb: pl.BlockSpec((None, CONV_W, W_D), lambda c: (layer, 0, cb))
    v8 = pl.BlockSpec((None, 1, H_D), lambda c: (layer, 0, 0))
    shp = lambda n: jax.ShapeDtypeStruct((B, S, n), f32)
    w_c, u_c, q_dec, k_dec, qk, gexp = pl.pallas_call(
        _gdn_prep_kernel,
        out_shape=(shp(W_D), shp(W_D), shp(W_D), shp(W_D), shp(H_D * L), shp(LANES)),
        grid=(nc,),
        in_specs=[blk(W_D, q_c0), blk(W_D, q_c0 + 1), blk(W_D, q_c0 + 2), prev(q_c0), prev(q_c0 + 1),
                  prev(q_c0 + 2), blk(2 * H_D), cw(0), cw(1), cw(2), v8, v8],
        out_specs=(blk(W_D), blk(W_D), blk(W_D), blk(W_D), blk(H_D * L), blk(LANES)),
        scratch_shapes=[pltpu.VMEM((3 * B, SUBLANES + L, W_D), f32)],
        compiler_params=_params("parallel"),
    )(proj3, proj3, proj3, proj3, proj3, proj3, ab_tail.reshape(B, S, 2 * H_D), conv_w, conv_w, conv_w,
      a_log.reshape(a_log.shape[0], 1, H_D), dt_bias.reshape(dt_bias.shape[0], 1, H_D))
    out = pl.pallas_call(
        _gdn_scan_kernel,
        out_shape=jax.ShapeDtypeStruct((B, S, W_D), bf16),
        grid=(nc,),
        in_specs=[blk(W_D), blk(W_D), blk(W_D), blk(W_D), blk(H_D * L), blk(LANES), blk(W_D, z_c0),
                  pl.BlockSpec((None, 1, DV_D), lambda c: (layer, 0, 0))],
        out_specs=blk(W_D),
        scratch_shapes=[pltpu.VMEM((B * H_D, DK_D, DV_D), f32)],
        compiler_params=_params("arbitrary"),
    )(w_c, u_c, q_dec, k_dec, qk, gexp, proj3, norm_w.reshape(norm_w.shape[0], 1, DV_D))
    return out.reshape(B * S, W_D)


def _xattn_kernel(q_ref, k_ref, v_ref, o_ref):
    cols = [slice(h * HD_X, (h + 1) * HD_X) for h in range(H_X)]
    s = [_dot_nt(q_ref[:, c], k_ref[:, c]) * (HD_X ** -0.5) for c in cols]
    p = [jnp.exp(x - jnp.max(x, -1, keepdims=True)) for x in s]
    p = [(x / jnp.sum(x, -1, keepdims=True)).astype(bf16) for x in p]
    for c, x in zip(cols, p):
        o_ref[:, c] = jnp.dot(x, v_ref[:, c], preferred_element_type=f32).astype(o_ref.dtype)


def _xattn(q, kv, B, S, M):
    T = B * S
    D = H_X * HD_X
    tq = min(S, 1024)
    nq = S // tq
    return pl.pallas_call(
        _xattn_kernel,
        out_shape=jax.ShapeDtypeStruct((T, D), bf16),
        grid=(B, nq),
        in_specs=[pl.BlockSpec((tq, D), lambda b, i: (b * nq + i, 0)),
                  pl.BlockSpec((M, D), lambda b, i: (b, 0)),
                  pl.BlockSpec((M, D), lambda b, i: (b, 1))],
        out_specs=pl.BlockSpec((tq, D), lambda b, i: (b * nq + i, 0)),
        compiler_params=_params("parallel", "parallel"),
    )(q, kv, kv)


def _route_weights(w_group, b_group, w_expert, b_expert, layer):
    D = w_group.shape[1]
    pad = LANES - N_EXP - NG_E
    w = jnp.concatenate([w_expert[layer], w_group[layer], jnp.zeros((D, pad), f32)], 1)
    b = jnp.concatenate([b_expert[layer], b_group[layer], jnp.zeros((pad,), f32)])[None, :]
    return w, b


MOE_TM = 256
LN_TM = 256


def _moe_plan(route, T):
    n_slots = TOPK_IN * T
    n_tiles = n_slots // MOE_TM + N_EXP
    ids = route[:, 2:2 + TOPK_IN].astype(jnp.int32)
    e_flat = ids.T.reshape(n_slots)
    onehot = (e_flat[:, None] == jnp.arange(N_EXP, dtype=jnp.int32)[None, :]).astype(jnp.int32)
    csum = jnp.cumsum(onehot, axis=0)
    rank = jnp.sum(onehot * csum, axis=1) - 1
    counts = csum[-1]
    tiles_per = (counts + MOE_TM - 1) // MOE_TM
    tile_end = jnp.cumsum(tiles_per)
    tile_start = tile_end - tiles_per
    dest = jnp.sum(onehot * tile_start[None, :], axis=1) * MOE_TM + rank
    tok = jnp.arange(n_slots, dtype=jnp.int32) % T
    src = jnp.zeros((n_tiles * MOE_TM,), jnp.int32).at[dest].set(tok, unique_indices=True)
    j = jnp.arange(n_tiles, dtype=jnp.int32)
    tile_e = jnp.sum((j[:, None] >= tile_end[None, :]).astype(jnp.int32), axis=1)
    last_e = jnp.max(jnp.where(counts > 0, jnp.arange(N_EXP, dtype=jnp.int32), 0))
    tile_e = jnp.minimum(tile_e, last_e)
    n_live = tile_end[-1:].astype(jnp.int32)
    return tile_e, n_live, src, dest.astype(jnp.int32), n_tiles


GATHER_UNROLL = 8


def _moe_group_kernel(tile_e_ref, nt_ref, src_ref, xp_ref, wg_ref, wu_ref, wd_ref, o_ref, xbuf, wgb, wub, wdb):
    i = pl.program_id(0)
    nt = nt_ref[0]
    tm = xbuf.shape[0]

    @pl.when((i < nt) & ((i == 0) | (tile_e_ref[i] != tile_e_ref[jnp.maximum(i - 1, 0)])))
    def _():
        wgb[...] = wg_ref[...].astype(bf16)
        wub[...] = wu_ref[...].astype(bf16)
        wdb[...] = wd_ref[...].astype(bf16)

    @pl.when(i < nt)
    def _():
        def body(j, c):
            rows = [xp_ref[pl.ds(src_ref[i * tm + j * GATHER_UNROLL + q], 1), :] for q in range(GATHER_UNROLL)]
            xbuf[pl.ds(pl.multiple_of(j * GATHER_UNROLL, GATHER_UNROLL), GATHER_UNROLL), :] = jnp.concatenate(rows, axis=0)
            return c
        lax.fori_loop(0, tm // GATHER_UNROLL, body, 0)
        lo, hi = _unpack_halves(xbuf[...])
        x = jnp.concatenate([lo.astype(bf16), hi.astype(bf16)], axis=1)
        h = _silu(jnp.dot(x, wgb[...], preferred_element_type=f32)) * \
            jnp.dot(x, wub[...], preferred_element_type=f32)
        o_ref[...] = _pack_halves(jnp.dot(h.astype(bf16), wdb[...], preferred_element_type=f32))

    @pl.when(i >= nt)
    def _():
        o_ref[...] = jnp.zeros_like(o_ref)


def _moe_group(xp, tile_e, n_live, src, n_tiles, w_gate, w_up, w_down, layer):
    T, half = xp.shape
    D = 2 * half
    wspec = lambda shp: pl.BlockSpec((None, None) + shp, lambda i, te, nt, sr: (layer, te[i], 0, 0))
    return pl.pallas_call(
        _moe_group_kernel,
        out_shape=jax.ShapeDtypeStruct((n_tiles * MOE_TM, half), jnp.uint32),
        grid_spec=pltpu.PrefetchScalarGridSpec(
            num_scalar_prefetch=3,
            grid=(n_tiles,),
            in_specs=[pl.BlockSpec(memory_space=pltpu.VMEM), wspec((D, D_E)), wspec((D, D_E)),
                      wspec((D_E, D))],
            out_specs=pl.BlockSpec((MOE_TM, half), lambda i, te, nt, sr: (i, 0)),
            scratch_shapes=[pltpu.VMEM((MOE_TM, half), jnp.uint32), pltpu.VMEM((D, D_E), bf16),
                            pltpu.VMEM((D, D_E), bf16), pltpu.VMEM((D_E, D), bf16)]),
        compiler_params=_params("arbitrary"),
    )(tile_e, n_live, src, xp, w_gate, w_up, w_down)


def _moe_combine_ln_kernel(pos_ref, x_ref, rt_ref, g_ref, b_ref, ys_hbm, o_ref, ob_ref, ybuf, sems):
    i = pl.program_id(0)
    n = pl.num_programs(0)
    tm = x_ref.shape[0]
    T = n * tm
    groups = tm // GATHER_UNROLL

    def issue(tile, slot):
        for k in range(TOPK_IN):
            base = (slot * TOPK_IN + k) * groups

            def body(j, c, k=k, base=base):
                for q in range(GATHER_UNROLL):
                    p = pos_ref[k * T + tile * tm + j * GATHER_UNROLL + q]
                    pltpu.make_async_copy(ys_hbm.at[pl.ds(p, 1), :], ybuf.at[base + j, pl.ds(q, 1), :],
                                          sems.at[slot]).start()
                return c
            lax.fori_loop(0, groups, body, 0)

    @pl.when(i == 0)
    def _():
        issue(0, 0)

    @pl.when(i + 1 < n)
    def _():
        issue(i + 1, (i + 1) % 2)

    slot = i % 2

    base = slot * TOPK_IN * groups

    def wait_group(j, c):
        pltpu.make_async_copy(ys_hbm.at[pl.ds(0, GATHER_UNROLL), :], ybuf.at[base + j], sems.at[slot]).wait()
        return c
    lax.fori_loop(0, TOPK_IN * groups, wait_group, 0)

    def gathered(k):
        return _unpack_halves(ybuf[pl.ds(base + k * groups, groups)].reshape(tm, ybuf.shape[2]))

    lo1, hi1 = gathered(0)
    lo2, hi2 = gathered(1)
    rt = rt_ref[...]
    w1, w2 = rt[:, 0:1], rt[:, 1:2]
    y = jnp.concatenate([w1 * lo1 + w2 * lo2, w1 * hi1 + w2 * hi2], axis=1)
    o = _layer_norm(ALPHA * x_ref[...] + y, g_ref[...], b_ref[...])
    o_ref[...] = o
    ob_ref[...] = o.astype(bf16)


def _moe_combine_ln(x, route, pos, ys, g, b, layer):
    T, D = x.shape
    tm = min(T, LN_TM)
    vec = pl.BlockSpec((None, 1, D), lambda i, p: (layer, 0, 0))
    row = pl.BlockSpec((tm, D), lambda i, p: (i, 0))
    return pl.pallas_call(
        _moe_combine_ln_kernel,
        out_shape=(jax.ShapeDtypeStruct((T, D), f32), jax.ShapeDtypeStruct((T, D), bf16)),
        grid_spec=pltpu.PrefetchScalarGridSpec(
            num_scalar_prefetch=1,
            grid=(T // tm,),
            in_specs=[row, pl.BlockSpec((tm, LANES), lambda i, p: (i, 0)), vec, vec,
                      pl.BlockSpec(memory_space=pl.ANY)],
            out_specs=(row, row),
            scratch_shapes=[pltpu.VMEM((2 * TOPK_IN * tm // GATHER_UNROLL, GATHER_UNROLL, D // 2), jnp.uint32),
                            pltpu.SemaphoreType.DMA((2,))]),
        compiler_params=_params("arbitrary"),
    )(pos, x, route, g.reshape(g.shape[0], 1, D), b.reshape(b.shape[0], 1, D), ys)


def _moe_ln(xf, xp, route, w_gate, w_up, w_down, g, b, layer):
    tile_e, n_live, src, pos, n_tiles = _moe_plan(route, xf.shape[0])
    ys = _moe_group(xp, tile_e, n_live, src, n_tiles, w_gate, w_up, w_down, layer)
    return _moe_combine_ln(xf, route, pos, ys, g, b, layer)


def kernel(x, mem, ab_w_in, rg_conv_w, rg_conv_b, rg_wa, rg_ba, rg_wx, rg_bx, rg_lam, ssd_conv_w, ssd_conv_b, ssd_dt_bias, ssd_a_log, ssd_d, ssd_norm_w, ab_w_out, cd_w_in, s5_a_re, s5_a_im, s5_log_step, s5_b_re, s5_b_im, s5_c_re, s5_c_im, s5_d, s5_glu_w, s5_glu_b, dn_conv_w, dn_a_log, dn_dt_bias, dn_norm_w, cd_w_out, xa_w_q, xa_w_kv, xa_w_o, moe_w_group, moe_b_group, moe_w_expert, moe_b_expert, moe_w_gate, moe_w_up, moe_w_down, ln1_g, ln1_b, ln2_g, ln2_b, ln3_g, ln3_b):
    B, S, D = x.shape
    M = mem.shape[1]
    T = B * S
    xf = x.reshape(T, D)
    xb = None
    memb = mem.reshape(B * M, D).astype(bf16)

    def gate_proj(tail_w):
        if xb is None:
            return _mm_hi(xf, tail_w, with_bf16=True)
        return _mm_hi(xf, tail_w), xb

    for l in range(DEPTH):
        i = l // 2
        if l % 2 == 0:
            dt_tail, xb = gate_proj(_tail_cols(ab_w_in, i, MAIN_AB))
            proj = _mm(xb, ab_w_in, i, MAIN_AB, f32)
            ya = _rglru(proj, B, S, i, rg_conv_w, rg_conv_b, rg_wa, rg_ba, rg_wx, rg_bx, rg_lam)
            yb = _ssd(proj, dt_tail, B, S, i, ssd_conv_w, ssd_conv_b, ssd_dt_bias, ssd_a_log, ssd_d,
                      ssd_norm_w)
            xf, xb = _mm_ln([ya, yb], ab_w_out[i].astype(bf16), xf, ln1_g, ln1_b, l)
        else:
            ab_tail, xb = gate_proj(_tail_cols(cd_w_in, i, MAIN_CD))
            proj = _mm(xb, cd_w_in, i, MAIN_CD, f32)
            yc = _s5(proj, B, S, i, s5_a_re, s5_a_im, s5_log_step, s5_b_re, s5_b_im, s5_c_re, s5_c_im,
                     s5_d, s5_glu_w, s5_glu_b)
            yd = _gdn(proj, ab_tail, B, S, i, dn_conv_w, dn_a_log, dn_dt_bias, dn_norm_w)
            xf, xb = _mm_ln([yc, yd], cd_w_out[i].astype(bf16), xf, ln1_g, ln1_b, l)
        q = _mm_resident(xb, xa_w_q[l].astype(bf16), bf16)
        kv = _mm(memb, xa_w_kv, l, 2 * D, bf16)
        att = _xattn(q, kv, B, S, M)
        xf, xp, route = _mm_ln([att], xa_w_o[l].astype(bf16), xf, ln2_g, ln2_b, l,
                               route_wb=_route_weights(moe_w_group, moe_b_group, moe_w_expert, moe_b_expert, l))
        xf, xb = _moe_ln(xf, xp, route, moe_w_gate, moe_w_up, moe_w_down, ln3_g, ln3_b, l)
    return xf.reshape(B, S, D)
```

```python
import functools
import math

import jax
import jax.numpy as jnp
from jax import lax
from jax.experimental import pallas as pl
from jax.experimental.pallas import tpu as pltpu

f32 = jnp.float32
bf16 = jnp.bfloat16

D_MODEL = 2048
DEPTH = 2
CHUNK = 64
CONV_W = 4
ALPHA = (2 * DEPTH) ** 0.25
LN_EPS = 1e-5
RMS_EPS = 1e-6
W_A = D_MODEL // 2
H_A = 8
BW_A = W_A // H_A
RG_C = 8.0
W_B = D_MODEL
HD_B = 64
H_B = W_B // HD_B
NG_B = 2
N_B = 128
HG_B = H_B // NG_B
CONV_B = W_B + 2 * NG_B * N_B
MAIN_AB = 2 * W_A + W_B + CONV_B
W_C = D_MODEL // 2
GS_C = 16
G_C = W_C // GS_C
P_C = 64
L_C = 16
H_D = 8
DK_D = D_MODEL // 16
DV_D = D_MODEL // 16
W_D = H_D * DV_D
QKV_D = 2 * H_D * DK_D + W_D
MAIN_CD = W_C + QKV_D + W_D
H_X = 4
HD_X = D_MODEL // H_X
NG_E = 4
E_PER = 8
N_EXP = NG_E * E_PER
TOPK_IN = 2
D_E = D_MODEL // 8

LANES = 128
SUBLANES = 8
S5_GB = LANES // GS_C
S5_NB = G_C // S5_GB
VMEM_LIMIT = 56 * 1024 * 1024


def _params(*sem):
    return pltpu.CompilerParams(dimension_semantics=sem, vmem_limit_bytes=VMEM_LIMIT)


def _sigmoid(x):
    return 1.0 / (1.0 + jnp.exp(-x))


def _silu(x):
    return x * _sigmoid(x)


def _softplus(x):
    return jnp.maximum(x, 0.0) + jnp.log(1.0 + jnp.exp(-jnp.abs(x)))


def _gelu_tanh(x):
    return 0.5 * x * (1.0 + jnp.tanh(math.sqrt(2.0 / math.pi) * (x + 0.044715 * (x * x * x))))


def _dot(a, b):
    return jnp.dot(a.astype(bf16), b.astype(bf16), preferred_element_type=f32)


def _dot_nt(a, b):
    return lax.dot_general(a.astype(bf16), b.astype(bf16), (((1,), (1,)), ((), ())),
                           preferred_element_type=f32)


def _dot_tn(a, b):
    return lax.dot_general(a.astype(bf16), b.astype(bf16), (((0,), (0,)), ((), ())),
                           preferred_element_type=f32)


def _split3(a):
    hi = a.astype(bf16)
    r = a - hi.astype(f32)
    mid = r.astype(bf16)
    lo = (r - mid.astype(f32)).astype(bf16)
    return hi, mid, lo


def _dot_exact_lhs(sel, b):
    s = sel.astype(bf16)
    b1, b2, b3 = _split3(b)
    d = functools.partial(jnp.dot, preferred_element_type=f32)
    return d(s, b1) + d(s, b2) + d(s, b3)


def _dot_exact_rhs(a, sel):
    s = sel.astype(bf16)
    a1, a2, a3 = _split3(a)
    d = functools.partial(jnp.dot, preferred_element_type=f32)
    return d(a1, s) + d(a2, s) + d(a3, s)


def _transpose_exact(a, eye):
    a1, a2, a3 = _split3(a)
    e = eye.astype(bf16)
    d = lambda x: lax.dot_general(e, x, (((1,), (1,)), ((), ())), preferred_element_type=f32)
    return d(a1) + d(a2) + d(a3)


def _dot3(a, b):
    a1 = a.astype(bf16)
    a2 = (a - a1.astype(f32)).astype(bf16)
    b1 = b.astype(bf16)
    b2 = (b - b1.astype(f32)).astype(bf16)
    d = functools.partial(jnp.dot, preferred_element_type=f32)
    return d(a1, b1) + (d(a1, b2) + d(a2, b1))


def _iota(shape, axis):
    return lax.broadcasted_iota(jnp.int32, shape, axis)


def _tri(n, strict=False):
    r, c = _iota((n, n), 0), _iota((n, n), 1)
    return (r > c) if strict else (r >= c)


def _mm_kernel(x_ref, w_ref, o_ref):
    o_ref[...] = jnp.dot(x_ref[...], w_ref[...].astype(bf16),
                         preferred_element_type=f32).astype(o_ref.dtype)


def _mm(x, w, layer, n_cols, out_dtype, tn=512):
    M, K = x.shape
    tm = min(M, 2048)
    return pl.pallas_call(
        _mm_kernel,
        out_shape=jax.ShapeDtypeStruct((M, n_cols), out_dtype),
        grid=(M // tm, n_cols // tn),
        in_specs=[pl.BlockSpec((tm, K), lambda i, j: (i, 0)),
                  pl.BlockSpec((None, K, tn), lambda i, j: (layer, 0, j))],
        out_specs=pl.BlockSpec((tm, tn), lambda i, j: (i, j)),
        compiler_params=_params("parallel", "parallel"),
    )(x, w)


def _mm_resident_kernel(x_ref, w_ref, o_ref):
    o_ref[...] = jnp.dot(x_ref[...], w_ref[...], preferred_element_type=f32).astype(o_ref.dtype)


def _mm_resident(x, w, out_dtype):
    M, K = x.shape
    tm = min(M, 1024)
    return pl.pallas_call(
        _mm_resident_kernel,
        out_shape=jax.ShapeDtypeStruct((M, w.shape[1]), out_dtype),
        grid=(M // tm,),
        in_specs=[pl.BlockSpec((tm, K), lambda i: (i, 0)), pl.BlockSpec(memory_space=pltpu.VMEM)],
        out_specs=pl.BlockSpec((tm, w.shape[1]), lambda i: (i, 0)),
        compiler_params=_params("parallel"),
    )(x, w)


def _tail_cols(w, layer, start):
    _, K, N = w.shape
    return lax.slice(w, (layer, 0, start), (layer + 1, K, N)).reshape(K, N - start)


def _mm_hi_kernel(x_ref, w_ref, o_ref, *xb_ref):
    x = x_ref[...]
    o_ref[...] = _dot3(x, w_ref[...])
    for r in xb_ref:
        r[...] = x.astype(bf16)


def _mm_hi(x, w, with_bf16=False):
    M, K = x.shape
    n = w.shape[1]
    tm = min(M, 512)
    row = pl.BlockSpec((tm, K), lambda i: (i, 0))
    out_shape = [jax.ShapeDtypeStruct((M, n), f32)] + ([jax.ShapeDtypeStruct((M, K), bf16)] if with_bf16 else [])
    out = pl.pallas_call(
        _mm_hi_kernel,
        out_shape=out_shape,
        grid=(M // tm,),
        in_specs=[row, pl.BlockSpec((K, n), lambda i: (0, 0))],
        out_specs=[pl.BlockSpec((tm, n), lambda i: (i, 0))] + ([row] if with_bf16 else []),
        compiler_params=_params("parallel"),
    )(x, w)
    return out if with_bf16 else out[0]


def _layer_norm(v, g, b):
    mu = jnp.mean(v, -1, keepdims=True)
    d = v - mu
    var = jnp.mean(d * d, -1, keepdims=True)
    return d * lax.rsqrt(var + LN_EPS) * g + b


HI_MASK = 0xFFFF0000


def _pack_halves(v):
    h = v.shape[1] // 2
    lo = lax.bitcast_convert_type(v[:, :h].astype(bf16).astype(f32), jnp.uint32)
    hi = lax.bitcast_convert_type(v[:, h:].astype(bf16).astype(f32), jnp.uint32)
    return (hi & jnp.uint32(HI_MASK)) | (lo >> 16)


def _unpack_halves(u):
    lo = lax.bitcast_convert_type(u << 16, f32)
    hi = lax.bitcast_convert_type(u & jnp.uint32(HI_MASK), f32)
    return lo, hi


def _route_top2(logits):
    lane = _iota(logits.shape, 1)
    neg = -jnp.inf
    big = jnp.int32(LANES)
    is_g = (lane >= N_EXP) & (lane < N_EXP + NG_E)
    gl = jnp.where(is_g, logits, neg)
    gmax = jnp.max(gl, -1, keepdims=True)
    g_lane = jnp.min(jnp.where(gl == gmax, lane, big), -1, keepdims=True)
    g_prob = 1.0 / jnp.sum(jnp.where(is_g, jnp.exp(gl - gmax), 0.0), -1, keepdims=True)
    e0 = (g_lane - N_EXP) * E_PER
    sel = jnp.where((lane >= e0) & (lane < e0 + E_PER), logits, neg)
    m1 = jnp.max(sel, -1, keepdims=True)
    i1 = jnp.min(jnp.where(sel == m1, lane, big), -1, keepdims=True)
    sel2 = jnp.where(lane == i1, neg, sel)
    m2 = jnp.max(sel2, -1, keepdims=True)
    i2 = jnp.min(jnp.where(sel2 == m2, lane, big), -1, keepdims=True)
    e2 = jnp.exp(m2 - m1)
    w1 = g_prob / (1.0 + e2)
    w2 = g_prob * e2 / (1.0 + e2)
    return (jnp.where(lane == 0, w1, 0.0) + jnp.where(lane == 1, w2, 0.0)
            + jnp.where(lane == 2, i1.astype(f32), 0.0) + jnp.where(lane == 3, i2.astype(f32), 0.0))


def _mm_ln_kernel(*refs, n_parts, routed):
    part_refs = refs[:n_parts]
    w_ref, x_ref, g_ref, b_ref = refs[n_parts:n_parts + 4]
    acc, off = None, 0
    for p in part_refs:
        c = jnp.dot(p[...], w_ref[off:off + p.shape[1], :], preferred_element_type=f32)
        acc = c if acc is None else acc + c
        off += p.shape[1]
    o = _layer_norm(ALPHA * x_ref[...] + acc, g_ref[...], b_ref[...])
    if routed:
        rw_ref, rb_ref, o_ref, o2_ref, rt_ref = refs[n_parts + 4:]
        o2_ref[...] = _pack_halves(o)
        rt_ref[...] = _route_top2(_dot3(o, rw_ref[...]) + rb_ref[...])
    else:
        o_ref, o2_ref = refs[n_parts + 4:]
        o2_ref[...] = o.astype(bf16)
    o_ref[...] = o


def _mm_ln(parts, w, x, g, b, layer, route_wb=None):
    T, D = x.shape
    tm = min(T, 512)
    routed = route_wb is not None
    vec = pl.BlockSpec((None, 1, D), lambda i: (layer, 0, 0))
    row = pl.BlockSpec((tm, D), lambda i: (i, 0))
    in_specs = [pl.BlockSpec((tm, a.shape[1]), lambda i: (i, 0)) for a in parts]
    in_specs += [pl.BlockSpec(memory_space=pltpu.VMEM), row, vec, vec]
    args = [*parts, w, x, g.reshape(g.shape[0], 1, D), b.reshape(b.shape[0], 1, D)]
    if routed:
        in_specs += [pl.BlockSpec((D, LANES), lambda i: (0, 0)), pl.BlockSpec((1, LANES), lambda i: (0, 0))]
        args += list(route_wb)
        out_shape = (jax.ShapeDtypeStruct((T, D), f32), jax.ShapeDtypeStruct((T, D // 2), jnp.uint32),
                     jax.ShapeDtypeStruct((T, LANES), f32))
        out_specs = (row, pl.BlockSpec((tm, D // 2), lambda i: (i, 0)), pl.BlockSpec((tm, LANES), lambda i: (i, 0)))
    else:
        out_shape = (jax.ShapeDtypeStruct((T, D), f32), jax.ShapeDtypeStruct((T, D), bf16))
        out_specs = (row, row)
    return pl.pallas_call(
        functools.partial(_mm_ln_kernel, n_parts=len(parts), routed=routed),
        out_shape=out_shape,
        grid=(T // tm,),
        in_specs=in_specs,
        out_specs=out_specs,
        compiler_params=_params("parallel"),
    )(*args)


def _conv_rows(x_ref, w, hist_ref):
    tt = x_ref.shape[0]
    hist_ref[SUBLANES:SUBLANES + tt, :] = x_ref[...]
    acc = _causal_taps(hist_ref[...], w, tt)
    hist_ref[0:SUBLANES, :] = hist_ref[tt:tt + SUBLANES, :]
    return acc


def _causal_taps(ext, w, tt):
    acc = None
    for k in range(CONV_W):
        back = CONV_W - 1 - k
        rows = (pltpu.roll(ext, back, 0) if back else ext)[SUBLANES:SUBLANES + tt, :]
        term = w[k:k + 1, :] * rows
        acc = term if acc is None else acc + term
    return acc


def _scan_affine(a, u):
    n = a.shape[0]
    row = _iota(a.shape, 0)
    d = 1
    while d < n:
        keep = row >= d
        a_s = pltpu.roll(a, d, 0)
        u_s = pltpu.roll(u, d, 0)
        u = u + jnp.where(keep, a * u_s, 0.0)
        a = jnp.where(keep, a * a_s, a)
        d *= 2
    return a, u


def _rglru_kernel(gate_ref, xa_ref, cw_ref, cb_ref, wa_ref, ba_ref, wx_ref, bx_ref, lam_ref,
                  o_ref, hist_ref, h_ref):
    @pl.when(pl.program_id(1) == 0)
    def _():
        h_ref[...] = jnp.zeros_like(h_ref)
        hist_ref[:, 0:SUBLANES, :] = jnp.zeros((hist_ref.shape[0], SUBLANES, hist_ref.shape[2]), f32)

    decay_rate = -RG_C * _softplus(-lam_ref[...])
    for b in range(xa_ref.shape[0]):
        xc = _conv_rows(xa_ref.at[b], cw_ref[...], hist_ref.at[b]) + cb_ref[...]
        r = _sigmoid(_dot(xc, wa_ref[...]) + ba_ref[...])
        i = _sigmoid(_dot(xc, wx_ref[...]) + bx_ref[...])
        log_a = r * decay_rate
        a = jnp.exp(log_a)
        u = jnp.sqrt(1.0 - jnp.exp(2.0 * log_a)) * (i * xc)
        a_cum, h = _scan_affine(a, u)
        h = h + a_cum * h_ref[b, 0:1, :]
        tt = h.shape[0]
        h_ref[b] = jnp.broadcast_to(h[tt - 1:tt, :], h_ref.shape[1:])
        o_ref[b] = (_gelu_tanh(gate_ref[b]) * h).astype(o_ref.dtype)


def _rglru(proj, B, S, layer, conv_w, conv_b, wa, ba, wx, bx, lam):
    tt = min(S, 512)
    ns = S // tt
    xa_col0 = W_A // BW_A
    vec = lambda a: a.reshape(a.shape[0], 1, W_A)
    vspec = pl.BlockSpec((None, 1, BW_A), lambda h, s: (layer, 0, h))
    wspec = pl.BlockSpec((None, None, BW_A, BW_A), lambda h, s: (layer, h, 0, 0))
    proj3 = proj.reshape(B, S, proj.shape[1])
    out = pl.pallas_call(
        _rglru_kernel,
        out_shape=jax.ShapeDtypeStruct((B, S, W_A), bf16),
        grid=(H_A, ns),
        in_specs=[pl.BlockSpec((B, tt, BW_A), lambda h, s: (0, s, h)),
                  pl.BlockSpec((B, tt, BW_A), lambda h, s: (0, s, xa_col0 + h)),
                  pl.BlockSpec((None, CONV_W, BW_A), lambda h, s: (layer, 0, h)),
                  vspec, wspec, vspec, wspec, vspec, vspec],
        out_specs=pl.BlockSpec((B, tt, BW_A), lambda h, s: (0, s, h)),
        scratch_shapes=[pltpu.VMEM((B, SUBLANES + tt, BW_A), f32), pltpu.VMEM((B, SUBLANES, BW_A), f32)],
        compiler_params=_params("parallel", "arbitrary"),
    )(proj3, proj3, conv_w, vec(conv_b), wa, vec(ba), wx, vec(bx), vec(lam))
    return out.reshape(B * S, W_A)


def _ssd_group(X, Bc, Cc, z, dt, a_neg, d_head, nw, st_ref):
    L = CHUNK
    adt = dt * a_neg
    tri = _tri(L)
    cs = _dot_exact_lhs(tri.astype(f32), adt)
    eye_h = (_iota((HG_B, HG_B), 0) == _iota((HG_B, HG_B), 1)).astype(f32)
    cs_t = _transpose_exact(cs, eye_h)
    cs_last = cs[L - 1:L, :]

    expand = (_iota((HG_B, HG_B * HD_B), 1) // HD_B == _iota((HG_B, HG_B * HD_B), 0)).astype(f32)
    dt_x = _dot_exact_rhs(dt, expand)
    ecs_x = _dot_exact_rhs(jnp.exp(cs), expand)
    dec_x = _dot_exact_rhs(jnp.exp(cs_last - cs), expand)
    cdec_x = _dot_exact_rhs(jnp.broadcast_to(jnp.exp(cs_last), (SUBLANES, HG_B)), expand)[0:1]
    d_x = _dot_exact_rhs(jnp.broadcast_to(d_head, (SUBLANES, HG_B)), expand)[0:1]

    xdt = X * dt_x
    cb = _dot_nt(Cc, Bc)
    xdt_b = xdt.astype(bf16)
    left = _iota((L, 2 * HD_B), 1) < HD_B
    pieces = []
    for j in range(0, HG_B, 2):
        pair = xdt_b[:, j * HD_B:(j + 2) * HD_B]
        outs = []
        for jj in (j, j + 1):
            seg = jnp.where(tri, cs[:, jj:jj + 1] - cs_t[jj:jj + 1, :], -jnp.inf)
            outs.append(jnp.dot((cb * jnp.exp(seg)).astype(bf16), pair, preferred_element_type=f32))
        pieces.append(jnp.where(left, outs[0], outs[1]))
    y_diag = jnp.concatenate(pieces, axis=1)

    state = st_ref[...]
    y_off = ecs_x * _dot(Cc, state)
    st_ref[...] = state * cdec_x + _dot_tn(Bc, xdt * dec_x)

    y = y_diag + y_off + X * d_x
    yg = y * _silu(z)
    return yg * lax.rsqrt(jnp.mean(yg * yg, -1, keepdims=True) + RMS_EPS) * nw


def _ssd_kernel(z_ref, x_ref, b_ref, c_ref, dt_ref, cw_ref, cb_ref, dtb_ref, alog_ref, d_ref, nw_ref,
                o_ref, hx_ref, hb_ref, hc_ref, st_ref):
    GW = W_B // NG_B

    @pl.when(pl.program_id(0) == 0)
    def _():
        st_ref[...] = jnp.zeros_like(st_ref)
        for h in (hx_ref, hb_ref, hc_ref):
            h[:, 0:SUBLANES, :] = jnp.zeros((h.shape[0], SUBLANES, h.shape[2]), f32)

    cw, cbias = cw_ref[...], cb_ref[...]
    c_off = W_B + NG_B * N_B
    a_all = -jnp.exp(alog_ref[...])
    d_all = d_ref[...]
    nw_all = nw_ref[...]
    for b in range(x_ref.shape[0]):
        X = _silu(_conv_rows(x_ref.at[b], cw[:, :W_B], hx_ref.at[b]) + cbias[:, :W_B])
        Bm = _silu(_conv_rows(b_ref.at[b], cw[:, W_B:c_off], hb_ref.at[b]) + cbias[:, W_B:c_off])
        Cm = _silu(_conv_rows(c_ref.at[b], cw[:, c_off:], hc_ref.at[b]) + cbias[:, c_off:])
        dt = _softplus(dt_ref[b] + dtb_ref[...])
        for g in range(NG_B):
            hs = slice(g * HG_B, (g + 1) * HG_B)
            ws = slice(g * GW, (g + 1) * GW)
            ns = slice(g * N_B, (g + 1) * N_B)
            out = _ssd_group(X[:, ws], Bm[:, ns], Cm[:, ns], z_ref[b, :, ws], dt[:, hs], a_all[:, hs],
                             d_all[:, hs], nw_all[:, ws], st_ref.at[b * NG_B + g])
            o_ref[b, :, ws] = out.astype(o_ref.dtype)


def _ssd(proj, dt_tail, B, S, layer, conv_w, conv_b, dt_bias, a_log, d, norm_w):
    L = CHUNK
    nc = S // L
    BC = NG_B * N_B
    per_batch = lambda a: a.reshape(B, S, a.shape[1])
    blk = lambda n, cb: pl.BlockSpec((B, L, n), lambda c: (0, c, cb))
    vec = lambda a: a.reshape(a.shape[0], 1, a.shape[-1])
    par = lambda r, n: pl.BlockSpec((None, r, n), lambda c: (layer, 0, 0))
    proj3 = per_batch(proj)
    out = pl.pallas_call(
        _ssd_kernel,
        out_shape=jax.ShapeDtypeStruct((B, S, W_B), bf16),
        grid=(nc,),
        in_specs=[blk(W_B, 2 * W_A // W_B), blk(W_B, (2 * W_A + W_B) // W_B),
                  blk(BC, (2 * W_A + 2 * W_B) // BC), blk(BC, (2 * W_A + 2 * W_B) // BC + 1),
                  blk(H_B, 0), par(CONV_W, CONV_B), par(1, CONV_B), par(1, H_B), par(1, H_B), par(1, H_B),
                  par(1, W_B)],
        out_specs=blk(W_B, 0),
        scratch_shapes=[pltpu.VMEM((B, SUBLANES + L, W_B), f32), pltpu.VMEM((B, SUBLANES + L, BC), f32),
                        pltpu.VMEM((B, SUBLANES + L, BC), f32), pltpu.VMEM((B * NG_B, N_B, W_B // NG_B), f32)],
        compiler_params=_params("arbitrary"),
    )(proj3, proj3, proj3, proj3, per_batch(dt_tail), conv_w, vec(conv_b), vec(dt_bias), vec(a_log), vec(d),
      vec(norm_w))
    return out.reshape(B * S, W_B)


def _s5_tables(a_re, a_im, log_step, b_re, b_im, c_re, c_im):
    L = L_C
    ar, ai = a_re.astype(f32), a_im.astype(f32)
    step = jnp.exp(log_step.astype(f32))[:, None]
    mag = jnp.exp(ar * step)
    lb_re, lb_im = mag * jnp.cos(ai * step), mag * jnp.sin(ai * step)
    den = ar * ar + ai * ai
    f_re = ((lb_re - 1.0) * ar + lb_im * ai) / den
    f_im = (lb_im * ar - (lb_re - 1.0) * ai) / den
    br, bi = b_re.astype(f32), b_im.astype(f32)
    bb_re = f_re[..., None] * br - f_im[..., None] * bi
    bb_im = f_re[..., None] * bi + f_im[..., None] * br
    cr, ci = c_re.astype(f32), c_im.astype(f32)

    def power(n):
        n = n.astype(f32)[None, :, None]
        m = jnp.exp(ar[:, None, :] * step[:, None, :] * n)
        ang = ai[:, None, :] * step[:, None, :] * n
        return m * jnp.cos(ang), m * jnp.sin(ang)

    j = jnp.arange(L)
    pr, pi = power(j)
    lbr = pr[..., None] * bb_re[:, None] - pi[..., None] * bb_im[:, None]
    lbi = pr[..., None] * bb_im[:, None] + pi[..., None] * bb_re[:, None]
    kern = (jnp.einsum('gop,gjpk->gjko', cr, lbr) - jnp.einsum('gop,gjpk->gjko', ci, lbi))
    lagb = kern.reshape(S5_NB, S5_GB, L, GS_C, GS_C).transpose(0, 2, 1, 3, 4)
    lagb = lagb.reshape(S5_NB, L, LANES, GS_C)
    rev = (L - 1) - j
    bend = jnp.concatenate([jnp.take(lbr, rev, axis=1), jnp.take(lbi, rev, axis=1)], 2)
    bendc = bend.reshape(S5_NB, S5_GB, L, 2 * P_C, GS_C).transpose(0, 2, 1, 4, 3)
    bendc = bendc.reshape(S5_NB, L, LANES, 2 * P_C)
    qr, qi = power(j + 1)
    car_re = cr[:, None] * qr[:, :, None, :] - ci[:, None] * qi[:, :, None, :]
    car_im = -(cr[:, None] * qi[:, :, None, :] + ci[:, None] * qr[:, :, None, :])
    car = jnp.concatenate([car_re, car_im], -1)
    ccarc = car.reshape(S5_NB, S5_GB, L, GS_C, 2 * P_C).transpose(0, 2, 4, 1, 3)
    ccarc = ccarc.reshape(S5_NB, L, 2 * P_C, LANES)
    return lagb, bendc.astype(bf16), ccarc.astype(bf16), power


def _s5_kernel(u_ref, lag_ref, bend_ref, ccar_ref, sc_ref, o_ref, toep_t, bend_t, ccar_t, *, seq_chunks):
    L = L_C
    SW = 2 * P_C

    zero = jnp.zeros((LANES, LANES), bf16)
    own_g = (_iota((LANES, LANES), 0) // GS_C) == (_iota((LANES, LANES), 1) // GS_C)
    lag = [jnp.where(own_g, jnp.concatenate([lag_ref[j]] * S5_GB, axis=1), 0.0).astype(bf16) for j in range(L)]
    for s in range(L):
        for t in range(L):
            toep_t[s * LANES:(s + 1) * LANES, t * LANES:(t + 1) * LANES] = lag[t - s] if t >= s else zero
    same_g = (_iota((LANES, S5_GB * SW), 0) // GS_C) == (_iota((LANES, S5_GB * SW), 1) // SW)
    for s in range(L):
        wide = jnp.concatenate([bend_ref[s]] * S5_GB, axis=1)
        bend_t[s * LANES:(s + 1) * LANES, :] = jnp.where(same_g, wide, jnp.zeros_like(wide))
    col_g = _iota((SW, LANES), 1) // GS_C
    for t in range(L):
        blk = ccar_ref[t]
        for g in range(S5_GB):
            ccar_t[g * SW:(g + 1) * SW, t * LANES:(t + 1) * LANES] = jnp.where(col_g == g, blk, jnp.zeros_like(blk))

    n = o_ref.shape[0] // L
    U = jnp.concatenate([u_ref[pl.ds(l, n, stride=L), :].astype(bf16) for l in range(L)], axis=1)
    Y = jnp.dot(U, toep_t[...], preferred_element_type=f32)
    H_all = jnp.dot(U, bend_t[...], preferred_element_type=f32)
    row = _iota((n, SW), 0) % seq_chunks
    prev = []
    for g in range(S5_GB):
        sl = slice(g * SW, (g + 1) * SW)
        H = H_all[:, sl]
        d, k = 1, 0
        while d < seq_chunks:
            hs = pltpu.roll(H, d, 0)
            sw = pltpu.roll(hs, P_C, 1)
            H = H + jnp.where(row >= d, sc_ref[2 * k:2 * k + 1, sl] * hs + sc_ref[2 * k + 1:2 * k + 2, sl] * sw, 0.0)
            d *= 2
            k += 1
        prev.append(jnp.where(row >= 1, pltpu.roll(H, 1, 0), 0.0).astype(bf16))
    Y = Y + jnp.dot(jnp.concatenate(prev, axis=1), ccar_t[...], preferred_element_type=f32)
    for l in range(L):
        o_ref[pl.ds(l, n, stride=L), :] = Y[:, l * LANES:(l + 1) * LANES]


def _s5_post_kernel(y_ref, u_ref, d_ref, w_ref, b_ref, o_ref):
    y = y_ref[...] + d_ref[...] * u_ref[...]
    g = _gelu_tanh(y)
    o_ref[...] = (g * _sigmoid(_dot(g, w_ref[...]) + b_ref[...])).astype(o_ref.dtype)


def _s5(proj, B, S, layer, a_re, a_im, log_step, b_re, b_im, c_re, c_im, d, glu_w, glu_b):
    T = B * S
    L = L_C
    nch = S // L
    lagb, bendc, ccarc, power = _s5_tables(a_re[layer], a_im[layer], log_step[layer], b_re[layer],
                                           b_im[layer], c_re[layer], c_im[layer])
    nsteps = max(1, (nch - 1).bit_length())
    sr, si = power(L * (2 ** jnp.arange(nsteps)))
    scan_c = jnp.stack([jnp.concatenate([sr, sr], -1), jnp.concatenate([-si, si], -1)], 2)
    scan8 = scan_c.reshape(S5_NB, S5_GB, 2 * nsteps, 2 * P_C).transpose(0, 2, 1, 3)
    scan8 = scan8.reshape(S5_NB, 2 * nsteps, S5_GB * 2 * P_C)
    tab = lambda r, c: pl.BlockSpec((None, L, r, c), lambda g: (g, 0, 0, 0))
    y = pl.pallas_call(
        functools.partial(_s5_kernel, seq_chunks=nch),
        out_shape=jax.ShapeDtypeStruct((T, W_C), f32),
        grid=(S5_NB,),
        in_specs=[pl.BlockSpec((T, LANES), lambda g: (0, g)),
                  tab(LANES, GS_C), tab(LANES, 2 * P_C), tab(2 * P_C, LANES),
                  pl.BlockSpec((None, 2 * nsteps, S5_GB * 2 * P_C), lambda g: (g, 0, 0))],
        out_specs=pl.BlockSpec((T, LANES), lambda g: (0, g)),
        scratch_shapes=[pltpu.VMEM((L * LANES, L * LANES), bf16),
                        pltpu.VMEM((L * LANES, S5_GB * 2 * P_C), bf16),
                        pltpu.VMEM((S5_GB * 2 * P_C, L * LANES), bf16)],
        compiler_params=_params("parallel"),
    )(proj, lagb, bendc, ccarc, scan8)
    tm = min(T, 1024)
    vec = pl.BlockSpec((None, 1, W_C), lambda i: (layer, 0, 0))
    return pl.pallas_call(
        _s5_post_kernel,
        out_shape=jax.ShapeDtypeStruct((T, W_C), bf16),
        grid=(T // tm,),
        in_specs=[pl.BlockSpec((tm, W_C), lambda i: (i, 0)),
                  pl.BlockSpec((tm, W_C), lambda i: (i, 0)),
                  vec,
                  pl.BlockSpec((None, W_C, W_C), lambda i: (layer, 0, 0)),
                  vec],
        out_specs=pl.BlockSpec((tm, W_C), lambda i: (i, 0)),
        compiler_params=_params("parallel"),
    )(y, proj, d.reshape(d.shape[0], 1, W_C), glu_w, glu_b.reshape(glu_b.shape[0], 1, W_C))


def _gdn_prep_kernel(q_ref, k_ref, v_ref, qp_ref, kp_ref, vp_ref, ab_ref, cwq_ref, cwk_ref, cwv_ref,
                     alog_ref, dtb_ref, w_ref, u_ref, qd_ref, kd_ref, qk_ref, ge_ref, hist_ref):
    L = CHUNK
    nb = q_ref.shape[0]
    has_prev = (pl.program_id(0) > 0).astype(f32)

    def conv(cur_ref, prev_ref, w_ref_, b, slot):
        hist = hist_ref.at[b * 3 + slot]
        hist[0:SUBLANES, :] = prev_ref[b] * has_prev
        hist[SUBLANES:SUBLANES + L, :] = cur_ref[b]
        return _silu(_causal_taps(hist[...], w_ref_[...], L))

    tri = _tri(L)
    tri_s = _tri(L, strict=True)
    eye_h = (_iota((H_D, H_D), 0) == _iota((H_D, H_D), 1)).astype(f32)
    eye = (_iota((L, L), 0) == _iota((L, L), 1)).astype(f32)
    q_all, k_all, v_all, beta_all, gcs_all, gcs_t = [], [], [], [], [], []
    for b in range(nb):
        q_all.append(conv(q_ref, qp_ref, cwq_ref, b, 0))
        k_all.append(conv(k_ref, kp_ref, cwk_ref, b, 1))
        v_all.append(conv(v_ref, vp_ref, cwv_ref, b, 2))
        ab = ab_ref[b]
        g = -jnp.exp(alog_ref[...]) * _softplus(ab[:, 0:H_D] + dtb_ref[...])
        beta_all.append(_sigmoid(ab[:, H_D:2 * H_D]))
        gcs_all.append(_dot_exact_lhs(tri.astype(f32), g))
        gcs_t.append(_transpose_exact(gcs_all[b], eye_h))
        ge_ref[b] = jnp.concatenate([jnp.exp(gcs_all[b]), jnp.zeros((L, LANES - H_D), f32)], axis=1)

    pairs = [(b, h) for b in range(nb) for h in range(H_D)]
    cols = lambda h: slice(h * DK_D, (h + 1) * DK_D)
    unit = lambda x: x * lax.rsqrt(jnp.sum(x * x, -1, keepdims=True) + 1e-6)
    gcs = [gcs_all[b][:, h:h + 1] for b, h in pairs]
    eg = [jnp.exp(g) for g in gcs]
    beta = [beta_all[b][:, h:h + 1] for b, h in pairs]
    qs = [unit(q_all[b][:, cols(h)]) * (DK_D ** -0.5) for b, h in pairs]
    ks = [unit(k_all[b][:, cols(h)]) for b, h in pairs]
    kbs = [k * bt for k, bt in zip(ks, beta)]
    decay = [jnp.exp(jnp.where(tri, g - gcs_t[b][h:h + 1, :], -jnp.inf)) for (b, h), g in zip(pairs, gcs)]
    kk = [_dot_nt(kb, k) for kb, k in zip(kbs, ks)]
    qk = [_dot_nt(q, k) for q, k in zip(qs, ks)]
    pw = [jnp.where(tri_s, -(a * d), 0.0) for a, d in zip(kk, decay)]
    inv = [eye + p for p in pw]
    for _ in range(int(math.log2(L)) - 1):
        pw = [_dot(p, p) for p in pw]
        inv = [a + _dot(a, p) for a, p in zip(inv, pw)]
    rhs = [jnp.concatenate([v_all[b][:, cols(h)] * bt, kb * e], axis=1)
           for (b, h), bt, kb, e in zip(pairs, beta, kbs, eg)]
    sol = [_dot(a, r) for a, r in zip(inv, rhs)]
    for i, (b, h) in enumerate(pairs):
        u_ref[b, :, cols(h)] = sol[i][:, :DV_D]
        w_ref[b, :, cols(h)] = sol[i][:, DV_D:]
        qk_ref[b, :, h * L:(h + 1) * L] = jnp.where(tri, qk[i] * decay[i], 0.0)
        qd_ref[b, :, cols(h)] = qs[i] * eg[i]
        kd_ref[b, :, cols(h)] = ks[i] * jnp.exp(gcs[i][L - 1:L, :] - gcs[i])


def _gdn_scan_kernel(w_ref, u_ref, qd_ref, kd_ref, qk_ref, ge_ref, z_ref, nw_ref, o_ref, st_ref):
    L = CHUNK

    @pl.when(pl.program_id(0) == 0)
    def _():
        st_ref[...] = jnp.zeros_like(st_ref)

    nw = nw_ref[...]
    pairs = [(b, h) for b in range(w_ref.shape[0]) for h in range(H_D)]
    cols = lambda h: slice(h * DK_D, (h + 1) * DK_D)
    state = [st_ref[b * H_D + h] for b, h in pairs]
    v_new = [u_ref[b, :, cols(h)] - _dot(w_ref[b, :, cols(h)], s) for (b, h), s in zip(pairs, state)]
    o = [_dot(qd_ref[b, :, cols(h)], s) + _dot(qk_ref[b, :, h * L:(h + 1) * L], v)
         for (b, h), s, v in zip(pairs, state, v_new)]
    for (b, h), s, v in zip(pairs, state, v_new):
        st_ref[b * H_D + h] = s * ge_ref[b, L - 1:L, h:h + 1] + _dot_tn(kd_ref[b, :, cols(h)], v)
    for (b, h), y in zip(pairs, o):
        y = y * lax.rsqrt(jnp.mean(y * y, -1, keepdims=True) + RMS_EPS) * nw
        o_ref[b, :, cols(h)] = (y * _silu(z_ref[b, :, cols(h)])).astype(o_ref.dtype)


def _gdn(proj, ab_tail, B, S, layer, conv_w, a_log, dt_bias, norm_w):
    L = CHUNK
    nc = S // L
    per8 = L // SUBLANES
    q_c0 = W_C // W_D
    z_c0 = (W_C + QKV_D) // W_D
    proj3 = proj.reshape(B, S, proj.shape[1])
    blk = lambda n, cb=0: pl.BlockSpec((B, L, n), lambda c: (0, c, cb))
    prev = lambda cb: pl.BlockSpec((B, SUBLANES, W_D), lambda c: (0, jnp.maximum(c * per8 - 1, 0), cb))
    cw = lambda cb: pl.BlockSpec((None, CONV_W, W_D), lambda c: (layer, 0, cb))
    v8 = pl.BlockSpec((None, 1, H_D), lambda c: (layer, 0, 0))
    shp = lambda n: jax.ShapeDtypeStruct((B, S, n), f32)
    w_c, u_c, q_dec, k_dec, qk, gexp = pl.pallas_call(
        _gdn_prep_kernel,
        out_shape=(shp(W_D), shp(W_D), shp(W_D), shp(W_D), shp(H_D * L), shp(LANES)),
        grid=(nc,),
        in_specs=[blk(W_D, q_c0), blk(W_D, q_c0 + 1), blk(W_D, q_c0 + 2), prev(q_c0), prev(q_c0 + 1),
                  prev(q_c0 + 2), blk(2 * H_D), cw(0), cw(1), cw(2), v8, v8],
        out_specs=(blk(W_D), blk(W_D), blk(W_D), blk(W_D), blk(H_D * L), blk(LANES)),
        scratch_shapes=[pltpu.VMEM((3 * B, SUBLANES + L, W_D), f32)],
        compiler_params=_params("parallel"),
    )(proj3, proj3, proj3, proj3, proj3, proj3, ab_tail.reshape(B, S, 2 * H_D), conv_w, conv_w, conv_w,
      a_log.reshape(a_log.shape[0], 1, H_D), dt_bias.reshape(dt_bias.shape[0], 1, H_D))
    out = pl.pallas_call(
        _gdn_scan_kernel,
        out_shape=jax.ShapeDtypeStruct((B, S, W_D), bf16),
        grid=(nc,),
        in_specs=[blk(W_D), blk(W_D), blk(W_D), blk(W_D), blk(H_D * L), blk(LANES), blk(W_D, z_c0),
                  pl.BlockSpec((None, 1, DV_D), lambda c: (layer, 0, 0))],
        out_specs=blk(W_D),
        scratch_shapes=[pltpu.VMEM((B * H_D, DK_D, DV_D), f32)],
        compiler_params=_params("arbitrary"),
    )(w_c, u_c, q_dec, k_dec, qk, gexp, proj3, norm_w.reshape(norm_w.shape[0], 1, DV_D))
    return out.reshape(B * S, W_D)


def _xattn_kernel(q_ref, k_ref, v_ref, o_ref):
    cols = [slice(h * HD_X, (h + 1) * HD_X) for h in range(H_X)]
    s = [_dot_nt(q_ref[:, c], k_ref[:, c]) * (HD_X ** -0.5) for c in cols]
    p = [jnp.exp(x - jnp.max(x, -1, keepdims=True)) for x in s]
    p = [(x / jnp.sum(x, -1, keepdims=True)).astype(bf16) for x in p]
    for c, x in zip(cols, p):
        o_ref[:, c] = jnp.dot(x, v_ref[:, c], preferred_element_type=f32).astype(o_ref.dtype)


def _xattn(q, kv, B, S, M):
    T = B * S
    D = H_X * HD_X
    tq = min(S, 1024)
    nq = S // tq
    return pl.pallas_call(
        _xattn_kernel,
        out_shape=jax.ShapeDtypeStruct((T, D), bf16),
        grid=(B, nq),
        in_specs=[pl.BlockSpec((tq, D), lambda b, i: (b * nq + i, 0)),
                  pl.BlockSpec((M, D), lambda b, i: (b, 0)),
                  pl.BlockSpec((M, D), lambda b, i: (b, 1))],
        out_specs=pl.BlockSpec((tq, D), lambda b, i: (b * nq + i, 0)),
        compiler_params=_params("parallel", "parallel"),
    )(q, kv, kv)


def _route_weights(w_group, b_group, w_expert, b_expert, layer):
    D = w_group.shape[1]
    pad = LANES - N_EXP - NG_E
    w = jnp.concatenate([w_expert[layer], w_group[layer], jnp.zeros((D, pad), f32)], 1)
    b = jnp.concatenate([b_expert[layer], b_group[layer], jnp.zeros((pad,), f32)])[None, :]
    return w, b


MOE_TM = 256
LN_TM = 256


def _moe_plan(route, T):
    n_slots = TOPK_IN * T
    n_tiles = n_slots // MOE_TM + N_EXP
    ids = route[:, 2:2 + TOPK_IN].astype(jnp.int32)
    e_flat = ids.T.reshape(n_slots)
    onehot = (e_flat[:, None] == jnp.arange(N_EXP, dtype=jnp.int32)[None, :]).astype(jnp.int32)
    csum = jnp.cumsum(onehot, axis=0)
    rank = jnp.sum(onehot * csum, axis=1) - 1
    counts = csum[-1]
    tiles_per = (counts + MOE_TM - 1) // MOE_TM
    tile_end = jnp.cumsum(tiles_per)
    tile_start = tile_end - tiles_per
    dest = jnp.sum(onehot * tile_start[None, :], axis=1) * MOE_TM + rank
    tok = jnp.arange(n_slots, dtype=jnp.int32) % T
    src = jnp.zeros((n_tiles * MOE_TM,), jnp.int32).at[dest].set(tok, unique_indices=True)
    j = jnp.arange(n_tiles, dtype=jnp.int32)
    tile_e = jnp.sum((j[:, None] >= tile_end[None, :]).astype(jnp.int32), axis=1)
    last_e = jnp.max(jnp.where(counts > 0, jnp.arange(N_EXP, dtype=jnp.int32), 0))
    tile_e = jnp.minimum(tile_e, last_e)
    n_live = tile_end[-1:].astype(jnp.int32)
    return tile_e, n_live, src, dest.astype(jnp.int32), n_tiles


GATHER_UNROLL = 8


def _moe_group_kernel(tile_e_ref, nt_ref, src_ref, xp_ref, wg_ref, wu_ref, wd_ref, o_ref, xbuf, wgb, wub, wdb):
    i = pl.program_id(0)
    nt = nt_ref[0]
    tm = xbuf.shape[0]

    @pl.when((i < nt) & ((i == 0) | (tile_e_ref[i] != tile_e_ref[jnp.maximum(i - 1, 0)])))
    def _():
        wgb[...] = wg_ref[...].astype(bf16)
        wub[...] = wu_ref[...].astype(bf16)
        wdb[...] = wd_ref[...].astype(bf16)

    @pl.when(i < nt)
    def _():
        def body(j, c):
            rows = [xp_ref[pl.ds(src_ref[i * tm + j * GATHER_UNROLL + q], 1), :] for q in range(GATHER_UNROLL)]
            xbuf[pl.ds(pl.multiple_of(j * GATHER_UNROLL, GATHER_UNROLL), GATHER_UNROLL), :] = jnp.concatenate(rows, axis=0)
            return c
        lax.fori_loop(0, tm // GATHER_UNROLL, body, 0)
        lo, hi = _unpack_halves(xbuf[...])
        x = jnp.concatenate([lo.astype(bf16), hi.astype(bf16)], axis=1)
        h = _silu(jnp.dot(x, wgb[...], preferred_element_type=f32)) * \
            jnp.dot(x, wub[...], preferred_element_type=f32)
        o_ref[...] = _pack_halves(jnp.dot(h.astype(bf16), wdb[...], preferred_element_type=f32))

    @pl.when(i >= nt)
    def _():
        o_ref[...] = jnp.zeros_like(o_ref)


def _moe_group(xp, tile_e, n_live, src, n_tiles, w_gate, w_up, w_down, layer):
    T, half = xp.shape
    D = 2 * half
    wspec = lambda shp: pl.BlockSpec((None, None) + shp, lambda i, te, nt, sr: (layer, te[i], 0, 0))
    return pl.pallas_call(
        _moe_group_kernel,
        out_shape=jax.ShapeDtypeStruct((n_tiles * MOE_TM, half), jnp.uint32),
        grid_spec=pltpu.PrefetchScalarGridSpec(
            num_scalar_prefetch=3,
            grid=(n_tiles,),
            in_specs=[pl.BlockSpec(memory_space=pltpu.VMEM), wspec((D, D_E)), wspec((D, D_E)),
                      wspec((D_E, D))],
            out_specs=pl.BlockSpec((MOE_TM, half), lambda i, te, nt, sr: (i, 0)),
            scratch_shapes=[pltpu.VMEM((MOE_TM, half), jnp.uint32), pltpu.VMEM((D, D_E), bf16),
                            pltpu.VMEM((D, D_E), bf16), pltpu.VMEM((D_E, D), bf16)]),
        compiler_params=_params("arbitrary"),
    )(tile_e, n_live, src, xp, w_gate, w_up, w_down)


def _moe_combine_ln_kernel(pos_ref, x_ref, rt_ref, g_ref, b_ref, ys_hbm, o_ref, ob_ref, ybuf, sems):
    i = pl.program_id(0)
    n = pl.num_programs(0)
    tm = x_ref.shape[0]
    T = n * tm
    groups = tm // GATHER_UNROLL

    def issue(tile, slot):
        for k in range(TOPK_IN):
            base = (slot * TOPK_IN + k) * groups

            def body(j, c, k=k, base=base):
                for q in range(GATHER_UNROLL):
                    p = pos_ref[k * T + tile * tm + j * GATHER_UNROLL + q]
                    pltpu.make_async_copy(ys_hbm.at[pl.ds(p, 1), :], ybuf.at[base + j, pl.ds(q, 1), :],
                                          sems.at[slot]).start()
                return c
            lax.fori_loop(0, groups, body, 0)

    @pl.when(i == 0)
    def _():
        issue(0, 0)

    @pl.when(i + 1 < n)
    def _():
        issue(i + 1, (i + 1) % 2)

    slot = i % 2

    base = slot * TOPK_IN * groups

    def wait_group(j, c):
        pltpu.make_async_copy(ys_hbm.at[pl.ds(0, GATHER_UNROLL), :], ybuf.at[base + j], sems.at[slot]).wait()
        return c
    lax.fori_loop(0, TOPK_IN * groups, wait_group, 0)

    def gathered(k):
        return _unpack_halves(ybuf[pl.ds(base + k * groups, groups)].reshape(tm, ybuf.shape[2]))

    lo1, hi1 = gathered(0)
    lo2, hi2 = gathered(1)
    rt = rt_ref[...]
    w1, w2 = rt[:, 0:1], rt[:, 1:2]
    y = jnp.concatenate([w1 * lo1 + w2 * lo2, w1 * hi1 + w2 * hi2], axis=1)
    o = _layer_norm(ALPHA * x_ref[...] + y, g_ref[...], b_ref[...])
    o_ref[...] = o
    ob_ref[...] = o.astype(bf16)


def _moe_combine_ln(x, route, pos, ys, g, b, layer):
    T, D = x.shape
    tm = min(T, LN_TM)
    vec = pl.BlockSpec((None, 1, D), lambda i, p: (layer, 0, 0))
    row = pl.BlockSpec((tm, D), lambda i, p: (i, 0))
    return pl.pallas_call(
        _moe_combine_ln_kernel,
        out_shape=(jax.ShapeDtypeStruct((T, D), f32), jax.ShapeDtypeStruct((T, D), bf16)),
        grid_spec=pltpu.PrefetchScalarGridSpec(
            num_scalar_prefetch=1,
            grid=(T // tm,),
            in_specs=[row, pl.BlockSpec((tm, LANES), lambda i, p: (i, 0)), vec, vec,
                      pl.BlockSpec(memory_space=pl.ANY)],
            out_specs=(row, row),
            scratch_shapes=[pltpu.VMEM((2 * TOPK_IN * tm // GATHER_UNROLL, GATHER_UNROLL, D // 2), jnp.uint32),
                            pltpu.SemaphoreType.DMA((2,))]),
        compiler_params=_params("arbitrary"),
    )(pos, x, route, g.reshape(g.shape[0], 1, D), b.reshape(b.shape[0], 1, D), ys)


def _moe_ln(xf, xp, route, w_gate, w_up, w_down, g, b, layer):
    tile_e, n_live, src, pos, n_tiles = _moe_plan(route, xf.shape[0])
    ys = _moe_group(xp, tile_e, n_live, src, n_tiles, w_gate, w_up, w_down, layer)
    return _moe_combine_ln(xf, route, pos, ys, g, b, layer)


def kernel(x, mem, ab_w_in, rg_conv_w, rg_conv_b, rg_wa, rg_ba, rg_wx, rg_bx, rg_lam, ssd_conv_w, ssd_conv_b, ssd_dt_bias, ssd_a_log, ssd_d, ssd_norm_w, ab_w_out, cd_w_in, s5_a_re, s5_a_im, s5_log_step, s5_b_re, s5_b_im, s5_c_re, s5_c_im, s5_d, s5_glu_w, s5_glu_b, dn_conv_w, dn_a_log, dn_dt_bias, dn_norm_w, cd_w_out, xa_w_q, xa_w_kv, xa_w_o, moe_w_group, moe_b_group, moe_w_expert, moe_b_expert, moe_w_gate, moe_w_up, moe_w_down, ln1_g, ln1_b, ln2_g, ln2_b, ln3_g, ln3_b):
    B, S, D = x.shape
    M = mem.shape[1]
    T = B * S
    xf = x.reshape(T, D)
    xb = None
    memb = mem.reshape(B * M, D).astype(bf16)

    def gate_proj(tail_w):
        if xb is None:
            return _mm_hi(xf, tail_w, with_bf16=True)
        return _mm_hi(xf, tail_w), xb

    for l in range(DEPTH):
        i = l // 2
        if l % 2 == 0:
            dt_tail, xb = gate_proj(_tail_cols(ab_w_in, i, MAIN_AB))
            proj = _mm(xb, ab_w_in, i, MAIN_AB, f32)
            ya = _rglru(proj, B, S, i, rg_conv_w, rg_conv_b, rg_wa, rg_ba, rg_wx, rg_bx, rg_lam)
            yb = _ssd(proj, dt_tail, B, S, i, ssd_conv_w, ssd_conv_b, ssd_dt_bias, ssd_a_log, ssd_d,
                      ssd_norm_w)
            xf, xb = _mm_ln([ya, yb], ab_w_out[i].astype(bf16), xf, ln1_g, ln1_b, l)
        else:
            ab_tail, xb = gate_proj(_tail_cols(cd_w_in, i, MAIN_CD))
            proj = _mm(xb, cd_w_in, i, MAIN_CD, f32)
            yc = _s5(proj, B, S, i, s5_a_re, s5_a_im, s5_log_step, s5_b_re, s5_b_im, s5_c_re, s5_c_im,
                     s5_d, s5_glu_w, s5_glu_b)
            yd = _gdn(proj, ab_tail, B, S, i, dn_conv_w, dn_a_log, dn_dt_bias, dn_norm_w)
            xf, xb = _mm_ln([yc, yd], cd_w_out[i].astype(bf16), xf, ln1_g, ln1_b, l)
        q = _mm_resident(xb, xa_w_q[l].astype(bf16), bf16)
        kv = _mm(memb, xa_w_kv, l, 2 * D, bf16)
        att = _xattn(q, kv, B, S, M)
        xf, xp, route = _mm_ln([att], xa_w_o[l].astype(bf16), xf, ln2_g, ln2_b, l,
                               route_wb=_route_weights(moe_w_group, moe_b_group, moe_w_expert, moe_b_expert, l))
        xf, xb = _moe_ln(xf, xp, route, moe_w_gate, moe_w_up, moe_w_down, ln3_g, ln3_b, l)
    return xf.reshape(B, S, D)
```
